```python
import math
import jax, jax.numpy as jnp
from jax import lax
import numpy as np

D_MODEL = 4096
BATCH = 1
SEQ = 8192
DEPTH = 1

HEAD_DIM = 128
N_DIFF_HEADS = 8
N_DSA_HEADS = 16
N_DSA_KV = 4
DSA_REP = N_DSA_HEADS // N_DSA_KV
N_IDX_HEADS = 32
IDX_DIM = 128
TOPK_MAX = 256
Q_BLOCK = 128
N_GROUPS = 4
EXPERTS_PER_GROUP = 8
N_EXPERTS = N_GROUPS * EXPERTS_PER_GROUP
TOP_E = 2
D_EXPERT = 1024
LN_EPS = 1e-5
RMS_EPS = 1e-5
DEEPNORM_ALPHA = (2.0 * DEPTH) ** 0.25
DEEPNORM_BETA = (8.0 * DEPTH) ** -0.25

DIFF_QK = N_DIFF_HEADS * 2 * HEAD_DIM
DIFF_V = N_DIFF_HEADS * 2 * HEAD_DIM
DSA_Q = N_DSA_HEADS * HEAD_DIM
DSA_KV = N_DSA_KV * HEAD_DIM
IDX_Q = N_IDX_HEADS * IDX_DIM
SPLITS = [DIFF_QK, DIFF_QK, DIFF_V, DSA_Q, DSA_KV, DSA_KV, IDX_Q, IDX_DIM, N_IDX_HEADS, D_MODEL, D_MODEL]
OFFSETS = [sum(SPLITS[:i]) for i in range(len(SPLITS) + 1)]
D_IN = OFFSETS[-1]

kernel_name = "hybrid_diffattn_dsa_hiermoe_deepnorm"


def alibi_slopes(n):
    return jnp.asarray(2.0 ** (-8.0 * np.arange(1, n + 1) / n), dtype=jnp.float32)


def layer_norm(x, g, b):
    xf = x.astype(jnp.float32)
    mu = jnp.mean(xf, axis=-1, keepdims=True)
    var = jnp.mean(jnp.square(xf - mu), axis=-1, keepdims=True)
    return ((xf - mu) * lax.rsqrt(var + LN_EPS) * g + b).astype(x.dtype)


def diff_attention(q, k, v, lam_q1, lam_k1, lam_q2, lam_k2, subln_g, layer_idx):
    B, S, H, _, Dh = q.shape
    lam_init = 0.8 - 0.6 * math.exp(-0.3 * layer_idx)
    lam = (jnp.exp(jnp.sum((lam_q1 * lam_k1).astype(jnp.float32)))
           - jnp.exp(jnp.sum((lam_q2 * lam_k2).astype(jnp.float32))) + lam_init)
    slopes = alibi_slopes(H)
    nb = S // Q_BLOCK
    qb = q.reshape(B, nb, Q_BLOCK, H, 2, Dh).swapaxes(0, 1)
    kpos = jnp.arange(S)
    scale = Dh ** -0.5

    def block(args):
        i, qi = args
        qpos = i * Q_BLOCK + jnp.arange(Q_BLOCK)
        s = jnp.einsum('bqhcd,bkhcd->bhcqk', qi, k, preferred_element_type=jnp.float32) * scale
        dist = (qpos[:, None] - kpos[None, :]).astype(jnp.float32)
        s = s - slopes[:, None, None, None] * dist
        s = jnp.where(dist >= 0, s, -jnp.inf)
        p = jax.nn.softmax(s, axis=-1)
        p = p[:, :, 0] - lam * p[:, :, 1]
        return jnp.einsum('bhqk,bkhe->bqhe', p.astype(v.dtype), v)

    o = lax.map(block, (jnp.arange(nb), qb))
    o = o.swapaxes(0, 1).reshape(B, S, H, 2 * Dh).astype(jnp.float32)
    o = o * lax.rsqrt(jnp.mean(jnp.square(o), axis=-1, keepdims=True) + RMS_EPS) * subln_g
    o = o * (1.0 - lam_init)
    return o.reshape(B, S, H * 2 * Dh).astype(v.dtype)


def dsa_attention(q, k, v, iq, ik, iw, topk):
    B, S, G, R, Dh = q.shape
    slopes = alibi_slopes(G * R).reshape(G, R)
    nb = S // Q_BLOCK
    to_blocks = lambda a: a.reshape((B, nb, Q_BLOCK) + a.shape[2:]).swapaxes(0, 1)
    kpos = jnp.arange(S)
    scale = Dh ** -0.5
    iw = iw.astype(jnp.float32) * (N_IDX_HEADS ** -0.5 * IDX_DIM ** -0.5)

    def block(args):
        i, qi, iqi, iwi = args
        qpos = i * Q_BLOCK + jnp.arange(Q_BLOCK)
        sc = jnp.einsum('bqhd,bkd->bqhk', iqi, ik, preferred_element_type=jnp.float32)
        score = jnp.einsum('bqh,bqhk->bqk', iwi, jax.nn.relu(sc))
        score = jnp.where(kpos[None, None, :] <= qpos[None, :, None], score, -jnp.inf)
        _, sel = lax.top_k(score, topk)
        valid = sel <= qpos[None, :, None]
        kg = jax.vmap(lambda kk, ix: kk[ix])(k, sel)
        vg = jax.vmap(lambda vv, ix: vv[ix])(v, sel)
        s = jnp.einsum('bqgrd,bqkgd->bqgrk', qi, kg, preferred_element_type=jnp.float32) * scale
        dist = (qpos[None, :, None] - sel).astype(jnp.float32)
        s = s - slopes[None, None, :, :, None] * dist[:, :, None, None, :]
        s = jnp.where(valid[:, :, None, None, :], s, -jnp.inf)
        p = jax.nn.softmax(s, axis=-1)
        return jnp.einsum('bqgrk,bqkgd->bqgrd', p.astype(vg.dtype), vg)

    o = lax.map(block, (jnp.arange(nb), to_blocks(q), to_blocks(iq), to_blocks(iw)))
    return o.swapaxes(0, 1).reshape(B, S, G * R * Dh)


def hier_moe(x, wg, bg, we, be, w1, w3, w2):
    B, S, D = x.shape
    g_logits = jnp.einsum('bsd,dg->bsg', x, wg, preferred_element_type=jnp.float32) + bg
    g_prob = jax.nn.softmax(g_logits, axis=-1)
    g_sel = jnp.argmax(g_logits, axis=-1)
    g_gate = jnp.take_along_axis(g_prob, g_sel[..., None], axis=-1)
    e_logits = (jnp.einsum('bsd,de->bse', x, we, preferred_element_type=jnp.float32) + be)
    e_logits = e_logits.reshape(B, S, N_GROUPS, EXPERTS_PER_GROUP)
    gidx = jnp.broadcast_to(g_sel[..., None, None], (B, S, 1, EXPERTS_PER_GROUP))
    e_logits = jnp.take_along_axis(e_logits, gidx, axis=2)[:, :, 0, :]
    top_v, top_i = lax.top_k(e_logits, TOP_E)
    gate = jax.nn.softmax(top_v, axis=-1) * g_gate
    eid = g_sel[..., None] * EXPERTS_PER_GROUP + top_i
    combine = jnp.sum(jax.nn.one_hot(eid, N_EXPERTS, dtype=jnp.float32) * gate[..., None], axis=-2)
    out = jnp.zeros((B, S, D), jnp.float32)
    for e in range(N_EXPERTS):
        hdn = jax.nn.silu(x @ w1[e]) * (x @ w3[e])
        out = out + combine[..., e:e + 1] * (hdn @ w2[e]).astype(jnp.float32)
    return out.astype(x.dtype)


def setup_inputs(seed: int = 0) -> dict:
    key = jax.random.key(seed)
    ks = jax.random.split(key, 24)
    f32 = jnp.float32
    nrm = lambda k, shape, sc: jax.random.normal(k, shape, f32) * sc
    col_scale = np.ones(D_IN, np.float32)
    col_scale[OFFSETS[2]:OFFSETS[3]] = DEEPNORM_BETA
    col_scale[OFFSETS[5]:OFFSETS[6]] = DEEPNORM_BETA
    D = D_MODEL
    return {
        "x": nrm(ks[0], (BATCH, SEQ, D), 1.0),
        "w_in": nrm(ks[1], (DEPTH, D, D_IN), D ** -0.5) * jnp.asarray(col_scale),
        "lam_q1": nrm(ks[2], (DEPTH, HEAD_DIM), 0.1),
        "lam_k1": nrm(ks[3], (DEPTH, HEAD_DIM), 0.1),
        "lam_q2": nrm(ks[4], (DEPTH, HEAD_DIM), 0.1),
        "lam_k2": nrm(ks[5], (DEPTH, HEAD_DIM), 0.1),
        "diff_subln_g": 1.0 + nrm(ks[6], (DEPTH, 2 * HEAD_DIM), 0.02),
        "w_pa": nrm(ks[7], (DEPTH, DIFF_V, D), DIFF_V ** -0.5 * DEEPNORM_BETA),
        "w_pb": nrm(ks[8], (DEPTH, DSA_Q, D), DSA_Q ** -0.5 * DEEPNORM_BETA),
        "w_o": nrm(ks[9], (DEPTH, D, D), D ** -0.5 * DEEPNORM_BETA),
        "ln1_g": 1.0 + nrm(ks[10], (DEPTH, D), 0.02),
        "ln1_b": nrm(ks[11], (DEPTH, D), 0.01),
        "router_wg": nrm(ks[12], (DEPTH, D, N_GROUPS), D ** -0.5),
        "router_bg": nrm(ks[13], (DEPTH, N_GROUPS), 0.01),
        "router_we": nrm(ks[14], (DEPTH, D, N_EXPERTS), D ** -0.5),
        "router_be": nrm(ks[15], (DEPTH, N_EXPERTS), 0.01),
        "w1": nrm(ks[16], (DEPTH, N_EXPERTS, D, D_EXPERT), D ** -0.5 * DEEPNORM_BETA),
        "w3": nrm(ks[17], (DEPTH, N_EXPERTS, D, D_EXPERT), D ** -0.5 * DEEPNORM_BETA),
        "w2": nrm(ks[18], (DEPTH, N_EXPERTS, D_EXPERT, D), D_EXPERT ** -0.5 * DEEPNORM_BETA),
        "ln2_g": 1.0 + nrm(ks[19], (DEPTH, D), 0.02),
        "ln2_b": nrm(ks[20], (DEPTH, D), 0.01),
    }


def reference(x, w_in, lam_q1, lam_k1, lam_q2, lam_k2, diff_subln_g, w_pa, w_pb, w_o,
              ln1_g, ln1_b, router_wg, router_bg, router_we, router_be, w1, w3, w2, ln2_g, ln2_b):
    B, S, D = x.shape
    topk = min(TOPK_MAX, S // 4)
    h = x
    for l in range(DEPTH):
        z = jnp.einsum('bsd,de->bse', h, w_in[l])
        dq, dk, dv, sq, sk, sv, iq, ik, iw, ga, gb = jnp.split(z, OFFSETS[1:-1], axis=-1)
        a = diff_attention(dq.reshape(B, S, N_DIFF_HEADS, 2, HEAD_DIM),
                           dk.reshape(B, S, N_DIFF_HEADS, 2, HEAD_DIM),
                           dv.reshape(B, S, N_DIFF_HEADS, 2 * HEAD_DIM),
                           lam_q1[l], lam_k1[l], lam_q2[l], lam_k2[l], diff_subln_g[l], l)
        b = dsa_attention(sq.reshape(B, S, N_DSA_KV, DSA_REP, HEAD_DIM),
                          sk.reshape(B, S, N_DSA_KV, HEAD_DIM),
                          sv.reshape(B, S, N_DSA_KV, HEAD_DIM),
                          iq.reshape(B, S, N_IDX_HEADS, IDX_DIM), ik, iw, topk)
        merged = jax.nn.sigmoid(ga) * (a @ w_pa[l]) + jax.nn.sigmoid(gb) * (b @ w_pb[l])
        y = merged @ w_o[l]
        h = layer_norm(DEEPNORM_ALPHA * h + y, ln1_g[l], ln1_b[l])
        m = hier_moe(h, router_wg[l], router_bg[l], router_we[l], router_be[l], w1[l], w3[l], w2[l])
        h = layer_norm(DEEPNORM_ALPHA * h + m, ln2_g[l], ln2_b[l])
    return h
```

```python
import functools
import math

import numpy as np
import jax
import jax.numpy as jnp
from jax import lax
from jax.experimental import pallas as pl
from jax.experimental.pallas import tpu as pltpu

HEAD_DIM = 128
N_DIFF_HEADS = 8
N_DSA_HEADS = 16
N_DSA_KV = 4
DSA_REP = N_DSA_HEADS // N_DSA_KV
N_IDX_HEADS = 32
IDX_DIM = 128
TOPK_MAX = 256
N_GROUPS = 4
EXPERTS_PER_GROUP = 8
N_EXPERTS = N_GROUPS * EXPERTS_PER_GROUP
LN_EPS = 1e-5
RMS_EPS = 1e-5
DEPTH = 1
DEEPNORM_ALPHA = (2.0 * DEPTH) ** 0.25
LAM_INIT = 0.8 - 0.6 * math.exp(-0.3 * 0)

LANES = 128
NEG = -1e30
INT_MIN = -(2 ** 31)
ROW_TILE = 256

F32 = jnp.float32
BF16 = jnp.bfloat16


def _cparams(sem, vmem_mb):
    return pltpu.CompilerParams(dimension_semantics=sem, vmem_limit_bytes=vmem_mb << 20)


def _dot_nt(a, b):
    return lax.dot_general(a, b, (((1,), (1,)), ((), ())), preferred_element_type=F32)


def _mm_kernel(a_ref, b_ref, o_ref):
    o_ref[...] = jnp.dot(a_ref[...], b_ref[...], preferred_element_type=F32).astype(o_ref.dtype)


def _matmul(a, b, out_dtype, tm, tn, name):
    m, k = a.shape
    n = b.shape[1]
    tm, tn = min(tm, m), min(tn, n)
    return pl.pallas_call(
        _mm_kernel,
        grid=(m // tm, n // tn),
        in_specs=[pl.BlockSpec((tm, k), lambda i, j: (i, 0)),
                  pl.BlockSpec((k, tn), lambda i, j: (0, j))],
        out_specs=pl.BlockSpec((tm, tn), lambda i, j: (i, j)),
        out_shape=jax.ShapeDtypeStruct((m, n), out_dtype),
        compiler_params=_cparams(("parallel", "arbitrary"), 48),
        name=name,
    )(a, b)


def _diff_kernel(slopes_ref, lam_ref, g_ref, q_ref, k_ref, v_ref, o_ref, m_scr, l_scr, acc_scr, *, tq):
    h, i, j = pl.program_id(0), pl.program_id(1), pl.program_id(2)

    @pl.when(j == 0)
    def _():
        m_scr[...] = jnp.full(m_scr.shape, NEG, F32)
        l_scr[...] = jnp.zeros(l_scr.shape, F32)
        acc_scr[...] = jnp.zeros(acc_scr.shape, F32)

    def step(masked):
        kcol = lax.broadcasted_iota(jnp.int32, (1, tq), 1)
        bias = slopes_ref[h] * ((j - i) * tq + kcol).astype(F32)
        if masked:
            qrow = lax.broadcasted_iota(jnp.int32, (tq, 1), 0)
            keep = kcol <= qrow
        v = v_ref[...]
        for c in range(2):
            q = q_ref[:, c * HEAD_DIM:(c + 1) * HEAD_DIM]
            k = k_ref[:, c * HEAD_DIM:(c + 1) * HEAD_DIM]
            s = _dot_nt(q, k) + bias
            if masked:
                s = jnp.where(keep, s, NEG)
            m_prev = m_scr[c]
            m_next = jnp.maximum(m_prev, jnp.max(s, axis=1, keepdims=True))
            p = jnp.exp(s - m_next[:, :1])
            alpha = jnp.exp(m_prev - m_next)
            l_scr[c] = alpha * l_scr[c] + jnp.sum(p, axis=1, keepdims=True)
            m_scr[c] = m_next
            pv = jnp.dot(p.astype(BF16), v, preferred_element_type=F32)
            acc_scr[c] = acc_scr[c] * alpha[:, :1] + pv

    @pl.when(j < i)
    def _():
        step(False)

    @pl.when(j == i)
    def _():
        step(True)
        lam = (jnp.exp(jnp.sum(lam_ref[0:1, :] * lam_ref[1:2, :], axis=1, keepdims=True))
               - jnp.exp(jnp.sum(lam_ref[2:3, :] * lam_ref[3:4, :], axis=1, keepdims=True)) + LAM_INIT)
        o = acc_scr[0] / l_scr[0][:, :1] - lam * (acc_scr[1] / l_scr[1][:, :1])
        o = o * lax.rsqrt(jnp.mean(o * o, axis=1, keepdims=True) + RMS_EPS) * g_ref[...]
        o_ref[...] = (o * (1.0 - LAM_INIT)).astype(o_ref.dtype)


def _diff_attention(z, lam4, subln_g, slopes, col0, s_len, tq):
    nq = s_len // tq
    w = 2 * HEAD_DIM
    qb, kb, vb = col0 // w, (col0 + 2048) // w, (col0 + 4096) // w
    return pl.pallas_call(
        functools.partial(_diff_kernel, tq=tq),
        grid=(N_DIFF_HEADS, nq, nq),
        in_specs=[pl.BlockSpec(memory_space=pltpu.SMEM),
                  pl.BlockSpec((4, HEAD_DIM), lambda h, i, j: (0, 0)),
                  pl.BlockSpec((1, w), lambda h, i, j: (0, 0)),
                  pl.BlockSpec((tq, w), lambda h, i, j: (i, qb + h)),
                  pl.BlockSpec((tq, w), lambda h, i, j: (jnp.minimum(j, i), kb + h)),
                  pl.BlockSpec((tq, w), lambda h, i, j: (jnp.minimum(j, i), vb + h))],
        out_specs=pl.BlockSpec((tq, w), lambda h, i, j: (i, h)),
        out_shape=jax.ShapeDtypeStruct((s_len, N_DIFF_HEADS * w), BF16),
        scratch_shapes=[pltpu.VMEM((2, tq, LANES), F32), pltpu.VMEM((2, tq, LANES), F32),
                        pltpu.VMEM((2, tq, w), F32)],
        compiler_params=_cparams(("parallel", "parallel", "arbitrary"), 40),
        name="diffattn",
    )(slopes, lam4, subln_g, z, z, z)


def _indexer_kernel(iq_ref, ik_ref, iw_ref, o_ref, iqh_scr, wb_scr, key_scr, acc_scr, *, tq, tkc, topk, iw_scale):
    i = pl.program_id(0)
    nk = key_scr.shape[0]
    nchunks = (i * tq + tq - 1) // tkc + 1
    reps = tkc // LANES

    for h in range(N_IDX_HEADS):
        iqh_scr[h] = iq_ref[:, h * IDX_DIM:(h + 1) * IDX_DIM]
        wb_scr[h] = jnp.broadcast_to(iw_ref[:, h:h + 1] * iw_scale, (tq, LANES))

    qrow = i * tq + lax.broadcasted_iota(jnp.int32, (tq, 1), 0)

    def chunk(c, carry):
        kc = ik_ref[pl.ds(pl.multiple_of(c * tkc, tkc), tkc), :]
        acc_scr[...] = jnp.zeros(acc_scr.shape, F32)

        def head(h, carry2):
            sc = _dot_nt(iqh_scr[h], kc)
            wv = jnp.concatenate([wb_scr[h]] * reps, axis=1)
            acc_scr[...] += wv * jnp.maximum(sc, 0.0)
            return carry2

        lax.fori_loop(0, N_IDX_HEADS, head, 0)
        bits = pltpu.bitcast(acc_scr[...], jnp.int32)
        skey = bits ^ ((bits >> 31) & 0x7FFFFFFF)
        kcol = c * tkc + lax.broadcasted_iota(jnp.int32, (1, tkc), 1)
        key_scr[c] = jnp.where(kcol <= qrow, skey, INT_MIN)
        return carry

    lax.fori_loop(0, nchunks, chunk, 0)

    def bit_step(bi, thr):
        cand = thr ^ lax.shift_left(jnp.int32(1), 31 - bi)

        def count(c, acc):
            x = jnp.where(key_scr[c] >= cand, 1, 0)
            part = x[:, 0:LANES]
            for r in range(1, reps):
                part = part + x[:, r * LANES:(r + 1) * LANES]
            return acc + part

        cnt = lax.fori_loop(0, nchunks, count, jnp.zeros((tq, LANES), jnp.int32))
        cnt = jnp.sum(cnt, axis=1, keepdims=True)
        return jnp.where(cnt >= topk, cand, thr)

    thr = lax.fori_loop(0, 32, bit_step, jnp.full((tq, 1), INT_MIN, jnp.int32))
    thr = jnp.maximum(thr, INT_MIN + 1)

    def emit(c, carry):
        o_ref[0, c] = jnp.where(key_scr[c] >= thr, 0.0, NEG).astype(o_ref.dtype)
        return carry

    lax.fori_loop(0, nchunks, emit, 0)

    def fill(c, carry):
        o_ref[0, c] = jnp.full((tq, tkc), NEG, o_ref.dtype)
        return carry

    lax.fori_loop(nchunks, nk, fill, 0)


def _indexer(z, ik, zs, s_len, tq, tkc, topk):
    nq, nk = s_len // tq, s_len // tkc
    iw_scale = N_IDX_HEADS ** -0.5 * IDX_DIM ** -0.5
    return pl.pallas_call(
        functools.partial(_indexer_kernel, tq=tq, tkc=tkc, topk=topk, iw_scale=iw_scale),
        grid=(nq,),
        in_specs=[pl.BlockSpec((tq, N_IDX_HEADS * IDX_DIM), lambda i: (i, 0)),
                  pl.BlockSpec((s_len, IDX_DIM), lambda i: (0, 0)),
                  pl.BlockSpec((tq, LANES), lambda i: (i, 1))],
        out_specs=pl.BlockSpec((1, nk, tq, tkc), lambda i: (i, 0, 0, 0)),
        out_shape=jax.ShapeDtypeStruct((nq, nk, tq, tkc), BF16),
        scratch_shapes=[pltpu.VMEM((N_IDX_HEADS, tq, IDX_DIM), BF16),
                        pltpu.VMEM((N_IDX_HEADS, tq, LANES), F32),
                        pltpu.VMEM((nk, tq, tkc), jnp.int32),
                        pltpu.VMEM((tq, tkc), F32)],
        compiler_params=_cparams(("parallel",), 48),
        name="indexer",
    )(z, ik, zs)


def _dsa_kernel(slopes_ref, q_ref, k_ref, v_ref, mb_ref, o_ref, qs_scr, m_scr, l_scr, acc_scr, *, tq, tk):
    g, i, j = pl.program_id(0), pl.program_id(1), pl.program_id(2)
    last = (i * tq + tq - 1) // tk

    @pl.when(j == 0)
    def _():
        m_scr[...] = jnp.full(m_scr.shape, NEG, F32)
        l_scr[...] = jnp.zeros(l_scr.shape, F32)
        acc_scr[...] = jnp.zeros(acc_scr.shape, F32)
        for r in range(DSA_REP):
            qs_scr[r * tq:(r + 1) * tq, :] = q_ref[:, r * HEAD_DIM:(r + 1) * HEAD_DIM]

    @pl.when(j <= last)
    def _():
        s_all = _dot_nt(qs_scr[...], k_ref[...])
        kpos = (j * tk - i * tq + lax.broadcasted_iota(jnp.int32, (1, tk), 1)).astype(F32)
        mb = mb_ref[0, 0].astype(F32)
        v = v_ref[...]
        for r in range(DSA_REP):
            rows = slice(r * tq, (r + 1) * tq)
            s = s_all[rows] + (mb + slopes_ref[g * DSA_REP + r] * kpos)
            m_prev = m_scr[rows]
            m_next = jnp.maximum(m_prev, jnp.max(s, axis=1, keepdims=True))
            p = jnp.exp(s - m_next[:, :1])
            alpha = jnp.exp(m_prev - m_next)
            l_scr[rows] = alpha * l_scr[rows] + jnp.sum(p, axis=1, keepdims=True)
            m_scr[rows] = m_next
            acc_scr[rows] = acc_scr[rows] * alpha + jnp.dot(p.astype(BF16), v, preferred_element_type=F32)

    @pl.when(j == last)
    def _():
        for r in range(DSA_REP):
            rows = slice(r * tq, (r + 1) * tq)
            o_ref[:, r * HEAD_DIM:(r + 1) * HEAD_DIM] = (acc_scr[rows] / l_scr[rows]).astype(o_ref.dtype)


def _dsa_attention(z, maskb, slopes, col_q, col_k, col_v, s_len, tq, tk):
    nq, nk = s_len // tq, s_len // tk
    qw = DSA_REP * HEAD_DIM
    qb, kb, vb = col_q // qw, col_k // HEAD_DIM, col_v // HEAD_DIM

    def last(i):
        return (i * tq + tq - 1) // tk

    return pl.pallas_call(
        functools.partial(_dsa_kernel, tq=tq, tk=tk),
        grid=(N_DSA_KV, nq, nk),
        in_specs=[pl.BlockSpec(memory_space=pltpu.SMEM),
                  pl.BlockSpec((tq, qw), lambda g, i, j: (i, qb + g)),
                  pl.BlockSpec((tk, HEAD_DIM), lambda g, i, j: (jnp.minimum(j, last(i)), kb + g)),
                  pl.BlockSpec((tk, HEAD_DIM), lambda g, i, j: (jnp.minimum(j, last(i)), vb + g)),
                  pl.BlockSpec((1, 1, tq, tk), lambda g, i, j: (i, jnp.minimum(j, last(i)), 0, 0))],
        out_specs=pl.BlockSpec((tq, qw), lambda g, i, j: (i, g)),
        out_shape=jax.ShapeDtypeStruct((s_len, N_DSA_HEADS * HEAD_DIM), BF16),
        scratch_shapes=[pltpu.VMEM((DSA_REP * tq, HEAD_DIM), BF16),
                        pltpu.VMEM((DSA_REP * tq, LANES), F32),
                        pltpu.VMEM((DSA_REP * tq, LANES), F32),
                        pltpu.VMEM((DSA_REP * tq, HEAD_DIM), F32)],
        compiler_params=_cparams(("parallel", "parallel", "arbitrary"), 40),
        name="dsa",
    )(slopes, z, z, z, maskb)


def _merge_kernel(x_ref, wga_ref, wgb_ref, a_ref, wpa_ref, b_ref, wpb_ref, o_ref):
    x = x_ref[...]
    ga = jnp.dot(x, wga_ref[...], preferred_element_type=F32)
    gb = jnp.dot(x, wgb_ref[...], preferred_element_type=F32)
    pa = jnp.dot(a_ref[...], wpa_ref[...], preferred_element_type=F32)
    pb = jnp.dot(b_ref[...], wpb_ref[...], preferred_element_type=F32)
    o_ref[...] = (jax.nn.sigmoid(ga) * pa + jax.nn.sigmoid(gb) * pb).astype(o_ref.dtype)


def _merge(xb, wga, wgb, a, wpa, b, wpb, tm, tn):
    m, d = xb.shape
    ka, kb = a.shape[1], b.shape[1]
    tm, tn = min(tm, m), min(tn, d)
    row = lambda i, j: (i, 0)
    col = lambda i, j: (0, j)
    return pl.pallas_call(
        _merge_kernel,
        grid=(m // tm, d // tn),
        in_specs=[pl.BlockSpec((tm, d), row), pl.BlockSpec((d, tn), col), pl.BlockSpec((d, tn), col),
                  pl.BlockSpec((tm, ka), row), pl.BlockSpec((ka, tn), col),
                  pl.BlockSpec((tm, kb), row), pl.BlockSpec((kb, tn), col)],
        out_specs=pl.BlockSpec((tm, tn), lambda i, j: (i, j)),
        out_shape=jax.ShapeDtypeStruct((m, d), BF16),
        compiler_params=_cparams(("parallel", "arbitrary"), 52),
        name="merge",
    )(xb, wga, wgb, a, wpa, b, wpb)


def _split_bf16(x):
    hi = x.astype(BF16)
    lo = (x - hi.astype(F32)).astype(BF16)
    return hi, lo


def _out_kernel(mg_ref, wo_ref, x_ref, g_ref, b_ref, wr_ref, br_ref,
                h_ref, route_ref, oh_ref, pre_scr, *, tn, nj):
    j = pl.program_id(1)
    y = jnp.dot(mg_ref[...], wo_ref[...], preferred_element_type=F32)
    pre_scr[j] = DEEPNORM_ALPHA * x_ref[...] + y

    @pl.when(j == nj - 1)
    def _():
        d = nj * tn
        tot = pre_scr[0].sum(axis=1, keepdims=True)
        for jj in range(1, nj):
            tot = tot + pre_scr[jj].sum(axis=1, keepdims=True)
        mu = tot / d
        sq = jnp.square(pre_scr[0] - mu).sum(axis=1, keepdims=True)
        for jj in range(1, nj):
            sq = sq + jnp.square(pre_scr[jj] - mu).sum(axis=1, keepdims=True)
        rstd = lax.rsqrt(sq / d + LN_EPS)
        logits = jnp.zeros(route_ref.shape, F32)
        for jj in range(nj):
            cs = slice(jj * tn, (jj + 1) * tn)
            hn = (pre_scr[jj] - mu) * rstd * g_ref[:, cs] + b_ref[:, cs]
            h_ref[:, cs] = hn
            h_hi, h_lo = _split_bf16(hn)
            w_hi, w_lo = _split_bf16(wr_ref[cs, :])
            logits = logits + (jnp.dot(h_hi, w_hi, preferred_element_type=F32)
                               + jnp.dot(h_hi, w_lo, preferred_element_type=F32)
                               + jnp.dot(h_lo, w_hi, preferred_element_type=F32))
        logits = logits + br_ref[...]
        lane = lax.broadcasted_iota(jnp.int32, logits.shape, 1)
        big = jnp.int32(4 * LANES)
        gl = jnp.where(lane < N_GROUPS, logits, -jnp.inf)
        gmax = jnp.max(gl, axis=1, keepdims=True)
        gsel = jnp.min(jnp.where(gl == gmax, lane, big), axis=1, keepdims=True)
        ggate = 1.0 / jnp.sum(jnp.exp(gl - gmax), axis=1, keepdims=True)
        eid = lane - N_GROUPS
        ingrp = (eid >= gsel * EXPERTS_PER_GROUP) & (eid < (gsel + 1) * EXPERTS_PER_GROUP)
        el = jnp.where(ingrp, logits, -jnp.inf)
        v1 = jnp.max(el, axis=1, keepdims=True)
        i1 = jnp.min(jnp.where(el == v1, lane, big), axis=1, keepdims=True)
        el2 = jnp.where(lane == i1, -jnp.inf, el)
        v2 = jnp.max(el2, axis=1, keepdims=True)
        i2 = jnp.min(jnp.where(el2 == v2, lane, big), axis=1, keepdims=True)
        t = jnp.exp(v2 - v1)
        g1 = ggate / (1.0 + t)
        g2 = g1 * t
        e1 = (i1 - N_GROUPS).astype(F32)
        e2 = (i2 - N_GROUPS).astype(F32)
        route_ref[...] = jnp.where(lane == 0, g1, jnp.where(lane == 1, g2,
                                   jnp.where(lane == 2, e1, jnp.where(lane == 3, e2, 0.0))))
        oh_ref[...] = jnp.where(lane == i1 - N_GROUPS, 1.0,
                                jnp.where(lane == i2 - N_GROUPS, 1.0, 0.0)).astype(oh_ref.dtype)


def _out_ln_router(mg, wo, x, g, b, wr, br, tm, tn):
    m, d = x.shape
    tm, tn = min(tm, m), min(tn, d)
    nj = d // tn
    return pl.pallas_call(
        functools.partial(_out_kernel, tn=tn, nj=nj),
        grid=(m // tm, nj),
        in_specs=[pl.BlockSpec((tm, d), lambda i, j: (i, 0)),
                  pl.BlockSpec((d, tn), lambda i, j: (0, j)),
                  pl.BlockSpec((tm, tn), lambda i, j: (i, j)),
                  pl.BlockSpec((1, d), lambda i, j: (0, 0)),
                  pl.BlockSpec((1, d), lambda i, j: (0, 0)),
                  pl.BlockSpec((d, LANES), lambda i, j: (0, 0)),
                  pl.BlockSpec((1, LANES), lambda i, j: (0, 0))],
        out_specs=[pl.BlockSpec((tm, d), lambda i, j: (i, 0)),
                   pl.BlockSpec((tm, LANES), lambda i, j: (i, 0)),
                   pl.BlockSpec((tm, LANES), lambda i, j: (i, 0))],
        out_shape=[jax.ShapeDtypeStruct((m, d), F32),
                   jax.ShapeDtypeStruct((m, LANES), F32),
                   jax.ShapeDtypeStruct((m, LANES), BF16)],
        scratch_shapes=[pltpu.VMEM((nj, tm, tn), F32)],
        compiler_params=_cparams(("parallel", "arbitrary"), 48),
        name="outln",
    )(mg, wo, x, g, b, wr, br)


def _rank_kernel(oh_ref, pos_ref, cnt_ref, base_scr, *, tb):
    @pl.when(pl.program_id(0) == 0)
    def _():
        base_scr[...] = jnp.zeros(base_scr.shape, F32)

    oh = oh_ref[...]
    r = lax.broadcasted_iota(jnp.int32, (tb, tb), 0)
    c = lax.broadcasted_iota(jnp.int32, (tb, tb), 1)
    tri = jnp.where(c <= r, 1.0, 0.0).astype(BF16)
    cs = jnp.dot(tri, oh, preferred_element_type=F32)
    pos_ref[...] = cs - oh.astype(F32) + base_scr[0:1, :]
    base_scr[...] = base_scr[...] + cs[tb - 1:tb, :]
    cnt_ref[...] = base_scr[...]


def _rank(onehot, tb):
    m = onehot.shape[0]
    tb = min(tb, m)
    return pl.pallas_call(
        functools.partial(_rank_kernel, tb=tb),
        grid=(m // tb,),
        in_specs=[pl.BlockSpec((tb, LANES), lambda i: (i, 0))],
        out_specs=[pl.BlockSpec((tb, LANES), lambda i: (i, 0)),
                   pl.BlockSpec((8, LANES), lambda i: (0, 0))],
        out_shape=[jax.ShapeDtypeStruct((m, LANES), F32), jax.ShapeDtypeStruct((8, LANES), F32)],
        scratch_shapes=[pltpu.VMEM((8, LANES), F32)],
        compiler_params=_cparams(("arbitrary",), 32),
        name="rank",
    )(onehot)


def _dest_kernel(pos_ref, route_ref, start_ref, dest_ref):
    lane = lax.broadcasted_iota(jnp.int32, pos_ref.shape, 1).astype(F32)
    v = pos_ref[...] + start_ref[...]
    d1 = jnp.sum(jnp.where(lane == route_ref[:, 2:3], v, 0.0), axis=1, keepdims=True)
    d2 = jnp.sum(jnp.where(lane == route_ref[:, 3:4], v, 0.0), axis=1, keepdims=True)
    dest_ref[...] = jnp.where(lane == 0.0, d1, jnp.where(lane == 1.0, d2, 0.0)).astype(jnp.int32)


def _dest(pos, route, start, tb):
    m = pos.shape[0]
    tb = min(tb, m)
    return pl.pallas_call(
        _dest_kernel,
        grid=(m // tb,),
        in_specs=[pl.BlockSpec((tb, LANES), lambda i: (i, 0)),
                  pl.BlockSpec((tb, LANES), lambda i: (i, 0)),
                  pl.BlockSpec((1, LANES), lambda i: (0, 0))],
        out_specs=pl.BlockSpec((tb, LANES), lambda i: (i, 0)),
        out_shape=jax.ShapeDtypeStruct((m, LANES), jnp.int32),
        compiler_params=_cparams(("parallel",), 32),
        name="dest",
    )(pos, route, start)


def _scatter_kernel(dest_ref, h_ref, xs_in_ref, xs_ref, sem, *, tb):
    del xs_in_ref
    base = pl.program_id(0) * tb

    def row_copy(t, slot):
        d = dest_ref[(base + t) * 2 + slot]
        return pltpu.make_async_copy(h_ref.at[pl.ds(t, 1), :], xs_ref.at[pl.ds(d, 1), :], sem)

    def start(t, carry):
        row_copy(t, 0).start()
        row_copy(t, 1).start()
        return carry

    def wait(t, carry):
        row_copy(t, 0).wait()
        row_copy(t, 1).wait()
        return carry

    lax.fori_loop(0, tb, start, 0)
    lax.fori_loop(0, tb, wait, 0)


def _scatter_rows(dest_flat, h, n_rows, tb):
    m, d = h.shape
    tb = min(tb, m)
    xs0 = jnp.zeros((n_rows, d), F32)
    return pl.pallas_call(
        functools.partial(_scatter_kernel, tb=tb),
        grid_spec=pltpu.PrefetchScalarGridSpec(
            num_scalar_prefetch=1,
            grid=(m // tb,),
            in_specs=[pl.BlockSpec((tb, d), lambda i, dest: (i, 0)),
                      pl.BlockSpec(memory_space=pl.ANY)],
            out_specs=pl.BlockSpec(memory_space=pl.ANY),
            scratch_shapes=[pltpu.SemaphoreType.DMA(())]),
        out_shape=jax.ShapeDtypeStruct((n_rows, d), F32),
        input_output_aliases={2: 0},
        compiler_params=_cparams(("arbitrary",), 32),
        name="scatter",
    )(dest_flat, h, xs0)


def _gm1_kernel(te_ref, nv_ref, xs_ref, w1_ref, w3_ref, o_ref, w1b_scr, w3b_scr):
    r = pl.program_id(1)
    fresh = jnp.logical_or(r == 0, te_ref[r] != te_ref[jnp.maximum(r - 1, 0)])

    @pl.when(jnp.logical_and(r < nv_ref[0], fresh))
    def _():
        w1b_scr[...] = w1_ref[0].astype(BF16)
        w3b_scr[...] = w3_ref[0].astype(BF16)

    @pl.when(r < nv_ref[0])
    def _():
        x = xs_ref[...].astype(BF16)
        a = jnp.dot(x, w1b_scr[...], preferred_element_type=F32)
        b = jnp.dot(x, w3b_scr[...], preferred_element_type=F32)
        o_ref[...] = (a * jax.nn.sigmoid(a) * b).astype(o_ref.dtype)

    @pl.when(r >= nv_ref[0])
    def _():
        o_ref[...] = jnp.zeros(o_ref.shape, o_ref.dtype)


def _gm1(te, nv, xs, w1, w3, tf):
    n_rows, d = xs.shape
    f = w1.shape[2]
    nt = n_rows // ROW_TILE

    def rc(r, nv_):
        return jnp.minimum(r, nv_[0] - 1)

    return pl.pallas_call(
        _gm1_kernel,
        grid_spec=pltpu.PrefetchScalarGridSpec(
            num_scalar_prefetch=2,
            grid=(f // tf, nt),
            in_specs=[pl.BlockSpec((ROW_TILE, d), lambda c, r, te_, nv_: (rc(r, nv_), 0)),
                      pl.BlockSpec((1, d, tf), lambda c, r, te_, nv_: (te_[rc(r, nv_)], 0, c)),
                      pl.BlockSpec((1, d, tf), lambda c, r, te_, nv_: (te_[rc(r, nv_)], 0, c))],
            out_specs=pl.BlockSpec((ROW_TILE, tf), lambda c, r, te_, nv_: (r, c)),
            scratch_shapes=[pltpu.VMEM((d, tf), BF16), pltpu.VMEM((d, tf), BF16)]),
        out_shape=jax.ShapeDtypeStruct((n_rows, f), BF16),
        compiler_params=_cparams(("arbitrary", "arbitrary"), 48),
        name="gm1",
    )(te, nv, xs, w1, w3)


def _gm2_kernel(te_ref, nv_ref, h_ref, w2_ref, o_ref, w2b_scr):
    r = pl.program_id(1)
    fresh = jnp.logical_or(r == 0, te_ref[r] != te_ref[jnp.maximum(r - 1, 0)])

    @pl.when(jnp.logical_and(r < nv_ref[0], fresh))
    def _():
        w2b_scr[...] = w2_ref[0].astype(BF16)

    @pl.when(r < nv_ref[0])
    def _():
        o_ref[...] = jnp.dot(h_ref[...], w2b_scr[...], preferred_element_type=F32)

    @pl.when(r >= nv_ref[0])
    def _():
        o_ref[...] = jnp.zeros(o_ref.shape, o_ref.dtype)


def _gm2(te, nv, hid, w2, tn):
    n_rows, f = hid.shape
    d = w2.shape[2]
    tn = min(tn, d)
    nt = n_rows // ROW_TILE

    def rc(r, nv_):
        return jnp.minimum(r, nv_[0] - 1)

    return pl.pallas_call(
        _gm2_kernel,
        grid_spec=pltpu.PrefetchScalarGridSpec(
            num_scalar_prefetch=2,
            grid=(d // tn, nt),
            in_specs=[pl.BlockSpec((ROW_TILE, f), lambda c, r, te_, nv_: (rc(r, nv_), 0)),
                      pl.BlockSpec((1, f, tn), lambda c, r, te_, nv_: (te_[rc(r, nv_)], 0, c))],
            out_specs=pl.BlockSpec((ROW_TILE, tn), lambda c, r, te_, nv_: (r, c)),
            scratch_shapes=[pltpu.VMEM((f, tn), BF16)]),
        out_shape=jax.ShapeDtypeStruct((n_rows, d), F32),
        compiler_params=_cparams(("arbitrary", "arbitrary"), 48),
        name="gm2",
    )(te, nv, hid, w2)


def _combine_kernel(dest_ref, y_ref, h_ref, route_ref, g_ref, b_ref, o_ref, ybuf, sem, *, tb):
    base = pl.program_id(0) * tb

    def row_copy(t, slot):
        d = dest_ref[(base + t) * 2 + slot]
        return pltpu.make_async_copy(y_ref.at[pl.ds(d, 1), :], ybuf.at[slot, pl.ds(t, 1), :], sem)

    def start(t, carry):
        row_copy(t, 0).start()
        row_copy(t, 1).start()
        return carry

    def wait(t, carry):
        row_copy(t, 0).wait()
        row_copy(t, 1).wait()
        return carry

    lax.fori_loop(0, tb, start, 0)
    lax.fori_loop(0, tb, wait, 0)
    moe = route_ref[:, 0:1] * ybuf[0] + route_ref[:, 1:2] * ybuf[1]
    pre = DEEPNORM_ALPHA * h_ref[...] + moe
    mu = jnp.mean(pre, axis=1, keepdims=True)
    var = jnp.mean(jnp.square(pre - mu), axis=1, keepdims=True)
    o_ref[...] = (pre - mu) * lax.rsqrt(var + LN_EPS) * g_ref[...] + b_ref[...]


def _combine_ln(dest_flat, y, h, route, g, b, tb):
    m, d = h.shape
    tb = min(tb, m)
    return pl.pallas_call(
        functools.partial(_combine_kernel, tb=tb),
        grid_spec=pltpu.PrefetchScalarGridSpec(
            num_scalar_prefetch=1,
            grid=(m // tb,),
            in_specs=[pl.BlockSpec(memory_space=pl.ANY),
                      pl.BlockSpec((tb, d), lambda i, dest: (i, 0)),
                      pl.BlockSpec((tb, LANES), lambda i, dest: (i, 0)),
                      pl.BlockSpec((1, d), lambda i, dest: (0, 0)),
                      pl.BlockSpec((1, d), lambda i, dest: (0, 0))],
            out_specs=pl.BlockSpec((tb, d), lambda i, dest: (i, 0)),
            scratch_shapes=[pltpu.VMEM((2, tb, d), F32), pltpu.SemaphoreType.DMA(())]),
        out_shape=jax.ShapeDtypeStruct((m, d), F32),
        compiler_params=_cparams(("arbitrary",), 40),
        name="combine",
    )(dest_flat, y, h, route, g, b)


def _alibi_slopes(n):
    return jnp.asarray(2.0 ** (-8.0 * np.arange(1, n + 1) / n), dtype=F32)


def kernel(x, w_in, lam_q1, lam_k1, lam_q2, lam_k2, diff_subln_g, w_pa, w_pb, w_o, ln1_g, ln1_b,
           router_wg, router_bg, router_we, router_be, w1, w3, w2, ln2_g, ln2_b):
    bsz, s_len, d = x.shape
    assert bsz == 1 and w_in.shape[0] == DEPTH
    topk = min(TOPK_MAX, s_len // 4)
    x2 = x[0]
    xb = x2.astype(BF16)

    qk_w = N_DIFF_HEADS * 2 * HEAD_DIM
    o_dq, o_dk, o_dv = 0, qk_w, 2 * qk_w
    o_sq = 3 * qk_w
    o_sk = o_sq + N_DSA_HEADS * HEAD_DIM
    o_sv = o_sk + N_DSA_KV * HEAD_DIM
    o_iq = o_sv + N_DSA_KV * HEAD_DIM
    o_ik = o_iq + N_IDX_HEADS * IDX_DIM
    o_iw = o_ik + IDX_DIM
    o_ga = o_iw + N_IDX_HEADS
    o_gb = o_ga + d
    w = w_in[0]
    scale = HEAD_DIM ** -0.5
    w_main = jnp.concatenate([w[:, o_iq:o_ik], w[:, o_dq:o_dk] * scale, w[:, o_dk:o_sq],
                              w[:, o_sq:o_sk] * scale, w[:, o_sk:o_iq]], axis=1).astype(BF16)
    w_small = jnp.concatenate([w[:, o_ik:o_ga], jnp.zeros((d, 2 * LANES - IDX_DIM - N_IDX_HEADS), F32)],
                              axis=1).astype(BF16)
    w_ga = w[:, o_ga:o_gb].astype(BF16)
    w_gb = w[:, o_gb:o_gb + d].astype(BF16)
    c_iq = 0
    c_dq = N_IDX_HEADS * IDX_DIM
    c_sq = c_dq + 3 * qk_w
    c_sk = c_sq + N_DSA_HEADS * HEAD_DIM
    c_sv = c_sk + N_DSA_KV * HEAD_DIM

    z = _matmul(xb, w_main, BF16, 1024, 512, "proj_main")
    zs = _matmul(xb, w_small, F32, 1024, 256, "proj_small")
    ik = zs[:, :IDX_DIM].astype(BF16)

    lam4 = jnp.stack([lam_q1[0], lam_k1[0], lam_q2[0], lam_k2[0]]).astype(F32)
    a = _diff_attention(z, lam4, diff_subln_g[0][None, :], _alibi_slopes(N_DIFF_HEADS), c_dq, s_len,
                        min(512, s_len))

    tq_i, tk_i = min(256, s_len), min(512, s_len)
    maskb = _indexer(z, ik, zs, s_len, tq_i, tk_i, topk)
    b = _dsa_attention(z, maskb, _alibi_slopes(N_DSA_HEADS), c_sq, c_sk, c_sv, s_len, tq_i, tk_i)

    merged = _merge(xb, w_ga, w_gb, a, w_pa[0].astype(BF16), b, w_pb[0].astype(BF16), 512, 256)

    wr = jnp.concatenate([router_wg[0], router_we[0],
                          jnp.zeros((d, LANES - N_GROUPS - N_EXPERTS), F32)], axis=1)
    br = jnp.concatenate([router_bg[0], router_be[0],
                          jnp.zeros((LANES - N_GROUPS - N_EXPERTS,), F32)])[None, :]
    h1, route, onehot = _out_ln_router(merged, w_o[0].astype(BF16), x2, ln1_g[0][None, :], ln1_b[0][None, :],
                                       wr, br, 256, 512)

    pos, cnt = _rank(onehot, 512)
    counts = cnt[0, :N_EXPERTS].astype(jnp.int32)
    padded = ((counts + ROW_TILE - 1) // ROW_TILE) * ROW_TILE
    ends = jnp.cumsum(padded)
    start = jnp.zeros((1, LANES), F32).at[0, :N_EXPERTS].set((ends - padded).astype(F32))
    n_tiles = (2 * s_len) // ROW_TILE + N_EXPERTS
    tile_ids = jnp.arange(n_tiles, dtype=jnp.int32)
    te = jnp.minimum(jnp.sum(tile_ids[:, None] >= (ends // ROW_TILE)[None, :], axis=1), N_EXPERTS - 1)
    te = te.astype(jnp.int32)
    nv = (ends[-1] // ROW_TILE).astype(jnp.int32)[None]
    dest = _dest(pos, route, start, 512)
    dest_flat = dest[:, :2].reshape(-1)

    xs = _scatter_rows(dest_flat, h1, n_tiles * ROW_TILE, 256)
    hid = _gm1(te, nv, xs, w1[0], w3[0], 256)
    y = _gm2(te, nv, hid, w2[0], 1024)
    out = _combine_ln(dest_flat, y, h1, route, ln2_g[0][None, :], ln2_b[0][None, :], 128)
    return out[None]
```

```python
import functools
import math

import numpy as np
import jax
import jax.numpy as jnp
from jax import lax
from jax.experimental import pallas as pl
from jax.experimental.pallas import tpu as pltpu

HEAD_DIM = 128
N_DIFF_HEADS = 8
N_DSA_HEADS = 16
N_DSA_KV = 4
DSA_REP = N_DSA_HEADS // N_DSA_KV
N_IDX_HEADS = 32
IDX_DIM = 128
TOPK_MAX = 256
N_GROUPS = 4
EXPERTS_PER_GROUP = 8
N_EXPERTS = N_GROUPS * EXPERTS_PER_GROUP
LN_EPS = 1e-5
RMS_EPS = 1e-5
DEPTH = 1
DEEPNORM_ALPHA = (2.0 * DEPTH) ** 0.25
LAM_INIT = 0.8 - 0.6 * math.exp(-0.3 * 0)

LANES = 128
SUBLANES = 8
NEG = -1e30
INT_MIN = -(2 ** 31)
ROW_TILE = 256
LOG2E = 1.4426950408889634
POS_RADIX = 256

F32 = jnp.float32
BF16 = jnp.bfloat16


def _cparams(sem, vmem_mb):
    return pltpu.CompilerParams(dimension_semantics=sem, vmem_limit_bytes=vmem_mb << 20)


def _dot_nt(a, b):
    return lax.dot_general(a, b, (((1,), (1,)), ((), ())), preferred_element_type=F32)


def _mm_kernel(a_ref, b_ref, o_ref):
    o_ref[...] = jnp.dot(a_ref[...], b_ref[...], preferred_element_type=F32).astype(o_ref.dtype)


def _matmul(a, b, out_dtype, tm, tn, name):
    m, k = a.shape
    n = b.shape[1]
    tm, tn = min(tm, m), min(tn, n)
    return pl.pallas_call(
        _mm_kernel,
        grid=(m // tm, n // tn),
        in_specs=[pl.BlockSpec((tm, k), lambda i, j: (i, 0)),
                  pl.BlockSpec((k, tn), lambda i, j: (0, j))],
        out_specs=pl.BlockSpec((tm, tn), lambda i, j: (i, j)),
        out_shape=jax.ShapeDtypeStruct((m, n), out_dtype),
        compiler_params=_cparams(("parallel", "arbitrary"), 48),
        name=name,
    )(a, b)


def _key_aug_table(s_len):
    pos = jnp.arange(s_len, dtype=jnp.int32)[:, None]
    lane = jnp.arange(LANES, dtype=jnp.int32)[None, :]
    hi = (pos // POS_RADIX).astype(F32)
    lo = (pos % POS_RADIX).astype(F32)
    t = jnp.where(lane < 2, hi, jnp.where(lane < 4, lo, jnp.where(lane < 7, 1.0, 0.0)))
    return t.astype(BF16)


def _bf16_piece(x):
    return x.astype(BF16).astype(F32)


def _query_aug(slope, qbase, rows):
    s2 = jnp.full((SUBLANES, LANES), slope, F32) * LOG2E
    big = s2 * POS_RADIX
    off = -s2 * jnp.full((SUBLANES, LANES), qbase, F32)
    big_hi = _bf16_piece(big)
    s2_hi = _bf16_piece(s2)
    off_hi = _bf16_piece(off)
    off_mid = _bf16_piece(off - off_hi)
    lane = lax.broadcasted_iota(jnp.int32, (SUBLANES, LANES), 1)
    pieces = [big_hi, big - big_hi, s2_hi, s2 - s2_hi, off_hi, off_mid, off - off_hi - off_mid]
    row = jnp.zeros((SUBLANES, LANES), F32)
    for n, piece in enumerate(pieces):
        row = jnp.where(lane == n, piece, row)
    return jnp.broadcast_to(row[0:1, :], (rows, LANES)).astype(BF16)


def _diff_kernel(slopes_ref, lam_ref, g_ref, q_ref, k_ref, ka_ref, vt_ref, o_ref,
                 qa_scr, kk_scr, m_scr, l_scr, acc_scr, *, tq):
    h, i, j = pl.program_id(0), pl.program_id(1), pl.program_id(2)
    reps = tq // LANES

    @pl.when(j == 0)
    def _():
        m_scr[...] = jnp.full(m_scr.shape, NEG, F32)
        l_scr[...] = jnp.zeros(l_scr.shape, F32)
        acc_scr[...] = jnp.zeros(acc_scr.shape, F32)
        qaug = _query_aug(slopes_ref[h], (i * tq).astype(F32), tq)
        for c in range(2):
            qa_scr[c, :, :HEAD_DIM] = q_ref[:, c * HEAD_DIM:(c + 1) * HEAD_DIM]
            qa_scr[c, :, HEAD_DIM:] = qaug

    def step(masked):
        if masked:
            keep = (lax.broadcasted_iota(jnp.int32, (tq, 1), 0)
                    <= lax.broadcasted_iota(jnp.int32, (1, tq), 1))
        ka = ka_ref[...]
        vt = vt_ref[...]
        for c in range(2):
            kk_scr[c, :, :HEAD_DIM] = k_ref[:, c * HEAD_DIM:(c + 1) * HEAD_DIM]
            kk_scr[c, :, HEAD_DIM:] = ka
            s = _dot_nt(kk_scr[c], qa_scr[c])
            if masked:
                s = jnp.where(keep, s, NEG)
            m_prev = m_scr[c]
            m_next = jnp.maximum(m_prev, jnp.max(s, axis=0, keepdims=True))
            p = jnp.exp2(s - m_next)
            alpha = jnp.exp2(m_prev - m_next)
            l_scr[c] = alpha * l_scr[c] + jnp.sum(p, axis=0, keepdims=True)
            m_scr[c] = m_next
            acc_scr[c] = acc_scr[c] * alpha + jnp.dot(vt, p.astype(BF16), preferred_element_type=F32)

    @pl.when(j < i)
    def _():
        step(False)

    @pl.when(j == i)
    def _():
        step(True)
        lam = (jnp.exp(jnp.sum(lam_ref[0:1, :] * lam_ref[1:2, :], axis=1, keepdims=True))
               - jnp.exp(jnp.sum(lam_ref[2:3, :] * lam_ref[3:4, :], axis=1, keepdims=True)) + LAM_INIT)
        o = acc_scr[0] / l_scr[0] - lam * (acc_scr[1] / l_scr[1])
        g = jnp.concatenate([g_ref[...]] * reps, axis=1)
        o = o * lax.rsqrt(jnp.mean(o * o, axis=0, keepdims=True) + RMS_EPS) * g
        o_ref[...] = (o * (1.0 - LAM_INIT)).T.astype(o_ref.dtype)


def _diff_attention(z, vt, kaug, lam4, g_lanes, slopes, col0, s_len, tq):
    nq = s_len // tq
    w = 2 * HEAD_DIM
    qb, kb = col0 // w, (col0 + N_DIFF_HEADS * w) // w
    return pl.pallas_call(
        functools.partial(_diff_kernel, tq=tq),
        grid=(N_DIFF_HEADS, nq, nq),
        in_specs=[pl.BlockSpec(memory_space=pltpu.SMEM),
                  pl.BlockSpec((4, HEAD_DIM), lambda h, i, j: (0, 0)),
                  pl.BlockSpec((w, LANES), lambda h, i, j: (0, 0)),
                  pl.BlockSpec((tq, w), lambda h, i, j: (i, qb + h)),
                  pl.BlockSpec((tq, w), lambda h, i, j: (jnp.minimum(j, i), kb + h)),
                  pl.BlockSpec((tq, LANES), lambda h, i, j: (jnp.minimum(j, i), 0)),
                  pl.BlockSpec((w, tq), lambda h, i, j: (h, jnp.minimum(j, i)))],
        out_specs=pl.BlockSpec((tq, w), lambda h, i, j: (i, h)),
        out_shape=jax.ShapeDtypeStruct((s_len, N_DIFF_HEADS * w), BF16),
        scratch_shapes=[pltpu.VMEM((2, tq, w), BF16), pltpu.VMEM((2, tq, w), BF16),
                        pltpu.VMEM((2, 1, tq), F32), pltpu.VMEM((2, 1, tq), F32),
                        pltpu.VMEM((2, w, tq), F32)],
        compiler_params=_cparams(("parallel", "parallel", "arbitrary"), 40),
        name="diffattn",
    )(slopes, lam4, g_lanes, z, z, kaug, vt)


def _indexer_kernel(iq_ref, ik_ref, iwt_ref, o_ref, iqh_scr, key_scr, acc_scr, *, tq, tkc, topk, iw_scale):
    i = pl.program_id(0)
    nk = key_scr.shape[0]
    nchunks = (i * tq + tq - 1) // tkc + 1

    for h in range(N_IDX_HEADS):
        iqh_scr[h] = iq_ref[:, h * IDX_DIM:(h + 1) * IDX_DIM]

    qpos = i * tq + lax.broadcasted_iota(jnp.int32, (1, tq), 1)

    def chunk(c, carry):
        kc = ik_ref[pl.ds(pl.multiple_of(c * tkc, tkc), tkc), :]
        acc_scr[...] = jnp.zeros(acc_scr.shape, F32)

        def head(h, carry2):
            sc = _dot_nt(kc, iqh_scr[h])
            acc_scr[...] += (iwt_ref[pl.ds(h, 1), :] * iw_scale) * jnp.maximum(sc, 0.0)
            return carry2

        lax.fori_loop(0, N_IDX_HEADS, head, 0, unroll=16)
        bits = pltpu.bitcast(acc_scr[...], jnp.int32)
        skey = bits ^ ((bits >> 31) & 0x7FFFFFFF)
        kpos = c * tkc + lax.broadcasted_iota(jnp.int32, (tkc, 1), 0)
        key_scr[c] = jnp.where(kpos <= qpos, skey, INT_MIN)
        return carry

    lax.fori_loop(0, nchunks, chunk, 0)

    def bit_step(bi, thr):
        cand = thr ^ lax.shift_left(jnp.int32(1), 31 - bi)

        def count(c, acc):
            x = jnp.where(key_scr[c] >= cand, 1, 0)
            return acc + jnp.sum(x.reshape(tkc // SUBLANES, SUBLANES, tq), axis=0)

        cnt = lax.fori_loop(0, nchunks, count, jnp.zeros((SUBLANES, tq), jnp.int32))
        cnt = jnp.sum(cnt, axis=0, keepdims=True)
        return jnp.where(cnt >= topk, cand, thr)

    thr = lax.fori_loop(0, 32, bit_step, jnp.full((1, tq), INT_MIN, jnp.int32))
    thr = jnp.maximum(thr, INT_MIN + 1)

    def emit(c, carry):
        o_ref[0, c] = jnp.where(key_scr[c] >= thr, 0.0, NEG).astype(o_ref.dtype)
        return carry

    lax.fori_loop(0, nchunks, emit, 0)

    def fill(c, carry):
        o_ref[0, c] = jnp.full((tkc, tq), NEG, o_ref.dtype)
        return carry

    lax.fori_loop(nchunks, nk, fill, 0)


def _indexer(z, ik, iwt, s_len, tq, tkc, topk):
    nq, nk = s_len // tq, s_len // tkc
    iw_scale = N_IDX_HEADS ** -0.5 * IDX_DIM ** -0.5
    return pl.pallas_call(
        functools.partial(_indexer_kernel, tq=tq, tkc=tkc, topk=topk, iw_scale=iw_scale),
        grid=(nq,),
        in_specs=[pl.BlockSpec((tq, N_IDX_HEADS * IDX_DIM), lambda i: (i, 0)),
                  pl.BlockSpec((s_len, IDX_DIM), lambda i: (0, 0)),
                  pl.BlockSpec((N_IDX_HEADS, tq), lambda i: (0, i))],
        out_specs=pl.BlockSpec((1, nk, tkc, tq), lambda i: (i, 0, 0, 0)),
        out_shape=jax.ShapeDtypeStruct((nq, nk, tkc, tq), BF16),
        scratch_shapes=[pltpu.VMEM((N_IDX_HEADS, tq, IDX_DIM), BF16),
                        pltpu.VMEM((nk, tkc, tq), jnp.int32),
                        pltpu.VMEM((tkc, tq), F32)],
        compiler_params=_cparams(("parallel",), 48),
        name="indexer",
    )(z, ik, iwt)


def _dsa_kernel(slopes_ref, q_ref, k_ref, ka_ref, vt_ref, mb_ref, o_ref,
                qa_scr, kk_scr, m_scr, l_scr, acc_scr, *, tq, tk):
    g, i, j = pl.program_id(0), pl.program_id(1), pl.program_id(2)
    last = (i * tq + tq - 1) // tk

    @pl.when(j == 0)
    def _():
        m_scr[...] = jnp.full(m_scr.shape, NEG, F32)
        l_scr[...] = jnp.zeros(l_scr.shape, F32)
        acc_scr[...] = jnp.zeros(acc_scr.shape, F32)
        for r in range(DSA_REP):
            rows = slice(r * tq, (r + 1) * tq)
            qa_scr[rows, :HEAD_DIM] = q_ref[:, r * HEAD_DIM:(r + 1) * HEAD_DIM]
            qa_scr[rows, HEAD_DIM:] = _query_aug(slopes_ref[g * DSA_REP + r], (i * tq).astype(F32), tq)

    @pl.when(j <= last)
    def _():
        kk_scr[:, :HEAD_DIM] = k_ref[...]
        kk_scr[:, HEAD_DIM:] = ka_ref[...]
        mb = mb_ref[0, 0].astype(F32)
        s = _dot_nt(kk_scr[...], qa_scr[...]) + jnp.concatenate([mb] * DSA_REP, axis=1)
        m_prev = m_scr[...]
        m_next = jnp.maximum(m_prev, jnp.max(s, axis=0, keepdims=True))
        p = jnp.exp2(s - m_next)
        alpha = jnp.exp2(m_prev - m_next)
        l_scr[...] = alpha * l_scr[...] + jnp.sum(p, axis=0, keepdims=True)
        m_scr[...] = m_next
        acc_scr[...] = acc_scr[...] * alpha + jnp.dot(vt_ref[...], p.astype(BF16), preferred_element_type=F32)

    @pl.when(j == last)
    def _():
        o = acc_scr[...] / l_scr[...]
        for r in range(DSA_REP):
            o_ref[:, r * HEAD_DIM:(r + 1) * HEAD_DIM] = o[:, r * tq:(r + 1) * tq].T.astype(o_ref.dtype)


def _dsa_attention(z, vt, kaug, maskb, slopes, col_q, col_k, s_len, tq, tk):
    nq, nk = s_len // tq, s_len // tk
    qw = DSA_REP * HEAD_DIM
    qb, kb = col_q // qw, col_k // HEAD_DIM

    def kv(i, j):
        return jnp.minimum(j, (i * tq + tq - 1) // tk)

    return pl.pallas_call(
        functools.partial(_dsa_kernel, tq=tq, tk=tk),
        grid=(N_DSA_KV, nq, nk),
        in_specs=[pl.BlockSpec(memory_space=pltpu.SMEM),
                  pl.BlockSpec((tq, qw), lambda g, i, j: (i, qb + g)),
                  pl.BlockSpec((tk, HEAD_DIM), lambda g, i, j: (kv(i, j), kb + g)),
                  pl.BlockSpec((tk, LANES), lambda g, i, j: (kv(i, j), 0)),
                  pl.BlockSpec((HEAD_DIM, tk), lambda g, i, j: (g, kv(i, j))),
                  pl.BlockSpec((1, 1, tk, tq), lambda g, i, j: (i, kv(i, j), 0, 0))],
        out_specs=pl.BlockSpec((tq, qw), lambda g, i, j: (i, g)),
        out_shape=jax.ShapeDtypeStruct((s_len, N_DSA_HEADS * HEAD_DIM), BF16),
        scratch_shapes=[pltpu.VMEM((DSA_REP * tq, 2 * HEAD_DIM), BF16),
                        pltpu.VMEM((tk, 2 * HEAD_DIM), BF16),
                        pltpu.VMEM((1, DSA_REP * tq), F32),
                        pltpu.VMEM((1, DSA_REP * tq), F32),
                        pltpu.VMEM((HEAD_DIM, DSA_REP * tq), F32)],
        compiler_params=_cparams(("parallel", "parallel", "arbitrary"), 40),
        name="dsa",
    )(slopes, z, z, kaug, vt, maskb)


def _merge_kernel(x_ref, wga_ref, wgb_ref, a_ref, wpa_ref, b_ref, wpb_ref, o_ref):
    x = x_ref[...]
    ga = jnp.dot(x, wga_ref[...], preferred_element_type=F32)
    gb = jnp.dot(x, wgb_ref[...], preferred_element_type=F32)
    pa = jnp.dot(a_ref[...], wpa_ref[...], preferred_element_type=F32)
    pb = jnp.dot(b_ref[...], wpb_ref[...], preferred_element_type=F32)
    o_ref[...] = (jax.nn.sigmoid(ga) * pa + jax.nn.sigmoid(gb) * pb).astype(o_ref.dtype)


def _merge(xb, wga, wgb, a, wpa, b, wpb, tm, tn):
    m, d = xb.shape
    ka, kb = a.shape[1], b.shape[1]
    tm, tn = min(tm, m), min(tn, d)
    row = lambda i, j: (i, 0)
    col = lambda i, j: (0, j)
    return pl.pallas_call(
        _merge_kernel,
        grid=(m // tm, d // tn),
        in_specs=[pl.BlockSpec((tm, d), row), pl.BlockSpec((d, tn), col), pl.BlockSpec((d, tn), col),
                  pl.BlockSpec((tm, ka), row), pl.BlockSpec((ka, tn), col),
                  pl.BlockSpec((tm, kb), row), pl.BlockSpec((kb, tn), col)],
        out_specs=pl.BlockSpec((tm, tn), lambda i, j: (i, j)),
        out_shape=jax.ShapeDtypeStruct((m, d), BF16),
        compiler_params=_cparams(("parallel", "arbitrary"), 52),
        name="merge",
    )(xb, wga, wgb, a, wpa, b, wpb)


def _split_bf16(x):
    hi = x.astype(BF16)
    lo = (x - hi.astype(F32)).astype(BF16)
    return hi, lo


def _out_kernel(mg_ref, wo_ref, x_ref, g_ref, b_ref, wr_ref, br_ref,
                h_ref, route_ref, oh_ref, pre_scr, *, tn, nj):
    j = pl.program_id(1)
    y = jnp.dot(mg_ref[...], wo_ref[...], preferred_element_type=F32)
    pre_scr[j] = DEEPNORM_ALPHA * x_ref[...] + y

    @pl.when(j == nj - 1)
    def _():
        d = nj * tn
        tot = pre_scr[0].sum(axis=1, keepdims=True)
        for jj in range(1, nj):
            tot = tot + pre_scr[jj].sum(axis=1, keepdims=True)
        mu = tot / d
        sq = jnp.square(pre_scr[0] - mu).sum(axis=1, keepdims=True)
        for jj in range(1, nj):
            sq = sq + jnp.square(pre_scr[jj] - mu).sum(axis=1, keepdims=True)
        rstd = lax.rsqrt(sq / d + LN_EPS)
        logits = jnp.zeros(route_ref.shape, F32)
        for jj in range(nj):
            cs = slice(jj * tn, (jj + 1) * tn)
            hn = (pre_scr[jj] - mu) * rstd * g_ref[:, cs] + b_ref[:, cs]
            h_ref[:, cs] = hn
            h_hi, h_lo = _split_bf16(hn)
            w_hi, w_lo = _split_bf16(wr_ref[cs, :])
            logits = logits + (jnp.dot(h_hi, w_hi, preferred_element_type=F32)
                               + jnp.dot(h_hi, w_lo, preferred_element_type=F32)
                               + jnp.dot(h_lo, w_hi, preferred_element_type=F32))
        logits = logits + br_ref[...]
        lane = lax.broadcasted_iota(jnp.int32, logits.shape, 1)
        big = jnp.int32(4 * LANES)
        gl = jnp.where(lane < N_GROUPS, logits, -jnp.inf)
        gmax = jnp.max(gl, axis=1, keepdims=True)
        gsel = jnp.min(jnp.where(gl == gmax, lane, big), axis=1, keepdims=True)
        ggate = 1.0 / jnp.sum(jnp.exp(gl - gmax), axis=1, keepdims=True)
        eid = lane - N_GROUPS
        ingrp = (eid >= gsel * EXPERTS_PER_GROUP) & (eid < (gsel + 1) * EXPERTS_PER_GROUP)
        el = jnp.where(ingrp, logits, -jnp.inf)
        v1 = jnp.max(el, axis=1, keepdims=True)
        i1 = jnp.min(jnp.where(el == v1, lane, big), axis=1, keepdims=True)
        el2 = jnp.where(lane == i1, -jnp.inf, el)
        v2 = jnp.max(el2, axis=1, keepdims=True)
        i2 = jnp.min(jnp.where(el2 == v2, lane, big), axis=1, keepdims=True)
        t = jnp.exp(v2 - v1)
        g1 = ggate / (1.0 + t)
        g2 = g1 * t
        e1 = (i1 - N_GROUPS).astype(F32)
        e2 = (i2 - N_GROUPS).astype(F32)
        route_ref[...] = jnp.where(lane == 0, g1, jnp.where(lane == 1, g2,
                                   jnp.where(lane == 2, e1, jnp.where(lane == 3, e2, 0.0))))
        oh_ref[...] = jnp.where(lane == i1 - N_GROUPS, 1.0,
                                jnp.where(lane == i2 - N_GROUPS, 1.0, 0.0)).astype(oh_ref.dtype)


def _out_ln_router(mg, wo, x, g, b, wr, br, tm, tn):
    m, d = x.shape
    tm, tn = min(tm, m), min(tn, d)
    nj = d // tn
    return pl.pallas_call(
        functools.partial(_out_kernel, tn=tn, nj=nj),
        grid=(m // tm, nj),
        in_specs=[pl.BlockSpec((tm, d), lambda i, j: (i, 0)),
                  pl.BlockSpec((d, tn), lambda i, j: (0, j)),
                  pl.BlockSpec((tm, tn), lambda i, j: (i, j)),
                  pl.BlockSpec((1, d), lambda i, j: (0, 0)),
                  pl.BlockSpec((1, d), lambda i, j: (0, 0)),
                  pl.BlockSpec((d, LANES), lambda i, j: (0, 0)),
                  pl.BlockSpec((1, LANES), lambda i, j: (0, 0))],
        out_specs=[pl.BlockSpec((tm, d), lambda i, j: (i, 0)),
                   pl.BlockSpec((tm, LANES), lambda i, j: (i, 0)),
                   pl.BlockSpec((tm, LANES), lambda i, j: (i, 0))],
        out_shape=[jax.ShapeDtypeStruct((m, d), F32),
                   jax.ShapeDtypeStruct((m, LANES), F32),
                   jax.ShapeDtypeStruct((m, LANES), BF16)],
        scratch_shapes=[pltpu.VMEM((nj, tm, tn), F32)],
        compiler_params=_cparams(("parallel", "arbitrary"), 56),
        name="outln",
    )(mg, wo, x, g, b, wr, br)


def _rank_kernel(oh_ref, pos_ref, cnt_ref, base_scr, *, tb):
    @pl.when(pl.program_id(0) == 0)
    def _():
        base_scr[...] = jnp.zeros(base_scr.shape, F32)

    oh = oh_ref[...]
    r = lax.broadcasted_iota(jnp.int32, (tb, tb), 0)
    c = lax.broadcasted_iota(jnp.int32, (tb, tb), 1)
    tri = jnp.where(c <= r, 1.0, 0.0).astype(BF16)
    cs = jnp.dot(tri, oh, preferred_element_type=F32)
    pos_ref[...] = cs - oh.astype(F32) + base_scr[0:1, :]
    base_scr[...] = base_scr[...] + cs[tb - 1:tb, :]
    cnt_ref[...] = base_scr[...]


def _rank(onehot, tb):
    m = onehot.shape[0]
    tb = min(tb, m)
    return pl.pallas_call(
        functools.partial(_rank_kernel, tb=tb),
        grid=(m // tb,),
        in_specs=[pl.BlockSpec((tb, LANES), lambda i: (i, 0))],
        out_specs=[pl.BlockSpec((tb, LANES), lambda i: (i, 0)),
                   pl.BlockSpec((8, LANES), lambda i: (0, 0))],
        out_shape=[jax.ShapeDtypeStruct((m, LANES), F32), jax.ShapeDtypeStruct((8, LANES), F32)],
        scratch_shapes=[pltpu.VMEM((8, LANES), F32)],
        compiler_params=_cparams(("arbitrary",), 32),
        name="rank",
    )(onehot)


def _dest_kernel(pos_ref, route_ref, start_ref, dest_ref):
    lane = lax.broadcasted_iota(jnp.int32, pos_ref.shape, 1).astype(F32)
    v = pos_ref[...] + start_ref[...]
    d1 = jnp.sum(jnp.where(lane == route_ref[:, 2:3], v, 0.0), axis=1, keepdims=True)
    d2 = jnp.sum(jnp.where(lane == route_ref[:, 3:4], v, 0.0), axis=1, keepdims=True)
    dest_ref[...] = jnp.where(lane == 0.0, d1, jnp.where(lane == 1.0, d2, 0.0)).astype(jnp.int32)


def _dest(pos, route, start, tb):
    m = pos.shape[0]
    tb = min(tb, m)
    return pl.pallas_call(
        _dest_kernel,
        grid=(m // tb,),
        in_specs=[pl.BlockSpec((tb, LANES), lambda i: (i, 0)),
                  pl.BlockSpec((tb, LANES), lambda i: (i, 0)),
                  pl.BlockSpec((1, LANES), lambda i: (0, 0))],
        out_specs=pl.BlockSpec((tb, LANES), lambda i: (i, 0)),
        out_shape=jax.ShapeDtypeStruct((m, LANES), jnp.int32),
        compiler_params=_cparams(("parallel",), 32),
        name="dest",
    )(pos, route, start)


def _scatter_kernel(dest_ref, h_ref, xs_in_ref, xs_ref, sem, *, tb):
    del xs_in_ref
    base = pl.program_id(0) * tb

    def row_copy(t, slot):
        d = dest_ref[(base + t) * 2 + slot]
        return pltpu.make_async_copy(h_ref.at[pl.ds(t, 1), :], xs_ref.at[pl.ds(d, 1), :], sem)

    def start(t, carry):
        row_copy(t, 0).start()
        row_copy(t, 1).start()
        return carry

    def wait(t, carry):
        row_copy(t, 0).wait()
        row_copy(t, 1).wait()
        return carry

    lax.fori_loop(0, tb, start, 0)
    lax.fori_loop(0, tb, wait, 0)


def _scatter_rows(dest_flat, h, n_rows, tb):
    m, d = h.shape
    tb = min(tb, m)
    xs0 = jnp.zeros((n_rows, d), F32)
    return pl.pallas_call(
        functools.partial(_scatter_kernel, tb=tb),
        grid_spec=pltpu.PrefetchScalarGridSpec(
            num_scalar_prefetch=1,
            grid=(m // tb,),
            in_specs=[pl.BlockSpec((tb, d), lambda i, dest: (i, 0)),
                      pl.BlockSpec(memory_space=pl.ANY)],
            out_specs=pl.BlockSpec(memory_space=pl.ANY),
            scratch_shapes=[pltpu.SemaphoreType.DMA(())]),
        out_shape=jax.ShapeDtypeStruct((n_rows, d), F32),
        input_output_aliases={2: 0},
        compiler_params=_cparams(("arbitrary",), 32),
        name="scatter",
    )(dest_flat, h, xs0)


def _gm1_kernel(te_ref, nv_ref, xs_ref, w1_ref, w3_ref, o_ref, w1b_scr, w3b_scr):
    r = pl.program_id(1)
    fresh = jnp.logical_or(r == 0, te_ref[r] != te_ref[jnp.maximum(r - 1, 0)])

    @pl.when(jnp.logical_and(r < nv_ref[0], fresh))
    def _():
        w1b_scr[...] = w1_ref[0].astype(BF16)
        w3b_scr[...] = w3_ref[0].astype(BF16)

    @pl.when(r < nv_ref[0])
    def _():
        x = xs_ref[...].astype(BF16)
        a = jnp.dot(x, w1b_scr[...], preferred_element_type=F32)
        b = jnp.dot(x, w3b_scr[...], preferred_element_type=F32)
        o_ref[...] = (a * jax.nn.sigmoid(a) * b).astype(o_ref.dtype)

    @pl.when(r >= nv_ref[0])
    def _():
        o_ref[...] = jnp.zeros(o_ref.shape, o_ref.dtype)


def _gm1(te, nv, xs, w1, w3, tf):
    n_rows, d = xs.shape
    f = w1.shape[2]
    nt = n_rows // ROW_TILE

    def rc(r, nv_):
        return jnp.minimum(r, nv_[0] - 1)

    return pl.pallas_call(
        _gm1_kernel,
        grid_spec=pltpu.PrefetchScalarGridSpec(
            num_scalar_prefetch=2,
            grid=(f // tf, nt),
            in_specs=[pl.BlockSpec((ROW_TILE, d), lambda c, r, te_, nv_: (rc(r, nv_), 0)),
                      pl.BlockSpec((1, d, tf), lambda c, r, te_, nv_: (te_[rc(r, nv_)], 0, c)),
                      pl.BlockSpec((1, d, tf), lambda c, r, te_, nv_: (te_[rc(r, nv_)], 0, c))],
            out_specs=pl.BlockSpec((ROW_TILE, tf), lambda c, r, te_, nv_: (r, c)),
            scratch_shapes=[pltpu.VMEM((d, tf), BF16), pltpu.VMEM((d, tf), BF16)]),
        out_shape=jax.ShapeDtypeStruct((n_rows, f), BF16),
        compiler_params=_cparams(("arbitrary", "arbitrary"), 48),
        name="gm1",
    )(te, nv, xs, w1, w3)


def _gm2_kernel(te_ref, nv_ref, h_ref, w2_ref, o_ref, w2b_scr):
    r = pl.program_id(1)
    fresh = jnp.logical_or(r == 0, te_ref[r] != te_ref[jnp.maximum(r - 1, 0)])

    @pl.when(jnp.logical_and(r < nv_ref[0], fresh))
    def _():
        w2b_scr[...] = w2_ref[0].astype(BF16)

    @pl.when(r < nv_ref[0])
    def _():
        o_ref[...] = jnp.dot(h_ref[...], w2b_scr[...], preferred_element_type=F32)

    @pl.when(r >= nv_ref[0])
    def _():
        o_ref[...] = jnp.zeros(o_ref.shape, o_ref.dtype)


def _gm2(te, nv, hid, w2, tn):
    n_rows, f = hid.shape
    d = w2.shape[2]
    tn = min(tn, d)
    nt = n_rows // ROW_TILE

    def rc(r, nv_):
        return jnp.minimum(r, nv_[0] - 1)

    return pl.pallas_call(
        _gm2_kernel,
        grid_spec=pltpu.PrefetchScalarGridSpec(
            num_scalar_prefetch=2,
            grid=(d // tn, nt),
            in_specs=[pl.BlockSpec((ROW_TILE, f), lambda c, r, te_, nv_: (rc(r, nv_), 0)),
                      pl.BlockSpec((1, f, tn), lambda c, r, te_, nv_: (te_[rc(r, nv_)], 0, c))],
            out_specs=pl.BlockSpec((ROW_TILE, tn), lambda c, r, te_, nv_: (r, c)),
            scratch_shapes=[pltpu.VMEM((f, tn), BF16)]),
        out_shape=jax.ShapeDtypeStruct((n_rows, d), F32),
        compiler_params=_cparams(("arbitrary", "arbitrary"), 48),
        name="gm2",
    )(te, nv, hid, w2)


def _combine_kernel(dest_ref, y_ref, h_ref, route_ref, g_ref, b_ref, o_ref, ybuf, sem, *, tb):
    base = pl.program_id(0) * tb

    def row_copy(t, slot):
        d = dest_ref[(base + t) * 2 + slot]
        return pltpu.make_async_copy(y_ref.at[pl.ds(d, 1), :], ybuf.at[slot, pl.ds(t, 1), :], sem)

    def start(t, carry):
        row_copy(t, 0).start()
        row_copy(t, 1).start()
        return carry

    def wait(t, carry):
        row_copy(t, 0).wait()
        row_copy(t, 1).wait()
        return carry

    lax.fori_loop(0, tb, start, 0)
    lax.fori_loop(0, tb, wait, 0)
    moe = route_ref[:, 0:1] * ybuf[0] + route_ref[:, 1:2] * ybuf[1]
    pre = DEEPNORM_ALPHA * h_ref[...] + moe
    mu = jnp.mean(pre, axis=1, keepdims=True)
    var = jnp.mean(jnp.square(pre - mu), axis=1, keepdims=True)
    o_ref[...] = (pre - mu) * lax.rsqrt(var + LN_EPS) * g_ref[...] + b_ref[...]


def _combine_ln(dest_flat, y, h, route, g, b, tb):
    m, d = h.shape
    tb = min(tb, m)
    return pl.pallas_call(
        functools.partial(_combine_kernel, tb=tb),
        grid_spec=pltpu.PrefetchScalarGridSpec(
            num_scalar_prefetch=1,
            grid=(m // tb,),
            in_specs=[pl.BlockSpec(memory_space=pl.ANY),
                      pl.BlockSpec((tb, d), lambda i, dest: (i, 0)),
                      pl.BlockSpec((tb, LANES), lambda i, dest: (i, 0)),
                      pl.BlockSpec((1, d), lambda i, dest: (0, 0)),
                      pl.BlockSpec((1, d), lambda i, dest: (0, 0))],
            out_specs=pl.BlockSpec((tb, d), lambda i, dest: (i, 0)),
            scratch_shapes=[pltpu.VMEM((2, tb, d), F32), pltpu.SemaphoreType.DMA(())]),
        out_shape=jax.ShapeDtypeStruct((m, d), F32),
        compiler_params=_cparams(("arbitrary",), 40),
        name="combine",
    )(dest_flat, y, h, route, g, b)


def _alibi_slopes(n):
    return jnp.asarray(2.0 ** (-8.0 * np.arange(1, n + 1) / n), dtype=F32)


def kernel(x, w_in, lam_q1, lam_k1, lam_q2, lam_k2, diff_subln_g, w_pa, w_pb, w_o, ln1_g, ln1_b,
           router_wg, router_bg, router_we, router_be, w1, w3, w2, ln2_g, ln2_b):
    bsz, s_len, d = x.shape
    assert bsz == 1 and w_in.shape[0] == DEPTH
    topk = min(TOPK_MAX, s_len // 4)
    x2 = x[0]
    xb = x2.astype(BF16)

    qk_w = N_DIFF_HEADS * 2 * HEAD_DIM
    o_dq, o_dk, o_dv = 0, qk_w, 2 * qk_w
    o_sq = 3 * qk_w
    o_sk = o_sq + N_DSA_HEADS * HEAD_DIM
    o_sv = o_sk + N_DSA_KV * HEAD_DIM
    o_iq = o_sv + N_DSA_KV * HEAD_DIM
    o_ik = o_iq + N_IDX_HEADS * IDX_DIM
    o_iw = o_ik + IDX_DIM
    o_ga = o_iw + N_IDX_HEADS
    o_gb = o_ga + d
    w = w_in[0]
    qscale = HEAD_DIM ** -0.5 * LOG2E
    w_main = jnp.concatenate([w[:, o_iq:o_ik], w[:, o_dq:o_dk] * qscale, w[:, o_dk:o_sq],
                              w[:, o_sq:o_sk] * qscale, w[:, o_sk:o_iq]], axis=1).astype(BF16)
    w_small = jnp.concatenate([w[:, o_ik:o_ga], jnp.zeros((d, 2 * LANES - IDX_DIM - N_IDX_HEADS), F32)],
                              axis=1).astype(BF16)
    w_ga = w[:, o_ga:o_gb].astype(BF16)
    w_gb = w[:, o_gb:o_gb + d].astype(BF16)
    c_dq = N_IDX_HEADS * IDX_DIM
    c_dv = c_dq + 2 * qk_w
    c_sq = c_dq + 3 * qk_w
    c_sk = c_sq + N_DSA_HEADS * HEAD_DIM
    c_sv = c_sk + N_DSA_KV * HEAD_DIM

    z = _matmul(xb, w_main, BF16, 1024, 512, "proj_main")
    zs = _matmul(xb, w_small, F32, 1024, 256, "proj_small")
    ik = zs[:, :IDX_DIM].astype(BF16)
    iwt = zs[:, IDX_DIM:IDX_DIM + N_IDX_HEADS].T
    dvt = z[:, c_dv:c_dv + qk_w].T
    svt = z[:, c_sv:c_sv + N_DSA_KV * HEAD_DIM].T
    kaug = _key_aug_table(s_len)

    lam4 = jnp.stack([lam_q1[0], lam_k1[0], lam_q2[0], lam_k2[0]]).astype(F32)
    g_lanes = jnp.broadcast_to(diff_subln_g[0][:, None], (2 * HEAD_DIM, LANES))
    a = _diff_attention(z, dvt, kaug, lam4, g_lanes, _alibi_slopes(N_DIFF_HEADS), c_dq, s_len,
                        min(512, s_len))

    tq_i, tk_i = min(256, s_len), min(512, s_len)
    maskb = _indexer(z, ik, iwt, s_len, tq_i, tk_i, topk)
    b = _dsa_attention(z, svt, kaug, maskb, _alibi_slopes(N_DSA_HEADS), c_sq, c_sk, s_len, tq_i, tk_i)

    merged = _merge(xb, w_ga, w_gb, a, w_pa[0].astype(BF16), b, w_pb[0].astype(BF16), 512, 256)

    wr = jnp.concatenate([router_wg[0], router_we[0],
                          jnp.zeros((d, LANES - N_GROUPS - N_EXPERTS), F32)], axis=1)
    br = jnp.concatenate([router_bg[0], router_be[0],
                          jnp.zeros((LANES - N_GROUPS - N_EXPERTS,), F32)])[None, :]
    h1, route, onehot = _out_ln_router(merged, w_o[0].astype(BF16), x2, ln1_g[0][None, :], ln1_b[0][None, :],
                                       wr, br, 512, 512)

    pos, cnt = _rank(onehot, 512)
    counts = cnt[0, :N_EXPERTS].astype(jnp.int32)
    padded = ((counts + ROW_TILE - 1) // ROW_TILE) * ROW_TILE
    ends = jnp.cumsum(padded)
    start = jnp.zeros((1, LANES), F32).at[0, :N_EXPERTS].set((ends - padded).astype(F32))
    n_tiles = (2 * s_len) // ROW_TILE + N_EXPERTS
    tile_ids = jnp.arange(n_tiles, dtype=jnp.int32)
    te = jnp.minimum(jnp.sum(tile_ids[:, None] >= (ends // ROW_TILE)[None, :], axis=1), N_EXPERTS - 1)
    te = te.astype(jnp.int32)
    nv = (ends[-1] // ROW_TILE).astype(jnp.int32)[None]
    dest = _dest(pos, route, start, 512)
    dest_flat = dest[:, :2].reshape(-1)

    xs = _scatter_rows(dest_flat, h1, n_tiles * ROW_TILE, 256)
    hid = _gm1(te, nv, xs, w1[0], w3[0], 256)
    y = _gm2(te, nv, hid, w2[0], 1024)
    out = _combine_ln(dest_flat, y, h1, route, ln2_g[0][None, :], ln2_b[0][None, :], 128)
    return out[None]
```

```python
import functools
import math

import numpy as np
import jax
import jax.numpy as jnp
from jax import lax
from jax.experimental import pallas as pl
from jax.experimental.pallas import tpu as pltpu

HEAD_DIM = 128
N_DIFF_HEADS = 8
N_DSA_HEADS = 16
N_DSA_KV = 4
DSA_REP = N_DSA_HEADS // N_DSA_KV
N_IDX_HEADS = 32
IDX_DIM = 128
TOPK_MAX = 256
N_GROUPS = 4
EXPERTS_PER_GROUP = 8
N_EXPERTS = N_GROUPS * EXPERTS_PER_GROUP
LN_EPS = 1e-5
RMS_EPS = 1e-5
DEPTH = 1
DEEPNORM_ALPHA = (2.0 * DEPTH) ** 0.25
LAM_INIT = 0.8 - 0.6 * math.exp(-0.3 * 0)

LANES = 128
SUBLANES = 8
NEG = -1e30
INT_MIN = -(2 ** 31)
ROW_TILE = 256
LOG2E = 1.4426950408889634
POS_RADIX = 256

F32 = jnp.float32
BF16 = jnp.bfloat16


def _cparams(sem, vmem_mb):
    return pltpu.CompilerParams(dimension_semantics=sem, vmem_limit_bytes=vmem_mb << 20)


def _dot_nt(a, b):
    return lax.dot_general(a, b, (((1,), (1,)), ((), ())), preferred_element_type=F32)


def _mm_kernel(a_ref, b_ref, o_ref):
    o_ref[...] = jnp.dot(a_ref[...], b_ref[...], preferred_element_type=F32).astype(o_ref.dtype)


def _matmul(a, b, out_dtype, tm, tn, name):
    m, k = a.shape
    n = b.shape[1]
    tm, tn = min(tm, m), min(tn, n)
    return pl.pallas_call(
        _mm_kernel,
        grid=(m // tm, n // tn),
        in_specs=[pl.BlockSpec((tm, k), lambda i, j: (i, 0)),
                  pl.BlockSpec((k, tn), lambda i, j: (0, j))],
        out_specs=pl.BlockSpec((tm, tn), lambda i, j: (i, j)),
        out_shape=jax.ShapeDtypeStruct((m, n), out_dtype),
        compiler_params=_cparams(("parallel", "arbitrary"), 48),
        name=name,
    )(a, b)


def _proj_kernel(x_ref, w_ref, sc_ref, o_ref, wb_scr):
    @pl.when(pl.program_id(1) == 0)
    def _():
        wb_scr[...] = w_ref[0].astype(BF16)

    acc = jnp.dot(x_ref[...], wb_scr[...], preferred_element_type=F32)
    o_ref[...] = (acc * sc_ref[...]).astype(o_ref.dtype)


def _proj_from_f32(xb, w3d, col_scale, n_cols, tm, tn):
    m, d = xb.shape
    tm = min(tm, m)
    return pl.pallas_call(
        _proj_kernel,
        grid=(n_cols // tn, m // tm),
        in_specs=[pl.BlockSpec((tm, d), lambda j, i: (i, 0)),
                  pl.BlockSpec((1, d, tn), lambda j, i: (0, 0, j)),
                  pl.BlockSpec((1, tn), lambda j, i: (0, j))],
        out_specs=pl.BlockSpec((tm, tn), lambda j, i: (i, j)),
        out_shape=jax.ShapeDtypeStruct((m, n_cols), BF16),
        scratch_shapes=[pltpu.VMEM((d, tn), BF16)],
        compiler_params=_cparams(("parallel", "arbitrary"), 48),
        name="proj_main",
    )(xb, w3d, col_scale)


def _key_aug_table(s_len):
    pos = jnp.arange(s_len, dtype=jnp.int32)[:, None]
    lane = jnp.arange(LANES, dtype=jnp.int32)[None, :]
    hi = (pos // POS_RADIX).astype(F32)
    lo = (pos % POS_RADIX).astype(F32)
    t = jnp.where(lane < 2, hi, jnp.where(lane < 4, lo, jnp.where(lane < 7, 1.0, 0.0)))
    return t.astype(BF16)


def _bf16_piece(x):
    return x.astype(BF16).astype(F32)


def _query_aug(slope, qbase, rows):
    s2 = jnp.full((SUBLANES, LANES), slope, F32) * LOG2E
    big = s2 * POS_RADIX
    off = -s2 * jnp.full((SUBLANES, LANES), qbase, F32)
    big_hi = _bf16_piece(big)
    s2_hi = _bf16_piece(s2)
    off_hi = _bf16_piece(off)
    off_mid = _bf16_piece(off - off_hi)
    lane = lax.broadcasted_iota(jnp.int32, (SUBLANES, LANES), 1)
    pieces = [big_hi, big - big_hi, s2_hi, s2 - s2_hi, off_hi, off_mid, off - off_hi - off_mid]
    row = jnp.zeros((SUBLANES, LANES), F32)
    for n, piece in enumerate(pieces):
        row = jnp.where(lane == n, piece, row)
    return jnp.broadcast_to(row[0:1, :], (rows, LANES)).astype(BF16)


QUERY_SUB = 256


def _causal_steps(nq, last_of):
    pairs = [(i, j) for i in range(nq) for j in range(last_of(i) + 1)]
    return (jnp.asarray([p[0] for p in pairs], jnp.int32), jnp.asarray([p[1] for p in pairs], jnp.int32))


def _online_softmax_pv(s, vt, m_scr, l_scr, acc_scr, idx, cols):
    at = idx + (slice(None), cols)
    m_prev = m_scr[at]
    m_next = jnp.maximum(m_prev, jnp.max(s, axis=0, keepdims=True))
    p = jnp.exp2(s - m_next)
    alpha = jnp.exp2(m_prev - m_next)
    l_scr[at] = alpha * l_scr[at] + jnp.sum(p, axis=0, keepdims=True)
    m_scr[at] = m_next
    acc_scr[at] = acc_scr[at] * alpha + jnp.dot(vt, p.astype(BF16), preferred_element_type=F32)


def _diff_kernel(qi_ref, kj_ref, slopes_ref, lam_ref, g_ref, q_ref, k_ref, ka_ref, vt_ref, o_ref,
                 qa_scr, kk_scr, m_scr, l_scr, acc_scr, *, tq):
    h, step_id = pl.program_id(0), pl.program_id(1)
    i, j = qi_ref[step_id], kj_ref[step_id]
    reps = tq // LANES
    nsub = tq // QUERY_SUB

    @pl.when(j == 0)
    def _():
        m_scr[...] = jnp.full(m_scr.shape, NEG, F32)
        l_scr[...] = jnp.zeros(l_scr.shape, F32)
        acc_scr[...] = jnp.zeros(acc_scr.shape, F32)
        qaug = _query_aug(slopes_ref[h], (i * tq).astype(F32), tq)
        for c in range(2):
            qa_scr[c, :, :HEAD_DIM] = q_ref[:, c * HEAD_DIM:(c + 1) * HEAD_DIM]
            qa_scr[c, :, HEAD_DIM:] = qaug

    def step(masked):
        ka = ka_ref[...]
        vt = vt_ref[...]
        for c in range(2):
            kk_scr[c, :, :HEAD_DIM] = k_ref[:, c * HEAD_DIM:(c + 1) * HEAD_DIM]
            kk_scr[c, :, HEAD_DIM:] = ka
        scores = [_dot_nt(kk_scr[c], qa_scr[c]) for c in range(2)]
        for c in range(2):
            for u in range(nsub):
                cols = slice(u * QUERY_SUB, (u + 1) * QUERY_SUB)
                s = scores[c][:, cols]
                if masked:
                    keep = (lax.broadcasted_iota(jnp.int32, (tq, 1), 0)
                            <= u * QUERY_SUB + lax.broadcasted_iota(jnp.int32, (1, QUERY_SUB), 1))
                    s = jnp.where(keep, s, NEG)
                _online_softmax_pv(s, vt, m_scr, l_scr, acc_scr, (c,), cols)

    @pl.when(j < i)
    def _():
        step(False)

    @pl.when(j == i)
    def _():
        step(True)
        lam = (jnp.exp(jnp.sum(lam_ref[0:1, :] * lam_ref[1:2, :], axis=1, keepdims=True))
               - jnp.exp(jnp.sum(lam_ref[2:3, :] * lam_ref[3:4, :], axis=1, keepdims=True)) + LAM_INIT)
        o = acc_scr[0] / l_scr[0] - lam * (acc_scr[1] / l_scr[1])
        g = jnp.concatenate([g_ref[...]] * reps, axis=1)
        o = o * lax.rsqrt(jnp.mean(o * o, axis=0, keepdims=True) + RMS_EPS) * g
        o_ref[...] = (o * (1.0 - LAM_INIT)).T.astype(o_ref.dtype)


def _diff_attention(z, vt, kaug, lam4, g_lanes, slopes, col_q, col_k, s_len, tq):
    nq = s_len // tq
    w = 2 * HEAD_DIM
    qb, kb = col_q // w, col_k // w
    qi, kj = _causal_steps(nq, lambda i: i)
    return pl.pallas_call(
        functools.partial(_diff_kernel, tq=tq),
        grid_spec=pltpu.PrefetchScalarGridSpec(
            num_scalar_prefetch=3,
            grid=(N_DIFF_HEADS, qi.shape[0]),
            in_specs=[pl.BlockSpec((4, HEAD_DIM), lambda h, s, qi_, kj_, sl_: (0, 0)),
                      pl.BlockSpec((w, LANES), lambda h, s, qi_, kj_, sl_: (0, 0)),
                      pl.BlockSpec((tq, w), lambda h, s, qi_, kj_, sl_: (qi_[s], qb + h)),
                      pl.BlockSpec((tq, w), lambda h, s, qi_, kj_, sl_: (kj_[s], kb + h)),
                      pl.BlockSpec((tq, LANES), lambda h, s, qi_, kj_, sl_: (kj_[s], 0)),
                      pl.BlockSpec((w, tq), lambda h, s, qi_, kj_, sl_: (h, kj_[s]))],
            out_specs=pl.BlockSpec((tq, w), lambda h, s, qi_, kj_, sl_: (qi_[s], h)),
            scratch_shapes=[pltpu.VMEM((2, tq, w), BF16), pltpu.VMEM((2, tq, w), BF16),
                            pltpu.VMEM((2, 1, tq), F32), pltpu.VMEM((2, 1, tq), F32),
                            pltpu.VMEM((2, w, tq), F32)]),
        out_shape=jax.ShapeDtypeStruct((s_len, N_DIFF_HEADS * w), BF16),
        compiler_params=_cparams(("parallel", "arbitrary"), 40),
        name="diffattn",
    )(qi, kj, slopes, lam4, g_lanes, z, z, kaug, vt)


IQ_SPLIT = 4


def _indexer_kernel(iq0_ref, iq1_ref, iq2_ref, iq3_ref, ik_ref, iwt_ref, o_ref,
                    iqh_scr, key_scr, acc_scr, pcut_scr, *, tq, tkc, topk, iw_scale, pos_bits):
    i = pl.program_id(0)
    nk = key_scr.shape[0]
    nchunks = (i * tq + tq - 1) // tkc + 1
    per = N_IDX_HEADS // IQ_SPLIT

    for h in range(N_IDX_HEADS):
        src = (iq0_ref, iq1_ref, iq2_ref, iq3_ref)[h // per]
        iqh_scr[h] = src[:, (h % per) * IDX_DIM:(h % per + 1) * IDX_DIM]

    qpos = i * tq + lax.broadcasted_iota(jnp.int32, (1, tq), 1)

    def key_pos(c):
        return c * tkc + lax.broadcasted_iota(jnp.int32, (tkc, 1), 0)

    def count_keys(pred):
        def body(c, acc):
            x = jnp.where(pred(c, key_scr[c]), 1, 0)
            return acc + jnp.sum(x.reshape(tkc // SUBLANES, SUBLANES, tq), axis=0)

        cnt = lax.fori_loop(0, nchunks, body, jnp.zeros((SUBLANES, tq), jnp.int32))
        return jnp.sum(cnt, axis=0, keepdims=True)

    def chunk(c, carry):
        kc = ik_ref[pl.ds(pl.multiple_of(c * tkc, tkc), tkc), :]
        acc_scr[...] = jnp.zeros(acc_scr.shape, F32)

        def head(h, carry2):
            sc = _dot_nt(kc, iqh_scr[h])
            acc_scr[...] += (iwt_ref[pl.ds(h, 1), :] * iw_scale) * jnp.maximum(sc, 0.0)
            return carry2

        lax.fori_loop(0, N_IDX_HEADS, head, 0, unroll=16)
        score = acc_scr[...]
        score = jnp.where(score == 0.0, 0.0, score)
        bits = pltpu.bitcast(score, jnp.int32)
        skey = bits ^ ((bits >> 31) & 0x7FFFFFFF)
        key_scr[c] = jnp.where(key_pos(c) <= qpos, skey, INT_MIN)
        return carry

    lax.fori_loop(0, nchunks, chunk, 0)

    def bit_step(bi, thr):
        cand = thr ^ lax.shift_left(jnp.int32(1), 31 - bi)
        return jnp.where(count_keys(lambda c, k: k >= cand) >= topk, cand, thr)

    thr = lax.fori_loop(0, 32, bit_step, jnp.full((1, tq), INT_MIN, jnp.int32))
    thr = jnp.maximum(thr, INT_MIN + 1)

    pcut_scr[...] = jnp.full(pcut_scr.shape, (1 << pos_bits) - 1, jnp.int32)

    @pl.when(jnp.max(count_keys(lambda c, k: k >= thr)) > topk)
    def _():
        need = topk - count_keys(lambda c, k: k > thr)

        def pos_step(bi, p):
            cand = p | lax.shift_left(jnp.int32(1), pos_bits - 1 - bi)
            tied_before = count_keys(lambda c, k: jnp.where(k == thr, key_pos(c), cand) < cand)
            return jnp.where(tied_before < need, cand, p)

        pcut_scr[...] = lax.fori_loop(0, pos_bits, pos_step, jnp.zeros((1, tq), jnp.int32))

    pcut = pcut_scr[...]

    def emit(c, carry):
        k = key_scr[c]
        tie_bias = jnp.where(key_pos(c) <= pcut, 0.0, NEG)
        o_ref[0, c] = jnp.where(k > thr, 0.0, jnp.where(k == thr, tie_bias, NEG)).astype(o_ref.dtype)
        return carry

    lax.fori_loop(0, nchunks, emit, 0)

    def fill(c, carry):
        o_ref[0, c] = jnp.full((tkc, tq), NEG, o_ref.dtype)
        return carry

    lax.fori_loop(nchunks, nk, fill, 0)


def _indexer(z, ik, iwt, col_iq, s_len, tq, tkc, topk):
    nq, nk = s_len // tq, s_len // tkc
    iw_scale = N_IDX_HEADS ** -0.5 * IDX_DIM ** -0.5
    wq = N_IDX_HEADS * IDX_DIM // IQ_SPLIT
    qb = col_iq // wq
    iq_specs = [pl.BlockSpec((tq, wq), functools.partial(lambda i, n: (i, qb + n), n=n)) for n in range(IQ_SPLIT)]
    return pl.pallas_call(
        functools.partial(_indexer_kernel, tq=tq, tkc=tkc, topk=topk, iw_scale=iw_scale,
                          pos_bits=max(1, (s_len - 1).bit_length())),
        grid=(nq,),
        in_specs=iq_specs + [pl.BlockSpec((s_len, IDX_DIM), lambda i: (0, 0)),
                             pl.BlockSpec((N_IDX_HEADS, tq), lambda i: (0, i))],
        out_specs=pl.BlockSpec((1, nk, tkc, tq), lambda i: (i, 0, 0, 0)),
        out_shape=jax.ShapeDtypeStruct((nq, nk, tkc, tq), BF16),
        scratch_shapes=[pltpu.VMEM((N_IDX_HEADS, tq, IDX_DIM), BF16),
                        pltpu.VMEM((nk, tkc, tq), jnp.int32),
                        pltpu.VMEM((tkc, tq), F32),
                        pltpu.VMEM((1, tq), jnp.int32)],
        compiler_params=_cparams(("parallel",), 48),
        name="indexer",
    )(z, z, z, z, ik, iwt)


def _dsa_kernel(qi_ref, kj_ref, slopes_ref, q_ref, k_ref, ka_ref, vt_ref, mb_ref, o_ref,
                qa_scr, kk_scr, m_scr, l_scr, acc_scr, *, tq, tk):
    g, step_id = pl.program_id(0), pl.program_id(1)
    i, j = qi_ref[step_id], kj_ref[step_id]
    last = (i * tq + tq - 1) // tk

    @pl.when(j == 0)
    def _():
        m_scr[...] = jnp.full(m_scr.shape, NEG, F32)
        l_scr[...] = jnp.zeros(l_scr.shape, F32)
        acc_scr[...] = jnp.zeros(acc_scr.shape, F32)
        for r in range(DSA_REP):
            rows = slice(r * tq, (r + 1) * tq)
            qa_scr[rows, :HEAD_DIM] = q_ref[:, r * HEAD_DIM:(r + 1) * HEAD_DIM]
            qa_scr[rows, HEAD_DIM:] = _query_aug(slopes_ref[g * DSA_REP + r], (i * tq).astype(F32), tq)

    kk_scr[:, :HEAD_DIM] = k_ref[...]
    kk_scr[:, HEAD_DIM:] = ka_ref[...]
    mb = mb_ref[0, 0].astype(F32)
    vt = vt_ref[...]

    s_all = _dot_nt(kk_scr[...], qa_scr[...])
    for r in range(DSA_REP):
        cols = slice(r * tq, (r + 1) * tq)
        _online_softmax_pv(s_all[:, cols] + mb, vt, m_scr, l_scr, acc_scr, (), cols)

    @pl.when(j == last)
    def _():
        o = acc_scr[...] / l_scr[...]
        for r in range(DSA_REP):
            o_ref[:, r * HEAD_DIM:(r + 1) * HEAD_DIM] = o[:, r * tq:(r + 1) * tq].T.astype(o_ref.dtype)


def _dsa_attention(z, vt, kaug, maskb, slopes, col_q, col_k, s_len, tq, tk):
    nq = s_len // tq
    qw = DSA_REP * HEAD_DIM
    qb, kb = col_q // qw, col_k // HEAD_DIM
    qi, kj = _causal_steps(nq, lambda i: (i * tq + tq - 1) // tk)
    return pl.pallas_call(
        functools.partial(_dsa_kernel, tq=tq, tk=tk),
        grid_spec=pltpu.PrefetchScalarGridSpec(
            num_scalar_prefetch=3,
            grid=(N_DSA_KV, qi.shape[0]),
            in_specs=[pl.BlockSpec((tq, qw), lambda g, s, qi_, kj_, sl_: (qi_[s], qb + g)),
                      pl.BlockSpec((tk, HEAD_DIM), lambda g, s, qi_, kj_, sl_: (kj_[s], kb + g)),
                      pl.BlockSpec((tk, LANES), lambda g, s, qi_, kj_, sl_: (kj_[s], 0)),
                      pl.BlockSpec((HEAD_DIM, tk), lambda g, s, qi_, kj_, sl_: (g, kj_[s])),
                      pl.BlockSpec((1, 1, tk, tq), lambda g, s, qi_, kj_, sl_: (qi_[s], kj_[s], 0, 0))],
            out_specs=pl.BlockSpec((tq, qw), lambda g, s, qi_, kj_, sl_: (qi_[s], g)),
            scratch_shapes=[pltpu.VMEM((DSA_REP * tq, 2 * HEAD_DIM), BF16),
                            pltpu.VMEM((tk, 2 * HEAD_DIM), BF16),
                            pltpu.VMEM((1, DSA_REP * tq), F32),
                            pltpu.VMEM((1, DSA_REP * tq), F32),
                            pltpu.VMEM((HEAD_DIM, DSA_REP * tq), F32)]),
        out_shape=jax.ShapeDtypeStruct((s_len, N_DSA_HEADS * HEAD_DIM), BF16),
        compiler_params=_cparams(("parallel", "arbitrary"), 40),
        name="dsa",
    )(qi, kj, slopes, z, z, kaug, vt, maskb)


def _merge_kernel(x_ref, wga_ref, wgb_ref, a_ref, wpa_ref, b_ref, wpb_ref, o_ref):
    x = x_ref[...]
    ga = jnp.dot(x, wga_ref[...], preferred_element_type=F32)
    gb = jnp.dot(x, wgb_ref[...], preferred_element_type=F32)
    pa = jnp.dot(a_ref[...], wpa_ref[...], preferred_element_type=F32)
    pb = jnp.dot(b_ref[...], wpb_ref[...], preferred_element_type=F32)
    o_ref[...] = (jax.nn.sigmoid(ga) * pa + jax.nn.sigmoid(gb) * pb).astype(o_ref.dtype)


def _merge(xb, w_gates, a, wpa, b, wpb, tm, tn):
    m, d = xb.shape
    ka, kb = a.shape[1], b.shape[1]
    tm, tn = min(tm, m), min(tn, d)
    row = lambda i, j: (i, 0)
    col = lambda i, j: (0, j)
    col_b = lambda i, j: (0, d // tn + j)
    return pl.pallas_call(
        _merge_kernel,
        grid=(m // tm, d // tn),
        in_specs=[pl.BlockSpec((tm, d), row), pl.BlockSpec((d, tn), col), pl.BlockSpec((d, tn), col_b),
                  pl.BlockSpec((tm, ka), row), pl.BlockSpec((ka, tn), col),
                  pl.BlockSpec((tm, kb), row), pl.BlockSpec((kb, tn), col)],
        out_specs=pl.BlockSpec((tm, tn), lambda i, j: (i, j)),
        out_shape=jax.ShapeDtypeStruct((m, d), BF16),
        compiler_params=_cparams(("parallel", "arbitrary"), 52),
        name="merge",
    )(xb, w_gates, w_gates, a, wpa, b, wpb)


def _split_bf16(x):
    hi = x.astype(BF16)
    lo = (x - hi.astype(F32)).astype(BF16)
    return hi, lo


def _out_kernel(mg_ref, wo_ref, x_ref, g_ref, b_ref, wr_ref, br_ref,
                h_ref, route_ref, oh_ref, pre_scr, *, tn, nj):
    j = pl.program_id(1)
    y = jnp.dot(mg_ref[...], wo_ref[...], preferred_element_type=F32)
    pre_scr[j] = DEEPNORM_ALPHA * x_ref[...] + y

    @pl.when(j == nj - 1)
    def _():
        d = nj * tn
        tot = pre_scr[0].sum(axis=1, keepdims=True)
        for jj in range(1, nj):
            tot = tot + pre_scr[jj].sum(axis=1, keepdims=True)
        mu = tot / d
        sq = jnp.square(pre_scr[0] - mu).sum(axis=1, keepdims=True)
        for jj in range(1, nj):
            sq = sq + jnp.square(pre_scr[jj] - mu).sum(axis=1, keepdims=True)
        rstd = lax.rsqrt(sq / d + LN_EPS)
        logits = jnp.zeros(route_ref.shape, F32)
        for jj in range(nj):
            cs = slice(jj * tn, (jj + 1) * tn)
            hn = (pre_scr[jj] - mu) * rstd * g_ref[:, cs] + b_ref[:, cs]
            h_ref[:, cs] = hn
            h_hi, h_lo = _split_bf16(hn)
            w_hi, w_lo = _split_bf16(wr_ref[cs, :])
            logits = logits + (jnp.dot(h_hi, w_hi, preferred_element_type=F32)
                               + jnp.dot(h_hi, w_lo, preferred_element_type=F32)
                               + jnp.dot(h_lo, w_hi, preferred_element_type=F32))
        logits = logits + br_ref[...]
        lane = lax.broadcasted_iota(jnp.int32, logits.shape, 1)
        big = jnp.int32(4 * LANES)
        gl = jnp.where(lane < N_GROUPS, logits, -jnp.inf)
        gmax = jnp.max(gl, axis=1, keepdims=True)
        gsel = jnp.min(jnp.where(gl == gmax, lane, big), axis=1, keepdims=True)
        ggate = 1.0 / jnp.sum(jnp.exp(gl - gmax), axis=1, keepdims=True)
        eid = lane - N_GROUPS
        ingrp = (eid >= gsel * EXPERTS_PER_GROUP) & (eid < (gsel + 1) * EXPERTS_PER_GROUP)
        el = jnp.where(ingrp, logits, -jnp.inf)
        v1 = jnp.max(el, axis=1, keepdims=True)
        i1 = jnp.min(jnp.where(el == v1, lane, big), axis=1, keepdims=True)
        el2 = jnp.where(lane == i1, -jnp.inf, el)
        v2 = jnp.max(el2, axis=1, keepdims=True)
        i2 = jnp.min(jnp.where(el2 == v2, lane, big), axis=1, keepdims=True)
        t = jnp.exp(v2 - v1)
        g1 = ggate / (1.0 + t)
        g2 = g1 * t
        e1 = (i1 - N_GROUPS).astype(F32)
        e2 = (i2 - N_GROUPS).astype(F32)
        route_ref[...] = jnp.where(lane == 0, g1, jnp.where(lane == 1, g2,
                                   jnp.where(lane == 2, e1, jnp.where(lane == 3, e2, 0.0))))
        oh_ref[...] = jnp.where(lane == i1 - N_GROUPS, 1.0,
                                jnp.where(lane == i2 - N_GROUPS, 1.0, 0.0)).astype(oh_ref.dtype)


def _out_ln_router(mg, wo, x, g, b, wr, br, tm, tn):
    m, d = x.shape
    tm, tn = min(tm, m), min(tn, d)
    nj = d // tn
    return pl.pallas_call(
        functools.partial(_out_kernel, tn=tn, nj=nj),
        grid=(m // tm, nj),
        in_specs=[pl.BlockSpec((tm, d), lambda i, j: (i, 0)),
                  pl.BlockSpec((d, tn), lambda i, j: (0, j)),
                  pl.BlockSpec((tm, tn), lambda i, j: (i, j)),
                  pl.BlockSpec((1, d), lambda i, j: (0, 0)),
                  pl.BlockSpec((1, d), lambda i, j: (0, 0)),
                  pl.BlockSpec((d, LANES), lambda i, j: (0, 0)),
                  pl.BlockSpec((1, LANES), lambda i, j: (0, 0))],
        out_specs=[pl.BlockSpec((tm, d), lambda i, j: (i, 0)),
                   pl.BlockSpec((tm, LANES), lambda i, j: (i, 0)),
                   pl.BlockSpec((tm, LANES), lambda i, j: (i, 0))],
        out_shape=[jax.ShapeDtypeStruct((m, d), F32),
                   jax.ShapeDtypeStruct((m, LANES), F32),
                   jax.ShapeDtypeStruct((m, LANES), BF16)],
        scratch_shapes=[pltpu.VMEM((nj, tm, tn), F32)],
        compiler_params=_cparams(("parallel", "arbitrary"), 56),
        name="outln",
    )(mg, wo, x, g, b, wr, br)


def _rank_kernel(oh_ref, pos_ref, cnt_ref, base_scr, *, tb):
    @pl.when(pl.program_id(0) == 0)
    def _():
        base_scr[...] = jnp.zeros(base_scr.shape, F32)

    oh = oh_ref[...]
    r = lax.broadcasted_iota(jnp.int32, (tb, tb), 0)
    c = lax.broadcasted_iota(jnp.int32, (tb, tb), 1)
    tri = jnp.where(c <= r, 1.0, 0.0).astype(BF16)
    cs = jnp.dot(tri, oh, preferred_element_type=F32)
    pos_ref[...] = cs - oh.astype(F32) + base_scr[0:1, :]
    base_scr[...] = base_scr[...] + cs[tb - 1:tb, :]
    cnt_ref[...] = base_scr[...]


def _rank(onehot, tb):
    m = onehot.shape[0]
    tb = min(tb, m)
    return pl.pallas_call(
        functools.partial(_rank_kernel, tb=tb),
        grid=(m // tb,),
        in_specs=[pl.BlockSpec((tb, LANES), lambda i: (i, 0))],
        out_specs=[pl.BlockSpec((tb, LANES), lambda i: (i, 0)),
                   pl.BlockSpec((8, LANES), lambda i: (0, 0))],
        out_shape=[jax.ShapeDtypeStruct((m, LANES), F32), jax.ShapeDtypeStruct((8, LANES), F32)],
        scratch_shapes=[pltpu.VMEM((8, LANES), F32)],
        compiler_params=_cparams(("arbitrary",), 32),
        name="rank",
    )(onehot)


def _dest_kernel(pos_ref, route_ref, start_ref, dest_ref):
    lane = lax.broadcasted_iota(jnp.int32, pos_ref.shape, 1).astype(F32)
    v = pos_ref[...] + start_ref[...]
    d1 = jnp.sum(jnp.where(lane == route_ref[:, 2:3], v, 0.0), axis=1, keepdims=True)
    d2 = jnp.sum(jnp.where(lane == route_ref[:, 3:4], v, 0.0), axis=1, keepdims=True)
    dest_ref[...] = jnp.where(lane == 0.0, d1, jnp.where(lane == 1.0, d2, 0.0)).astype(jnp.int32)


def _dest(pos, route, start, tb):
    m = pos.shape[0]
    tb = min(tb, m)
    return pl.pallas_call(
        _dest_kernel,
        grid=(m // tb,),
        in_specs=[pl.BlockSpec((tb, LANES), lambda i: (i, 0)),
                  pl.BlockSpec((tb, LANES), lambda i: (i, 0)),
                  pl.BlockSpec((1, LANES), lambda i: (0, 0))],
        out_specs=pl.BlockSpec((tb, LANES), lambda i: (i, 0)),
        out_shape=jax.ShapeDtypeStruct((m, LANES), jnp.int32),
        compiler_params=_cparams(("parallel",), 32),
        name="dest",
    )(pos, route, start)


def _scatter_kernel(dest_ref, h_ref, xs_in_ref, xs_ref, sem, *, tb):
    del xs_in_ref
    base = pl.program_id(0) * tb

    def row_copy(t, slot):
        d = dest_ref[(base + t) * 2 + slot]
        return pltpu.make_async_copy(h_ref.at[pl.ds(t, 1), :], xs_ref.at[pl.ds(d, 1), :], sem)

    def start(t, carry):
        row_copy(t, 0).start()
        row_copy(t, 1).start()
        return carry

    def wait(t, carry):
        row_copy(t, 0).wait()
        row_copy(t, 1).wait()
        return carry

    lax.fori_loop(0, tb, start, 0)
    lax.fori_loop(0, tb, wait, 0)


def _scatter_rows(dest_flat, h, n_rows, tb):
    m, d = h.shape
    tb = min(tb, m)
    xs0 = jnp.zeros((n_rows, d), F32)
    return pl.pallas_call(
        functools.partial(_scatter_kernel, tb=tb),
        grid_spec=pltpu.PrefetchScalarGridSpec(
            num_scalar_prefetch=1,
            grid=(m // tb,),
            in_specs=[pl.BlockSpec((tb, d), lambda i, dest: (i, 0)),
                      pl.BlockSpec(memory_space=pl.ANY)],
            out_specs=pl.BlockSpec(memory_space=pl.ANY),
            scratch_shapes=[pltpu.SemaphoreType.DMA(())]),
        out_shape=jax.ShapeDtypeStruct((n_rows, d), F32),
        input_output_aliases={2: 0},
        compiler_params=_cparams(("arbitrary",), 32),
        name="scatter",
    )(dest_flat, h, xs0)


def _gm1_kernel(te_ref, nv_ref, xs_ref, w1_ref, w3_ref, o_ref, w1b_scr, w3b_scr):
    r = pl.program_id(1)
    fresh = jnp.logical_or(r == 0, te_ref[r] != te_ref[jnp.maximum(r - 1, 0)])

    @pl.when(jnp.logical_and(r < nv_ref[0], fresh))
    def _():
        w1b_scr[...] = w1_ref[0].astype(BF16)
        w3b_scr[...] = w3_ref[0].astype(BF16)

    @pl.when(r < nv_ref[0])
    def _():
        x = xs_ref[...].astype(BF16)
        a = jnp.dot(x, w1b_scr[...], preferred_element_type=F32)
        b = jnp.dot(x, w3b_scr[...], preferred_element_type=F32)
        o_ref[...] = (a * jax.nn.sigmoid(a) * b).astype(o_ref.dtype)

    @pl.when(r >= nv_ref[0])
    def _():
        o_ref[...] = jnp.zeros(o_ref.shape, o_ref.dtype)


def _gm1(te, nv, xs, w1, w3, tf):
    n_rows, d = xs.shape
    f = w1.shape[2]
    nt = n_rows // ROW_TILE

    def rc(r, nv_):
        return jnp.minimum(r, nv_[0] - 1)

    return pl.pallas_call(
        _gm1_kernel,
        grid_spec=pltpu.PrefetchScalarGridSpec(
            num_scalar_prefetch=2,
            grid=(f // tf, nt),
            in_specs=[pl.BlockSpec((ROW_TILE, d), lambda c, r, te_, nv_: (rc(r, nv_), 0)),
                      pl.BlockSpec((1, d, tf), lambda c, r, te_, nv_: (te_[rc(r, nv_)], 0, c)),
                      pl.BlockSpec((1, d, tf), lambda c, r, te_, nv_: (te_[rc(r, nv_)], 0, c))],
            out_specs=pl.BlockSpec((ROW_TILE, tf), lambda c, r, te_, nv_: (r, c)),
            scratch_shapes=[pltpu.VMEM((d, tf), BF16), pltpu.VMEM((d, tf), BF16)]),
        out_shape=jax.ShapeDtypeStruct((n_rows, f), BF16),
        compiler_params=_cparams(("arbitrary", "arbitrary"), 56),
        name="gm1",
    )(te, nv, xs, w1, w3)


def _gm2_kernel(te_ref, nv_ref, h_ref, w2_ref, o_ref, w2b_scr):
    r = pl.program_id(1)
    fresh = jnp.logical_or(r == 0, te_ref[r] != te_ref[jnp.maximum(r - 1, 0)])

    @pl.when(jnp.logical_and(r < nv_ref[0], fresh))
    def _():
        w2b_scr[...] = w2_ref[0].astype(BF16)

    @pl.when(r < nv_ref[0])
    def _():
        o_ref[...] = jnp.dot(h_ref[...], w2b_scr[...], preferred_element_type=F32)

    @pl.when(r >= nv_ref[0])
    def _():
        o_ref[...] = jnp.zeros(o_ref.shape, o_ref.dtype)


def _gm2(te, nv, hid, w2, tn):
    n_rows, f = hid.shape
    d = w2.shape[2]
    tn = min(tn, d)
    nt = n_rows // ROW_TILE

    def rc(r, nv_):
        return jnp.minimum(r, nv_[0] - 1)

    return pl.pallas_call(
        _gm2_kernel,
        grid_spec=pltpu.PrefetchScalarGridSpec(
            num_scalar_prefetch=2,
            grid=(d // tn, nt),
            in_specs=[pl.BlockSpec((ROW_TILE, f), lambda c, r, te_, nv_: (rc(r, nv_), 0)),
                      pl.BlockSpec((1, f, tn), lambda c, r, te_, nv_: (te_[rc(r, nv_)], 0, c))],
            out_specs=pl.BlockSpec((ROW_TILE, tn), lambda c, r, te_, nv_: (r, c)),
            scratch_shapes=[pltpu.VMEM((f, tn), BF16)]),
        out_shape=jax.ShapeDtypeStruct((n_rows, d), F32),
        compiler_params=_cparams(("arbitrary", "arbitrary"), 48),
        name="gm2",
    )(te, nv, hid, w2)


def _combine_kernel(dest_ref, y_ref, h_ref, route_ref, g_ref, b_ref, o_ref, ybuf, sem, *, tb):
    base = pl.program_id(0) * tb

    def row_copy(t, slot):
        d = dest_ref[(base + t) * 2 + slot]
        return pltpu.make_async_copy(y_ref.at[pl.ds(d, 1), :], ybuf.at[slot, pl.ds(t, 1), :], sem)

    def start(t, carry):
        row_copy(t, 0).start()
        row_copy(t, 1).start()
        return carry

    def wait(t, carry):
        row_copy(t, 0).wait()
        row_copy(t, 1).wait()
        return carry

    lax.fori_loop(0, tb, start, 0)
    lax.fori_loop(0, tb, wait, 0)
    moe = route_ref[:, 0:1] * ybuf[0] + route_ref[:, 1:2] * ybuf[1]
    pre = DEEPNORM_ALPHA * h_ref[...] + moe
    mu = jnp.mean(pre, axis=1, keepdims=True)
    var = jnp.mean(jnp.square(pre - mu), axis=1, keepdims=True)
    o_ref[...] = (pre - mu) * lax.rsqrt(var + LN_EPS) * g_ref[...] + b_ref[...]


def _combine_ln(dest_flat, y, h, route, g, b, tb):
    m, d = h.shape
    tb = min(tb, m)
    return pl.pallas_call(
        functools.partial(_combine_kernel, tb=tb),
        grid_spec=pltpu.PrefetchScalarGridSpec(
            num_scalar_prefetch=1,
            grid=(m // tb,),
            in_specs=[pl.BlockSpec(memory_space=pl.ANY),
                      pl.BlockSpec((tb, d), lambda i, dest: (i, 0)),
                      pl.BlockSpec((tb, LANES), lambda i, dest: (i, 0)),
                      pl.BlockSpec((1, d), lambda i, dest: (0, 0)),
                      pl.BlockSpec((1, d), lambda i, dest: (0, 0))],
            out_specs=pl.BlockSpec((tb, d), lambda i, dest: (i, 0)),
            scratch_shapes=[pltpu.VMEM((2, tb, d), F32), pltpu.SemaphoreType.DMA(())]),
        out_shape=jax.ShapeDtypeStruct((m, d), F32),
        compiler_params=_cparams(("arbitrary",), 40),
        name="combine",
    )(dest_flat, y, h, route, g, b)


def _alibi_slopes(n):
    return jnp.asarray(2.0 ** (-8.0 * np.arange(1, n + 1) / n), dtype=F32)


def kernel(x, w_in, lam_q1, lam_k1, lam_q2, lam_k2, diff_subln_g, w_pa, w_pb, w_o, ln1_g, ln1_b,
           router_wg, router_bg, router_we, router_be, w1, w3, w2, ln2_g, ln2_b):
    bsz, s_len, d = x.shape
    assert bsz == 1 and w_in.shape[0] == DEPTH
    topk = min(TOPK_MAX, s_len // 4)
    x2 = x[0]
    xb = x2.astype(BF16)

    qk_w = N_DIFF_HEADS * 2 * HEAD_DIM
    c_dq, c_dk, c_dv = 0, qk_w, 2 * qk_w
    c_sq = 3 * qk_w
    c_sk = c_sq + N_DSA_HEADS * HEAD_DIM
    c_sv = c_sk + N_DSA_KV * HEAD_DIM
    c_iq = c_sv + N_DSA_KV * HEAD_DIM
    c_ik = c_iq + N_IDX_HEADS * IDX_DIM
    c_ga = c_ik + IDX_DIM + N_IDX_HEADS
    qscale = HEAD_DIM ** -0.5 * LOG2E
    col = np.arange(c_ik)
    is_q = ((col >= c_dq) & (col < c_dk)) | ((col >= c_sq) & (col < c_sk))
    col_scale = jnp.asarray(np.where(is_q, qscale, 1.0)[None, :], F32)
    w = w_in[0]
    w_small = jnp.concatenate([w[:, c_ik:c_ga], jnp.zeros((d, 2 * LANES - IDX_DIM - N_IDX_HEADS), F32)],
                              axis=1).astype(BF16)
    w_gates = w[:, c_ga:c_ga + 2 * d].astype(BF16)

    z = _proj_from_f32(xb, w_in, col_scale, c_ik, 1024, 512)
    zs = _matmul(xb, w_small, F32, 1024, 256, "proj_small")
    ik = zs[:, :IDX_DIM].astype(BF16)
    iwt = zs[:, IDX_DIM:IDX_DIM + N_IDX_HEADS].T
    dvt = z[:, c_dv:c_dv + qk_w].T
    svt = z[:, c_sv:c_sv + N_DSA_KV * HEAD_DIM].T
    kaug = _key_aug_table(s_len)

    lam4 = jnp.stack([lam_q1[0], lam_k1[0], lam_q2[0], lam_k2[0]]).astype(F32)
    g_lanes = jnp.broadcast_to(diff_subln_g[0][:, None], (2 * HEAD_DIM, LANES))
    a = _diff_attention(z, dvt, kaug, lam4, g_lanes, _alibi_slopes(N_DIFF_HEADS), c_dq, c_dk, s_len,
                        min(512, s_len))

    tq_i, tk_i = min(256, s_len), min(512, s_len)
    maskb = _indexer(z, ik, iwt, c_iq, s_len, tq_i, tk_i, topk)
    b = _dsa_attention(z, svt, kaug, maskb, _alibi_slopes(N_DSA_HEADS), c_sq, c_sk, s_len, tq_i, tk_i)

    merged = _merge(xb, w_gates, a, w_pa[0].astype(BF16), b, w_pb[0].astype(BF16), 512, 256)

    wr = jnp.concatenate([router_wg[0], router_we[0],
                          jnp.zeros((d, LANES - N_GROUPS - N_EXPERTS), F32)], axis=1)
    br = jnp.concatenate([router_bg[0], router_be[0],
                          jnp.zeros((LANES - N_GROUPS - N_EXPERTS,), F32)])[None, :]
    h1, route, onehot = _out_ln_router(merged, w_o[0].astype(BF16), x2, ln1_g[0][None, :], ln1_b[0][None, :],
                                       wr, br, 512, 512)

    pos, cnt = _rank(onehot, 512)
    counts = cnt[0, :N_EXPERTS].astype(jnp.int32)
    padded = ((counts + ROW_TILE - 1) // ROW_TILE) * ROW_TILE
    ends = jnp.cumsum(padded)
    start = jnp.zeros((1, LANES), F32).at[0, :N_EXPERTS].set((ends - padded).astype(F32))
    n_tiles = (2 * s_len) // ROW_TILE + N_EXPERTS
    tile_ids = jnp.arange(n_tiles, dtype=jnp.int32)
    te = jnp.minimum(jnp.sum(tile_ids[:, None] >= (ends // ROW_TILE)[None, :], axis=1), N_EXPERTS - 1)
    te = te.astype(jnp.int32)
    nv = (ends[-1] // ROW_TILE).astype(jnp.int32)[None]
    dest = _dest(pos, route, start, 512)
    dest_flat = dest[:, :2].reshape(-1)

    xs = _scatter_rows(dest_flat, h1, n_tiles * ROW_TILE, 256)
    hid = _gm1(te, nv, xs, w1[0], w3[0], min(512, w1.shape[3]))
    y = _gm2(te, nv, hid, w2[0], 1024)
    out = _combine_ln(dest_flat, y, h1, route, ln2_g[0][None, :], ln2_b[0][None, :], 128)
    return out[None]
```

```python
import functools
import math

import numpy as np
import jax
import jax.numpy as jnp
from jax import lax
from jax.experimental import pallas as pl
from jax.experimental.pallas import tpu as pltpu

HEAD_DIM = 128
N_DIFF_HEADS = 8
N_DSA_HEADS = 16
N_DSA_KV = 4
DSA_REP = N_DSA_HEADS // N_DSA_KV
N_IDX_HEADS = 32
IDX_DIM = 128
TOPK_MAX = 256
N_GROUPS = 4
EXPERTS_PER_GROUP = 8
N_EXPERTS = N_GROUPS * EXPERTS_PER_GROUP
LN_EPS = 1e-5
RMS_EPS = 1e-5
DEPTH = 1
DEEPNORM_ALPHA = (2.0 * DEPTH) ** 0.25
LAM_INIT = 0.8 - 0.6 * math.exp(-0.3 * 0)

LANES = 128
SUBLANES = 8
NEG = -1e30
INT_MIN = -(2 ** 31)
ROW_TILE = 256
LOG2E = 1.4426950408889634
POS_RADIX = 256

F32 = jnp.float32
BF16 = jnp.bfloat16


def _cparams(sem, vmem_mb):
    return pltpu.CompilerParams(dimension_semantics=sem, vmem_limit_bytes=vmem_mb << 20)


def _dot_nt(a, b):
    return lax.dot_general(a, b, (((1,), (1,)), ((), ())), preferred_element_type=F32)


def _proj_kernel(x_ref, wt_ref, sc_ref, o_ref, wb_scr):
    @pl.when(pl.program_id(1) == 0)
    def _():
        wb_scr[...] = wt_ref[0].astype(BF16)

    o_ref[...] = (_dot_nt(x_ref[...], wb_scr[...]) * sc_ref[...]).astype(o_ref.dtype)


def _proj_from_f32(xb, wt3d, col_scale, n_cols, tm, tn):
    m, d = xb.shape
    tm = min(tm, m)
    return pl.pallas_call(
        _proj_kernel,
        grid=(n_cols // tn, m // tm),
        in_specs=[pl.BlockSpec((tm, d), lambda j, i: (i, 0)),
                  pl.BlockSpec((1, tn, d), lambda j, i: (0, j, 0)),
                  pl.BlockSpec((1, tn), lambda j, i: (0, j))],
        out_specs=pl.BlockSpec((tm, tn), lambda j, i: (i, j)),
        out_shape=jax.ShapeDtypeStruct((m, n_cols), BF16),
        scratch_shapes=[pltpu.VMEM((tn, d), BF16)],
        compiler_params=_cparams(("parallel", "arbitrary"), 48),
        name="proj_main",
    )(xb, wt3d, col_scale)


def _mm_nt_kernel(a_ref, bt_ref, o_ref):
    o_ref[...] = _dot_nt(a_ref[...], bt_ref[...]).astype(o_ref.dtype)


def _matmul_nt(a, bt, out_dtype, tm, name):
    m, k = a.shape
    n = bt.shape[0]
    tm = min(tm, m)
    return pl.pallas_call(
        _mm_nt_kernel,
        grid=(m // tm,),
        in_specs=[pl.BlockSpec((tm, k), lambda i: (i, 0)),
                  pl.BlockSpec((n, k), lambda i: (0, 0))],
        out_specs=pl.BlockSpec((tm, n), lambda i: (i, 0)),
        out_shape=jax.ShapeDtypeStruct((m, n), out_dtype),
        compiler_params=_cparams(("parallel",), 48),
        name=name,
    )(a, bt)


def _key_aug_table(s_len):
    pos = jnp.arange(s_len, dtype=jnp.int32)[:, None]
    lane = jnp.arange(LANES, dtype=jnp.int32)[None, :]
    hi = (pos // POS_RADIX).astype(F32)
    lo = (pos % POS_RADIX).astype(F32)
    t = jnp.where(lane < 2, hi, jnp.where(lane < 4, lo, jnp.where(lane < 7, 1.0, 0.0)))
    return t.astype(BF16)


def _bf16_piece(x):
    return x.astype(BF16).astype(F32)


def _query_aug(slope, qbase, rows):
    s2 = jnp.full((SUBLANES, LANES), slope, F32) * LOG2E
    big = s2 * POS_RADIX
    off = -s2 * jnp.full((SUBLANES, LANES), qbase, F32)
    big_hi = _bf16_piece(big)
    s2_hi = _bf16_piece(s2)
    off_hi = _bf16_piece(off)
    off_mid = _bf16_piece(off - off_hi)
    lane = lax.broadcasted_iota(jnp.int32, (SUBLANES, LANES), 1)
    pieces = [big_hi, big - big_hi, s2_hi, s2 - s2_hi, off_hi, off_mid, off - off_hi - off_mid]
    row = jnp.zeros((SUBLANES, LANES), F32)
    for n, piece in enumerate(pieces):
        row = jnp.where(lane == n, piece, row)
    return jnp.broadcast_to(row[0:1, :], (rows, LANES)).astype(BF16)


QUERY_SUB = 256


def _causal_steps(nq, last_of):
    pairs = [(i, j) for i in range(nq) for j in range(last_of(i) + 1)]
    return (jnp.asarray([p[0] for p in pairs], jnp.int32), jnp.asarray([p[1] for p in pairs], jnp.int32))


ONES_ROWS = 16


def _with_ones_rows(vt, n_blocks):
    r = vt.shape[0] // n_blocks
    v3 = vt.reshape(n_blocks, r, vt.shape[1])
    ones = jnp.ones((n_blocks, ONES_ROWS, vt.shape[1]), vt.dtype)
    return jnp.concatenate([v3, ones], axis=1).reshape(n_blocks * (r + ONES_ROWS), vt.shape[1])


def _online_softmax_pv(s, vt1, m_scr, acc_scr, idx, cols):
    at = idx + (slice(None), cols)
    m_prev = m_scr[at]
    m_next = jnp.maximum(m_prev, jnp.max(s, axis=0, keepdims=True))
    p = jnp.exp2((s - m_next).astype(BF16))
    alpha = jnp.exp2(m_prev - m_next)
    m_scr[at] = m_next
    acc_scr[at] = acc_scr[at] * alpha + jnp.dot(vt1, p, preferred_element_type=F32)


DIFF_HEADS_PER_STEP = 2


def _diff_kernel(qi_ref, kj_ref, slopes_ref, lam_ref, g_ref, q_ref, k_ref, ka_ref, vt_ref, o_ref,
                 qa_scr, kk_scr, m_scr, acc_scr, *, tq):
    hp, step_id = pl.program_id(0), pl.program_id(1)
    i, j = qi_ref[step_id], kj_ref[step_id]
    reps = tq // LANES
    nsub = tq // QUERY_SUB
    w = 2 * HEAD_DIM
    maps = [(hh, c) for hh in range(DIFF_HEADS_PER_STEP) for c in range(2)]

    def map_cols(hh, c):
        return slice(hh * w + c * HEAD_DIM, hh * w + (c + 1) * HEAD_DIM)

    @pl.when(j == 0)
    def _():
        m_scr[...] = jnp.full(m_scr.shape, NEG, F32)
        acc_scr[...] = jnp.zeros(acc_scr.shape, F32)
        for hh in range(DIFF_HEADS_PER_STEP):
            qaug = _query_aug(slopes_ref[hp * DIFF_HEADS_PER_STEP + hh], (i * tq).astype(F32), tq)
            for c in range(2):
                qa_scr[2 * hh + c, :, :HEAD_DIM] = q_ref[:, map_cols(hh, c)]
                qa_scr[2 * hh + c, :, HEAD_DIM:] = qaug

    def step(masked):
        ka = ka_ref[...]
        for hh, c in maps:
            kk_scr[2 * hh + c, :, :HEAD_DIM] = k_ref[:, map_cols(hh, c)]
            kk_scr[2 * hh + c, :, HEAD_DIM:] = ka
        scores = [_dot_nt(kk_scr[n], qa_scr[n]) for n in range(len(maps))]
        for n, (hh, c) in enumerate(maps):
            vt1 = vt_ref[hh * (w + ONES_ROWS):(hh + 1) * (w + ONES_ROWS), :]
            for u in range(nsub):
                cols = slice(u * QUERY_SUB, (u + 1) * QUERY_SUB)
                s = scores[n][:, cols]
                if masked:
                    keep = (lax.broadcasted_iota(jnp.int32, (tq, 1), 0)
                            <= u * QUERY_SUB + lax.broadcasted_iota(jnp.int32, (1, QUERY_SUB), 1))
                    s = jnp.where(keep, s, NEG)
                _online_softmax_pv(s, vt1, m_scr, acc_scr, (n,), cols)

    @pl.when(j < i)
    def _():
        step(False)

    @pl.when(j == i)
    def _():
        step(True)
        lam = (jnp.exp(jnp.sum(lam_ref[0:1, :] * lam_ref[1:2, :], axis=1, keepdims=True))
               - jnp.exp(jnp.sum(lam_ref[2:3, :] * lam_ref[3:4, :], axis=1, keepdims=True)) + LAM_INIT)
        g = jnp.concatenate([g_ref[...]] * reps, axis=1)
        for hh in range(DIFF_HEADS_PER_STEP):
            n = 2 * hh
            o = (acc_scr[n, :w, :] / acc_scr[n, w:w + 1, :]
                 - lam * (acc_scr[n + 1, :w, :] / acc_scr[n + 1, w:w + 1, :]))
            o = o * lax.rsqrt(jnp.mean(o * o, axis=0, keepdims=True) + RMS_EPS) * g
            o_ref[:, hh * w:(hh + 1) * w] = (o * (1.0 - LAM_INIT)).T.astype(o_ref.dtype)


def _diff_attention(z, vt, kaug, lam4, g_lanes, slopes, col_q, col_k, s_len, tq):
    nq = s_len // tq
    w = 2 * HEAD_DIM
    wb = DIFF_HEADS_PER_STEP * w
    nmaps = 2 * DIFF_HEADS_PER_STEP
    qb, kb = col_q // wb, col_k // wb
    qi, kj = _causal_steps(nq, lambda i: i)
    return pl.pallas_call(
        functools.partial(_diff_kernel, tq=tq),
        grid_spec=pltpu.PrefetchScalarGridSpec(
            num_scalar_prefetch=3,
            grid=(N_DIFF_HEADS // DIFF_HEADS_PER_STEP, qi.shape[0]),
            in_specs=[pl.BlockSpec((4, HEAD_DIM), lambda h, s, qi_, kj_, sl_: (0, 0)),
                      pl.BlockSpec((w, LANES), lambda h, s, qi_, kj_, sl_: (0, 0)),
                      pl.BlockSpec((tq, wb), lambda h, s, qi_, kj_, sl_: (qi_[s], qb + h)),
                      pl.BlockSpec((tq, wb), lambda h, s, qi_, kj_, sl_: (kj_[s], kb + h)),
                      pl.BlockSpec((tq, LANES), lambda h, s, qi_, kj_, sl_: (kj_[s], 0)),
                      pl.BlockSpec((DIFF_HEADS_PER_STEP * (w + ONES_ROWS), tq),
                                   lambda h, s, qi_, kj_, sl_: (h, kj_[s]))],
            out_specs=pl.BlockSpec((tq, wb), lambda h, s, qi_, kj_, sl_: (qi_[s], h)),
            scratch_shapes=[pltpu.VMEM((nmaps, tq, w), BF16), pltpu.VMEM((nmaps, tq, w), BF16),
                            pltpu.VMEM((nmaps, 1, tq), F32),
                            pltpu.VMEM((nmaps, w + ONES_ROWS, tq), F32)]),
        out_shape=jax.ShapeDtypeStruct((s_len, N_DIFF_HEADS * w), BF16),
        compiler_params=_cparams(("parallel", "arbitrary"), 40),
        name="diffattn",
    )(qi, kj, slopes, lam4, g_lanes, z, z, kaug, vt)


IQ_SPLIT = 4


def _indexer_kernel(iq0_ref, iq1_ref, iq2_ref, iq3_ref, ik_ref, iwt_ref, o_ref,
                    iqh_scr, key_scr, acc_scr, pcut_scr, *, tq, tkc, topk, iw_scale, pos_bits):
    i = pl.program_id(0)
    nk = key_scr.shape[0]
    nchunks = (i * tq + tq - 1) // tkc + 1
    per = N_IDX_HEADS // IQ_SPLIT

    for h in range(N_IDX_HEADS):
        src = (iq0_ref, iq1_ref, iq2_ref, iq3_ref)[h // per]
        iqh_scr[h] = src[:, (h % per) * IDX_DIM:(h % per + 1) * IDX_DIM]

    qpos = i * tq + lax.broadcasted_iota(jnp.int32, (1, tq), 1)

    def key_pos(c):
        return c * tkc + lax.broadcasted_iota(jnp.int32, (tkc, 1), 0)

    def count_keys(pred):
        def body(c, acc):
            x = jnp.where(pred(c, key_scr[c]), 1, 0)
            return acc + jnp.sum(x.reshape(tkc // SUBLANES, SUBLANES, tq), axis=0)

        cnt = lax.fori_loop(0, nchunks, body, jnp.zeros((SUBLANES, tq), jnp.int32))
        return jnp.sum(cnt, axis=0, keepdims=True)

    def chunk(c, carry):
        kc = ik_ref[pl.ds(pl.multiple_of(c * tkc, tkc), tkc), :]
        acc_scr[...] = jnp.zeros(acc_scr.shape, F32)

        def head(h, carry2):
            sc = _dot_nt(kc, iqh_scr[h])
            acc_scr[...] += (iwt_ref[pl.ds(h, 1), :] * iw_scale) * jnp.maximum(sc, 0.0)
            return carry2

        lax.fori_loop(0, N_IDX_HEADS, head, 0, unroll=16)
        score = acc_scr[...]
        score = jnp.where(score == 0.0, 0.0, score)
        bits = pltpu.bitcast(score, jnp.int32)
        skey = bits ^ ((bits >> 31) & 0x7FFFFFFF)
        key_scr[c] = jnp.where(key_pos(c) <= qpos, skey, INT_MIN)
        return carry

    lax.fori_loop(0, nchunks, chunk, 0)

    def bit_step(bi, thr):
        cand = thr ^ lax.shift_left(jnp.int32(1), 31 - bi)
        return jnp.where(count_keys(lambda c, k: k >= cand) >= topk, cand, thr)

    thr = lax.fori_loop(0, 32, bit_step, jnp.full((1, tq), INT_MIN, jnp.int32))
    thr = jnp.maximum(thr, INT_MIN + 1)

    pcut_scr[...] = jnp.full(pcut_scr.shape, (1 << pos_bits) - 1, jnp.int32)

    @pl.when(jnp.max(count_keys(lambda c, k: k >= thr)) > topk)
    def _():
        need = topk - count_keys(lambda c, k: k > thr)

        def pos_step(bi, p):
            cand = p | lax.shift_left(jnp.int32(1), pos_bits - 1 - bi)
            tied_before = count_keys(lambda c, k: jnp.where(k == thr, key_pos(c), cand) < cand)
            return jnp.where(tied_before < need, cand, p)

        pcut_scr[...] = lax.fori_loop(0, pos_bits, pos_step, jnp.zeros((1, tq), jnp.int32))

    pcut = pcut_scr[...]

    def emit(c, carry):
        k = key_scr[c]
        tie_bias = jnp.where(key_pos(c) <= pcut, 0.0, NEG)
        o_ref[0, c] = jnp.where(k > thr, 0.0, jnp.where(k == thr, tie_bias, NEG)).astype(o_ref.dtype)
        return carry

    lax.fori_loop(0, nchunks, emit, 0)

    def fill(c, carry):
        o_ref[0, c] = jnp.full((tkc, tq), NEG, o_ref.dtype)
        return carry

    lax.fori_loop(nchunks, nk, fill, 0)


def _indexer(z, ik, iwt, col_iq, s_len, tq, tkc, topk):
    nq, nk = s_len // tq, s_len // tkc
    iw_scale = N_IDX_HEADS ** -0.5 * IDX_DIM ** -0.5
    wq = N_IDX_HEADS * IDX_DIM // IQ_SPLIT
    qb = col_iq // wq
    iq_specs = [pl.BlockSpec((tq, wq), functools.partial(lambda i, n: (i, qb + n), n=n)) for n in range(IQ_SPLIT)]
    return pl.pallas_call(
        functools.partial(_indexer_kernel, tq=tq, tkc=tkc, topk=topk, iw_scale=iw_scale,
                          pos_bits=max(1, (s_len - 1).bit_length())),
        grid=(nq,),
        in_specs=iq_specs + [pl.BlockSpec((s_len, IDX_DIM), lambda i: (0, 0)),
                             pl.BlockSpec((N_IDX_HEADS, tq), lambda i: (0, i))],
        out_specs=pl.BlockSpec((1, nk, tkc, tq), lambda i: (i, 0, 0, 0)),
        out_shape=jax.ShapeDtypeStruct((nq, nk, tkc, tq), BF16),
        scratch_shapes=[pltpu.VMEM((N_IDX_HEADS, tq, IDX_DIM), BF16),
                        pltpu.VMEM((nk, tkc, tq), jnp.int32),
                        pltpu.VMEM((tkc, tq), F32),
                        pltpu.VMEM((1, tq), jnp.int32)],
        compiler_params=_cparams(("parallel",), 48),
        name="indexer",
    )(z, z, z, z, ik, iwt)


DSA_GROUPS_PER_STEP = 2


def _dsa_kernel(qi_ref, kj_ref, slopes_ref, q_ref, k_ref, ka_ref, vt_ref, mb_ref, o_ref,
                qa_scr, kk_scr, m_scr, acc_scr, *, tq, tk):
    gp, step_id = pl.program_id(0), pl.program_id(1)
    i, j = qi_ref[step_id], kj_ref[step_id]
    last = (i * tq + tq - 1) // tk
    heads = DSA_GROUPS_PER_STEP * DSA_REP

    @pl.when(j == 0)
    def _():
        m_scr[...] = jnp.full(m_scr.shape, NEG, F32)
        acc_scr[...] = jnp.zeros(acc_scr.shape, F32)
        for n in range(heads):
            rows = slice(n * tq, (n + 1) * tq)
            qa_scr[rows, :HEAD_DIM] = q_ref[:, n * HEAD_DIM:(n + 1) * HEAD_DIM]
            qa_scr[rows, HEAD_DIM:] = _query_aug(slopes_ref[gp * heads + n], (i * tq).astype(F32), tq)

    ka = ka_ref[...]
    for gg in range(DSA_GROUPS_PER_STEP):
        kk_scr[gg, :, :HEAD_DIM] = k_ref[:, gg * HEAD_DIM:(gg + 1) * HEAD_DIM]
        kk_scr[gg, :, HEAD_DIM:] = ka
    mb = mb_ref[0, 0].astype(F32)
    scores = [_dot_nt(kk_scr[gg], qa_scr[gg * DSA_REP * tq:(gg + 1) * DSA_REP * tq, :])
              for gg in range(DSA_GROUPS_PER_STEP)]
    for gg in range(DSA_GROUPS_PER_STEP):
        vt1 = vt_ref[gg * (HEAD_DIM + ONES_ROWS):(gg + 1) * (HEAD_DIM + ONES_ROWS), :]
        for r in range(DSA_REP):
            n = gg * DSA_REP + r
            _online_softmax_pv(scores[gg][:, r * tq:(r + 1) * tq] + mb, vt1, m_scr, acc_scr, (),
                               slice(n * tq, (n + 1) * tq))

    @pl.when(j == last)
    def _():
        o = acc_scr[:HEAD_DIM, :] / acc_scr[HEAD_DIM:HEAD_DIM + 1, :]
        for n in range(heads):
            o_ref[:, n * HEAD_DIM:(n + 1) * HEAD_DIM] = o[:, n * tq:(n + 1) * tq].T.astype(o_ref.dtype)


def _dsa_attention(z, vt, kaug, maskb, slopes, col_q, col_k, s_len, tq, tk):
    nq = s_len // tq
    gps = DSA_GROUPS_PER_STEP
    heads = gps * DSA_REP
    qw, kw = heads * HEAD_DIM, gps * HEAD_DIM
    qb, kb = col_q // qw, col_k // kw
    qi, kj = _causal_steps(nq, lambda i: (i * tq + tq - 1) // tk)
    return pl.pallas_call(
        functools.partial(_dsa_kernel, tq=tq, tk=tk),
        grid_spec=pltpu.PrefetchScalarGridSpec(
            num_scalar_prefetch=3,
            grid=(N_DSA_KV // gps, qi.shape[0]),
            in_specs=[pl.BlockSpec((tq, qw), lambda g, s, qi_, kj_, sl_: (qi_[s], qb + g)),
                      pl.BlockSpec((tk, kw), lambda g, s, qi_, kj_, sl_: (kj_[s], kb + g)),
                      pl.BlockSpec((tk, LANES), lambda g, s, qi_, kj_, sl_: (kj_[s], 0)),
                      pl.BlockSpec((gps * (HEAD_DIM + ONES_ROWS), tk), lambda g, s, qi_, kj_, sl_: (g, kj_[s])),
                      pl.BlockSpec((1, 1, tk, tq), lambda g, s, qi_, kj_, sl_: (qi_[s], kj_[s], 0, 0))],
            out_specs=pl.BlockSpec((tq, qw), lambda g, s, qi_, kj_, sl_: (qi_[s], g)),
            scratch_shapes=[pltpu.VMEM((heads * tq, 2 * HEAD_DIM), BF16),
                            pltpu.VMEM((gps, tk, 2 * HEAD_DIM), BF16),
                            pltpu.VMEM((1, heads * tq), F32),
                            pltpu.VMEM((HEAD_DIM + ONES_ROWS, heads * tq), F32)]),
        out_shape=jax.ShapeDtypeStruct((s_len, N_DSA_HEADS * HEAD_DIM), BF16),
        compiler_params=_cparams(("parallel", "arbitrary"), 40),
        name="dsa",
    )(qi, kj, slopes, z, z, kaug, vt, maskb)


def _merge_kernel(x_ref, wga_ref, wgb_ref, a_ref, wpa_ref, b_ref, wpb_ref, o_ref):
    x = x_ref[...]
    ga = _dot_nt(x, wga_ref[...])
    gb = _dot_nt(x, wgb_ref[...])
    pa = jnp.dot(a_ref[...], wpa_ref[...], preferred_element_type=F32)
    pb = jnp.dot(b_ref[...], wpb_ref[...], preferred_element_type=F32)
    o_ref[...] = (jax.nn.sigmoid(ga) * pa + jax.nn.sigmoid(gb) * pb).astype(o_ref.dtype)


def _merge(xb, wt_gates, a, wpa, b, wpb, tm, tn):
    m, d = xb.shape
    ka, kb = a.shape[1], b.shape[1]
    tm, tn = min(tm, m), min(tn, d)
    row = lambda i, j: (i, 0)
    col = lambda i, j: (0, j)
    return pl.pallas_call(
        _merge_kernel,
        grid=(m // tm, d // tn),
        in_specs=[pl.BlockSpec((tm, d), row),
                  pl.BlockSpec((tn, d), lambda i, j: (j, 0)), pl.BlockSpec((tn, d), lambda i, j: (d // tn + j, 0)),
                  pl.BlockSpec((tm, ka), row), pl.BlockSpec((ka, tn), col),
                  pl.BlockSpec((tm, kb), row), pl.BlockSpec((kb, tn), col)],
        out_specs=pl.BlockSpec((tm, tn), lambda i, j: (i, j)),
        out_shape=jax.ShapeDtypeStruct((m, d), BF16),
        compiler_params=_cparams(("parallel", "arbitrary"), 52),
        name="merge",
    )(xb, wt_gates, wt_gates, a, wpa, b, wpb)


def _split_bf16(x):
    hi = x.astype(BF16)
    lo = (x - hi.astype(F32)).astype(BF16)
    return hi, lo


def _out_kernel(mg_ref, wo_ref, x_ref, g_ref, b_ref, wr_ref, br_ref,
                h_ref, route_ref, oh_ref, pre_scr, *, tn, nj):
    j = pl.program_id(1)
    y = jnp.dot(mg_ref[...], wo_ref[...], preferred_element_type=F32)
    pre_scr[j] = DEEPNORM_ALPHA * x_ref[...] + y

    @pl.when(j == nj - 1)
    def _():
        d = nj * tn
        tot = pre_scr[0].sum(axis=1, keepdims=True)
        for jj in range(1, nj):
            tot = tot + pre_scr[jj].sum(axis=1, keepdims=True)
        mu = tot / d
        sq = jnp.square(pre_scr[0] - mu).sum(axis=1, keepdims=True)
        for jj in range(1, nj):
            sq = sq + jnp.square(pre_scr[jj] - mu).sum(axis=1, keepdims=True)
        rstd = lax.rsqrt(sq / d + LN_EPS)
        logits = jnp.zeros(route_ref.shape, F32)
        for jj in range(nj):
            cs = slice(jj * tn, (jj + 1) * tn)
            hn = (pre_scr[jj] - mu) * rstd * g_ref[:, cs] + b_ref[:, cs]
            h_ref[:, cs] = hn
            h_hi, h_lo = _split_bf16(hn)
            w_hi, w_lo = _split_bf16(wr_ref[cs, :])
            logits = logits + (jnp.dot(h_hi, w_hi, preferred_element_type=F32)
                               + jnp.dot(h_hi, w_lo, preferred_element_type=F32)
                               + jnp.dot(h_lo, w_hi, preferred_element_type=F32))
        logits = logits + br_ref[...]
        lane = lax.broadcasted_iota(jnp.int32, logits.shape, 1)
        big = jnp.int32(4 * LANES)
        gl = jnp.where(lane < N_GROUPS, logits, -jnp.inf)
        gmax = jnp.max(gl, axis=1, keepdims=True)
        gsel = jnp.min(jnp.where(gl == gmax, lane, big), axis=1, keepdims=True)
        ggate = 1.0 / jnp.sum(jnp.exp(gl - gmax), axis=1, keepdims=True)
        eid = lane - N_GROUPS
        ingrp = (eid >= gsel * EXPERTS_PER_GROUP) & (eid < (gsel + 1) * EXPERTS_PER_GROUP)
        el = jnp.where(ingrp, logits, -jnp.inf)
        v1 = jnp.max(el, axis=1, keepdims=True)
        i1 = jnp.min(jnp.where(el == v1, lane, big), axis=1, keepdims=True)
        el2 = jnp.where(lane == i1, -jnp.inf, el)
        v2 = jnp.max(el2, axis=1, keepdims=True)
        i2 = jnp.min(jnp.where(el2 == v2, lane, big), axis=1, keepdims=True)
        t = jnp.exp(v2 - v1)
        g1 = ggate / (1.0 + t)
        g2 = g1 * t
        e1 = (i1 - N_GROUPS).astype(F32)
        e2 = (i2 - N_GROUPS).astype(F32)
        route_ref[...] = jnp.where(lane == 0, g1, jnp.where(lane == 1, g2,
                                   jnp.where(lane == 2, e1, jnp.where(lane == 3, e2, 0.0))))
        oh_ref[...] = jnp.where(lane == i1 - N_GROUPS, 1.0,
                                jnp.where(lane == i2 - N_GROUPS, 1.0, 0.0)).astype(oh_ref.dtype)


def _out_ln_router(mg, wo, x, g, b, wr, br, tm, tn):
    m, d = x.shape
    tm, tn = min(tm, m), min(tn, d)
    nj = d // tn
    return pl.pallas_call(
        functools.partial(_out_kernel, tn=tn, nj=nj),
        grid=(m // tm, nj),
        in_specs=[pl.BlockSpec((tm, d), lambda i, j: (i, 0)),
                  pl.BlockSpec((d, tn), lambda i, j: (0, j)),
                  pl.BlockSpec((tm, tn), lambda i, j: (i, j)),
                  pl.BlockSpec((1, d), lambda i, j: (0, 0)),
                  pl.BlockSpec((1, d), lambda i, j: (0, 0)),
                  pl.BlockSpec((d, LANES), lambda i, j: (0, 0)),
                  pl.BlockSpec((1, LANES), lambda i, j: (0, 0))],
        out_specs=[pl.BlockSpec((tm, d), lambda i, j: (i, 0)),
                   pl.BlockSpec((tm, LANES), lambda i, j: (i, 0)),
                   pl.BlockSpec((tm, LANES), lambda i, j: (i, 0))],
        out_shape=[jax.ShapeDtypeStruct((m, d), F32),
                   jax.ShapeDtypeStruct((m, LANES), F32),
                   jax.ShapeDtypeStruct((m, LANES), BF16)],
        scratch_shapes=[pltpu.VMEM((nj, tm, tn), F32)],
        compiler_params=_cparams(("parallel", "arbitrary"), 56),
        name="outln",
    )(mg, wo, x, g, b, wr, br)


def _rank_kernel(oh_ref, pos_ref, cnt_ref, base_scr, *, tb):
    @pl.when(pl.program_id(0) == 0)
    def _():
        base_scr[...] = jnp.zeros(base_scr.shape, F32)

    oh = oh_ref[...]
    r = lax.broadcasted_iota(jnp.int32, (tb, tb), 0)
    c = lax.broadcasted_iota(jnp.int32, (tb, tb), 1)
    tri = jnp.where(c <= r, 1.0, 0.0).astype(BF16)
    cs = jnp.dot(tri, oh, preferred_element_type=F32)
    pos_ref[...] = cs - oh.astype(F32) + base_scr[0:1, :]
    base_scr[...] = base_scr[...] + cs[tb - 1:tb, :]
    cnt_ref[...] = base_scr[...]


def _rank(onehot, tb):
    m = onehot.shape[0]
    tb = min(tb, m)
    return pl.pallas_call(
        functools.partial(_rank_kernel, tb=tb),
        grid=(m // tb,),
        in_specs=[pl.BlockSpec((tb, LANES), lambda i: (i, 0))],
        out_specs=[pl.BlockSpec((tb, LANES), lambda i: (i, 0)),
                   pl.BlockSpec((8, LANES), lambda i: (0, 0))],
        out_shape=[jax.ShapeDtypeStruct((m, LANES), F32), jax.ShapeDtypeStruct((8, LANES), F32)],
        scratch_shapes=[pltpu.VMEM((8, LANES), F32)],
        compiler_params=_cparams(("arbitrary",), 32),
        name="rank",
    )(onehot)


def _dest_kernel(pos_ref, route_ref, start_ref, dest_ref):
    lane = lax.broadcasted_iota(jnp.int32, pos_ref.shape, 1).astype(F32)
    v = pos_ref[...] + start_ref[...]
    d1 = jnp.sum(jnp.where(lane == route_ref[:, 2:3], v, 0.0), axis=1, keepdims=True)
    d2 = jnp.sum(jnp.where(lane == route_ref[:, 3:4], v, 0.0), axis=1, keepdims=True)
    dest_ref[...] = jnp.where(lane == 0.0, d1, jnp.where(lane == 1.0, d2, 0.0)).astype(jnp.int32)


def _dest(pos, route, start, tb):
    m = pos.shape[0]
    tb = min(tb, m)
    return pl.pallas_call(
        _dest_kernel,
        grid=(m // tb,),
        in_specs=[pl.BlockSpec((tb, LANES), lambda i: (i, 0)),
                  pl.BlockSpec((tb, LANES), lambda i: (i, 0)),
                  pl.BlockSpec((1, LANES), lambda i: (0, 0))],
        out_specs=pl.BlockSpec((tb, LANES), lambda i: (i, 0)),
        out_shape=jax.ShapeDtypeStruct((m, LANES), jnp.int32),
        compiler_params=_cparams(("parallel",), 32),
        name="dest",
    )(pos, route, start)


def _scatter_kernel(dest_ref, h_ref, xs_in_ref, xs_ref, sem, *, tb):
    del xs_in_ref
    base = pl.program_id(0) * tb

    def row_copy(t, slot):
        d = dest_ref[(base + t) * 2 + slot]
        return pltpu.make_async_copy(h_ref.at[pl.ds(t, 1), :], xs_ref.at[pl.ds(d, 1), :], sem)

    def start(t, carry):
        row_copy(t, 0).start()
        row_copy(t, 1).start()
        return carry

    def wait(t, carry):
        row_copy(t, 0).wait()
        row_copy(t, 1).wait()
        return carry

    lax.fori_loop(0, tb, start, 0)
    lax.fori_loop(0, tb, wait, 0)


def _scatter_rows(dest_flat, h, n_rows, tb):
    m, d = h.shape
    tb = min(tb, m)
    xs0 = jnp.zeros((n_rows, d), F32)
    return pl.pallas_call(
        functools.partial(_scatter_kernel, tb=tb),
        grid_spec=pltpu.PrefetchScalarGridSpec(
            num_scalar_prefetch=1,
            grid=(m // tb,),
            in_specs=[pl.BlockSpec((tb, d), lambda i, dest: (i, 0)),
                      pl.BlockSpec(memory_space=pl.ANY)],
            out_specs=pl.BlockSpec(memory_space=pl.ANY),
            scratch_shapes=[pltpu.SemaphoreType.DMA(())]),
        out_shape=jax.ShapeDtypeStruct((n_rows, d), F32),
        input_output_aliases={2: 0},
        compiler_params=_cparams(("arbitrary",), 32),
        name="scatter",
    )(dest_flat, h, xs0)


def _gm1_kernel(te_ref, nv_ref, xs_ref, w1_ref, w3_ref, o_ref, w1b_scr, w3b_scr):
    r = pl.program_id(1)
    fresh = jnp.logical_or(r == 0, te_ref[r] != te_ref[jnp.maximum(r - 1, 0)])

    @pl.when(jnp.logical_and(r < nv_ref[0], fresh))
    def _():
        w1b_scr[...] = w1_ref[0].astype(BF16)
        w3b_scr[...] = w3_ref[0].astype(BF16)

    @pl.when(r < nv_ref[0])
    def _():
        x = xs_ref[...].astype(BF16)
        a = jnp.dot(x, w1b_scr[...], preferred_element_type=F32)
        b = jnp.dot(x, w3b_scr[...], preferred_element_type=F32)
        o_ref[...] = (a * jax.nn.sigmoid(a) * b).astype(o_ref.dtype)

    @pl.when(r >= nv_ref[0])
    def _():
        o_ref[...] = jnp.zeros(o_ref.shape, o_ref.dtype)


def _gm1(te, nv, xs, w1, w3, tf):
    n_rows, d = xs.shape
    f = w1.shape[2]
    nt = n_rows // ROW_TILE

    def rc(r, nv_):
        return jnp.minimum(r, nv_[0] - 1)

    return pl.pallas_call(
        _gm1_kernel,
        grid_spec=pltpu.PrefetchScalarGridSpec(
            num_scalar_prefetch=2,
            grid=(f // tf, nt),
            in_specs=[pl.BlockSpec((ROW_TILE, d), lambda c, r, te_, nv_: (rc(r, nv_), 0)),
                      pl.BlockSpec((1, d, tf), lambda c, r, te_, nv_: (te_[rc(r, nv_)], 0, c)),
                      pl.BlockSpec((1, d, tf), lambda c, r, te_, nv_: (te_[rc(r, nv_)], 0, c))],
            out_specs=pl.BlockSpec((ROW_TILE, tf), lambda c, r, te_, nv_: (r, c)),
            scratch_shapes=[pltpu.VMEM((d, tf), BF16), pltpu.VMEM((d, tf), BF16)]),
        out_shape=jax.ShapeDtypeStruct((n_rows, f), BF16),
        compiler_params=_cparams(("arbitrary", "arbitrary"), 56),
        name="gm1",
    )(te, nv, xs, w1, w3)


def _gm2_kernel(te_ref, nv_ref, h_ref, w2_ref, o_ref, w2b_scr):
    r = pl.program_id(1)
    fresh = jnp.logical_or(r == 0, te_ref[r] != te_ref[jnp.maximum(r - 1, 0)])

    @pl.when(jnp.logical_and(r < nv_ref[0], fresh))
    def _():
        w2b_scr[...] = w2_ref[0].astype(BF16)

    @pl.when(r < nv_ref[0])
    def _():
        o_ref[...] = jnp.dot(h_ref[...], w2b_scr[...], preferred_element_type=F32)

    @pl.when(r >= nv_ref[0])
    def _():
        o_ref[...] = jnp.zeros(o_ref.shape, o_ref.dtype)


def _gm2(te, nv, hid, w2, tn):
    n_rows, f = hid.shape
    d = w2.shape[2]
    tn = min(tn, d)
    nt = n_rows // ROW_TILE

    def rc(r, nv_):
        return jnp.minimum(r, nv_[0] - 1)

    return pl.pallas_call(
        _gm2_kernel,
        grid_spec=pltpu.PrefetchScalarGridSpec(
            num_scalar_prefetch=2,
            grid=(d // tn, nt),
            in_specs=[pl.BlockSpec((ROW_TILE, f), lambda c, r, te_, nv_: (rc(r, nv_), 0)),
                      pl.BlockSpec((1, f, tn), lambda c, r, te_, nv_: (te_[rc(r, nv_)], 0, c))],
            out_specs=pl.BlockSpec((ROW_TILE, tn), lambda c, r, te_, nv_: (r, c)),
            scratch_shapes=[pltpu.VMEM((f, tn), BF16)]),
        out_shape=jax.ShapeDtypeStruct((n_rows, d), F32),
        compiler_params=_cparams(("arbitrary", "arbitrary"), 48),
        name="gm2",
    )(te, nv, hid, w2)


def _combine_kernel(dest_ref, y_ref, h_ref, route_ref, g_ref, b_ref, o_ref, ybuf, sem, *, tb):
    base = pl.program_id(0) * tb

    def row_copy(t, slot):
        d = dest_ref[(base + t) * 2 + slot]
        return pltpu.make_async_copy(y_ref.at[pl.ds(d, 1), :], ybuf.at[slot, pl.ds(t, 1), :], sem)

    def start(t, carry):
        row_copy(t, 0).start()
        row_copy(t, 1).start()
        return carry

    def wait(t, carry):
        row_copy(t, 0).wait()
        row_copy(t, 1).wait()
        return carry

    lax.fori_loop(0, tb, start, 0)
    lax.fori_loop(0, tb, wait, 0)
    moe = route_ref[:, 0:1] * ybuf[0] + route_ref[:, 1:2] * ybuf[1]
    pre = DEEPNORM_ALPHA * h_ref[...] + moe
    mu = jnp.mean(pre, axis=1, keepdims=True)
    var = jnp.mean(jnp.square(pre - mu), axis=1, keepdims=True)
    o_ref[...] = (pre - mu) * lax.rsqrt(var + LN_EPS) * g_ref[...] + b_ref[...]


def _combine_ln(dest_flat, y, h, route, g, b, tb):
    m, d = h.shape
    tb = min(tb, m)
    return pl.pallas_call(
        functools.partial(_combine_kernel, tb=tb),
        grid_spec=pltpu.PrefetchScalarGridSpec(
            num_scalar_prefetch=1,
            grid=(m // tb,),
            in_specs=[pl.BlockSpec(memory_space=pl.ANY),
                      pl.BlockSpec((tb, d), lambda i, dest: (i, 0)),
                      pl.BlockSpec((tb, LANES), lambda i, dest: (i, 0)),
                      pl.BlockSpec((1, d), lambda i, dest: (0, 0)),
                      pl.BlockSpec((1, d), lambda i, dest: (0, 0))],
            out_specs=pl.BlockSpec((tb, d), lambda i, dest: (i, 0)),
            scratch_shapes=[pltpu.VMEM((2, tb, d), F32), pltpu.SemaphoreType.DMA(())]),
        out_shape=jax.ShapeDtypeStruct((m, d), F32),
        compiler_params=_cparams(("arbitrary",), 40),
        name="combine",
    )(dest_flat, y, h, route, g, b)


def _alibi_slopes(n):
    return jnp.asarray(2.0 ** (-8.0 * np.arange(1, n + 1) / n), dtype=F32)


def kernel(x, w_in, lam_q1, lam_k1, lam_q2, lam_k2, diff_subln_g, w_pa, w_pb, w_o, ln1_g, ln1_b,
           router_wg, router_bg, router_we, router_be, w1, w3, w2, ln2_g, ln2_b):
    bsz, s_len, d = x.shape
    assert bsz == 1 and w_in.shape[0] == DEPTH
    topk = min(TOPK_MAX, s_len // 4)
    x2 = x[0]
    xb = x2.astype(BF16)

    qk_w = N_DIFF_HEADS * 2 * HEAD_DIM
    c_dq, c_dk, c_dv = 0, qk_w, 2 * qk_w
    c_sq = 3 * qk_w
    c_sk = c_sq + N_DSA_HEADS * HEAD_DIM
    c_sv = c_sk + N_DSA_KV * HEAD_DIM
    c_iq = c_sv + N_DSA_KV * HEAD_DIM
    c_ik = c_iq + N_IDX_HEADS * IDX_DIM
    c_ga = c_ik + IDX_DIM + N_IDX_HEADS
    qscale = HEAD_DIM ** -0.5 * LOG2E
    col = np.arange(c_ik)
    is_q = ((col >= c_dq) & (col < c_dk)) | ((col >= c_sq) & (col < c_sk))
    col_scale = jnp.asarray(np.where(is_q, qscale, 1.0)[None, :], F32)
    wt3d = jnp.swapaxes(w_in, 1, 2)
    wt = wt3d[0]
    wt_small = jnp.concatenate([wt[c_ik:c_ga], jnp.zeros((2 * LANES - IDX_DIM - N_IDX_HEADS, d), F32)],
                               axis=0).astype(BF16)
    wt_gates = wt[c_ga:c_ga + 2 * d].astype(BF16)

    z = _proj_from_f32(xb, wt3d, col_scale, c_ik, 1024, 512)
    zs = _matmul_nt(xb, wt_small, F32, 1024, "proj_small")
    ik = zs[:, :IDX_DIM].astype(BF16)
    iwt = zs[:, IDX_DIM:IDX_DIM + N_IDX_HEADS].T
    dvt = _with_ones_rows(z[:, c_dv:c_dv + qk_w].T, N_DIFF_HEADS)
    svt = _with_ones_rows(z[:, c_sv:c_sv + N_DSA_KV * HEAD_DIM].T, N_DSA_KV)
    kaug = _key_aug_table(s_len)

    lam4 = jnp.stack([lam_q1[0], lam_k1[0], lam_q2[0], lam_k2[0]]).astype(F32)
    g_lanes = jnp.broadcast_to(diff_subln_g[0][:, None], (2 * HEAD_DIM, LANES))
    a = _diff_attention(z, dvt, kaug, lam4, g_lanes, _alibi_slopes(N_DIFF_HEADS), c_dq, c_dk, s_len,
                        min(512, s_len))

    tq_i, tk_i = min(256, s_len), min(512, s_len)
    maskb = _indexer(z, ik, iwt, c_iq, s_len, tq_i, tk_i, topk)
    b = _dsa_attention(z, svt, kaug, maskb, _alibi_slopes(N_DSA_HEADS), c_sq, c_sk, s_len, tq_i, tk_i)

    merged = _merge(xb, wt_gates, a, w_pa[0].astype(BF16), b, w_pb[0].astype(BF16), 512, 256)

    wr = jnp.concatenate([router_wg[0], router_we[0],
                          jnp.zeros((d, LANES - N_GROUPS - N_EXPERTS), F32)], axis=1)
    br = jnp.concatenate([router_bg[0], router_be[0],
                          jnp.zeros((LANES - N_GROUPS - N_EXPERTS,), F32)])[None, :]
    h1, route, onehot = _out_ln_router(merged, w_o[0].astype(BF16), x2, ln1_g[0][None, :], ln1_b[0][None, :],
                                       wr, br, 512, 512)

    pos, cnt = _rank(onehot, 512)
    counts = cnt[0, :N_EXPERTS].astype(jnp.int32)
    padded = ((counts + ROW_TILE - 1) // ROW_TILE) * ROW_TILE
    ends = jnp.cumsum(padded)
    start = jnp.zeros((1, LANES), F32).at[0, :N_EXPERTS].set((ends - padded).astype(F32))
    n_tiles = (2 * s_len) // ROW_TILE + N_EXPERTS
    tile_ids = jnp.arange(n_tiles, dtype=jnp.int32)
    te = jnp.minimum(jnp.sum(tile_ids[:, None] >= (ends // ROW_TILE)[None, :], axis=1), N_EXPERTS - 1)
    te = te.astype(jnp.int32)
    nv = (ends[-1] // ROW_TILE).astype(jnp.int32)[None]
    dest = _dest(pos, route, start, 512)
    dest_flat = dest[:, :2].reshape(-1)

    xs = _scatter_rows(dest_flat, h1, n_tiles * ROW_TILE, 256)
    hid = _gm1(te, nv, xs, w1[0], w3[0], min(512, w1.shape[3]))
    y = _gm2(te, nv, hid, w2[0], 1024)
    out = _combine_ln(dest_flat, y, h1, route, ln2_g[0][None, :], ln2_b[0][None, :], 128)
    return out[None]
```

```python
import functools
import math

import numpy as np
import jax
import jax.numpy as jnp
from jax import lax
from jax.experimental import pallas as pl
from jax.experimental.pallas import tpu as pltpu

HEAD_DIM = 128
N_DIFF_HEADS = 8
N_DSA_HEADS = 16
N_DSA_KV = 4
DSA_REP = N_DSA_HEADS // N_DSA_KV
N_IDX_HEADS = 32
IDX_DIM = 128
TOPK_MAX = 256
N_GROUPS = 4
EXPERTS_PER_GROUP = 8
N_EXPERTS = N_GROUPS * EXPERTS_PER_GROUP
LN_EPS = 1e-5
RMS_EPS = 1e-5
DEPTH = 1
DEEPNORM_ALPHA = (2.0 * DEPTH) ** 0.25
LAM_INIT = 0.8 - 0.6 * math.exp(-0.3 * 0)

LANES = 128
SUBLANES = 8
NEG = -1e30
INT_MIN = -(2 ** 31)
ROW_TILE = 256
LOG2E = 1.4426950408889634
POS_RADIX = 256

F32 = jnp.float32
BF16 = jnp.bfloat16


def _cparams(sem, vmem_mb):
    return pltpu.CompilerParams(dimension_semantics=sem, vmem_limit_bytes=vmem_mb << 20)


def _dot_nt(a, b):
    return lax.dot_general(a, b, (((1,), (1,)), ((), ())), preferred_element_type=F32)


def _proj_kernel(x_ref, wt_ref, sc_ref, o_ref, wb_scr):
    @pl.when(pl.program_id(1) == 0)
    def _():
        wb_scr[...] = wt_ref[0].astype(BF16)

    o_ref[...] = (_dot_nt(x_ref[...], wb_scr[...]) * sc_ref[...]).astype(o_ref.dtype)


def _proj_from_f32(xb, wt3d, col_scale, n_cols, tm, tn):
    m, d = xb.shape
    tm = min(tm, m)
    return pl.pallas_call(
        _proj_kernel,
        grid=(n_cols // tn, m // tm),
        in_specs=[pl.BlockSpec((tm, d), lambda j, i: (i, 0)),
                  pl.BlockSpec((1, tn, d), lambda j, i: (0, j, 0)),
                  pl.BlockSpec((1, tn), lambda j, i: (0, j))],
        out_specs=pl.BlockSpec((tm, tn), lambda j, i: (i, j)),
        out_shape=jax.ShapeDtypeStruct((m, n_cols), BF16),
        scratch_shapes=[pltpu.VMEM((tn, d), BF16)],
        compiler_params=_cparams(("parallel", "arbitrary"), 48),
        name="proj_main",
    )(xb, wt3d, col_scale)


def _cast_rows_kernel(w_ref, o_ref):
    o_ref[...] = w_ref[...].astype(o_ref.dtype)


def _cast_rows(wt, row0, n_rows, tr):
    d = wt.shape[1]
    assert row0 % SUBLANES == 0 and n_rows % tr == 0
    return pl.pallas_call(
        _cast_rows_kernel,
        grid=(n_rows // tr,),
        in_specs=[pl.BlockSpec((pl.Element(tr), pl.Element(d)),
                               lambda i: (pl.multiple_of(row0 + i * tr, SUBLANES), 0))],
        out_specs=pl.BlockSpec((tr, d), lambda i: (i, 0)),
        out_shape=jax.ShapeDtypeStruct((n_rows, d), BF16),
        compiler_params=_cparams(("parallel",), 48),
        name="cast_gates",
    )(wt)


def _mm_nt_kernel(a_ref, bt_ref, o_ref):
    o_ref[...] = _dot_nt(a_ref[...], bt_ref[...]).astype(o_ref.dtype)


def _matmul_nt(a, bt, out_dtype, tm, name):
    m, k = a.shape
    n = bt.shape[0]
    tm = min(tm, m)
    return pl.pallas_call(
        _mm_nt_kernel,
        grid=(m // tm,),
        in_specs=[pl.BlockSpec((tm, k), lambda i: (i, 0)),
                  pl.BlockSpec((n, k), lambda i: (0, 0))],
        out_specs=pl.BlockSpec((tm, n), lambda i: (i, 0)),
        out_shape=jax.ShapeDtypeStruct((m, n), out_dtype),
        compiler_params=_cparams(("parallel",), 48),
        name=name,
    )(a, bt)


def _key_aug_table(s_len):
    pos = jnp.arange(s_len, dtype=jnp.int32)[:, None]
    lane = jnp.arange(LANES, dtype=jnp.int32)[None, :]
    hi = (pos // POS_RADIX).astype(F32)
    lo = (pos % POS_RADIX).astype(F32)
    t = jnp.where(lane < 2, hi, jnp.where(lane < 4, lo, jnp.where(lane < 7, 1.0, 0.0)))
    return t.astype(BF16)


def _bf16_piece(x):
    return x.astype(BF16).astype(F32)


def _query_aug(slope, qbase, rows):
    s2 = jnp.full((SUBLANES, LANES), slope, F32) * LOG2E
    big = s2 * POS_RADIX
    off = -s2 * jnp.full((SUBLANES, LANES), qbase, F32)
    big_hi = _bf16_piece(big)
    s2_hi = _bf16_piece(s2)
    off_hi = _bf16_piece(off)
    off_mid = _bf16_piece(off - off_hi)
    lane = lax.broadcasted_iota(jnp.int32, (SUBLANES, LANES), 1)
    pieces = [big_hi, big - big_hi, s2_hi, s2 - s2_hi, off_hi, off_mid, off - off_hi - off_mid]
    row = jnp.zeros((SUBLANES, LANES), F32)
    for n, piece in enumerate(pieces):
        row = jnp.where(lane == n, piece, row)
    return jnp.broadcast_to(row[0:1, :], (rows, LANES)).astype(BF16)


QUERY_SUB = 256


def _causal_steps(nq, last_of):
    pairs = [(i, j) for i in range(nq) for j in range(last_of(i) + 1)]
    return (jnp.asarray([p[0] for p in pairs], jnp.int32), jnp.asarray([p[1] for p in pairs], jnp.int32))


ONES_ROWS = 16


def _with_ones_rows(vt, n_blocks):
    r = vt.shape[0] // n_blocks
    v3 = vt.reshape(n_blocks, r, vt.shape[1])
    ones = jnp.ones((n_blocks, ONES_ROWS, vt.shape[1]), vt.dtype)
    return jnp.concatenate([v3, ones], axis=1).reshape(n_blocks * (r + ONES_ROWS), vt.shape[1])


def _online_softmax_pv(s, vt1, m_scr, acc_scr, idx, cols):
    at = idx + (slice(None), cols)
    m_prev = m_scr[at]
    m_next = jnp.maximum(m_prev, jnp.max(s, axis=0, keepdims=True))
    p = jnp.exp2((s - m_next).astype(BF16))
    alpha = jnp.exp2(m_prev - m_next)
    m_scr[at] = m_next
    acc_scr[at] = acc_scr[at] * alpha + jnp.dot(vt1, p, preferred_element_type=F32)


DIFF_HEADS_PER_STEP = 2


def _diff_kernel(qi_ref, kj_ref, slopes_ref, lam_ref, g_ref, q_ref, k_ref, ka_ref, vt_ref, o_ref,
                 qa_scr, kk_scr, m_scr, acc_scr, *, tq):
    hp, step_id = pl.program_id(0), pl.program_id(1)
    i, j = qi_ref[step_id], kj_ref[step_id]
    reps = tq // LANES
    nsub = tq // QUERY_SUB
    w = 2 * HEAD_DIM
    maps = [(hh, c) for hh in range(DIFF_HEADS_PER_STEP) for c in range(2)]

    def map_cols(hh, c):
        return slice(hh * w + c * HEAD_DIM, hh * w + (c + 1) * HEAD_DIM)

    @pl.when(j == 0)
    def _():
        m_scr[...] = jnp.full(m_scr.shape, NEG, F32)
        acc_scr[...] = jnp.zeros(acc_scr.shape, F32)
        for hh in range(DIFF_HEADS_PER_STEP):
            qaug = _query_aug(slopes_ref[hp * DIFF_HEADS_PER_STEP + hh], (i * tq).astype(F32), tq)
            for c in range(2):
                qa_scr[2 * hh + c, :, :HEAD_DIM] = q_ref[:, map_cols(hh, c)]
                qa_scr[2 * hh + c, :, HEAD_DIM:] = qaug

    def step(masked):
        ka = ka_ref[...]
        for hh, c in maps:
            kk_scr[2 * hh + c, :, :HEAD_DIM] = k_ref[:, map_cols(hh, c)]
            kk_scr[2 * hh + c, :, HEAD_DIM:] = ka
        scores = [_dot_nt(kk_scr[n], qa_scr[n]) for n in range(len(maps))]
        for n, (hh, c) in enumerate(maps):
            vt1 = vt_ref[hh * (w + ONES_ROWS):(hh + 1) * (w + ONES_ROWS), :]
            for u in range(nsub):
                cols = slice(u * QUERY_SUB, (u + 1) * QUERY_SUB)
                s = scores[n][:, cols]
                if masked:
                    keep = (lax.broadcasted_iota(jnp.int32, (tq, 1), 0)
                            <= u * QUERY_SUB + lax.broadcasted_iota(jnp.int32, (1, QUERY_SUB), 1))
                    s = jnp.where(keep, s, NEG)
                _online_softmax_pv(s, vt1, m_scr, acc_scr, (n,), cols)

    @pl.when(j < i)
    def _():
        step(False)

    @pl.when(j == i)
    def _():
        step(True)
        lam = (jnp.exp(jnp.sum(lam_ref[0:1, :] * lam_ref[1:2, :], axis=1, keepdims=True))
               - jnp.exp(jnp.sum(lam_ref[2:3, :] * lam_ref[3:4, :], axis=1, keepdims=True)) + LAM_INIT)
        g = jnp.concatenate([g_ref[...]] * reps, axis=1)
        for hh in range(DIFF_HEADS_PER_STEP):
            n = 2 * hh
            o = (acc_scr[n, :w, :] / acc_scr[n, w:w + 1, :]
                 - lam * (acc_scr[n + 1, :w, :] / acc_scr[n + 1, w:w + 1, :]))
            o = o * lax.rsqrt(jnp.mean(o * o, axis=0, keepdims=True) + RMS_EPS) * g
            o_ref[:, hh * w:(hh + 1) * w] = (o * (1.0 - LAM_INIT)).T.astype(o_ref.dtype)


def _diff_attention(z, vt, kaug, lam4, g_lanes, slopes, col_q, col_k, s_len, tq):
    nq = s_len // tq
    w = 2 * HEAD_DIM
    wb = DIFF_HEADS_PER_STEP * w
    nmaps = 2 * DIFF_HEADS_PER_STEP
    qb, kb = col_q // wb, col_k // wb
    qi, kj = _causal_steps(nq, lambda i: i)
    return pl.pallas_call(
        functools.partial(_diff_kernel, tq=tq),
        grid_spec=pltpu.PrefetchScalarGridSpec(
            num_scalar_prefetch=3,
            grid=(N_DIFF_HEADS // DIFF_HEADS_PER_STEP, qi.shape[0]),
            in_specs=[pl.BlockSpec((4, HEAD_DIM), lambda h, s, qi_, kj_, sl_: (0, 0)),
                      pl.BlockSpec((w, LANES), lambda h, s, qi_, kj_, sl_: (0, 0)),
                      pl.BlockSpec((tq, wb), lambda h, s, qi_, kj_, sl_: (qi_[s], qb + h)),
                      pl.BlockSpec((tq, wb), lambda h, s, qi_, kj_, sl_: (kj_[s], kb + h)),
                      pl.BlockSpec((tq, LANES), lambda h, s, qi_, kj_, sl_: (kj_[s], 0)),
                      pl.BlockSpec((DIFF_HEADS_PER_STEP * (w + ONES_ROWS), tq),
                                   lambda h, s, qi_, kj_, sl_: (h, kj_[s]))],
            out_specs=pl.BlockSpec((tq, wb), lambda h, s, qi_, kj_, sl_: (qi_[s], h)),
            scratch_shapes=[pltpu.VMEM((nmaps, tq, w), BF16), pltpu.VMEM((nmaps, tq, w), BF16),
                            pltpu.VMEM((nmaps, 1, tq), F32),
                            pltpu.VMEM((nmaps, w + ONES_ROWS, tq), F32)]),
        out_shape=jax.ShapeDtypeStruct((s_len, N_DIFF_HEADS * w), BF16),
        compiler_params=_cparams(("parallel", "arbitrary"), 40),
        name="diffattn",
    )(qi, kj, slopes, lam4, g_lanes, z, z, kaug, vt)


IQ_SPLIT = 4


def _indexer_kernel(iq0_ref, iq1_ref, iq2_ref, iq3_ref, ik_ref, iwt_ref, o_ref,
                    iqh_scr, key_scr, acc_scr, pcut_scr, *, tq, tkc, topk, iw_scale, pos_bits):
    i = pl.program_id(0)
    nk = key_scr.shape[0]
    nchunks = (i * tq + tq - 1) // tkc + 1
    per = N_IDX_HEADS // IQ_SPLIT

    for h in range(N_IDX_HEADS):
        src = (iq0_ref, iq1_ref, iq2_ref, iq3_ref)[h // per]
        iqh_scr[h] = src[:, (h % per) * IDX_DIM:(h % per + 1) * IDX_DIM]

    qpos = i * tq + lax.broadcasted_iota(jnp.int32, (1, tq), 1)

    def key_pos(c):
        return c * tkc + lax.broadcasted_iota(jnp.int32, (tkc, 1), 0)

    def count_keys(pred):
        def body(c, acc):
            x = jnp.where(pred(c, key_scr[c]), 1, 0)
            return acc + jnp.sum(x.reshape(tkc // SUBLANES, SUBLANES, tq), axis=0)

        cnt = lax.fori_loop(0, nchunks, body, jnp.zeros((SUBLANES, tq), jnp.int32))
        return jnp.sum(cnt, axis=0, keepdims=True)

    def chunk(c, carry):
        kc = ik_ref[pl.ds(pl.multiple_of(c * tkc, tkc), tkc), :]
        acc_scr[...] = jnp.zeros(acc_scr.shape, F32)

        def head(h, carry2):
            sc = _dot_nt(kc, iqh_scr[h])
            acc_scr[...] += (iwt_ref[pl.ds(h, 1), :] * iw_scale) * jnp.maximum(sc, 0.0)
            return carry2

        lax.fori_loop(0, N_IDX_HEADS, head, 0, unroll=16)
        score = acc_scr[...]
        score = jnp.where(score == 0.0, 0.0, score)
        bits = pltpu.bitcast(score, jnp.int32)
        skey = bits ^ ((bits >> 31) & 0x7FFFFFFF)
        key_scr[c] = jnp.where(key_pos(c) <= qpos, skey, INT_MIN)
        return carry

    lax.fori_loop(0, nchunks, chunk, 0)

    def bit_step(bi, thr):
        cand = thr ^ lax.shift_left(jnp.int32(1), 31 - bi)
        return jnp.where(count_keys(lambda c, k: k >= cand) >= topk, cand, thr)

    thr = lax.fori_loop(0, 32, bit_step, jnp.full((1, tq), INT_MIN, jnp.int32))
    thr = jnp.maximum(thr, INT_MIN + 1)

    pcut_scr[...] = jnp.full(pcut_scr.shape, (1 << pos_bits) - 1, jnp.int32)

    @pl.when(jnp.max(count_keys(lambda c, k: k >= thr)) > topk)
    def _():
        need = topk - count_keys(lambda c, k: k > thr)

        def pos_step(bi, p):
            cand = p | lax.shift_left(jnp.int32(1), pos_bits - 1 - bi)
            tied_before = count_keys(lambda c, k: jnp.where(k == thr, key_pos(c), cand) < cand)
            return jnp.where(tied_before < need, cand, p)

        pcut_scr[...] = lax.fori_loop(0, pos_bits, pos_step, jnp.zeros((1, tq), jnp.int32))

    pcut = pcut_scr[...]

    def emit(c, carry):
        k = key_scr[c]
        tie_bias = jnp.where(key_pos(c) <= pcut, 0.0, NEG)
        o_ref[0, c] = jnp.where(k > thr, 0.0, jnp.where(k == thr, tie_bias, NEG)).astype(o_ref.dtype)
        return carry

    lax.fori_loop(0, nchunks, emit, 0)

    def fill(c, carry):
        o_ref[0, c] = jnp.full((tkc, tq), NEG, o_ref.dtype)
        return carry

    lax.fori_loop(nchunks, nk, fill, 0)


def _indexer(z, ik, iwt, col_iq, s_len, tq, tkc, topk):
    nq, nk = s_len // tq, s_len // tkc
    iw_scale = N_IDX_HEADS ** -0.5 * IDX_DIM ** -0.5
    wq = N_IDX_HEADS * IDX_DIM // IQ_SPLIT
    qb = col_iq // wq
    iq_specs = [pl.BlockSpec((tq, wq), functools.partial(lambda i, n: (i, qb + n), n=n)) for n in range(IQ_SPLIT)]
    return pl.pallas_call(
        functools.partial(_indexer_kernel, tq=tq, tkc=tkc, topk=topk, iw_scale=iw_scale,
                          pos_bits=max(1, (s_len - 1).bit_length())),
        grid=(nq,),
        in_specs=iq_specs + [pl.BlockSpec((s_len, IDX_DIM), lambda i: (0, 0)),
                             pl.BlockSpec((N_IDX_HEADS, tq), lambda i: (0, i))],
        out_specs=pl.BlockSpec((1, nk, tkc, tq), lambda i: (i, 0, 0, 0)),
        out_shape=jax.ShapeDtypeStruct((nq, nk, tkc, tq), BF16),
        scratch_shapes=[pltpu.VMEM((N_IDX_HEADS, tq, IDX_DIM), BF16),
                        pltpu.VMEM((nk, tkc, tq), jnp.int32),
                        pltpu.VMEM((tkc, tq), F32),
                        pltpu.VMEM((1, tq), jnp.int32)],
        compiler_params=_cparams(("parallel",), 48),
        name="indexer",
    )(z, z, z, z, ik, iwt)


DSA_GROUPS_PER_STEP = 2


def _dsa_kernel(qi_ref, kj_ref, slopes_ref, q_ref, k_ref, ka_ref, vt_ref, mb_ref, o_ref,
                qa_scr, kk_scr, m_scr, acc_scr, *, tq, tk):
    gp, step_id = pl.program_id(0), pl.program_id(1)
    i, j = qi_ref[step_id], kj_ref[step_id]
    last = (i * tq + tq - 1) // tk
    heads = DSA_GROUPS_PER_STEP * DSA_REP

    @pl.when(j == 0)
    def _():
        m_scr[...] = jnp.full(m_scr.shape, NEG, F32)
        acc_scr[...] = jnp.zeros(acc_scr.shape, F32)
        for n in range(heads):
            rows = slice(n * tq, (n + 1) * tq)
            qa_scr[rows, :HEAD_DIM] = q_ref[:, n * HEAD_DIM:(n + 1) * HEAD_DIM]
            qa_scr[rows, HEAD_DIM:] = _query_aug(slopes_ref[gp * heads + n], (i * tq).astype(F32), tq)

    ka = ka_ref[...]
    for gg in range(DSA_GROUPS_PER_STEP):
        kk_scr[gg, :, :HEAD_DIM] = k_ref[:, gg * HEAD_DIM:(gg + 1) * HEAD_DIM]
        kk_scr[gg, :, HEAD_DIM:] = ka
    mb = mb_ref[0, 0].astype(F32)
    scores = [_dot_nt(kk_scr[gg], qa_scr[gg * DSA_REP * tq:(gg + 1) * DSA_REP * tq, :])
              for gg in range(DSA_GROUPS_PER_STEP)]
    for gg in range(DSA_GROUPS_PER_STEP):
        vt1 = vt_ref[gg * (HEAD_DIM + ONES_ROWS):(gg + 1) * (HEAD_DIM + ONES_ROWS), :]
        for r in range(DSA_REP):
            n = gg * DSA_REP + r
            _online_softmax_pv(scores[gg][:, r * tq:(r + 1) * tq] + mb, vt1, m_scr, acc_scr, (),
                               slice(n * tq, (n + 1) * tq))

    @pl.when(j == last)
    def _():
        o = acc_scr[:HEAD_DIM, :] / acc_scr[HEAD_DIM:HEAD_DIM + 1, :]
        for n in range(heads):
            o_ref[:, n * HEAD_DIM:(n + 1) * HEAD_DIM] = o[:, n * tq:(n + 1) * tq].T.astype(o_ref.dtype)


def _dsa_attention(z, vt, kaug, maskb, slopes, col_q, col_k, s_len, tq, tk):
    nq = s_len // tq
    gps = DSA_GROUPS_PER_STEP
    heads = gps * DSA_REP
    qw, kw = heads * HEAD_DIM, gps * HEAD_DIM
    qb, kb = col_q // qw, col_k // kw
    qi, kj = _causal_steps(nq, lambda i: (i * tq + tq - 1) // tk)
    return pl.pallas_call(
        functools.partial(_dsa_kernel, tq=tq, tk=tk),
        grid_spec=pltpu.PrefetchScalarGridSpec(
            num_scalar_prefetch=3,
            grid=(N_DSA_KV // gps, qi.shape[0]),
            in_specs=[pl.BlockSpec((tq, qw), lambda g, s, qi_, kj_, sl_: (qi_[s], qb + g)),
                      pl.BlockSpec((tk, kw), lambda g, s, qi_, kj_, sl_: (kj_[s], kb + g)),
                      pl.BlockSpec((tk, LANES), lambda g, s, qi_, kj_, sl_: (kj_[s], 0)),
                      pl.BlockSpec((gps * (HEAD_DIM + ONES_ROWS), tk), lambda g, s, qi_, kj_, sl_: (g, kj_[s])),
                      pl.BlockSpec((1, 1, tk, tq), lambda g, s, qi_, kj_, sl_: (qi_[s], kj_[s], 0, 0))],
            out_specs=pl.BlockSpec((tq, qw), lambda g, s, qi_, kj_, sl_: (qi_[s], g)),
            scratch_shapes=[pltpu.VMEM((heads * tq, 2 * HEAD_DIM), BF16),
                            pltpu.VMEM((gps, tk, 2 * HEAD_DIM), BF16),
                            pltpu.VMEM((1, heads * tq), F32),
                            pltpu.VMEM((HEAD_DIM + ONES_ROWS, heads * tq), F32)]),
        out_shape=jax.ShapeDtypeStruct((s_len, N_DSA_HEADS * HEAD_DIM), BF16),
        compiler_params=_cparams(("parallel", "arbitrary"), 40),
        name="dsa",
    )(qi, kj, slopes, z, z, kaug, vt, maskb)


def _merge_kernel(x_ref, wga_ref, wgb_ref, a_ref, wpa_ref, b_ref, wpb_ref, o_ref):
    x = x_ref[...]
    ga = _dot_nt(x, wga_ref[...])
    gb = _dot_nt(x, wgb_ref[...])
    pa = jnp.dot(a_ref[...], wpa_ref[...], preferred_element_type=F32)
    pb = jnp.dot(b_ref[...], wpb_ref[...], preferred_element_type=F32)
    o_ref[...] = (jax.nn.sigmoid(ga) * pa + jax.nn.sigmoid(gb) * pb).astype(o_ref.dtype)


def _merge(xb, wt_gates, a, wpa, b, wpb, tm, tn):
    m, d = xb.shape
    ka, kb = a.shape[1], b.shape[1]
    tm, tn = min(tm, m), min(tn, d)
    row = lambda i, j: (i, 0)
    col = lambda i, j: (0, j)
    return pl.pallas_call(
        _merge_kernel,
        grid=(m // tm, d // tn),
        in_specs=[pl.BlockSpec((tm, d), row),
                  pl.BlockSpec((tn, d), lambda i, j: (j, 0)), pl.BlockSpec((tn, d), lambda i, j: (d // tn + j, 0)),
                  pl.BlockSpec((tm, ka), row), pl.BlockSpec((ka, tn), col),
                  pl.BlockSpec((tm, kb), row), pl.BlockSpec((kb, tn), col)],
        out_specs=pl.BlockSpec((tm, tn), lambda i, j: (i, j)),
        out_shape=jax.ShapeDtypeStruct((m, d), BF16),
        compiler_params=_cparams(("parallel", "arbitrary"), 52),
        name="merge",
    )(xb, wt_gates, wt_gates, a, wpa, b, wpb)


def _split_bf16(x):
    hi = x.astype(BF16)
    lo = (x - hi.astype(F32)).astype(BF16)
    return hi, lo


def _out_kernel(mg_ref, wo_ref, x_ref, g_ref, b_ref, wr_ref, br_ref,
                h_ref, route_ref, oh_ref, pre_scr, *, tn, nj):
    j = pl.program_id(1)
    y = jnp.dot(mg_ref[...], wo_ref[...], preferred_element_type=F32)
    pre_scr[j] = DEEPNORM_ALPHA * x_ref[...] + y

    @pl.when(j == nj - 1)
    def _():
        d = nj * tn
        tot = pre_scr[0].sum(axis=1, keepdims=True)
        for jj in range(1, nj):
            tot = tot + pre_scr[jj].sum(axis=1, keepdims=True)
        mu = tot / d
        sq = jnp.square(pre_scr[0] - mu).sum(axis=1, keepdims=True)
        for jj in range(1, nj):
            sq = sq + jnp.square(pre_scr[jj] - mu).sum(axis=1, keepdims=True)
        rstd = lax.rsqrt(sq / d + LN_EPS)
        logits = jnp.zeros(route_ref.shape, F32)
        for jj in range(nj):
            cs = slice(jj * tn, (jj + 1) * tn)
            hn = (pre_scr[jj] - mu) * rstd * g_ref[:, cs] + b_ref[:, cs]
            h_ref[:, cs] = hn
            h_hi, h_lo = _split_bf16(hn)
            w_hi, w_lo = _split_bf16(wr_ref[cs, :])
            logits = logits + (jnp.dot(h_hi, w_hi, preferred_element_type=F32)
                               + jnp.dot(h_hi, w_lo, preferred_element_type=F32)
                               + jnp.dot(h_lo, w_hi, preferred_element_type=F32))
        logits = logits + br_ref[...]
        lane = lax.broadcasted_iota(jnp.int32, logits.shape, 1)
        big = jnp.int32(4 * LANES)
        gl = jnp.where(lane < N_GROUPS, logits, -jnp.inf)
        gmax = jnp.max(gl, axis=1, keepdims=True)
        gsel = jnp.min(jnp.where(gl == gmax, lane, big), axis=1, keepdims=True)
        ggate = 1.0 / jnp.sum(jnp.exp(gl - gmax), axis=1, keepdims=True)
        eid = lane - N_GROUPS
        ingrp = (eid >= gsel * EXPERTS_PER_GROUP) & (eid < (gsel + 1) * EXPERTS_PER_GROUP)
        el = jnp.where(ingrp, logits, -jnp.inf)
        v1 = jnp.max(el, axis=1, keepdims=True)
        i1 = jnp.min(jnp.where(el == v1, lane, big), axis=1, keepdims=True)
        el2 = jnp.where(lane == i1, -jnp.inf, el)
        v2 = jnp.max(el2, axis=1, keepdims=True)
        i2 = jnp.min(jnp.where(el2 == v2, lane, big), axis=1, keepdims=True)
        t = jnp.exp(v2 - v1)
        g1 = ggate / (1.0 + t)
        g2 = g1 * t
        e1 = (i1 - N_GROUPS).astype(F32)
        e2 = (i2 - N_GROUPS).astype(F32)
        route_ref[...] = jnp.where(lane == 0, g1, jnp.where(lane == 1, g2,
                                   jnp.where(lane == 2, e1, jnp.where(lane == 3, e2, 0.0))))
        oh_ref[...] = jnp.where(lane == i1 - N_GROUPS, 1.0,
                                jnp.where(lane == i2 - N_GROUPS, 1.0, 0.0)).astype(oh_ref.dtype)


def _out_ln_router(mg, wo, x, g, b, wr, br, tm, tn):
    m, d = x.shape
    tm, tn = min(tm, m), min(tn, d)
    nj = d // tn
    return pl.pallas_call(
        functools.partial(_out_kernel, tn=tn, nj=nj),
        grid=(m // tm, nj),
        in_specs=[pl.BlockSpec((tm, d), lambda i, j: (i, 0)),
                  pl.BlockSpec((d, tn), lambda i, j: (0, j)),
                  pl.BlockSpec((tm, tn), lambda i, j: (i, j)),
                  pl.BlockSpec((1, d), lambda i, j: (0, 0)),
                  pl.BlockSpec((1, d), lambda i, j: (0, 0)),
                  pl.BlockSpec((d, LANES), lambda i, j: (0, 0)),
                  pl.BlockSpec((1, LANES), lambda i, j: (0, 0))],
        out_specs=[pl.BlockSpec((tm, d), lambda i, j: (i, 0)),
                   pl.BlockSpec((tm, LANES), lambda i, j: (i, 0)),
                   pl.BlockSpec((tm, LANES), lambda i, j: (i, 0))],
        out_shape=[jax.ShapeDtypeStruct((m, d), F32),
                   jax.ShapeDtypeStruct((m, LANES), F32),
                   jax.ShapeDtypeStruct((m, LANES), BF16)],
        scratch_shapes=[pltpu.VMEM((nj, tm, tn), F32)],
        compiler_params=_cparams(("parallel", "arbitrary"), 56),
        name="outln",
    )(mg, wo, x, g, b, wr, br)


def _rank_kernel(oh_ref, pos_ref, cnt_ref, base_scr, *, tb):
    @pl.when(pl.program_id(0) == 0)
    def _():
        base_scr[...] = jnp.zeros(base_scr.shape, F32)

    oh = oh_ref[...]
    r = lax.broadcasted_iota(jnp.int32, (tb, tb), 0)
    c = lax.broadcasted_iota(jnp.int32, (tb, tb), 1)
    tri = jnp.where(c <= r, 1.0, 0.0).astype(BF16)
    cs = jnp.dot(tri, oh, preferred_element_type=F32)
    pos_ref[...] = cs - oh.astype(F32) + base_scr[0:1, :]
    base_scr[...] = base_scr[...] + cs[tb - 1:tb, :]
    cnt_ref[...] = base_scr[...]


def _rank(onehot, tb):
    m = onehot.shape[0]
    tb = min(tb, m)
    return pl.pallas_call(
        functools.partial(_rank_kernel, tb=tb),
        grid=(m // tb,),
        in_specs=[pl.BlockSpec((tb, LANES), lambda i: (i, 0))],
        out_specs=[pl.BlockSpec((tb, LANES), lambda i: (i, 0)),
                   pl.BlockSpec((8, LANES), lambda i: (0, 0))],
        out_shape=[jax.ShapeDtypeStruct((m, LANES), F32), jax.ShapeDtypeStruct((8, LANES), F32)],
        scratch_shapes=[pltpu.VMEM((8, LANES), F32)],
        compiler_params=_cparams(("arbitrary",), 32),
        name="rank",
    )(onehot)


def _dest_kernel(pos_ref, route_ref, start_ref, dest_ref):
    lane = lax.broadcasted_iota(jnp.int32, pos_ref.shape, 1).astype(F32)
    v = pos_ref[...] + start_ref[...]
    d1 = jnp.sum(jnp.where(lane == route_ref[:, 2:3], v, 0.0), axis=1, keepdims=True)
    d2 = jnp.sum(jnp.where(lane == route_ref[:, 3:4], v, 0.0), axis=1, keepdims=True)
    dest_ref[...] = jnp.where(lane == 0.0, d1, jnp.where(lane == 1.0, d2, 0.0)).astype(jnp.int32)


def _dest(pos, route, start, tb):
    m = pos.shape[0]
    tb = min(tb, m)
    return pl.pallas_call(
        _dest_kernel,
        grid=(m // tb,),
        in_specs=[pl.BlockSpec((tb, LANES), lambda i: (i, 0)),
                  pl.BlockSpec((tb, LANES), lambda i: (i, 0)),
                  pl.BlockSpec((1, LANES), lambda i: (0, 0))],
        out_specs=pl.BlockSpec((tb, LANES), lambda i: (i, 0)),
        out_shape=jax.ShapeDtypeStruct((m, LANES), jnp.int32),
        compiler_params=_cparams(("parallel",), 32),
        name="dest",
    )(pos, route, start)


def _pack_bf16_pair(left, right):
    lo = pltpu.bitcast(left.astype(BF16).astype(F32), jnp.int32)
    hi = pltpu.bitcast(right.astype(BF16).astype(F32), jnp.int32)
    return hi | lax.shift_right_logical(lo, 16)


def _unpack_bf16_pair(words):
    left = pltpu.bitcast(lax.shift_left(words, 16), F32)
    right = pltpu.bitcast(words & jnp.int32(-65536), F32)
    return left, right


def _scatter_kernel(dest_ref, h_ref, xs_in_ref, xs_ref, hp_scr, sem, *, tb):
    del xs_in_ref
    base = pl.program_id(0) * tb
    half = h_ref.shape[1] // 2
    hp_scr[...] = _pack_bf16_pair(h_ref[:, :half], h_ref[:, half:])

    def row_copy(t, slot):
        d = dest_ref[(base + t) * 2 + slot]
        return pltpu.make_async_copy(hp_scr.at[pl.ds(t, 1), :], xs_ref.at[pl.ds(d, 1), :], sem)

    def start(t, carry):
        row_copy(t, 0).start()
        row_copy(t, 1).start()
        return carry

    def wait(t, carry):
        row_copy(t, 0).wait()
        row_copy(t, 1).wait()
        return carry

    lax.fori_loop(0, tb, start, 0)
    lax.fori_loop(0, tb, wait, 0)


def _scatter_rows(dest_flat, h, n_rows, tb):
    m, d = h.shape
    tb = min(tb, m)
    xs0 = jnp.zeros((n_rows, d // 2), jnp.int32)
    return pl.pallas_call(
        functools.partial(_scatter_kernel, tb=tb),
        grid_spec=pltpu.PrefetchScalarGridSpec(
            num_scalar_prefetch=1,
            grid=(m // tb,),
            in_specs=[pl.BlockSpec((tb, d), lambda i, dest: (i, 0)),
                      pl.BlockSpec(memory_space=pl.ANY)],
            out_specs=pl.BlockSpec(memory_space=pl.ANY),
            scratch_shapes=[pltpu.VMEM((tb, d // 2), jnp.int32), pltpu.SemaphoreType.DMA(())]),
        out_shape=jax.ShapeDtypeStruct((n_rows, d // 2), jnp.int32),
        input_output_aliases={2: 0},
        compiler_params=_cparams(("arbitrary",), 32),
        name="scatter",
    )(dest_flat, h, xs0)


def _gm1_kernel(te_ref, nv_ref, xs_ref, w1_ref, w3_ref, o_ref, w1b_scr, w3b_scr):
    r = pl.program_id(1)
    fresh = jnp.logical_or(r == 0, te_ref[r] != te_ref[jnp.maximum(r - 1, 0)])

    @pl.when(jnp.logical_and(r < nv_ref[0], fresh))
    def _():
        w1b_scr[...] = w1_ref[0].astype(BF16)
        w3b_scr[...] = w3_ref[0].astype(BF16)

    @pl.when(r < nv_ref[0])
    def _():
        left, right = _unpack_bf16_pair(xs_ref[...])
        x = jnp.concatenate([left.astype(BF16), right.astype(BF16)], axis=1)
        a = jnp.dot(x, w1b_scr[...], preferred_element_type=F32)
        b = jnp.dot(x, w3b_scr[...], preferred_element_type=F32)
        o_ref[...] = (a * jax.nn.sigmoid(a) * b).astype(o_ref.dtype)

    @pl.when(r >= nv_ref[0])
    def _():
        o_ref[...] = jnp.zeros(o_ref.shape, o_ref.dtype)


def _gm1(te, nv, xs, w1, w3, tf):
    n_rows = xs.shape[0]
    d, f = w1.shape[1], w1.shape[2]
    nt = n_rows // ROW_TILE

    def rc(r, nv_):
        return jnp.minimum(r, nv_[0] - 1)

    return pl.pallas_call(
        _gm1_kernel,
        grid_spec=pltpu.PrefetchScalarGridSpec(
            num_scalar_prefetch=2,
            grid=(f // tf, nt),
            in_specs=[pl.BlockSpec((ROW_TILE, d // 2), lambda c, r, te_, nv_: (rc(r, nv_), 0)),
                      pl.BlockSpec((1, d, tf), lambda c, r, te_, nv_: (te_[rc(r, nv_)], 0, c)),
                      pl.BlockSpec((1, d, tf), lambda c, r, te_, nv_: (te_[rc(r, nv_)], 0, c))],
            out_specs=pl.BlockSpec((ROW_TILE, tf), lambda c, r, te_, nv_: (r, c)),
            scratch_shapes=[pltpu.VMEM((d, tf), BF16), pltpu.VMEM((d, tf), BF16)]),
        out_shape=jax.ShapeDtypeStruct((n_rows, f), BF16),
        compiler_params=_cparams(("arbitrary", "arbitrary"), 56),
        name="gm1",
    )(te, nv, xs, w1, w3)


def _gm2_kernel(te_ref, nv_ref, h_ref, w2_ref, o_ref, w2b_scr):
    r = pl.program_id(1)
    fresh = jnp.logical_or(r == 0, te_ref[r] != te_ref[jnp.maximum(r - 1, 0)])

    @pl.when(jnp.logical_and(r < nv_ref[0], fresh))
    def _():
        w2b_scr[...] = w2_ref[0].astype(BF16)

    @pl.when(r < nv_ref[0])
    def _():
        y = jnp.dot(h_ref[...], w2b_scr[...], preferred_element_type=F32)
        half = y.shape[1] // 2
        o_ref[...] = _pack_bf16_pair(y[:, :half], y[:, half:])

    @pl.when(r >= nv_ref[0])
    def _():
        o_ref[...] = jnp.zeros(o_ref.shape, o_ref.dtype)


def _gm2(te, nv, hid, w2, tn):
    n_rows, f = hid.shape
    d = w2.shape[2]
    tn = min(tn, d)
    nt = n_rows // ROW_TILE

    def rc(r, nv_):
        return jnp.minimum(r, nv_[0] - 1)

    return pl.pallas_call(
        _gm2_kernel,
        grid_spec=pltpu.PrefetchScalarGridSpec(
            num_scalar_prefetch=2,
            grid=(d // tn, nt),
            in_specs=[pl.BlockSpec((ROW_TILE, f), lambda c, r, te_, nv_: (rc(r, nv_), 0)),
                      pl.BlockSpec((1, f, tn), lambda c, r, te_, nv_: (te_[rc(r, nv_)], 0, c))],
            out_specs=pl.BlockSpec((ROW_TILE, tn // 2), lambda c, r, te_, nv_: (r, c)),
            scratch_shapes=[pltpu.VMEM((f, tn), BF16)]),
        out_shape=jax.ShapeDtypeStruct((n_rows, d // 2), jnp.int32),
        compiler_params=_cparams(("arbitrary", "arbitrary"), 48),
        name="gm2",
    )(te, nv, hid, w2)


def _combine_kernel(dest_ref, y_ref, h_ref, route_ref, g_ref, b_ref, o_ref, ybuf, sem, *, tb, tn):
    i, nblk = pl.program_id(0), pl.num_programs(0)

    def row_copy(blk, buf, t, e):
        d = dest_ref[(blk * tb + t) * 2 + e]
        return pltpu.make_async_copy(y_ref.at[pl.ds(d, 1), :], ybuf.at[buf, e, pl.ds(t, 1), :], sem.at[buf])

    def fetch(blk, buf):
        def body(t, carry):
            row_copy(blk, buf, t, 0).start()
            row_copy(blk, buf, t, 1).start()
            return carry

        lax.fori_loop(0, tb, body, 0)

    @pl.when(i == 0)
    def _():
        fetch(0, 0)

    @pl.when(i + 1 < nblk)
    def _():
        fetch(i + 1, (i + 1) % 2)

    cur = i % 2

    def wait(t, carry):
        row_copy(i, cur, t, 0).wait()
        row_copy(i, cur, t, 1).wait()
        return carry

    lax.fori_loop(0, tb, wait, 0)
    half = tn // 2
    pieces = []
    for c in range(h_ref.shape[1] // tn):
        l0, r0 = _unpack_bf16_pair(ybuf[cur, 0, :, c * half:(c + 1) * half])
        l1, r1 = _unpack_bf16_pair(ybuf[cur, 1, :, c * half:(c + 1) * half])
        pieces += [route_ref[:, 0:1] * l0 + route_ref[:, 1:2] * l1, route_ref[:, 0:1] * r0 + route_ref[:, 1:2] * r1]
    moe = jnp.concatenate(pieces, axis=1)
    pre = DEEPNORM_ALPHA * h_ref[...] + moe
    mu = jnp.mean(pre, axis=1, keepdims=True)
    var = jnp.mean(jnp.square(pre - mu), axis=1, keepdims=True)
    o_ref[...] = (pre - mu) * lax.rsqrt(var + LN_EPS) * g_ref[...] + b_ref[...]


def _combine_ln(dest_flat, y, h, route, g, b, tb, tn):
    m, d = h.shape
    tb = min(tb, m)
    return pl.pallas_call(
        functools.partial(_combine_kernel, tb=tb, tn=tn),
        grid_spec=pltpu.PrefetchScalarGridSpec(
            num_scalar_prefetch=1,
            grid=(m // tb,),
            in_specs=[pl.BlockSpec(memory_space=pl.ANY),
                      pl.BlockSpec((tb, d), lambda i, dest: (i, 0)),
                      pl.BlockSpec((tb, LANES), lambda i, dest: (i, 0)),
                      pl.BlockSpec((1, d), lambda i, dest: (0, 0)),
                      pl.BlockSpec((1, d), lambda i, dest: (0, 0))],
            out_specs=pl.BlockSpec((tb, d), lambda i, dest: (i, 0)),
            scratch_shapes=[pltpu.VMEM((2, 2, tb, d // 2), jnp.int32), pltpu.SemaphoreType.DMA((2,))]),
        out_shape=jax.ShapeDtypeStruct((m, d), F32),
        compiler_params=_cparams(("arbitrary",), 40),
        name="combine",
    )(dest_flat, y, h, route, g, b)


def _alibi_slopes(n):
    return jnp.asarray(2.0 ** (-8.0 * np.arange(1, n + 1) / n), dtype=F32)


def kernel(x, w_in, lam_q1, lam_k1, lam_q2, lam_k2, diff_subln_g, w_pa, w_pb, w_o, ln1_g, ln1_b,
           router_wg, router_bg, router_we, router_be, w1, w3, w2, ln2_g, ln2_b):
    bsz, s_len, d = x.shape
    assert bsz == 1 and w_in.shape[0] == DEPTH
    topk = min(TOPK_MAX, s_len // 4)
    x2 = x[0]
    xb = x2.astype(BF16)

    qk_w = N_DIFF_HEADS * 2 * HEAD_DIM
    c_dq, c_dk, c_dv = 0, qk_w, 2 * qk_w
    c_sq = 3 * qk_w
    c_sk = c_sq + N_DSA_HEADS * HEAD_DIM
    c_sv = c_sk + N_DSA_KV * HEAD_DIM
    c_iq = c_sv + N_DSA_KV * HEAD_DIM
    c_ik = c_iq + N_IDX_HEADS * IDX_DIM
    c_ga = c_ik + IDX_DIM + N_IDX_HEADS
    qscale = HEAD_DIM ** -0.5 * LOG2E
    col = np.arange(c_ik)
    is_q = ((col >= c_dq) & (col < c_dk)) | ((col >= c_sq) & (col < c_sk))
    col_scale = jnp.asarray(np.where(is_q, qscale, 1.0)[None, :], F32)
    wt3d = jnp.swapaxes(w_in, 1, 2)
    wt = wt3d[0]
    wt_small = jnp.concatenate([wt[c_ik:c_ga], jnp.zeros((2 * LANES - IDX_DIM - N_IDX_HEADS, d), F32)],
                               axis=0).astype(BF16)
    wt_gates = _cast_rows(wt, c_ga, 2 * d, min(512, d))

    z = _proj_from_f32(xb, wt3d, col_scale, c_ik, 1024, 512)
    zs = _matmul_nt(xb, wt_small, F32, 1024, "proj_small")
    ik = zs[:, :IDX_DIM].astype(BF16)
    iwt = zs[:, IDX_DIM:IDX_DIM + N_IDX_HEADS].T
    dvt = _with_ones_rows(z[:, c_dv:c_dv + qk_w].T, N_DIFF_HEADS)
    svt = _with_ones_rows(z[:, c_sv:c_sv + N_DSA_KV * HEAD_DIM].T, N_DSA_KV)
    kaug = _key_aug_table(s_len)

    lam4 = jnp.stack([lam_q1[0], lam_k1[0], lam_q2[0], lam_k2[0]]).astype(F32)
    g_lanes = jnp.broadcast_to(diff_subln_g[0][:, None], (2 * HEAD_DIM, LANES))
    a = _diff_attention(z, dvt, kaug, lam4, g_lanes, _alibi_slopes(N_DIFF_HEADS), c_dq, c_dk, s_len,
                        min(512, s_len))

    tq_i, tk_i = min(256, s_len), min(512, s_len)
    maskb = _indexer(z, ik, iwt, c_iq, s_len, tq_i, tk_i, topk)
    b = _dsa_attention(z, svt, kaug, maskb, _alibi_slopes(N_DSA_HEADS), c_sq, c_sk, s_len, tq_i, tk_i)

    merged = _merge(xb, wt_gates, a, w_pa[0].astype(BF16), b, w_pb[0].astype(BF16), 512, 256)

    wr = jnp.concatenate([router_wg[0], router_we[0],
                          jnp.zeros((d, LANES - N_GROUPS - N_EXPERTS), F32)], axis=1)
    br = jnp.concatenate([router_bg[0], router_be[0],
                          jnp.zeros((LANES - N_GROUPS - N_EXPERTS,), F32)])[None, :]
    h1, route, onehot = _out_ln_router(merged, w_o[0].astype(BF16), x2, ln1_g[0][None, :], ln1_b[0][None, :],
                                       wr, br, 512, 512)

    pos, cnt = _rank(onehot, 512)
    counts = cnt[0, :N_EXPERTS].astype(jnp.int32)
    padded = ((counts + ROW_TILE - 1) // ROW_TILE) * ROW_TILE
    ends = jnp.cumsum(padded)
    start = jnp.zeros((1, LANES), F32).at[0, :N_EXPERTS].set((ends - padded).astype(F32))
    n_tiles = (2 * s_len) // ROW_TILE + N_EXPERTS
    tile_ids = jnp.arange(n_tiles, dtype=jnp.int32)
    te = jnp.minimum(jnp.sum(tile_ids[:, None] >= (ends // ROW_TILE)[None, :], axis=1), N_EXPERTS - 1)
    te = te.astype(jnp.int32)
    nv = (ends[-1] // ROW_TILE).astype(jnp.int32)[None]
    dest = _dest(pos, route, start, 512)
    dest_flat = dest[:, :2].reshape(-1)

    xs = _scatter_rows(dest_flat, h1, n_tiles * ROW_TILE, 256)
    hid = _gm1(te, nv, xs, w1[0], w3[0], min(512, w1.shape[3]))
    tn_y = min(1024, d)
    y = _gm2(te, nv, hid, w2[0], tn_y)
    out = _combine_ln(dest_flat, y, h1, route, ln2_g[0][None, :], ln2_b[0][None, :], 128, tn_y)
    return out[None]
```

```python
import functools
import math

import numpy as np
import jax
import jax.numpy as jnp
from jax import lax
from jax.experimental import pallas as pl
from jax.experimental.pallas import tpu as pltpu

HEAD_DIM = 128
N_DIFF_HEADS = 8
N_DSA_HEADS = 16
N_DSA_KV = 4
DSA_REP = N_DSA_HEADS // N_DSA_KV
N_IDX_HEADS = 32
IDX_DIM = 128
TOPK_MAX = 256
N_GROUPS = 4
EXPERTS_PER_GROUP = 8
N_EXPERTS = N_GROUPS * EXPERTS_PER_GROUP
LN_EPS = 1e-5
RMS_EPS = 1e-5
DEPTH = 1
DEEPNORM_ALPHA = (2.0 * DEPTH) ** 0.25
LAM_INIT = 0.8 - 0.6 * math.exp(-0.3 * 0)

LANES = 128
SUBLANES = 8
NEG = -1e30
INT_MIN = -(2 ** 31)
ROW_TILE = 256
LOG2E = 1.4426950408889634
POS_RADIX = 256

F32 = jnp.float32
BF16 = jnp.bfloat16


def _cparams(sem, vmem_mb):
    return pltpu.CompilerParams(dimension_semantics=sem, vmem_limit_bytes=vmem_mb << 20)


def _dot_nt(a, b):
    return lax.dot_general(a, b, (((1,), (1,)), ((), ())), preferred_element_type=F32)


def _proj_kernel(x_ref, wt_ref, sc_ref, o_ref, wb_scr):
    @pl.when(pl.program_id(1) == 0)
    def _():
        wb_scr[...] = wt_ref[0].astype(BF16)

    o_ref[...] = (_dot_nt(x_ref[...], wb_scr[...]) * sc_ref[...]).astype(o_ref.dtype)


def _proj_from_f32(xb, wt3d, col_scale, n_cols, tm, tn):
    m, d = xb.shape
    tm = min(tm, m)
    return pl.pallas_call(
        _proj_kernel,
        grid=(n_cols // tn, m // tm),
        in_specs=[pl.BlockSpec((tm, d), lambda j, i: (i, 0)),
                  pl.BlockSpec((1, tn, d), lambda j, i: (0, j, 0)),
                  pl.BlockSpec((1, tn), lambda j, i: (0, j))],
        out_specs=pl.BlockSpec((tm, tn), lambda j, i: (i, j)),
        out_shape=jax.ShapeDtypeStruct((m, n_cols), BF16),
        scratch_shapes=[pltpu.VMEM((tn, d), BF16)],
        compiler_params=_cparams(("parallel", "arbitrary"), 48),
        name="proj_main",
    )(xb, wt3d, col_scale)


def _cast_rows_kernel(w_ref, o_ref):
    o_ref[...] = w_ref[...].astype(o_ref.dtype)


def _cast_rows(wt, row0, n_rows, tr):
    d = wt.shape[1]
    assert row0 % SUBLANES == 0 and n_rows % tr == 0
    return pl.pallas_call(
        _cast_rows_kernel,
        grid=(n_rows // tr,),
        in_specs=[pl.BlockSpec((pl.Element(tr), pl.Element(d)),
                               lambda i: (pl.multiple_of(row0 + i * tr, SUBLANES), 0))],
        out_specs=pl.BlockSpec((tr, d), lambda i: (i, 0)),
        out_shape=jax.ShapeDtypeStruct((n_rows, d), BF16),
        compiler_params=_cparams(("parallel",), 48),
        name="cast_gates",
    )(wt)


def _mm_nt_kernel(a_ref, bt_ref, o_ref):
    o_ref[...] = _dot_nt(a_ref[...], bt_ref[...]).astype(o_ref.dtype)


def _matmul_nt(a, bt, out_dtype, tm, name):
    m, k = a.shape
    n = bt.shape[0]
    tm = min(tm, m)
    return pl.pallas_call(
        _mm_nt_kernel,
        grid=(m // tm,),
        in_specs=[pl.BlockSpec((tm, k), lambda i: (i, 0)),
                  pl.BlockSpec((n, k), lambda i: (0, 0))],
        out_specs=pl.BlockSpec((tm, n), lambda i: (i, 0)),
        out_shape=jax.ShapeDtypeStruct((m, n), out_dtype),
        compiler_params=_cparams(("parallel",), 48),
        name=name,
    )(a, bt)


def _key_aug_table(s_len):
    pos = jnp.arange(s_len, dtype=jnp.int32)[:, None]
    lane = jnp.arange(LANES, dtype=jnp.int32)[None, :]
    hi = (pos // POS_RADIX).astype(F32)
    lo = (pos % POS_RADIX).astype(F32)
    t = jnp.where(lane < 2, hi, jnp.where(lane < 4, lo, jnp.where(lane < 7, 1.0, 0.0)))
    return t.astype(BF16)


def _bf16_piece(x):
    return x.astype(BF16).astype(F32)


def _query_aug(slope, qbase, rows):
    s2 = jnp.full((SUBLANES, LANES), slope, F32) * LOG2E
    big = s2 * POS_RADIX
    off = -s2 * jnp.full((SUBLANES, LANES), qbase, F32)
    big_hi = _bf16_piece(big)
    s2_hi = _bf16_piece(s2)
    off_hi = _bf16_piece(off)
    off_mid = _bf16_piece(off - off_hi)
    lane = lax.broadcasted_iota(jnp.int32, (SUBLANES, LANES), 1)
    pieces = [big_hi, big - big_hi, s2_hi, s2 - s2_hi, off_hi, off_mid, off - off_hi - off_mid]
    row = jnp.zeros((SUBLANES, LANES), F32)
    for n, piece in enumerate(pieces):
        row = jnp.where(lane == n, piece, row)
    return jnp.broadcast_to(row[0:1, :], (rows, LANES)).astype(BF16)


QUERY_SUB = 256


def _causal_steps(nq, last_of):
    pairs = [(i, j) for i in range(nq) for j in range(last_of(i) + 1)]
    return (jnp.asarray([p[0] for p in pairs], jnp.int32), jnp.asarray([p[1] for p in pairs], jnp.int32))


ONES_ROWS = 16


def _with_ones_rows(vt, n_blocks):
    r = vt.shape[0] // n_blocks
    v3 = vt.reshape(n_blocks, r, vt.shape[1])
    ones = jnp.ones((n_blocks, ONES_ROWS, vt.shape[1]), vt.dtype)
    return jnp.concatenate([v3, ones], axis=1).reshape(n_blocks * (r + ONES_ROWS), vt.shape[1])


def _online_softmax_pv(s, vt1, m_scr, acc_scr, idx, cols):
    at = idx + (slice(None), cols)
    m_prev = m_scr[at]
    m_next = jnp.maximum(m_prev, jnp.max(s, axis=0, keepdims=True))
    p = jnp.exp2((s - m_next).astype(BF16))
    alpha = jnp.exp2(m_prev - m_next)
    m_scr[at] = m_next
    acc_scr[at] = acc_scr[at] * alpha + jnp.dot(vt1, p, preferred_element_type=F32)


DIFF_HEADS_PER_STEP = 2


def _diff_kernel(qi_ref, kj_ref, slopes_ref, lam_ref, g_ref, q_ref, k_ref, ka_ref, vt_ref, o_ref,
                 qa_scr, kk_scr, m_scr, acc_scr, *, tq):
    hp, step_id = pl.program_id(0), pl.program_id(1)
    i, j = qi_ref[step_id], kj_ref[step_id]
    reps = tq // LANES
    nsub = tq // QUERY_SUB
    w = 2 * HEAD_DIM
    maps = [(hh, c) for hh in range(DIFF_HEADS_PER_STEP) for c in range(2)]

    def map_cols(hh, c):
        return slice(hh * w + c * HEAD_DIM, hh * w + (c + 1) * HEAD_DIM)

    @pl.when(j == 0)
    def _():
        m_scr[...] = jnp.full(m_scr.shape, NEG, F32)
        acc_scr[...] = jnp.zeros(acc_scr.shape, F32)
        for hh in range(DIFF_HEADS_PER_STEP):
            qaug = _query_aug(slopes_ref[hp * DIFF_HEADS_PER_STEP + hh], (i * tq).astype(F32), tq)
            for c in range(2):
                qa_scr[2 * hh + c, :, :HEAD_DIM] = q_ref[:, map_cols(hh, c)]
                qa_scr[2 * hh + c, :, HEAD_DIM:] = qaug

    def step(masked):
        ka = ka_ref[...]
        for hh, c in maps:
            kk_scr[2 * hh + c, :, :HEAD_DIM] = k_ref[:, map_cols(hh, c)]
            kk_scr[2 * hh + c, :, HEAD_DIM:] = ka
        scores = [_dot_nt(kk_scr[n], qa_scr[n]) for n in range(len(maps))]
        for n, (hh, c) in enumerate(maps):
            vt1 = vt_ref[hh * (w + ONES_ROWS):(hh + 1) * (w + ONES_ROWS), :]
            for u in range(nsub):
                cols = slice(u * QUERY_SUB, (u + 1) * QUERY_SUB)
                s = scores[n][:, cols]
                if masked:
                    keep = (lax.broadcasted_iota(jnp.int32, (tq, 1), 0)
                            <= u * QUERY_SUB + lax.broadcasted_iota(jnp.int32, (1, QUERY_SUB), 1))
                    s = jnp.where(keep, s, NEG)
                _online_softmax_pv(s, vt1, m_scr, acc_scr, (n,), cols)

    @pl.when(j < i)
    def _():
        step(False)

    @pl.when(j == i)
    def _():
        step(True)
        lam = (jnp.exp(jnp.sum(lam_ref[0:1, :] * lam_ref[1:2, :], axis=1, keepdims=True))
               - jnp.exp(jnp.sum(lam_ref[2:3, :] * lam_ref[3:4, :], axis=1, keepdims=True)) + LAM_INIT)
        g = jnp.concatenate([g_ref[...]] * reps, axis=1)
        for hh in range(DIFF_HEADS_PER_STEP):
            n = 2 * hh
            o = (acc_scr[n, :w, :] / acc_scr[n, w:w + 1, :]
                 - lam * (acc_scr[n + 1, :w, :] / acc_scr[n + 1, w:w + 1, :]))
            o = o * lax.rsqrt(jnp.mean(o * o, axis=0, keepdims=True) + RMS_EPS) * g
            o_ref[:, hh * w:(hh + 1) * w] = (o * (1.0 - LAM_INIT)).T.astype(o_ref.dtype)


def _diff_attention(z, vt, kaug, lam4, g_lanes, slopes, col_q, col_k, s_len, tq):
    nq = s_len // tq
    w = 2 * HEAD_DIM
    wb = DIFF_HEADS_PER_STEP * w
    nmaps = 2 * DIFF_HEADS_PER_STEP
    qb, kb = col_q // wb, col_k // wb
    qi, kj = _causal_steps(nq, lambda i: i)
    return pl.pallas_call(
        functools.partial(_diff_kernel, tq=tq),
        grid_spec=pltpu.PrefetchScalarGridSpec(
            num_scalar_prefetch=3,
            grid=(N_DIFF_HEADS // DIFF_HEADS_PER_STEP, qi.shape[0]),
            in_specs=[pl.BlockSpec((4, HEAD_DIM), lambda h, s, qi_, kj_, sl_: (0, 0)),
                      pl.BlockSpec((w, LANES), lambda h, s, qi_, kj_, sl_: (0, 0)),
                      pl.BlockSpec((tq, wb), lambda h, s, qi_, kj_, sl_: (qi_[s], qb + h)),
                      pl.BlockSpec((tq, wb), lambda h, s, qi_, kj_, sl_: (kj_[s], kb + h)),
                      pl.BlockSpec((tq, LANES), lambda h, s, qi_, kj_, sl_: (kj_[s], 0)),
                      pl.BlockSpec((DIFF_HEADS_PER_STEP * (w + ONES_ROWS), tq),
                                   lambda h, s, qi_, kj_, sl_: (h, kj_[s]))],
            out_specs=pl.BlockSpec((tq, wb), lambda h, s, qi_, kj_, sl_: (qi_[s], h)),
            scratch_shapes=[pltpu.VMEM((nmaps, tq, w), BF16), pltpu.VMEM((nmaps, tq, w), BF16),
                            pltpu.VMEM((nmaps, 1, tq), F32),
                            pltpu.VMEM((nmaps, w + ONES_ROWS, tq), F32)]),
        out_shape=jax.ShapeDtypeStruct((s_len, N_DIFF_HEADS * w), BF16),
        compiler_params=_cparams(("parallel", "arbitrary"), 40),
        name="diffattn",
    )(qi, kj, slopes, lam4, g_lanes, z, z, kaug, vt)


IQ_SPLIT = 4


HALF_BITS = 16
HALF_MASK = (1 << HALF_BITS) - 1
I16_MIN = -(1 << (HALF_BITS - 1))


def _indexer_kernel(iq0_ref, iq1_ref, iq2_ref, iq3_ref, ik_ref, iwt_ref, o_ref,
                    iqh_scr, key_scr, hi_scr, lo_scr, acc_scr, pcut_scr, *, tq, tkc, topk, iw_scale, pos_bits):
    i = pl.program_id(0)
    nk = key_scr.shape[0]
    nchunks = (i * tq + tq - 1) // tkc + 1
    per = N_IDX_HEADS // IQ_SPLIT

    for h in range(N_IDX_HEADS):
        src = (iq0_ref, iq1_ref, iq2_ref, iq3_ref)[h // per]
        iqh_scr[h] = src[:, (h % per) * IDX_DIM:(h % per + 1) * IDX_DIM]

    qpos = i * tq + lax.broadcasted_iota(jnp.int32, (1, tq), 1)

    def key_pos(c):
        return c * tkc + lax.broadcasted_iota(jnp.int32, (tkc, 1), 0)

    def count_keys(pred):
        def body(c, acc):
            x = jnp.where(pred(c, key_scr[c]), 1, 0)
            return acc + jnp.sum(x.reshape(tkc // SUBLANES, SUBLANES, tq), axis=0)

        cnt = lax.fori_loop(0, nchunks, body, jnp.zeros((SUBLANES, tq), jnp.int32))
        return jnp.sum(cnt, axis=0, keepdims=True)

    def chunk(c, carry):
        kc = ik_ref[pl.ds(pl.multiple_of(c * tkc, tkc), tkc), :]
        acc_scr[...] = jnp.zeros(acc_scr.shape, F32)

        def head(h, carry2):
            sc = _dot_nt(kc, iqh_scr[h])
            acc_scr[...] += (iwt_ref[pl.ds(h, 1), :] * iw_scale) * jnp.maximum(sc, 0.0)
            return carry2

        lax.fori_loop(0, N_IDX_HEADS, head, 0, unroll=16)
        score = acc_scr[...]
        score = jnp.where(score == 0.0, 0.0, score)
        bits = pltpu.bitcast(score, jnp.int32)
        skey = bits ^ ((bits >> 31) & 0x7FFFFFFF)
        key = jnp.where(key_pos(c) <= qpos, skey, INT_MIN)
        key_scr[c] = key
        hi_scr[c] = (key >> HALF_BITS).astype(jnp.int16)
        lo_scr[c] = ((key & HALF_MASK) + I16_MIN).astype(jnp.int16)
        return carry

    lax.fori_loop(0, nchunks, chunk, 0)

    def count_halves(half_scr, pred):
        rows = 2 * SUBLANES

        def body(c, acc):
            x = jnp.where(pred(half_scr[c]), jnp.int16(1), jnp.int16(0))
            for g in range(tkc // rows):
                acc = acc + x[g * rows:(g + 1) * rows, :]
            return acc

        cnt = lax.fori_loop(0, nchunks, body, jnp.zeros((rows, tq), jnp.int16))
        return jnp.sum(cnt.astype(jnp.int32), axis=0, keepdims=True)

    def radix_select(half_scr, want):
        def bit_step(bi, u):
            cand = u | lax.shift_left(jnp.int32(1), HALF_BITS - 1 - bi)
            cand16 = (cand + I16_MIN).astype(jnp.int16)
            return jnp.where(count_halves(half_scr, lambda k: k >= cand16) >= want, cand, u)

        return lax.fori_loop(0, HALF_BITS, bit_step, jnp.zeros((1, tq), jnp.int32)) + I16_MIN

    t_hi = radix_select(hi_scr, topk)
    t_hi16 = t_hi.astype(jnp.int16)
    want_lo = topk - count_halves(hi_scr, lambda k: k > t_hi16)

    def keep_lo(c, carry):
        lo_scr[c] = jnp.where(hi_scr[c] == t_hi16, lo_scr[c], jnp.int16(I16_MIN))
        return carry

    lax.fori_loop(0, nchunks, keep_lo, 0)
    t_lo = radix_select(lo_scr, want_lo)
    thr = lax.shift_left(t_hi, HALF_BITS) | (t_lo - I16_MIN)
    thr = jnp.maximum(thr, INT_MIN + 1)

    pcut_scr[...] = jnp.full(pcut_scr.shape, (1 << pos_bits) - 1, jnp.int32)

    @pl.when(jnp.max(count_keys(lambda c, k: k >= thr)) > topk)
    def _():
        need = topk - count_keys(lambda c, k: k > thr)

        def pos_step(bi, p):
            cand = p | lax.shift_left(jnp.int32(1), pos_bits - 1 - bi)
            tied_before = count_keys(lambda c, k: jnp.where(k == thr, key_pos(c), cand) < cand)
            return jnp.where(tied_before < need, cand, p)

        pcut_scr[...] = lax.fori_loop(0, pos_bits, pos_step, jnp.zeros((1, tq), jnp.int32))

    pcut = pcut_scr[...]

    def emit(c, carry):
        k = key_scr[c]
        tie_bias = jnp.where(key_pos(c) <= pcut, 0.0, NEG)
        o_ref[0, c] = jnp.where(k > thr, 0.0, jnp.where(k == thr, tie_bias, NEG)).astype(o_ref.dtype)
        return carry

    lax.fori_loop(0, nchunks, emit, 0)

    def fill(c, carry):
        o_ref[0, c] = jnp.full((tkc, tq), NEG, o_ref.dtype)
        return carry

    lax.fori_loop(nchunks, nk, fill, 0)


def _indexer(z, ik, iwt, col_iq, s_len, tq, tkc, topk):
    nq, nk = s_len // tq, s_len // tkc
    iw_scale = N_IDX_HEADS ** -0.5 * IDX_DIM ** -0.5
    wq = N_IDX_HEADS * IDX_DIM // IQ_SPLIT
    qb = col_iq // wq
    iq_specs = [pl.BlockSpec((tq, wq), functools.partial(lambda i, n: (i, qb + n), n=n)) for n in range(IQ_SPLIT)]
    return pl.pallas_call(
        functools.partial(_indexer_kernel, tq=tq, tkc=tkc, topk=topk, iw_scale=iw_scale,
                          pos_bits=max(1, (s_len - 1).bit_length())),
        grid=(nq,),
        in_specs=iq_specs + [pl.BlockSpec((s_len, IDX_DIM), lambda i: (0, 0)),
                             pl.BlockSpec((N_IDX_HEADS, tq), lambda i: (0, i))],
        out_specs=pl.BlockSpec((1, nk, tkc, tq), lambda i: (i, 0, 0, 0)),
        out_shape=jax.ShapeDtypeStruct((nq, nk, tkc, tq), BF16),
        scratch_shapes=[pltpu.VMEM((N_IDX_HEADS, tq, IDX_DIM), BF16),
                        pltpu.VMEM((nk, tkc, tq), jnp.int32),
                        pltpu.VMEM((nk, tkc, tq), jnp.int16),
                        pltpu.VMEM((nk, tkc, tq), jnp.int16),
                        pltpu.VMEM((tkc, tq), F32),
                        pltpu.VMEM((1, tq), jnp.int32)],
        compiler_params=_cparams(("parallel",), 48),
        name="indexer",
    )(z, z, z, z, ik, iwt)


DSA_GROUPS_PER_STEP = 2


def _dsa_kernel(qi_ref, kj_ref, slopes_ref, q_ref, k_ref, ka_ref, vt_ref, mb_ref, o_ref,
                qa_scr, kk_scr, m_scr, acc_scr, *, tq, tk):
    gp, step_id = pl.program_id(0), pl.program_id(1)
    i, j = qi_ref[step_id], kj_ref[step_id]
    last = (i * tq + tq - 1) // tk
    heads = DSA_GROUPS_PER_STEP * DSA_REP

    @pl.when(j == 0)
    def _():
        m_scr[...] = jnp.full(m_scr.shape, NEG, F32)
        acc_scr[...] = jnp.zeros(acc_scr.shape, F32)
        for n in range(heads):
            rows = slice(n * tq, (n + 1) * tq)
            qa_scr[rows, :HEAD_DIM] = q_ref[:, n * HEAD_DIM:(n + 1) * HEAD_DIM]
            qa_scr[rows, HEAD_DIM:] = _query_aug(slopes_ref[gp * heads + n], (i * tq).astype(F32), tq)

    ka = ka_ref[...]
    for gg in range(DSA_GROUPS_PER_STEP):
        kk_scr[gg, :, :HEAD_DIM] = k_ref[:, gg * HEAD_DIM:(gg + 1) * HEAD_DIM]
        kk_scr[gg, :, HEAD_DIM:] = ka
    mb = mb_ref[0, 0].astype(F32)
    scores = [_dot_nt(kk_scr[gg], qa_scr[gg * DSA_REP * tq:(gg + 1) * DSA_REP * tq, :])
              for gg in range(DSA_GROUPS_PER_STEP)]
    for gg in range(DSA_GROUPS_PER_STEP):
        vt1 = vt_ref[gg * (HEAD_DIM + ONES_ROWS):(gg + 1) * (HEAD_DIM + ONES_ROWS), :]
        for r in range(DSA_REP):
            n = gg * DSA_REP + r
            _online_softmax_pv(scores[gg][:, r * tq:(r + 1) * tq] + mb, vt1, m_scr, acc_scr, (),
                               slice(n * tq, (n + 1) * tq))

    @pl.when(j == last)
    def _():
        o = acc_scr[:HEAD_DIM, :] / acc_scr[HEAD_DIM:HEAD_DIM + 1, :]
        for n in range(heads):
            o_ref[:, n * HEAD_DIM:(n + 1) * HEAD_DIM] = o[:, n * tq:(n + 1) * tq].T.astype(o_ref.dtype)


def _dsa_attention(z, vt, kaug, maskb, slopes, col_q, col_k, s_len, tq, tk):
    nq = s_len // tq
    gps = DSA_GROUPS_PER_STEP
    heads = gps * DSA_REP
    qw, kw = heads * HEAD_DIM, gps * HEAD_DIM
    qb, kb = col_q // qw, col_k // kw
    qi, kj = _causal_steps(nq, lambda i: (i * tq + tq - 1) // tk)
    return pl.pallas_call(
        functools.partial(_dsa_kernel, tq=tq, tk=tk),
        grid_spec=pltpu.PrefetchScalarGridSpec(
            num_scalar_prefetch=3,
            grid=(N_DSA_KV // gps, qi.shape[0]),
            in_specs=[pl.BlockSpec((tq, qw), lambda g, s, qi_, kj_, sl_: (qi_[s], qb + g)),
                      pl.BlockSpec((tk, kw), lambda g, s, qi_, kj_, sl_: (kj_[s], kb + g)),
                      pl.BlockSpec((tk, LANES), lambda g, s, qi_, kj_, sl_: (kj_[s], 0)),
                      pl.BlockSpec((gps * (HEAD_DIM + ONES_ROWS), tk), lambda g, s, qi_, kj_, sl_: (g, kj_[s])),
                      pl.BlockSpec((1, 1, tk, tq), lambda g, s, qi_, kj_, sl_: (qi_[s], kj_[s], 0, 0))],
            out_specs=pl.BlockSpec((tq, qw), lambda g, s, qi_, kj_, sl_: (qi_[s], g)),
            scratch_shapes=[pltpu.VMEM((heads * tq, 2 * HEAD_DIM), BF16),
                            pltpu.VMEM((gps, tk, 2 * HEAD_DIM), BF16),
                            pltpu.VMEM((1, heads * tq), F32),
                            pltpu.VMEM((HEAD_DIM + ONES_ROWS, heads * tq), F32)]),
        out_shape=jax.ShapeDtypeStruct((s_len, N_DSA_HEADS * HEAD_DIM), BF16),
        compiler_params=_cparams(("parallel", "arbitrary"), 40),
        name="dsa",
    )(qi, kj, slopes, z, z, kaug, vt, maskb)


def _merge_kernel(x_ref, wga_ref, wgb_ref, a_ref, wpa_ref, b_ref, wpb_ref, o_ref):
    x = x_ref[...]
    ga = _dot_nt(x, wga_ref[...])
    gb = _dot_nt(x, wgb_ref[...])
    pa = jnp.dot(a_ref[...], wpa_ref[...], preferred_element_type=F32)
    pb = jnp.dot(b_ref[...], wpb_ref[...], preferred_element_type=F32)
    o_ref[...] = (jax.nn.sigmoid(ga) * pa + jax.nn.sigmoid(gb) * pb).astype(o_ref.dtype)


def _merge(xb, wt_gates, a, wpa, b, wpb, tm, tn):
    m, d = xb.shape
    ka, kb = a.shape[1], b.shape[1]
    tm, tn = min(tm, m), min(tn, d)
    row = lambda i, j: (i, 0)
    col = lambda i, j: (0, j)
    return pl.pallas_call(
        _merge_kernel,
        grid=(m // tm, d // tn),
        in_specs=[pl.BlockSpec((tm, d), row),
                  pl.BlockSpec((tn, d), lambda i, j: (j, 0)), pl.BlockSpec((tn, d), lambda i, j: (d // tn + j, 0)),
                  pl.BlockSpec((tm, ka), row), pl.BlockSpec((ka, tn), col),
                  pl.BlockSpec((tm, kb), row), pl.BlockSpec((kb, tn), col)],
        out_specs=pl.BlockSpec((tm, tn), lambda i, j: (i, j)),
        out_shape=jax.ShapeDtypeStruct((m, d), BF16),
        compiler_params=_cparams(("parallel", "arbitrary"), 52),
        name="merge",
    )(xb, wt_gates, wt_gates, a, wpa, b, wpb)


def _split_bf16(x):
    hi = x.astype(BF16)
    lo = (x - hi.astype(F32)).astype(BF16)
    return hi, lo


def _out_kernel(mg_ref, wo_ref, x_ref, g_ref, b_ref, wr_ref, br_ref,
                h_ref, route_ref, oh_ref, pre_scr, *, tn, nj):
    j = pl.program_id(1)
    y = jnp.dot(mg_ref[...], wo_ref[...], preferred_element_type=F32)
    pre_scr[j] = DEEPNORM_ALPHA * x_ref[...] + y

    @pl.when(j == nj - 1)
    def _():
        d = nj * tn
        tot = pre_scr[0].sum(axis=1, keepdims=True)
        for jj in range(1, nj):
            tot = tot + pre_scr[jj].sum(axis=1, keepdims=True)
        mu = tot / d
        sq = jnp.square(pre_scr[0] - mu).sum(axis=1, keepdims=True)
        for jj in range(1, nj):
            sq = sq + jnp.square(pre_scr[jj] - mu).sum(axis=1, keepdims=True)
        rstd = lax.rsqrt(sq / d + LN_EPS)
        logits = jnp.zeros(route_ref.shape, F32)
        for jj in range(nj):
            cs = slice(jj * tn, (jj + 1) * tn)
            hn = (pre_scr[jj] - mu) * rstd * g_ref[:, cs] + b_ref[:, cs]
            h_ref[:, cs] = hn
            h_hi, h_lo = _split_bf16(hn)
            w_hi, w_lo = _split_bf16(wr_ref[cs, :])
            logits = logits + (jnp.dot(h_hi, w_hi, preferred_element_type=F32)
                               + jnp.dot(h_hi, w_lo, preferred_element_type=F32)
                               + jnp.dot(h_lo, w_hi, preferred_element_type=F32))
        logits = logits + br_ref[...]
        lane = lax.broadcasted_iota(jnp.int32, logits.shape, 1)
        big = jnp.int32(4 * LANES)
        gl = jnp.where(lane < N_GROUPS, logits, -jnp.inf)
        gmax = jnp.max(gl, axis=1, keepdims=True)
        gsel = jnp.min(jnp.where(gl == gmax, lane, big), axis=1, keepdims=True)
        ggate = 1.0 / jnp.sum(jnp.exp(gl - gmax), axis=1, keepdims=True)
        eid = lane - N_GROUPS
        ingrp = (eid >= gsel * EXPERTS_PER_GROUP) & (eid < (gsel + 1) * EXPERTS_PER_GROUP)
        el = jnp.where(ingrp, logits, -jnp.inf)
        v1 = jnp.max(el, axis=1, keepdims=True)
        i1 = jnp.min(jnp.where(el == v1, lane, big), axis=1, keepdims=True)
        el2 = jnp.where(lane == i1, -jnp.inf, el)
        v2 = jnp.max(el2, axis=1, keepdims=True)
        i2 = jnp.min(jnp.where(el2 == v2, lane, big), axis=1, keepdims=True)
        t = jnp.exp(v2 - v1)
        g1 = ggate / (1.0 + t)
        g2 = g1 * t
        e1 = (i1 - N_GROUPS).astype(F32)
        e2 = (i2 - N_GROUPS).astype(F32)
        route_ref[...] = jnp.where(lane == 0, g1, jnp.where(lane == 1, g2,
                                   jnp.where(lane == 2, e1, jnp.where(lane == 3, e2, 0.0))))
        oh_ref[...] = jnp.where(lane == i1 - N_GROUPS, 1.0,
                                jnp.where(lane == i2 - N_GROUPS, 1.0, 0.0)).astype(oh_ref.dtype)


def _out_ln_router(mg, wo, x, g, b, wr, br, tm, tn):
    m, d = x.shape
    tm, tn = min(tm, m), min(tn, d)
    nj = d // tn
    return pl.pallas_call(
        functools.partial(_out_kernel, tn=tn, nj=nj),
        grid=(m // tm, nj),
        in_specs=[pl.BlockSpec((tm, d), lambda i, j: (i, 0)),
                  pl.BlockSpec((d, tn), lambda i, j: (0, j)),
                  pl.BlockSpec((tm, tn), lambda i, j: (i, j)),
                  pl.BlockSpec((1, d), lambda i, j: (0, 0)),
                  pl.BlockSpec((1, d), lambda i, j: (0, 0)),
                  pl.BlockSpec((d, LANES), lambda i, j: (0, 0)),
                  pl.BlockSpec((1, LANES), lambda i, j: (0, 0))],
        out_specs=[pl.BlockSpec((tm, d), lambda i, j: (i, 0)),
                   pl.BlockSpec((tm, LANES), lambda i, j: (i, 0)),
                   pl.BlockSpec((tm, LANES), lambda i, j: (i, 0))],
        out_shape=[jax.ShapeDtypeStruct((m, d), F32),
                   jax.ShapeDtypeStruct((m, LANES), F32),
                   jax.ShapeDtypeStruct((m, LANES), BF16)],
        scratch_shapes=[pltpu.VMEM((nj, tm, tn), F32)],
        compiler_params=_cparams(("parallel", "arbitrary"), 56),
        name="outln",
    )(mg, wo, x, g, b, wr, br)


def _rank_kernel(oh_ref, pos_ref, cnt_ref, base_scr, *, tb):
    @pl.when(pl.program_id(0) == 0)
    def _():
        base_scr[...] = jnp.zeros(base_scr.shape, F32)

    oh = oh_ref[...]
    r = lax.broadcasted_iota(jnp.int32, (tb, tb), 0)
    c = lax.broadcasted_iota(jnp.int32, (tb, tb), 1)
    tri = jnp.where(c <= r, 1.0, 0.0).astype(BF16)
    cs = jnp.dot(tri, oh, preferred_element_type=F32)
    pos_ref[...] = cs - oh.astype(F32) + base_scr[0:1, :]
    base_scr[...] = base_scr[...] + cs[tb - 1:tb, :]
    cnt_ref[...] = base_scr[...]


def _rank(onehot, tb):
    m = onehot.shape[0]
    tb = min(tb, m)
    return pl.pallas_call(
        functools.partial(_rank_kernel, tb=tb),
        grid=(m // tb,),
        in_specs=[pl.BlockSpec((tb, LANES), lambda i: (i, 0))],
        out_specs=[pl.BlockSpec((tb, LANES), lambda i: (i, 0)),
                   pl.BlockSpec((8, LANES), lambda i: (0, 0))],
        out_shape=[jax.ShapeDtypeStruct((m, LANES), F32), jax.ShapeDtypeStruct((8, LANES), F32)],
        scratch_shapes=[pltpu.VMEM((8, LANES), F32)],
        compiler_params=_cparams(("arbitrary",), 32),
        name="rank",
    )(onehot)


def _dest_kernel(pos_ref, route_ref, start_ref, dest_ref):
    lane = lax.broadcasted_iota(jnp.int32, pos_ref.shape, 1).astype(F32)
    v = pos_ref[...] + start_ref[...]
    d1 = jnp.sum(jnp.where(lane == route_ref[:, 2:3], v, 0.0), axis=1, keepdims=True)
    d2 = jnp.sum(jnp.where(lane == route_ref[:, 3:4], v, 0.0), axis=1, keepdims=True)
    dest_ref[...] = jnp.where(lane == 0.0, d1, jnp.where(lane == 1.0, d2, 0.0)).astype(jnp.int32)


def _dest(pos, route, start, tb):
    m = pos.shape[0]
    tb = min(tb, m)
    return pl.pallas_call(
        _dest_kernel,
        grid=(m // tb,),
        in_specs=[pl.BlockSpec((tb, LANES), lambda i: (i, 0)),
                  pl.BlockSpec((tb, LANES), lambda i: (i, 0)),
                  pl.BlockSpec((1, LANES), lambda i: (0, 0))],
        out_specs=pl.BlockSpec((tb, LANES), lambda i: (i, 0)),
        out_shape=jax.ShapeDtypeStruct((m, LANES), jnp.int32),
        compiler_params=_cparams(("parallel",), 32),
        name="dest",
    )(pos, route, start)


def _pack_bf16_pair(left, right):
    lo = pltpu.bitcast(left.astype(BF16).astype(F32), jnp.int32)
    hi = pltpu.bitcast(right.astype(BF16).astype(F32), jnp.int32)
    return hi | lax.shift_right_logical(lo, 16)


def _unpack_bf16_pair(words):
    left = pltpu.bitcast(lax.shift_left(words, 16), F32)
    right = pltpu.bitcast(words & jnp.int32(-65536), F32)
    return left, right


def _scatter_kernel(dest_ref, h_ref, xs_in_ref, xs_ref, hp_scr, sem, *, tb):
    del xs_in_ref
    base = pl.program_id(0) * tb
    half = h_ref.shape[1] // 2
    hp_scr[...] = _pack_bf16_pair(h_ref[:, :half], h_ref[:, half:])

    def row_copy(t, slot):
        d = dest_ref[(base + t) * 2 + slot]
        return pltpu.make_async_copy(hp_scr.at[pl.ds(t, 1), :], xs_ref.at[pl.ds(d, 1), :], sem)

    def start(t, carry):
        row_copy(t, 0).start()
        row_copy(t, 1).start()
        return carry

    def wait(t, carry):
        row_copy(t, 0).wait()
        row_copy(t, 1).wait()
        return carry

    lax.fori_loop(0, tb, start, 0)
    lax.fori_loop(0, tb, wait, 0)


def _scatter_rows(dest_flat, h, n_rows, tb):
    m, d = h.shape
    tb = min(tb, m)
    xs0 = jnp.zeros((n_rows, d // 2), jnp.int32)
    return pl.pallas_call(
        functools.partial(_scatter_kernel, tb=tb),
        grid_spec=pltpu.PrefetchScalarGridSpec(
            num_scalar_prefetch=1,
            grid=(m // tb,),
            in_specs=[pl.BlockSpec((tb, d), lambda i, dest: (i, 0)),
                      pl.BlockSpec(memory_space=pl.ANY)],
            out_specs=pl.BlockSpec(memory_space=pl.ANY),
            scratch_shapes=[pltpu.VMEM((tb, d // 2), jnp.int32), pltpu.SemaphoreType.DMA(())]),
        out_shape=jax.ShapeDtypeStruct((n_rows, d // 2), jnp.int32),
        input_output_aliases={2: 0},
        compiler_params=_cparams(("arbitrary",), 32),
        name="scatter",
    )(dest_flat, h, xs0)


CAST_ROWS = 256


def _segment_tables(te, nv, end_tile, n_tiles):
    ids = jnp.arange(n_tiles, dtype=jnp.int32)
    fresh = jnp.concatenate([jnp.ones((1,), bool), te[1:] != te[:-1]]) & (ids < nv[0])
    seg = jnp.cumsum(fresh.astype(jnp.int32)) - 1
    after = end_tile[te]
    nxt = jnp.where(after < nv[0], te[jnp.minimum(after, n_tiles - 1)], -1)
    nseg = seg[jnp.maximum(nv[0] - 1, 0)][None] + 1
    return seg.astype(jnp.int32), nxt.astype(jnp.int32), nseg.astype(jnp.int32)


def _expert_weight_stream(te_ref, nv_ref, seg_ref, nxt_ref, nseg_ref, w_hbms, slabs, bf16_scrs, sem, width):
    c, r = pl.program_id(0), pl.program_id(1)
    valid = r < nv_ref[0]
    fresh = jnp.logical_and(valid, jnp.logical_or(r == 0, te_ref[r] != te_ref[jnp.maximum(r - 1, 0)]))
    ordinal = c * nseg_ref[0] + seg_ref[r]
    slot = ordinal % 2

    def slab_copies(expert, chunk, s):
        cols = pl.ds(pl.multiple_of(chunk * width, width), width)
        return [pltpu.make_async_copy(w.at[expert, :, cols], slab.at[s], sem.at[n, s])
                for n, (w, slab) in enumerate(zip(w_hbms, slabs))]

    def start(expert, chunk, s):
        for cp in slab_copies(expert, chunk, s):
            cp.start()

    @pl.when(jnp.logical_and(fresh, ordinal == 0))
    def _():
        start(te_ref[0], 0, 0)

    @pl.when(fresh)
    def _():
        for cp in slab_copies(te_ref[r], c, slot):
            cp.wait()
        for slab, scr in zip(slabs, bf16_scrs):
            def cast_rows(n, carry, slab=slab, scr=scr):
                rows = pl.ds(pl.multiple_of(n * CAST_ROWS, CAST_ROWS), CAST_ROWS)
                scr[rows, :] = slab[slot, rows, :].astype(BF16)
                return carry

            lax.fori_loop(0, scr.shape[0] // CAST_ROWS, cast_rows, 0)
        nxt = nxt_ref[r]

        @pl.when(nxt >= 0)
        def _():
            start(nxt, c, 1 - slot)

        @pl.when(jnp.logical_and(nxt < 0, c + 1 < pl.num_programs(0)))
        def _():
            start(te_ref[0], c + 1, 1 - slot)

    return valid


def _gm1_kernel(te_ref, nv_ref, seg_ref, nxt_ref, nseg_ref, xs_ref, w1_ref, w3_ref, o_ref,
                w1f_scr, w3f_scr, w1b_scr, w3b_scr, sem):
    valid = _expert_weight_stream(te_ref, nv_ref, seg_ref, nxt_ref, nseg_ref, (w1_ref, w3_ref),
                                  (w1f_scr, w3f_scr), (w1b_scr, w3b_scr), sem, w1b_scr.shape[1])

    @pl.when(valid)
    def _():
        left, right = _unpack_bf16_pair(xs_ref[...])
        x = jnp.concatenate([left.astype(BF16), right.astype(BF16)], axis=1)
        a = jnp.dot(x, w1b_scr[...], preferred_element_type=F32)
        b = jnp.dot(x, w3b_scr[...], preferred_element_type=F32)
        o_ref[...] = (a * jax.nn.sigmoid(a) * b).astype(o_ref.dtype)

    @pl.when(jnp.logical_not(valid))
    def _():
        o_ref[...] = jnp.zeros(o_ref.shape, o_ref.dtype)


def _gm1(tables, xs, w1, w3, tf):
    n_rows = xs.shape[0]
    d, f = w1.shape[1], w1.shape[2]
    nt = n_rows // ROW_TILE

    def rc(r, nv_):
        return jnp.minimum(r, nv_[0] - 1)

    return pl.pallas_call(
        _gm1_kernel,
        grid_spec=pltpu.PrefetchScalarGridSpec(
            num_scalar_prefetch=len(tables),
            grid=(f // tf, nt),
            in_specs=[pl.BlockSpec((ROW_TILE, d // 2), lambda c, r, te_, nv_, *_: (rc(r, nv_), 0)),
                      pl.BlockSpec(memory_space=pl.ANY),
                      pl.BlockSpec(memory_space=pl.ANY)],
            out_specs=pl.BlockSpec((ROW_TILE, tf), lambda c, r, *_: (r, c)),
            scratch_shapes=[pltpu.VMEM((2, d, tf), F32), pltpu.VMEM((2, d, tf), F32),
                            pltpu.VMEM((d, tf), BF16), pltpu.VMEM((d, tf), BF16),
                            pltpu.SemaphoreType.DMA((2, 2))]),
        out_shape=jax.ShapeDtypeStruct((n_rows, f), BF16),
        compiler_params=_cparams(("arbitrary", "arbitrary"), 56),
        name="gm1",
    )(*tables, xs, w1, w3)


def _gm2_kernel(te_ref, nv_ref, seg_ref, nxt_ref, nseg_ref, h_ref, w2_ref, o_ref, w2f_scr, w2b_scr, sem):
    valid = _expert_weight_stream(te_ref, nv_ref, seg_ref, nxt_ref, nseg_ref, (w2_ref,),
                                  (w2f_scr,), (w2b_scr,), sem, w2b_scr.shape[1])

    @pl.when(valid)
    def _():
        y = jnp.dot(h_ref[...], w2b_scr[...], preferred_element_type=F32)
        half = y.shape[1] // 2
        o_ref[...] = _pack_bf16_pair(y[:, :half], y[:, half:])

    @pl.when(jnp.logical_not(valid))
    def _():
        o_ref[...] = jnp.zeros(o_ref.shape, o_ref.dtype)


def _gm2(tables, hid, w2, tn):
    n_rows, f = hid.shape
    d = w2.shape[2]
    tn = min(tn, d)
    nt = n_rows // ROW_TILE

    def rc(r, nv_):
        return jnp.minimum(r, nv_[0] - 1)

    return pl.pallas_call(
        _gm2_kernel,
        grid_spec=pltpu.PrefetchScalarGridSpec(
            num_scalar_prefetch=len(tables),
            grid=(d // tn, nt),
            in_specs=[pl.BlockSpec((ROW_TILE, f), lambda c, r, te_, nv_, *_: (rc(r, nv_), 0)),
                      pl.BlockSpec(memory_space=pl.ANY)],
            out_specs=pl.BlockSpec((ROW_TILE, tn // 2), lambda c, r, *_: (r, c)),
            scratch_shapes=[pltpu.VMEM((2, f, tn), F32), pltpu.VMEM((f, tn), BF16),
                            pltpu.SemaphoreType.DMA((1, 2))]),
        out_shape=jax.ShapeDtypeStruct((n_rows, d // 2), jnp.int32),
        compiler_params=_cparams(("arbitrary", "arbitrary"), 48),
        name="gm2",
    )(*tables, hid, w2)


def _combine_kernel(dest_ref, y_ref, h_ref, route_ref, g_ref, b_ref, o_ref, ybuf, sem, *, tb, tn):
    i, nblk = pl.program_id(0), pl.num_programs(0)

    def row_copy(blk, buf, t, e):
        d = dest_ref[(blk * tb + t) * 2 + e]
        return pltpu.make_async_copy(y_ref.at[pl.ds(d, 1), :], ybuf.at[buf, e, pl.ds(t, 1), :], sem.at[buf])

    def fetch(blk, buf):
        def body(t, carry):
            row_copy(blk, buf, t, 0).start()
            row_copy(blk, buf, t, 1).start()
            return carry

        lax.fori_loop(0, tb, body, 0)

    @pl.when(i == 0)
    def _():
        fetch(0, 0)

    @pl.when(i + 1 < nblk)
    def _():
        fetch(i + 1, (i + 1) % 2)

    cur = i % 2

    def wait(t, carry):
        row_copy(i, cur, t, 0).wait()
        row_copy(i, cur, t, 1).wait()
        return carry

    lax.fori_loop(0, tb, wait, 0)
    half = tn // 2
    pieces = []
    for c in range(h_ref.shape[1] // tn):
        l0, r0 = _unpack_bf16_pair(ybuf[cur, 0, :, c * half:(c + 1) * half])
        l1, r1 = _unpack_bf16_pair(ybuf[cur, 1, :, c * half:(c + 1) * half])
        pieces += [route_ref[:, 0:1] * l0 + route_ref[:, 1:2] * l1, route_ref[:, 0:1] * r0 + route_ref[:, 1:2] * r1]
    moe = jnp.concatenate(pieces, axis=1)
    pre = DEEPNORM_ALPHA * h_ref[...] + moe
    mu = jnp.mean(pre, axis=1, keepdims=True)
    var = jnp.mean(jnp.square(pre - mu), axis=1, keepdims=True)
    o_ref[...] = (pre - mu) * lax.rsqrt(var + LN_EPS) * g_ref[...] + b_ref[...]


def _combine_ln(dest_flat, y, h, route, g, b, tb, tn):
    m, d = h.shape
    tb = min(tb, m)
    return pl.pallas_call(
        functools.partial(_combine_kernel, tb=tb, tn=tn),
        grid_spec=pltpu.PrefetchScalarGridSpec(
            num_scalar_prefetch=1,
            grid=(m // tb,),
            in_specs=[pl.BlockSpec(memory_space=pl.ANY),
                      pl.BlockSpec((tb, d), lambda i, dest: (i, 0)),
                      pl.BlockSpec((tb, LANES), lambda i, dest: (i, 0)),
                      pl.BlockSpec((1, d), lambda i, dest: (0, 0)),
                      pl.BlockSpec((1, d), lambda i, dest: (0, 0))],
            out_specs=pl.BlockSpec((tb, d), lambda i, dest: (i, 0)),
            scratch_shapes=[pltpu.VMEM((2, 2, tb, d // 2), jnp.int32), pltpu.SemaphoreType.DMA((2,))]),
        out_shape=jax.ShapeDtypeStruct((m, d), F32),
        compiler_params=_cparams(("arbitrary",), 40),
        name="combine",
    )(dest_flat, y, h, route, g, b)


def _alibi_slopes(n):
    return jnp.asarray(2.0 ** (-8.0 * np.arange(1, n + 1) / n), dtype=F32)


def kernel(x, w_in, lam_q1, lam_k1, lam_q2, lam_k2, diff_subln_g, w_pa, w_pb, w_o, ln1_g, ln1_b,
           router_wg, router_bg, router_we, router_be, w1, w3, w2, ln2_g, ln2_b):
    bsz, s_len, d = x.shape
    assert bsz == 1 and w_in.shape[0] == DEPTH
    topk = min(TOPK_MAX, s_len // 4)
    x2 = x[0]
    xb = x2.astype(BF16)

    qk_w = N_DIFF_HEADS * 2 * HEAD_DIM
    c_dq, c_dk, c_dv = 0, qk_w, 2 * qk_w
    c_sq = 3 * qk_w
    c_sk = c_sq + N_DSA_HEADS * HEAD_DIM
    c_sv = c_sk + N_DSA_KV * HEAD_DIM
    c_iq = c_sv + N_DSA_KV * HEAD_DIM
    c_ik = c_iq + N_IDX_HEADS * IDX_DIM
    c_ga = c_ik + IDX_DIM + N_IDX_HEADS
    qscale = HEAD_DIM ** -0.5 * LOG2E
    col = np.arange(c_ik)
    is_q = ((col >= c_dq) & (col < c_dk)) | ((col >= c_sq) & (col < c_sk))
    col_scale = jnp.asarray(np.where(is_q, qscale, 1.0)[None, :], F32)
    wt3d = jnp.swapaxes(w_in, 1, 2)
    wt = wt3d[0]
    wt_small = jnp.concatenate([wt[c_ik:c_ga], jnp.zeros((2 * LANES - IDX_DIM - N_IDX_HEADS, d), F32)],
                               axis=0).astype(BF16)
    wt_gates = _cast_rows(wt, c_ga, 2 * d, min(512, d))

    z = _proj_from_f32(xb, wt3d, col_scale, c_ik, 1024, 512)
    zs = _matmul_nt(xb, wt_small, F32, 1024, "proj_small")
    ik = zs[:, :IDX_DIM].astype(BF16)
    iwt = zs[:, IDX_DIM:IDX_DIM + N_IDX_HEADS].T
    dvt = _with_ones_rows(z[:, c_dv:c_dv + qk_w].T, N_DIFF_HEADS)
    svt = _with_ones_rows(z[:, c_sv:c_sv + N_DSA_KV * HEAD_DIM].T, N_DSA_KV)
    kaug = _key_aug_table(s_len)

    lam4 = jnp.stack([lam_q1[0], lam_k1[0], lam_q2[0], lam_k2[0]]).astype(F32)
    g_lanes = jnp.broadcast_to(diff_subln_g[0][:, None], (2 * HEAD_DIM, LANES))
    a = _diff_attention(z, dvt, kaug, lam4, g_lanes, _alibi_slopes(N_DIFF_HEADS), c_dq, c_dk, s_len,
                        min(512, s_len))

    tq_i, tk_i = min(256, s_len), min(512, s_len)
    maskb = _indexer(z, ik, iwt, c_iq, s_len, tq_i, tk_i, topk)
    b = _dsa_attention(z, svt, kaug, maskb, _alibi_slopes(N_DSA_HEADS), c_sq, c_sk, s_len, tq_i, tk_i)

    merged = _merge(xb, wt_gates, a, w_pa[0].astype(BF16), b, w_pb[0].astype(BF16), 512, 256)

    wr = jnp.concatenate([router_wg[0], router_we[0],
                          jnp.zeros((d, LANES - N_GROUPS - N_EXPERTS), F32)], axis=1)
    br = jnp.concatenate([router_bg[0], router_be[0],
                          jnp.zeros((LANES - N_GROUPS - N_EXPERTS,), F32)])[None, :]
    h1, route, onehot = _out_ln_router(merged, w_o[0].astype(BF16), x2, ln1_g[0][None, :], ln1_b[0][None, :],
                                       wr, br, 512, 512)

    pos, cnt = _rank(onehot, 512)
    counts = cnt[0, :N_EXPERTS].astype(jnp.int32)
    padded = ((counts + ROW_TILE - 1) // ROW_TILE) * ROW_TILE
    ends = jnp.cumsum(padded)
    start = jnp.zeros((1, LANES), F32).at[0, :N_EXPERTS].set((ends - padded).astype(F32))
    n_tiles = (2 * s_len) // ROW_TILE + N_EXPERTS
    tile_ids = jnp.arange(n_tiles, dtype=jnp.int32)
    te = jnp.minimum(jnp.sum(tile_ids[:, None] >= (ends // ROW_TILE)[None, :], axis=1), N_EXPERTS - 1)
    te = te.astype(jnp.int32)
    nv = (ends[-1] // ROW_TILE).astype(jnp.int32)[None]
    dest = _dest(pos, route, start, 512)
    dest_flat = dest[:, :2].reshape(-1)

    xs = _scatter_rows(dest_flat, h1, n_tiles * ROW_TILE, 256)
    tables = (te, nv) + _segment_tables(te, nv, ends // ROW_TILE, n_tiles)
    hid = _gm1(tables, xs, w1[0], w3[0], min(512, w1.shape[3]))
    tn_y = min(1024, d)
    y = _gm2(tables, hid, w2[0], tn_y)
    out = _combine_ln(dest_flat, y, h1, route, ln2_g[0][None, :], ln2_b[0][None, :], 128, tn_y)
    return out[None]
```

```python
import functools
import math

import numpy as np
import jax
import jax.numpy as jnp
from jax import lax
from jax.experimental import pallas as pl
from jax.experimental.pallas import tpu as pltpu

HEAD_DIM = 128
N_DIFF_HEADS = 8
N_DSA_HEADS = 16
N_DSA_KV = 4
DSA_REP = N_DSA_HEADS // N_DSA_KV
N_IDX_HEADS = 32
IDX_DIM = 128
TOPK_MAX = 256
N_GROUPS = 4
EXPERTS_PER_GROUP = 8
N_EXPERTS = N_GROUPS * EXPERTS_PER_GROUP
LN_EPS = 1e-5
RMS_EPS = 1e-5
DEPTH = 1
DEEPNORM_ALPHA = (2.0 * DEPTH) ** 0.25
LAM_INIT = 0.8 - 0.6 * math.exp(-0.3 * 0)

LANES = 128
SUBLANES = 8
NEG = -1e30
INT_MIN = -(2 ** 31)
ROW_TILE = 256
LOG2E = 1.4426950408889634
POS_RADIX = 256

F32 = jnp.float32
BF16 = jnp.bfloat16


def _cparams(sem, vmem_mb):
    return pltpu.CompilerParams(dimension_semantics=sem, vmem_limit_bytes=vmem_mb << 20)


def _dot_nt(a, b):
    return lax.dot_general(a, b, (((1,), (1,)), ((), ())), preferred_element_type=F32)


def _proj_kernel(x_ref, wt_ref, sc_ref, o_ref, wb_scr):
    @pl.when(pl.program_id(1) == 0)
    def _():
        wb_scr[...] = wt_ref[0].astype(BF16)

    o_ref[...] = (_dot_nt(x_ref[...], wb_scr[...]) * sc_ref[...]).astype(o_ref.dtype)


def _proj_from_f32(xb, wt3d, col_scale, n_cols, tm, tn):
    m, d = xb.shape
    tm = min(tm, m)
    return pl.pallas_call(
        _proj_kernel,
        grid=(n_cols // tn, m // tm),
        in_specs=[pl.BlockSpec((tm, d), lambda j, i: (i, 0)),
                  pl.BlockSpec((1, tn, d), lambda j, i: (0, j, 0)),
                  pl.BlockSpec((1, tn), lambda j, i: (0, j))],
        out_specs=pl.BlockSpec((tm, tn), lambda j, i: (i, j)),
        out_shape=jax.ShapeDtypeStruct((m, n_cols), BF16),
        scratch_shapes=[pltpu.VMEM((tn, d), BF16)],
        compiler_params=_cparams(("parallel", "arbitrary"), 48),
        name="proj_main",
    )(xb, wt3d, col_scale)


def _cast_rows_kernel(w_ref, o_ref):
    o_ref[...] = w_ref[...].astype(o_ref.dtype)


def _cast_rows(wt, row0, n_rows, tr):
    d = wt.shape[1]
    assert row0 % SUBLANES == 0 and n_rows % tr == 0
    return pl.pallas_call(
        _cast_rows_kernel,
        grid=(n_rows // tr,),
        in_specs=[pl.BlockSpec((pl.Element(tr), pl.Element(d)),
                               lambda i: (pl.multiple_of(row0 + i * tr, SUBLANES), 0))],
        out_specs=pl.BlockSpec((tr, d), lambda i: (i, 0)),
        out_shape=jax.ShapeDtypeStruct((n_rows, d), BF16),
        compiler_params=_cparams(("parallel",), 48),
        name="cast_gates",
    )(wt)


def _mm_nt_kernel(a_ref, bt_ref, o_ref):
    o_ref[...] = _dot_nt(a_ref[...], bt_ref[...]).astype(o_ref.dtype)


def _matmul_nt(a, bt, out_dtype, tm, name):
    m, k = a.shape
    n = bt.shape[0]
    tm = min(tm, m)
    return pl.pallas_call(
        _mm_nt_kernel,
        grid=(m // tm,),
        in_specs=[pl.BlockSpec((tm, k), lambda i: (i, 0)),
                  pl.BlockSpec((n, k), lambda i: (0, 0))],
        out_specs=pl.BlockSpec((tm, n), lambda i: (i, 0)),
        out_shape=jax.ShapeDtypeStruct((m, n), out_dtype),
        compiler_params=_cparams(("parallel",), 48),
        name=name,
    )(a, bt)


def _key_aug_table(s_len):
    pos = jnp.arange(s_len, dtype=jnp.int32)[:, None]
    lane = jnp.arange(LANES, dtype=jnp.int32)[None, :]
    hi = (pos // POS_RADIX).astype(F32)
    lo = (pos % POS_RADIX).astype(F32)
    t = jnp.where(lane < 2, hi, jnp.where(lane < 4, lo, jnp.where(lane < 7, 1.0, 0.0)))
    return t.astype(BF16)


def _bf16_piece(x):
    return x.astype(BF16).astype(F32)


def _query_aug(slope, qbase, rows):
    s2 = jnp.full((SUBLANES, LANES), slope, F32) * LOG2E
    big = s2 * POS_RADIX
    off = -s2 * jnp.full((SUBLANES, LANES), qbase, F32)
    big_hi = _bf16_piece(big)
    s2_hi = _bf16_piece(s2)
    off_hi = _bf16_piece(off)
    off_mid = _bf16_piece(off - off_hi)
    lane = lax.broadcasted_iota(jnp.int32, (SUBLANES, LANES), 1)
    pieces = [big_hi, big - big_hi, s2_hi, s2 - s2_hi, off_hi, off_mid, off - off_hi - off_mid]
    row = jnp.zeros((SUBLANES, LANES), F32)
    for n, piece in enumerate(pieces):
        row = jnp.where(lane == n, piece, row)
    return jnp.broadcast_to(row[0:1, :], (rows, LANES)).astype(BF16)


QUERY_SUB = 256


def _causal_steps(nq, last_of):
    pairs = [(i, j) for i in range(nq) for j in range(last_of(i) + 1)]
    return (jnp.asarray([p[0] for p in pairs], jnp.int32), jnp.asarray([p[1] for p in pairs], jnp.int32))


ONES_ROWS = 16


def _with_ones_rows(vt, n_blocks):
    r = vt.shape[0] // n_blocks
    v3 = vt.reshape(n_blocks, r, vt.shape[1])
    ones = jnp.ones((n_blocks, ONES_ROWS, vt.shape[1]), vt.dtype)
    return jnp.concatenate([v3, ones], axis=1).reshape(n_blocks * (r + ONES_ROWS), vt.shape[1])


def _online_softmax_pv(s, vt1, m_scr, acc_scr, idx, cols):
    at = idx + (slice(None), cols)
    m_prev = m_scr[at]
    m_next = jnp.maximum(m_prev, jnp.max(s, axis=0, keepdims=True))
    p = jnp.exp2((s - m_next).astype(BF16))
    alpha = jnp.exp2(m_prev - m_next)
    m_scr[at] = m_next
    acc_scr[at] = acc_scr[at] * alpha + jnp.dot(vt1, p, preferred_element_type=F32)


DIFF_HEADS_PER_STEP = 2


def _diff_kernel(qi_ref, kj_ref, slopes_ref, lam_ref, g_ref, q_ref, k_ref, ka_ref, vt_ref, o_ref,
                 qa_scr, kk_scr, s0_scr, s1_scr, m_scr, acc_scr, *, tq, n_steps):
    hp, n = pl.program_id(0), pl.program_id(1)
    reps = tq // LANES
    nsub = tq // QUERY_SUB
    w = 2 * HEAD_DIM
    maps = [(hh, c) for hh in range(DIFF_HEADS_PER_STEP) for c in range(2)]
    na = jnp.minimum(n, n_steps - 1)
    nb = jnp.maximum(n - 1, 0)
    ia = qi_ref[na]
    ib, jb = qi_ref[nb], kj_ref[nb]
    has_a, has_b = n < n_steps, n >= 1
    diag_b = jnp.logical_and(has_b, jb == ib)

    score_slots = (s0_scr, s1_scr)

    def map_cols(hh, c):
        return slice(hh * w + c * HEAD_DIM, hh * w + (c + 1) * HEAD_DIM)

    def load_queries():
        for hh in range(DIFF_HEADS_PER_STEP):
            qaug = _query_aug(slopes_ref[hp * DIFF_HEADS_PER_STEP + hh], (ia * tq).astype(F32), tq)
            for c in range(2):
                qa_scr[2 * hh + c, :, :HEAD_DIM] = q_ref[:, map_cols(hh, c)]
                qa_scr[2 * hh + c, :, HEAD_DIM:] = qaug

    def form_scores(slot):
        ka = ka_ref[...]
        for hh, c in maps:
            kk_scr[2 * hh + c, :, :HEAD_DIM] = k_ref[:, map_cols(hh, c)]
            kk_scr[2 * hh + c, :, HEAD_DIM:] = ka
        for m in range(len(maps)):
            score_slots[slot][m] = _dot_nt(kk_scr[m], qa_scr[m])

    def softmax_pv(slot, masked):
        for m, (hh, c) in enumerate(maps):
            vt1 = vt_ref[hh * (w + ONES_ROWS):(hh + 1) * (w + ONES_ROWS), :]
            for u in range(nsub):
                cols = slice(u * QUERY_SUB, (u + 1) * QUERY_SUB)
                s = score_slots[slot][m, :, cols]
                if masked:
                    keep = (lax.broadcasted_iota(jnp.int32, (tq, 1), 0)
                            <= u * QUERY_SUB + lax.broadcasted_iota(jnp.int32, (1, QUERY_SUB), 1))
                    s = jnp.where(keep, s, NEG)
                _online_softmax_pv(s, vt1, m_scr, acc_scr, (m,), cols)

    @pl.when(jnp.logical_and(has_b, jb == 0))
    def _():
        m_scr[...] = jnp.full(m_scr.shape, NEG, F32)
        acc_scr[...] = jnp.zeros(acc_scr.shape, F32)

    def emit_block():
        lam = (jnp.exp(jnp.sum(lam_ref[0:1, :] * lam_ref[1:2, :], axis=1, keepdims=True))
               - jnp.exp(jnp.sum(lam_ref[2:3, :] * lam_ref[3:4, :], axis=1, keepdims=True)) + LAM_INIT)
        g = jnp.concatenate([g_ref[...]] * reps, axis=1)
        for hh in range(DIFF_HEADS_PER_STEP):
            m = 2 * hh
            o = (acc_scr[m, :w, :] / acc_scr[m, w:w + 1, :]
                 - lam * (acc_scr[m + 1, :w, :] / acc_scr[m + 1, w:w + 1, :]))
            o = o * lax.rsqrt(jnp.mean(o * o, axis=0, keepdims=True) + RMS_EPS) * g
            o_ref[:, hh * w:(hh + 1) * w] = (o * (1.0 - LAM_INIT)).T.astype(o_ref.dtype)

    for par in range(2):
        mine = (n % 2) == par

        @pl.when(jnp.logical_and(mine, jnp.logical_and(has_b, jnp.logical_not(diag_b))))
        def _(par=par):
            form_scores(par)
            softmax_pv(1 - par, False)

        @pl.when(jnp.logical_and(mine, diag_b))
        def _(par=par):
            softmax_pv(1 - par, True)
            emit_block()

        @pl.when(jnp.logical_and(mine, jnp.logical_and(has_a, jnp.logical_or(n == 0, diag_b))))
        def _(par=par):
            load_queries()
            form_scores(par)


def _diff_attention(z, vt, kaug, lam4, g_lanes, slopes, col_q, col_k, s_len, tq):
    nq = s_len // tq
    w = 2 * HEAD_DIM
    wb = DIFF_HEADS_PER_STEP * w
    nmaps = 2 * DIFF_HEADS_PER_STEP
    qb, kb = col_q // wb, col_k // wb
    qi, kj = _causal_steps(nq, lambda i: i)
    n_steps = int(qi.shape[0])

    def ahead(s):
        return jnp.minimum(s, n_steps - 1)

    def behind(s):
        return jnp.maximum(s - 1, 0)

    return pl.pallas_call(
        functools.partial(_diff_kernel, tq=tq, n_steps=n_steps),
        grid_spec=pltpu.PrefetchScalarGridSpec(
            num_scalar_prefetch=3,
            grid=(N_DIFF_HEADS // DIFF_HEADS_PER_STEP, n_steps + 1),
            in_specs=[pl.BlockSpec((4, HEAD_DIM), lambda h, s, qi_, kj_, sl_: (0, 0)),
                      pl.BlockSpec((w, LANES), lambda h, s, qi_, kj_, sl_: (0, 0)),
                      pl.BlockSpec((tq, wb), lambda h, s, qi_, kj_, sl_: (qi_[ahead(s)], qb + h)),
                      pl.BlockSpec((tq, wb), lambda h, s, qi_, kj_, sl_: (kj_[ahead(s)], kb + h)),
                      pl.BlockSpec((tq, LANES), lambda h, s, qi_, kj_, sl_: (kj_[ahead(s)], 0)),
                      pl.BlockSpec((DIFF_HEADS_PER_STEP * (w + ONES_ROWS), tq),
                                   lambda h, s, qi_, kj_, sl_: (h, kj_[behind(s)]))],
            out_specs=pl.BlockSpec((tq, wb), lambda h, s, qi_, kj_, sl_: (qi_[behind(s)], h)),
            scratch_shapes=[pltpu.VMEM((nmaps, tq, w), BF16), pltpu.VMEM((nmaps, tq, w), BF16),
                            pltpu.VMEM((nmaps, tq, tq), F32), pltpu.VMEM((nmaps, tq, tq), F32),
                            pltpu.VMEM((nmaps, 1, tq), F32),
                            pltpu.VMEM((nmaps, w + ONES_ROWS, tq), F32)]),
        out_shape=jax.ShapeDtypeStruct((s_len, N_DIFF_HEADS * w), BF16),
        compiler_params=_cparams(("parallel", "arbitrary"), 48),
        name="diffattn",
    )(qi, kj, slopes, lam4, g_lanes, z, z, kaug, vt)


IQ_SPLIT = 4


HALF_BITS = 16
HALF_MASK = (1 << HALF_BITS) - 1
I16_MIN = -(1 << (HALF_BITS - 1))


def _indexer_kernel(iq0_ref, iq1_ref, iq2_ref, iq3_ref, ik_ref, iwt_ref, o_ref,
                    iqh_scr, key_scr, hi_scr, lo_scr, acc_scr, pcut_scr, *, tq, tkc, topk, iw_scale, pos_bits):
    i = pl.program_id(0)
    nk = key_scr.shape[0]
    nchunks = (i * tq + tq - 1) // tkc + 1
    per = N_IDX_HEADS // IQ_SPLIT

    for h in range(N_IDX_HEADS):
        src = (iq0_ref, iq1_ref, iq2_ref, iq3_ref)[h // per]
        iqh_scr[h] = src[:, (h % per) * IDX_DIM:(h % per + 1) * IDX_DIM]

    qpos = i * tq + lax.broadcasted_iota(jnp.int32, (1, tq), 1)

    def key_pos(c):
        return c * tkc + lax.broadcasted_iota(jnp.int32, (tkc, 1), 0)

    def count_keys(pred):
        def body(c, acc):
            x = jnp.where(pred(c, key_scr[c]), 1, 0)
            return acc + jnp.sum(x.reshape(tkc // SUBLANES, SUBLANES, tq), axis=0)

        cnt = lax.fori_loop(0, nchunks, body, jnp.zeros((SUBLANES, tq), jnp.int32))
        return jnp.sum(cnt, axis=0, keepdims=True)

    def chunk(c, carry):
        kc = ik_ref[pl.ds(pl.multiple_of(c * tkc, tkc), tkc), :]
        acc_scr[...] = jnp.zeros(acc_scr.shape, F32)

        def head(h, carry2):
            sc = _dot_nt(kc, iqh_scr[h])
            acc_scr[...] += (iwt_ref[pl.ds(h, 1), :] * iw_scale) * jnp.maximum(sc, 0.0)
            return carry2

        lax.fori_loop(0, N_IDX_HEADS, head, 0, unroll=16)
        score = acc_scr[...]
        score = jnp.where(score == 0.0, 0.0, score)
        bits = pltpu.bitcast(score, jnp.int32)
        skey = bits ^ ((bits >> 31) & 0x7FFFFFFF)
        key = jnp.where(key_pos(c) <= qpos, skey, INT_MIN)
        key_scr[c] = key
        hi_scr[c] = (key >> HALF_BITS).astype(jnp.int16)
        lo_scr[c] = ((key & HALF_MASK) + I16_MIN).astype(jnp.int16)
        return carry

    lax.fori_loop(0, nchunks, chunk, 0)

    def count_halves(half_scr, pred):
        rows = 2 * SUBLANES

        def body(c, acc):
            x = jnp.where(pred(half_scr[c]), jnp.int16(1), jnp.int16(0))
            for g in range(tkc // rows):
                acc = acc + x[g * rows:(g + 1) * rows, :]
            return acc

        cnt = lax.fori_loop(0, nchunks, body, jnp.zeros((rows, tq), jnp.int16))
        return jnp.sum(cnt.astype(jnp.int32), axis=0, keepdims=True)

    def radix_select(half_scr, want):
        def bit_step(bi, u):
            cand = u | lax.shift_left(jnp.int32(1), HALF_BITS - 1 - bi)
            cand16 = (cand + I16_MIN).astype(jnp.int16)
            return jnp.where(count_halves(half_scr, lambda k: k >= cand16) >= want, cand, u)

        return lax.fori_loop(0, HALF_BITS, bit_step, jnp.zeros((1, tq), jnp.int32)) + I16_MIN

    t_hi = radix_select(hi_scr, topk)
    t_hi16 = t_hi.astype(jnp.int16)
    want_lo = topk - count_halves(hi_scr, lambda k: k > t_hi16)

    def keep_lo(c, carry):
        lo_scr[c] = jnp.where(hi_scr[c] == t_hi16, lo_scr[c], jnp.int16(I16_MIN))
        return carry

    lax.fori_loop(0, nchunks, keep_lo, 0)
    t_lo = radix_select(lo_scr, want_lo)
    thr = lax.shift_left(t_hi, HALF_BITS) | (t_lo - I16_MIN)
    thr = jnp.maximum(thr, INT_MIN + 1)

    pcut_scr[...] = jnp.full(pcut_scr.shape, (1 << pos_bits) - 1, jnp.int32)

    @pl.when(jnp.max(count_keys(lambda c, k: k >= thr)) > topk)
    def _():
        need = topk - count_keys(lambda c, k: k > thr)

        def pos_step(bi, p):
            cand = p | lax.shift_left(jnp.int32(1), pos_bits - 1 - bi)
            tied_before = count_keys(lambda c, k: jnp.where(k == thr, key_pos(c), cand) < cand)
            return jnp.where(tied_before < need, cand, p)

        pcut_scr[...] = lax.fori_loop(0, pos_bits, pos_step, jnp.zeros((1, tq), jnp.int32))

    pcut = pcut_scr[...]

    def emit(c, carry):
        k = key_scr[c]
        tie_bias = jnp.where(key_pos(c) <= pcut, 0.0, NEG)
        o_ref[0, c] = jnp.where(k > thr, 0.0, jnp.where(k == thr, tie_bias, NEG)).astype(o_ref.dtype)
        return carry

    lax.fori_loop(0, nchunks, emit, 0)

    def fill(c, carry):
        o_ref[0, c] = jnp.full((tkc, tq), NEG, o_ref.dtype)
        return carry

    lax.fori_loop(nchunks, nk, fill, 0)


def _indexer(z, ik, iwt, col_iq, s_len, tq, tkc, topk):
    nq, nk = s_len // tq, s_len // tkc
    iw_scale = N_IDX_HEADS ** -0.5 * IDX_DIM ** -0.5
    wq = N_IDX_HEADS * IDX_DIM // IQ_SPLIT
    qb = col_iq // wq
    iq_specs = [pl.BlockSpec((tq, wq), functools.partial(lambda i, n: (i, qb + n), n=n)) for n in range(IQ_SPLIT)]
    return pl.pallas_call(
        functools.partial(_indexer_kernel, tq=tq, tkc=tkc, topk=topk, iw_scale=iw_scale,
                          pos_bits=max(1, (s_len - 1).bit_length())),
        grid=(nq,),
        in_specs=iq_specs + [pl.BlockSpec((s_len, IDX_DIM), lambda i: (0, 0)),
                             pl.BlockSpec((N_IDX_HEADS, tq), lambda i: (0, i))],
        out_specs=pl.BlockSpec((1, nk, tkc, tq), lambda i: (i, 0, 0, 0)),
        out_shape=jax.ShapeDtypeStruct((nq, nk, tkc, tq), BF16),
        scratch_shapes=[pltpu.VMEM((N_IDX_HEADS, tq, IDX_DIM), BF16),
                        pltpu.VMEM((nk, tkc, tq), jnp.int32),
                        pltpu.VMEM((nk, tkc, tq), jnp.int16),
                        pltpu.VMEM((nk, tkc, tq), jnp.int16),
                        pltpu.VMEM((tkc, tq), F32),
                        pltpu.VMEM((1, tq), jnp.int32)],
        compiler_params=_cparams(("parallel",), 48),
        name="indexer",
    )(z, z, z, z, ik, iwt)


DSA_GROUPS_PER_STEP = 2


def _dsa_kernel(qi_ref, kj_ref, slopes_ref, q_ref, k_ref, ka_ref, vt_ref, mb_ref, o_ref,
                qa_scr, kk_scr, m_scr, acc_scr, *, tq, tk):
    gp, step_id = pl.program_id(0), pl.program_id(1)
    i, j = qi_ref[step_id], kj_ref[step_id]
    last = (i * tq + tq - 1) // tk
    heads = DSA_GROUPS_PER_STEP * DSA_REP

    @pl.when(j == 0)
    def _():
        m_scr[...] = jnp.full(m_scr.shape, NEG, F32)
        acc_scr[...] = jnp.zeros(acc_scr.shape, F32)
        for n in range(heads):
            rows = slice(n * tq, (n + 1) * tq)
            qa_scr[rows, :HEAD_DIM] = q_ref[:, n * HEAD_DIM:(n + 1) * HEAD_DIM]
            qa_scr[rows, HEAD_DIM:] = _query_aug(slopes_ref[gp * heads + n], (i * tq).astype(F32), tq)

    ka = ka_ref[...]
    for gg in range(DSA_GROUPS_PER_STEP):
        kk_scr[gg, :, :HEAD_DIM] = k_ref[:, gg * HEAD_DIM:(gg + 1) * HEAD_DIM]
        kk_scr[gg, :, HEAD_DIM:] = ka
    mb = mb_ref[0, 0].astype(F32)
    scores = [_dot_nt(kk_scr[gg], qa_scr[gg * DSA_REP * tq:(gg + 1) * DSA_REP * tq, :])
              for gg in range(DSA_GROUPS_PER_STEP)]
    for gg in range(DSA_GROUPS_PER_STEP):
        vt1 = vt_ref[gg * (HEAD_DIM + ONES_ROWS):(gg + 1) * (HEAD_DIM + ONES_ROWS), :]
        for r in range(DSA_REP):
            n = gg * DSA_REP + r
            _online_softmax_pv(scores[gg][:, r * tq:(r + 1) * tq] + mb, vt1, m_scr, acc_scr, (),
                               slice(n * tq, (n + 1) * tq))

    @pl.when(j == last)
    def _():
        o = acc_scr[:HEAD_DIM, :] / acc_scr[HEAD_DIM:HEAD_DIM + 1, :]
        for n in range(heads):
            o_ref[:, n * HEAD_DIM:(n + 1) * HEAD_DIM] = o[:, n * tq:(n + 1) * tq].T.astype(o_ref.dtype)


def _dsa_attention(z, vt, kaug, maskb, slopes, col_q, col_k, s_len, tq, tk):
    nq = s_len // tq
    gps = DSA_GROUPS_PER_STEP
    heads = gps * DSA_REP
    qw, kw = heads * HEAD_DIM, gps * HEAD_DIM
    qb, kb = col_q // qw, col_k // kw
    qi, kj = _causal_steps(nq, lambda i: (i * tq + tq - 1) // tk)
    return pl.pallas_call(
        functools.partial(_dsa_kernel, tq=tq, tk=tk),
        grid_spec=pltpu.PrefetchScalarGridSpec(
            num_scalar_prefetch=3,
            grid=(N_DSA_KV // gps, qi.shape[0]),
            in_specs=[pl.BlockSpec((tq, qw), lambda g, s, qi_, kj_, sl_: (qi_[s], qb + g)),
                      pl.BlockSpec((tk, kw), lambda g, s, qi_, kj_, sl_: (kj_[s], kb + g)),
                      pl.BlockSpec((tk, LANES), lambda g, s, qi_, kj_, sl_: (kj_[s], 0)),
                      pl.BlockSpec((gps * (HEAD_DIM + ONES_ROWS), tk), lambda g, s, qi_, kj_, sl_: (g, kj_[s])),
                      pl.BlockSpec((1, 1, tk, tq), lambda g, s, qi_, kj_, sl_: (qi_[s], kj_[s], 0, 0))],
            out_specs=pl.BlockSpec((tq, qw), lambda g, s, qi_, kj_, sl_: (qi_[s], g)),
            scratch_shapes=[pltpu.VMEM((heads * tq, 2 * HEAD_DIM), BF16),
                            pltpu.VMEM((gps, tk, 2 * HEAD_DIM), BF16),
                            pltpu.VMEM((1, heads * tq), F32),
                            pltpu.VMEM((HEAD_DIM + ONES_ROWS, heads * tq), F32)]),
        out_shape=jax.ShapeDtypeStruct((s_len, N_DSA_HEADS * HEAD_DIM), BF16),
        compiler_params=_cparams(("parallel", "arbitrary"), 40),
        name="dsa",
    )(qi, kj, slopes, z, z, kaug, vt, maskb)


def _merge_kernel(x_ref, wga_ref, wgb_ref, a_ref, wpa_ref, b_ref, wpb_ref, o_ref):
    x = x_ref[...]
    ga = _dot_nt(x, wga_ref[...])
    gb = _dot_nt(x, wgb_ref[...])
    pa = jnp.dot(a_ref[...], wpa_ref[...], preferred_element_type=F32)
    pb = jnp.dot(b_ref[...], wpb_ref[...], preferred_element_type=F32)
    o_ref[...] = (jax.nn.sigmoid(ga) * pa + jax.nn.sigmoid(gb) * pb).astype(o_ref.dtype)


def _merge(xb, wt_gates, a, wpa, b, wpb, tm, tn):
    m, d = xb.shape
    ka, kb = a.shape[1], b.shape[1]
    tm, tn = min(tm, m), min(tn, d)
    row = lambda i, j: (i, 0)
    col = lambda i, j: (0, j)
    return pl.pallas_call(
        _merge_kernel,
        grid=(m // tm, d // tn),
        in_specs=[pl.BlockSpec((tm, d), row),
                  pl.BlockSpec((tn, d), lambda i, j: (j, 0)), pl.BlockSpec((tn, d), lambda i, j: (d // tn + j, 0)),
                  pl.BlockSpec((tm, ka), row), pl.BlockSpec((ka, tn), col),
                  pl.BlockSpec((tm, kb), row), pl.BlockSpec((kb, tn), col)],
        out_specs=pl.BlockSpec((tm, tn), lambda i, j: (i, j)),
        out_shape=jax.ShapeDtypeStruct((m, d), BF16),
        compiler_params=_cparams(("parallel", "arbitrary"), 52),
        name="merge",
    )(xb, wt_gates, wt_gates, a, wpa, b, wpb)


def _split_bf16(x):
    hi = x.astype(BF16)
    lo = (x - hi.astype(F32)).astype(BF16)
    return hi, lo


def _out_kernel(mg_ref, wo_ref, x_ref, g_ref, b_ref, wr_ref, br_ref,
                h_ref, route_ref, oh_ref, pre_scr, *, tn, nj):
    j = pl.program_id(1)
    y = jnp.dot(mg_ref[...], wo_ref[...], preferred_element_type=F32)
    pre_scr[j] = DEEPNORM_ALPHA * x_ref[...] + y

    @pl.when(j == nj - 1)
    def _():
        d = nj * tn
        tot = pre_scr[0].sum(axis=1, keepdims=True)
        for jj in range(1, nj):
            tot = tot + pre_scr[jj].sum(axis=1, keepdims=True)
        mu = tot / d
        sq = jnp.square(pre_scr[0] - mu).sum(axis=1, keepdims=True)
        for jj in range(1, nj):
            sq = sq + jnp.square(pre_scr[jj] - mu).sum(axis=1, keepdims=True)
        rstd = lax.rsqrt(sq / d + LN_EPS)
        logits = jnp.zeros(route_ref.shape, F32)
        for jj in range(nj):
            cs = slice(jj * tn, (jj + 1) * tn)
            hn = (pre_scr[jj] - mu) * rstd * g_ref[:, cs] + b_ref[:, cs]
            h_ref[:, cs] = hn
            h_hi, h_lo = _split_bf16(hn)
            w_hi, w_lo = _split_bf16(wr_ref[cs, :])
            logits = logits + (jnp.dot(h_hi, w_hi, preferred_element_type=F32)
                               + jnp.dot(h_hi, w_lo, preferred_element_type=F32)
                               + jnp.dot(h_lo, w_hi, preferred_element_type=F32))
        logits = logits + br_ref[...]
        lane = lax.broadcasted_iota(jnp.int32, logits.shape, 1)
        big = jnp.int32(4 * LANES)
        gl = jnp.where(lane < N_GROUPS, logits, -jnp.inf)
        gmax = jnp.max(gl, axis=1, keepdims=True)
        gsel = jnp.min(jnp.where(gl == gmax, lane, big), axis=1, keepdims=True)
        ggate = 1.0 / jnp.sum(jnp.exp(gl - gmax), axis=1, keepdims=True)
        eid = lane - N_GROUPS
        ingrp = (eid >= gsel * EXPERTS_PER_GROUP) & (eid < (gsel + 1) * EXPERTS_PER_GROUP)
        el = jnp.where(ingrp, logits, -jnp.inf)
        v1 = jnp.max(el, axis=1, keepdims=True)
        i1 = jnp.min(jnp.where(el == v1, lane, big), axis=1, keepdims=True)
        el2 = jnp.where(lane == i1, -jnp.inf, el)
        v2 = jnp.max(el2, axis=1, keepdims=True)
        i2 = jnp.min(jnp.where(el2 == v2, lane, big), axis=1, keepdims=True)
        t = jnp.exp(v2 - v1)
        g1 = ggate / (1.0 + t)
        g2 = g1 * t
        e1 = (i1 - N_GROUPS).astype(F32)
        e2 = (i2 - N_GROUPS).astype(F32)
        route_ref[...] = jnp.where(lane == 0, g1, jnp.where(lane == 1, g2,
                                   jnp.where(lane == 2, e1, jnp.where(lane == 3, e2, 0.0))))
        oh_ref[...] = jnp.where(lane == i1 - N_GROUPS, 1.0,
                                jnp.where(lane == i2 - N_GROUPS, 1.0, 0.0)).astype(oh_ref.dtype)


def _out_ln_router(mg, wo, x, g, b, wr, br, tm, tn):
    m, d = x.shape
    tm, tn = min(tm, m), min(tn, d)
    nj = d // tn
    return pl.pallas_call(
        functools.partial(_out_kernel, tn=tn, nj=nj),
        grid=(m // tm, nj),
        in_specs=[pl.BlockSpec((tm, d), lambda i, j: (i, 0)),
                  pl.BlockSpec((d, tn), lambda i, j: (0, j)),
                  pl.BlockSpec((tm, tn), lambda i, j: (i, j)),
                  pl.BlockSpec((1, d), lambda i, j: (0, 0)),
                  pl.BlockSpec((1, d), lambda i, j: (0, 0)),
                  pl.BlockSpec((d, LANES), lambda i, j: (0, 0)),
                  pl.BlockSpec((1, LANES), lambda i, j: (0, 0))],
        out_specs=[pl.BlockSpec((tm, d), lambda i, j: (i, 0)),
                   pl.BlockSpec((tm, LANES), lambda i, j: (i, 0)),
                   pl.BlockSpec((tm, LANES), lambda i, j: (i, 0))],
        out_shape=[jax.ShapeDtypeStruct((m, d), F32),
                   jax.ShapeDtypeStruct((m, LANES), F32),
                   jax.ShapeDtypeStruct((m, LANES), BF16)],
        scratch_shapes=[pltpu.VMEM((nj, tm, tn), F32)],
        compiler_params=_cparams(("parallel", "arbitrary"), 56),
        name="outln",
    )(mg, wo, x, g, b, wr, br)


def _rank_kernel(oh_ref, pos_ref, cnt_ref, base_scr, *, tb):
    @pl.when(pl.program_id(0) == 0)
    def _():
        base_scr[...] = jnp.zeros(base_scr.shape, F32)

    oh = oh_ref[...]
    r = lax.broadcasted_iota(jnp.int32, (tb, tb), 0)
    c = lax.broadcasted_iota(jnp.int32, (tb, tb), 1)
    tri = jnp.where(c <= r, 1.0, 0.0).astype(BF16)
    cs = jnp.dot(tri, oh, preferred_element_type=F32)
    pos_ref[...] = cs - oh.astype(F32) + base_scr[0:1, :]
    base_scr[...] = base_scr[...] + cs[tb - 1:tb, :]
    cnt_ref[...] = base_scr[...]


def _rank(onehot, tb):
    m = onehot.shape[0]
    tb = min(tb, m)
    return pl.pallas_call(
        functools.partial(_rank_kernel, tb=tb),
        grid=(m // tb,),
        in_specs=[pl.BlockSpec((tb, LANES), lambda i: (i, 0))],
        out_specs=[pl.BlockSpec((tb, LANES), lambda i: (i, 0)),
                   pl.BlockSpec((8, LANES), lambda i: (0, 0))],
        out_shape=[jax.ShapeDtypeStruct((m, LANES), F32), jax.ShapeDtypeStruct((8, LANES), F32)],
        scratch_shapes=[pltpu.VMEM((8, LANES), F32)],
        compiler_params=_cparams(("arbitrary",), 32),
        name="rank",
    )(onehot)


def _dest_kernel(pos_ref, route_ref, start_ref, dest_ref):
    lane = lax.broadcasted_iota(jnp.int32, pos_ref.shape, 1).astype(F32)
    v = pos_ref[...] + start_ref[...]
    d1 = jnp.sum(jnp.where(lane == route_ref[:, 2:3], v, 0.0), axis=1, keepdims=True)
    d2 = jnp.sum(jnp.where(lane == route_ref[:, 3:4], v, 0.0), axis=1, keepdims=True)
    dest_ref[...] = jnp.where(lane == 0.0, d1, jnp.where(lane == 1.0, d2, 0.0)).astype(jnp.int32)


def _dest(pos, route, start, tb):
    m = pos.shape[0]
    tb = min(tb, m)
    return pl.pallas_call(
        _dest_kernel,
        grid=(m // tb,),
        in_specs=[pl.BlockSpec((tb, LANES), lambda i: (i, 0)),
                  pl.BlockSpec((tb, LANES), lambda i: (i, 0)),
                  pl.BlockSpec((1, LANES), lambda i: (0, 0))],
        out_specs=pl.BlockSpec((tb, LANES), lambda i: (i, 0)),
        out_shape=jax.ShapeDtypeStruct((m, LANES), jnp.int32),
        compiler_params=_cparams(("parallel",), 32),
        name="dest",
    )(pos, route, start)


DMA_ISSUE_UNROLL = 4


def _pack_bf16_pair(left, right):
    lo = pltpu.bitcast(left.astype(BF16).astype(F32), jnp.int32)
    hi = pltpu.bitcast(right.astype(BF16).astype(F32), jnp.int32)
    return hi | lax.shift_right_logical(lo, 16)


def _unpack_bf16_pair(words):
    left = pltpu.bitcast(lax.shift_left(words, 16), F32)
    right = pltpu.bitcast(words & jnp.int32(-65536), F32)
    return left, right


def _scatter_kernel(dest_ref, h_ref, xs_in_ref, xs_ref, hp_scr, sem, *, tb):
    del xs_in_ref
    base = pl.program_id(0) * tb
    half = h_ref.shape[1] // 2
    hp_scr[...] = _pack_bf16_pair(h_ref[:, :half], h_ref[:, half:])

    def row_copy(t, slot):
        d = dest_ref[(base + t) * 2 + slot]
        return pltpu.make_async_copy(hp_scr.at[pl.ds(t, 1), :], xs_ref.at[pl.ds(d, 1), :], sem)

    def start(t, carry):
        row_copy(t, 0).start()
        row_copy(t, 1).start()
        return carry

    lax.fori_loop(0, tb, start, 0, unroll=DMA_ISSUE_UNROLL)
    for _ in range(2):
        pltpu.make_async_copy(hp_scr, xs_ref.at[pl.ds(0, tb), :], sem).wait()


def _scatter_rows(dest_flat, h, n_rows, tb):
    m, d = h.shape
    tb = min(tb, m)
    xs0 = jnp.zeros((n_rows, d // 2), jnp.int32)
    return pl.pallas_call(
        functools.partial(_scatter_kernel, tb=tb),
        grid_spec=pltpu.PrefetchScalarGridSpec(
            num_scalar_prefetch=1,
            grid=(m // tb,),
            in_specs=[pl.BlockSpec((tb, d), lambda i, dest: (i, 0)),
                      pl.BlockSpec(memory_space=pl.ANY)],
            out_specs=pl.BlockSpec(memory_space=pl.ANY),
            scratch_shapes=[pltpu.VMEM((tb, d // 2), jnp.int32), pltpu.SemaphoreType.DMA(())]),
        out_shape=jax.ShapeDtypeStruct((n_rows, d // 2), jnp.int32),
        input_output_aliases={2: 0},
        compiler_params=_cparams(("arbitrary",), 32),
        name="scatter",
    )(dest_flat, h, xs0)


CAST_ROWS = 256


def _segment_tables(te, nv, end_tile, n_tiles):
    ids = jnp.arange(n_tiles, dtype=jnp.int32)
    fresh = jnp.concatenate([jnp.ones((1,), bool), te[1:] != te[:-1]]) & (ids < nv[0])
    seg = jnp.cumsum(fresh.astype(jnp.int32)) - 1
    after = end_tile[te]
    nxt = jnp.where(after < nv[0], te[jnp.minimum(after, n_tiles - 1)], -1)
    nseg = seg[jnp.maximum(nv[0] - 1, 0)][None] + 1
    return seg.astype(jnp.int32), nxt.astype(jnp.int32), nseg.astype(jnp.int32)


def _expert_weight_stream(te_ref, nv_ref, seg_ref, nxt_ref, nseg_ref, w_hbms, slabs, bf16_scrs, sem, width):
    c, r = pl.program_id(0), pl.program_id(1)
    valid = r < nv_ref[0]
    fresh = jnp.logical_and(valid, jnp.logical_or(r == 0, te_ref[r] != te_ref[jnp.maximum(r - 1, 0)]))
    ordinal = c * nseg_ref[0] + seg_ref[r]
    slot = ordinal % 2

    def slab_copies(expert, chunk, s):
        cols = pl.ds(pl.multiple_of(chunk * width, width), width)
        return [pltpu.make_async_copy(w.at[expert, :, cols], slab.at[s], sem.at[n, s])
                for n, (w, slab) in enumerate(zip(w_hbms, slabs))]

    def start(expert, chunk, s):
        for cp in slab_copies(expert, chunk, s):
            cp.start()

    @pl.when(jnp.logical_and(fresh, ordinal == 0))
    def _():
        start(te_ref[0], 0, 0)

    @pl.when(fresh)
    def _():
        for cp in slab_copies(te_ref[r], c, slot):
            cp.wait()
        for slab, scr in zip(slabs, bf16_scrs):
            def cast_rows(n, carry, slab=slab, scr=scr):
                rows = pl.ds(pl.multiple_of(n * CAST_ROWS, CAST_ROWS), CAST_ROWS)
                scr[rows, :] = slab[slot, rows, :].astype(BF16)
                return carry

            lax.fori_loop(0, scr.shape[0] // CAST_ROWS, cast_rows, 0)
        nxt = nxt_ref[r]

        @pl.when(nxt >= 0)
        def _():
            start(nxt, c, 1 - slot)

        @pl.when(jnp.logical_and(nxt < 0, c + 1 < pl.num_programs(0)))
        def _():
            start(te_ref[0], c + 1, 1 - slot)

    return valid


def _gm1_kernel(te_ref, nv_ref, seg_ref, nxt_ref, nseg_ref, xs_ref, w1_ref, w3_ref, o_ref,
                w1f_scr, w3f_scr, w1b_scr, w3b_scr, sem):
    valid = _expert_weight_stream(te_ref, nv_ref, seg_ref, nxt_ref, nseg_ref, (w1_ref, w3_ref),
                                  (w1f_scr, w3f_scr), (w1b_scr, w3b_scr), sem, w1b_scr.shape[1])

    @pl.when(valid)
    def _():
        left, right = _unpack_bf16_pair(xs_ref[...])
        x = jnp.concatenate([left.astype(BF16), right.astype(BF16)], axis=1)
        a = jnp.dot(x, w1b_scr[...], preferred_element_type=F32)
        b = jnp.dot(x, w3b_scr[...], preferred_element_type=F32)
        o_ref[...] = (a * jax.nn.sigmoid(a) * b).astype(o_ref.dtype)

    @pl.when(jnp.logical_not(valid))
    def _():
        o_ref[...] = jnp.zeros(o_ref.shape, o_ref.dtype)


def _gm1(tables, xs, w1, w3, tf):
    n_rows = xs.shape[0]
    d, f = w1.shape[1], w1.shape[2]
    nt = n_rows // ROW_TILE

    def rc(r, nv_):
        return jnp.minimum(r, nv_[0] - 1)

    return pl.pallas_call(
        _gm1_kernel,
        grid_spec=pltpu.PrefetchScalarGridSpec(
            num_scalar_prefetch=len(tables),
            grid=(f // tf, nt),
            in_specs=[pl.BlockSpec((ROW_TILE, d // 2), lambda c, r, te_, nv_, *_: (rc(r, nv_), 0)),
                      pl.BlockSpec(memory_space=pl.ANY),
                      pl.BlockSpec(memory_space=pl.ANY)],
            out_specs=pl.BlockSpec((ROW_TILE, tf), lambda c, r, *_: (r, c)),
            scratch_shapes=[pltpu.VMEM((2, d, tf), F32), pltpu.VMEM((2, d, tf), F32),
                            pltpu.VMEM((d, tf), BF16), pltpu.VMEM((d, tf), BF16),
                            pltpu.SemaphoreType.DMA((2, 2))]),
        out_shape=jax.ShapeDtypeStruct((n_rows, f), BF16),
        compiler_params=_cparams(("arbitrary", "arbitrary"), 56),
        name="gm1",
    )(*tables, xs, w1, w3)


def _gm2_kernel(te_ref, nv_ref, seg_ref, nxt_ref, nseg_ref, h_ref, w2_ref, o_ref, w2f_scr, w2b_scr, sem):
    valid = _expert_weight_stream(te_ref, nv_ref, seg_ref, nxt_ref, nseg_ref, (w2_ref,),
                                  (w2f_scr,), (w2b_scr,), sem, w2b_scr.shape[1])

    @pl.when(valid)
    def _():
        y = jnp.dot(h_ref[...], w2b_scr[...], preferred_element_type=F32)
        half = y.shape[1] // 2
        o_ref[...] = _pack_bf16_pair(y[:, :half], y[:, half:])

    @pl.when(jnp.logical_not(valid))
    def _():
        o_ref[...] = jnp.zeros(o_ref.shape, o_ref.dtype)


def _gm2(tables, hid, w2, tn):
    n_rows, f = hid.shape
    d = w2.shape[2]
    tn = min(tn, d)
    nt = n_rows // ROW_TILE

    def rc(r, nv_):
        return jnp.minimum(r, nv_[0] - 1)

    return pl.pallas_call(
        _gm2_kernel,
        grid_spec=pltpu.PrefetchScalarGridSpec(
            num_scalar_prefetch=len(tables),
            grid=(d // tn, nt),
            in_specs=[pl.BlockSpec((ROW_TILE, f), lambda c, r, te_, nv_, *_: (rc(r, nv_), 0)),
                      pl.BlockSpec(memory_space=pl.ANY)],
            out_specs=pl.BlockSpec((ROW_TILE, tn // 2), lambda c, r, *_: (r, c)),
            scratch_shapes=[pltpu.VMEM((2, f, tn), F32), pltpu.VMEM((f, tn), BF16),
                            pltpu.SemaphoreType.DMA((1, 2))]),
        out_shape=jax.ShapeDtypeStruct((n_rows, d // 2), jnp.int32),
        compiler_params=_cparams(("arbitrary", "arbitrary"), 48),
        name="gm2",
    )(*tables, hid, w2)


def _combine_kernel(dest_ref, y_ref, h_ref, route_ref, g_ref, b_ref, o_ref, ybuf, sem, *, tb, tn):
    i, nblk = pl.program_id(0), pl.num_programs(0)

    def row_copy(blk, buf, t, e):
        d = dest_ref[(blk * tb + t) * 2 + e]
        return pltpu.make_async_copy(y_ref.at[pl.ds(d, 1), :], ybuf.at[buf, e, pl.ds(t, 1), :], sem.at[buf])

    def fetch(blk, buf):
        def body(t, carry):
            row_copy(blk, buf, t, 0).start()
            row_copy(blk, buf, t, 1).start()
            return carry

        lax.fori_loop(0, tb, body, 0, unroll=DMA_ISSUE_UNROLL)

    @pl.when(i == 0)
    def _():
        fetch(0, 0)

    @pl.when(i + 1 < nblk)
    def _():
        fetch(i + 1, (i + 1) % 2)

    cur = i % 2

    for e in range(2):
        pltpu.make_async_copy(y_ref.at[pl.ds(0, tb), :], ybuf.at[cur, e], sem.at[cur]).wait()
    half = tn // 2
    pieces = []
    for c in range(h_ref.shape[1] // tn):
        l0, r0 = _unpack_bf16_pair(ybuf[cur, 0, :, c * half:(c + 1) * half])
        l1, r1 = _unpack_bf16_pair(ybuf[cur, 1, :, c * half:(c + 1) * half])
        pieces += [route_ref[:, 0:1] * l0 + route_ref[:, 1:2] * l1, route_ref[:, 0:1] * r0 + route_ref[:, 1:2] * r1]
    moe = jnp.concatenate(pieces, axis=1)
    pre = DEEPNORM_ALPHA * h_ref[...] + moe
    mu = jnp.mean(pre, axis=1, keepdims=True)
    var = jnp.mean(jnp.square(pre - mu), axis=1, keepdims=True)
    o_ref[...] = (pre - mu) * lax.rsqrt(var + LN_EPS) * g_ref[...] + b_ref[...]


def _combine_ln(dest_flat, y, h, route, g, b, tb, tn):
    m, d = h.shape
    tb = min(tb, m)
    return pl.pallas_call(
        functools.partial(_combine_kernel, tb=tb, tn=tn),
        grid_spec=pltpu.PrefetchScalarGridSpec(
            num_scalar_prefetch=1,
            grid=(m // tb,),
            in_specs=[pl.BlockSpec(memory_space=pl.ANY),
                      pl.BlockSpec((tb, d), lambda i, dest: (i, 0)),
                      pl.BlockSpec((tb, LANES), lambda i, dest: (i, 0)),
                      pl.BlockSpec((1, d), lambda i, dest: (0, 0)),
                      pl.BlockSpec((1, d), lambda i, dest: (0, 0))],
            out_specs=pl.BlockSpec((tb, d), lambda i, dest: (i, 0)),
            scratch_shapes=[pltpu.VMEM((2, 2, tb, d // 2), jnp.int32), pltpu.SemaphoreType.DMA((2,))]),
        out_shape=jax.ShapeDtypeStruct((m, d), F32),
        compiler_params=_cparams(("arbitrary",), 40),
        name="combine",
    )(dest_flat, y, h, route, g, b)


def _alibi_slopes(n):
    return jnp.asarray(2.0 ** (-8.0 * np.arange(1, n + 1) / n), dtype=F32)


def kernel(x, w_in, lam_q1, lam_k1, lam_q2, lam_k2, diff_subln_g, w_pa, w_pb, w_o, ln1_g, ln1_b,
           router_wg, router_bg, router_we, router_be, w1, w3, w2, ln2_g, ln2_b):
    bsz, s_len, d = x.shape
    assert bsz == 1 and w_in.shape[0] == DEPTH
    topk = min(TOPK_MAX, s_len // 4)
    x2 = x[0]
    xb = x2.astype(BF16)

    qk_w = N_DIFF_HEADS * 2 * HEAD_DIM
    c_dq, c_dk, c_dv = 0, qk_w, 2 * qk_w
    c_sq = 3 * qk_w
    c_sk = c_sq + N_DSA_HEADS * HEAD_DIM
    c_sv = c_sk + N_DSA_KV * HEAD_DIM
    c_iq = c_sv + N_DSA_KV * HEAD_DIM
    c_ik = c_iq + N_IDX_HEADS * IDX_DIM
    c_ga = c_ik + IDX_DIM + N_IDX_HEADS
    qscale = HEAD_DIM ** -0.5 * LOG2E
    col = np.arange(c_ik)
    is_q = ((col >= c_dq) & (col < c_dk)) | ((col >= c_sq) & (col < c_sk))
    col_scale = jnp.asarray(np.where(is_q, qscale, 1.0)[None, :], F32)
    wt3d = jnp.swapaxes(w_in, 1, 2)
    wt = wt3d[0]
    wt_small = jnp.concatenate([wt[c_ik:c_ga], jnp.zeros((2 * LANES - IDX_DIM - N_IDX_HEADS, d), F32)],
                               axis=0).astype(BF16)
    wt_gates = _cast_rows(wt, c_ga, 2 * d, min(512, d))

    z = _proj_from_f32(xb, wt3d, col_scale, c_ik, 1024, 512)
    zs = _matmul_nt(xb, wt_small, F32, 1024, "proj_small")
    ik = zs[:, :IDX_DIM].astype(BF16)
    iwt = zs[:, IDX_DIM:IDX_DIM + N_IDX_HEADS].T
    dvt = _with_ones_rows(z[:, c_dv:c_dv + qk_w].T, N_DIFF_HEADS)
    svt = _with_ones_rows(z[:, c_sv:c_sv + N_DSA_KV * HEAD_DIM].T, N_DSA_KV)
    kaug = _key_aug_table(s_len)

    lam4 = jnp.stack([lam_q1[0], lam_k1[0], lam_q2[0], lam_k2[0]]).astype(F32)
    g_lanes = jnp.broadcast_to(diff_subln_g[0][:, None], (2 * HEAD_DIM, LANES))
    a = _diff_attention(z, dvt, kaug, lam4, g_lanes, _alibi_slopes(N_DIFF_HEADS), c_dq, c_dk, s_len,
                        min(512, s_len))

    tq_i, tk_i = min(256, s_len), min(512, s_len)
    maskb = _indexer(z, ik, iwt, c_iq, s_len, tq_i, tk_i, topk)
    b = _dsa_attention(z, svt, kaug, maskb, _alibi_slopes(N_DSA_HEADS), c_sq, c_sk, s_len, tq_i, tk_i)

    merged = _merge(xb, wt_gates, a, w_pa[0].astype(BF16), b, w_pb[0].astype(BF16), 512, 256)

    wr = jnp.concatenate([router_wg[0], router_we[0],
                          jnp.zeros((d, LANES - N_GROUPS - N_EXPERTS), F32)], axis=1)
    br = jnp.concatenate([router_bg[0], router_be[0],
                          jnp.zeros((LANES - N_GROUPS - N_EXPERTS,), F32)])[None, :]
    h1, route, onehot = _out_ln_router(merged, w_o[0].astype(BF16), x2, ln1_g[0][None, :], ln1_b[0][None, :],
                                       wr, br, 512, 512)

    pos, cnt = _rank(onehot, 512)
    counts = cnt[0, :N_EXPERTS].astype(jnp.int32)
    padded = ((counts + ROW_TILE - 1) // ROW_TILE) * ROW_TILE
    ends = jnp.cumsum(padded)
    start = jnp.zeros((1, LANES), F32).at[0, :N_EXPERTS].set((ends - padded).astype(F32))
    n_tiles = (2 * s_len) // ROW_TILE + N_EXPERTS
    tile_ids = jnp.arange(n_tiles, dtype=jnp.int32)
    te = jnp.minimum(jnp.sum(tile_ids[:, None] >= (ends // ROW_TILE)[None, :], axis=1), N_EXPERTS - 1)
    te = te.astype(jnp.int32)
    nv = (ends[-1] // ROW_TILE).astype(jnp.int32)[None]
    dest = _dest(pos, route, start, 512)
    dest_flat = dest[:, :2].reshape(-1)

    xs = _scatter_rows(dest_flat, h1, n_tiles * ROW_TILE, 256)
    tables = (te, nv) + _segment_tables(te, nv, ends // ROW_TILE, n_tiles)
    hid = _gm1(tables, xs, w1[0], w3[0], min(512, w1.shape[3]))
    tn_y = min(1024, d)
    y = _gm2(tables, hid, w2[0], tn_y)
    out = _combine_ln(dest_flat, y, h1, route, ln2_g[0][None, :], ln2_b[0][None, :], 128, tn_y)
    return out[None]
```

```python
import functools
import math

import numpy as np
import jax
import jax.numpy as jnp
from jax import lax
from jax.experimental import pallas as pl
from jax.experimental.pallas import tpu as pltpu

HEAD_DIM = 128
N_DIFF_HEADS = 8
N_DSA_HEADS = 16
N_DSA_KV = 4
DSA_REP = N_DSA_HEADS // N_DSA_KV
N_IDX_HEADS = 32
IDX_DIM = 128
TOPK_MAX = 256
N_GROUPS = 4
EXPERTS_PER_GROUP = 8
N_EXPERTS = N_GROUPS * EXPERTS_PER_GROUP
LN_EPS = 1e-5
RMS_EPS = 1e-5
DEPTH = 1
DEEPNORM_ALPHA = (2.0 * DEPTH) ** 0.25
LAM_INIT = 0.8 - 0.6 * math.exp(-0.3 * 0)

LANES = 128
SUBLANES = 8
NEG = -1e30
INT_MIN = -(2 ** 31)
ROW_TILE = 256
LOG2E = 1.4426950408889634
POS_RADIX = 256

PROJ_TILE = (1024, 512)
ATTN_TILE = 512
DSA_TILE = (256, 512)
MERGE_TILE = (1024, 256)
OUT_TILE = (512, 512)
RANK_ROWS = 512
SCATTER_ROWS = 256
COMBINE_ROWS = 128
GM1_COLS = 512
GM2_COLS = 1024

F32 = jnp.float32
BF16 = jnp.bfloat16


def _cparams(sem, vmem_mb):
    return pltpu.CompilerParams(dimension_semantics=sem, vmem_limit_bytes=vmem_mb << 20)


def _dot_nt(a, b):
    return lax.dot_general(a, b, (((1,), (1,)), ((), ())), preferred_element_type=F32)


def _proj_kernel(x_ref, wt_ref, sc_ref, o_ref, wb_scr):
    @pl.when(pl.program_id(1) == 0)
    def _():
        wb_scr[...] = wt_ref[0].astype(BF16)

    o_ref[...] = (_dot_nt(x_ref[...], wb_scr[...]) * sc_ref[...]).astype(o_ref.dtype)


def _proj_from_f32(xb, wt3d, col_scale, n_cols, tm, tn):
    m, d = xb.shape
    tm = min(tm, m)
    return pl.pallas_call(
        _proj_kernel,
        grid=(n_cols // tn, m // tm),
        in_specs=[pl.BlockSpec((tm, d), lambda j, i: (i, 0)),
                  pl.BlockSpec((1, tn, d), lambda j, i: (0, j, 0)),
                  pl.BlockSpec((1, tn), lambda j, i: (0, j))],
        out_specs=pl.BlockSpec((tm, tn), lambda j, i: (i, j)),
        out_shape=jax.ShapeDtypeStruct((m, n_cols), BF16),
        scratch_shapes=[pltpu.VMEM((tn, d), BF16)],
        compiler_params=_cparams(("parallel", "arbitrary"), 48),
        name="proj_main",
    )(xb, wt3d, col_scale)


def _cast_rows_kernel(w_ref, o_ref):
    o_ref[...] = w_ref[...].astype(o_ref.dtype)


def _cast_rows(wt, row0, n_rows, tr):
    d = wt.shape[1]
    assert row0 % SUBLANES == 0 and n_rows % tr == 0
    return pl.pallas_call(
        _cast_rows_kernel,
        grid=(n_rows // tr,),
        in_specs=[pl.BlockSpec((pl.Element(tr), pl.Element(d)),
                               lambda i: (pl.multiple_of(row0 + i * tr, SUBLANES), 0))],
        out_specs=pl.BlockSpec((tr, d), lambda i: (i, 0)),
        out_shape=jax.ShapeDtypeStruct((n_rows, d), BF16),
        compiler_params=_cparams(("parallel",), 48),
        name="cast_gates",
    )(wt)


def _mm_nt_kernel(a_ref, bt_ref, o_ref):
    o_ref[...] = _dot_nt(a_ref[...], bt_ref[...]).astype(o_ref.dtype)


def _matmul_nt(a, bt, out_dtype, tm, name):
    m, k = a.shape
    n = bt.shape[0]
    tm = min(tm, m)
    return pl.pallas_call(
        _mm_nt_kernel,
        grid=(m // tm,),
        in_specs=[pl.BlockSpec((tm, k), lambda i: (i, 0)),
                  pl.BlockSpec((n, k), lambda i: (0, 0))],
        out_specs=pl.BlockSpec((tm, n), lambda i: (i, 0)),
        out_shape=jax.ShapeDtypeStruct((m, n), out_dtype),
        compiler_params=_cparams(("parallel",), 48),
        name=name,
    )(a, bt)


def _key_aug_table(s_len):
    pos = jnp.arange(s_len, dtype=jnp.int32)[:, None]
    lane = jnp.arange(LANES, dtype=jnp.int32)[None, :]
    hi = (pos // POS_RADIX).astype(F32)
    lo = (pos % POS_RADIX).astype(F32)
    t = jnp.where(lane < 2, hi, jnp.where(lane < 4, lo, jnp.where(lane < 7, 1.0, 0.0)))
    return t.astype(BF16)


def _bf16_piece(x):
    return x.astype(BF16).astype(F32)


def _query_aug(slope, qbase, rows):
    s2 = jnp.full((SUBLANES, LANES), slope, F32) * LOG2E
    big = s2 * POS_RADIX
    off = -s2 * jnp.full((SUBLANES, LANES), qbase, F32)
    big_hi = _bf16_piece(big)
    s2_hi = _bf16_piece(s2)
    off_hi = _bf16_piece(off)
    off_mid = _bf16_piece(off - off_hi)
    lane = lax.broadcasted_iota(jnp.int32, (SUBLANES, LANES), 1)
    pieces = [big_hi, big - big_hi, s2_hi, s2 - s2_hi, off_hi, off_mid, off - off_hi - off_mid]
    row = jnp.zeros((SUBLANES, LANES), F32)
    for n, piece in enumerate(pieces):
        row = jnp.where(lane == n, piece, row)
    return jnp.broadcast_to(row[0:1, :], (rows, LANES)).astype(BF16)


QUERY_SUB = 256


def _causal_steps(nq, last_of):
    pairs = [(i, j) for i in range(nq) for j in range(last_of(i) + 1)]
    return (jnp.asarray([p[0] for p in pairs], jnp.int32), jnp.asarray([p[1] for p in pairs], jnp.int32))


ONES_ROWS = 16


def _with_ones_rows(vt, n_blocks):
    r = vt.shape[0] // n_blocks
    v3 = vt.reshape(n_blocks, r, vt.shape[1])
    ones = jnp.ones((n_blocks, ONES_ROWS, vt.shape[1]), vt.dtype)
    return jnp.concatenate([v3, ones], axis=1).reshape(n_blocks * (r + ONES_ROWS), vt.shape[1])


def _online_softmax_pv(s, vt1, m_scr, acc_scr, idx, cols):
    at = idx + (slice(None), cols)
    m_prev = m_scr[at]
    m_next = jnp.maximum(m_prev, jnp.max(s, axis=0, keepdims=True))
    p = jnp.exp2((s - m_next).astype(BF16))
    alpha = jnp.exp2(m_prev - m_next)
    m_scr[at] = m_next
    acc_scr[at] = acc_scr[at] * alpha + jnp.dot(vt1, p, preferred_element_type=F32)


DIFF_HEADS_PER_STEP = 2


def _diff_kernel(qi_ref, kj_ref, slopes_ref, lam_ref, g_ref, q_ref, k_ref, ka_ref, vt_ref, o_ref,
                 qa_scr, kk_scr, s0_scr, s1_scr, m_scr, acc_scr, *, tq, n_steps):
    hp, n = pl.program_id(0), pl.program_id(1)
    reps = tq // LANES
    nsub = tq // QUERY_SUB
    w = 2 * HEAD_DIM
    maps = [(hh, c) for hh in range(DIFF_HEADS_PER_STEP) for c in range(2)]
    na = jnp.minimum(n, n_steps - 1)
    nb = jnp.maximum(n - 1, 0)
    ia = qi_ref[na]
    ib, jb = qi_ref[nb], kj_ref[nb]
    has_a, has_b = n < n_steps, n >= 1
    diag_b = jnp.logical_and(has_b, jb == ib)

    score_slots = (s0_scr, s1_scr)

    def map_cols(hh, c):
        return slice(hh * w + c * HEAD_DIM, hh * w + (c + 1) * HEAD_DIM)

    def load_queries():
        for hh in range(DIFF_HEADS_PER_STEP):
            qaug = _query_aug(slopes_ref[hp * DIFF_HEADS_PER_STEP + hh], (ia * tq).astype(F32), tq)
            for c in range(2):
                qa_scr[2 * hh + c, :, :HEAD_DIM] = q_ref[:, map_cols(hh, c)]
                qa_scr[2 * hh + c, :, HEAD_DIM:] = qaug

    def form_scores(slot):
        ka = ka_ref[...]
        for hh, c in maps:
            kk_scr[2 * hh + c, :, :HEAD_DIM] = k_ref[:, map_cols(hh, c)]
            kk_scr[2 * hh + c, :, HEAD_DIM:] = ka
        for m in range(len(maps)):
            score_slots[slot][m] = _dot_nt(kk_scr[m], qa_scr[m])

    def softmax_pv(slot, masked):
        for m, (hh, c) in enumerate(maps):
            vt1 = vt_ref[hh * (w + ONES_ROWS):(hh + 1) * (w + ONES_ROWS), :]
            for u in range(nsub):
                cols = slice(u * QUERY_SUB, (u + 1) * QUERY_SUB)
                s = score_slots[slot][m, :, cols]
                if masked:
                    keep = (lax.broadcasted_iota(jnp.int32, (tq, 1), 0)
                            <= u * QUERY_SUB + lax.broadcasted_iota(jnp.int32, (1, QUERY_SUB), 1))
                    s = jnp.where(keep, s, NEG)
                _online_softmax_pv(s, vt1, m_scr, acc_scr, (m,), cols)

    @pl.when(jnp.logical_and(has_b, jb == 0))
    def _():
        m_scr[...] = jnp.full(m_scr.shape, NEG, F32)
        acc_scr[...] = jnp.zeros(acc_scr.shape, F32)

    def emit_block():
        lam = (jnp.exp(jnp.sum(lam_ref[0:1, :] * lam_ref[1:2, :], axis=1, keepdims=True))
               - jnp.exp(jnp.sum(lam_ref[2:3, :] * lam_ref[3:4, :], axis=1, keepdims=True)) + LAM_INIT)
        g = jnp.concatenate([g_ref[...]] * reps, axis=1)
        for hh in range(DIFF_HEADS_PER_STEP):
            m = 2 * hh
            o = (acc_scr[m, :w, :] / acc_scr[m, w:w + 1, :]
                 - lam * (acc_scr[m + 1, :w, :] / acc_scr[m + 1, w:w + 1, :]))
            o = o * lax.rsqrt(jnp.mean(o * o, axis=0, keepdims=True) + RMS_EPS) * g
            o_ref[:, hh * w:(hh + 1) * w] = (o * (1.0 - LAM_INIT)).T.astype(o_ref.dtype)

    for par in range(2):
        mine = (n % 2) == par

        @pl.when(jnp.logical_and(mine, jnp.logical_and(has_b, jnp.logical_not(diag_b))))
        def _(par=par):
            form_scores(par)
            softmax_pv(1 - par, False)

        @pl.when(jnp.logical_and(mine, jnp.logical_and(diag_b, has_a)))
        def _(par=par):
            load_queries()
            form_scores(par)
            softmax_pv(1 - par, True)
            emit_block()

        @pl.when(jnp.logical_and(mine, jnp.logical_and(diag_b, jnp.logical_not(has_a))))
        def _(par=par):
            softmax_pv(1 - par, True)
            emit_block()

        @pl.when(jnp.logical_and(mine, n == 0))
        def _(par=par):
            load_queries()
            form_scores(par)


def _diff_attention(z, vt, kaug, lam4, g_lanes, slopes, col_q, col_k, s_len, tq):
    nq = s_len // tq
    w = 2 * HEAD_DIM
    wb = DIFF_HEADS_PER_STEP * w
    nmaps = 2 * DIFF_HEADS_PER_STEP
    qb, kb = col_q // wb, col_k // wb
    qi, kj = _causal_steps(nq, lambda i: i)
    n_steps = int(qi.shape[0])

    def ahead(s):
        return jnp.minimum(s, n_steps - 1)

    def behind(s):
        return jnp.maximum(s - 1, 0)

    return pl.pallas_call(
        functools.partial(_diff_kernel, tq=tq, n_steps=n_steps),
        grid_spec=pltpu.PrefetchScalarGridSpec(
            num_scalar_prefetch=3,
            grid=(N_DIFF_HEADS // DIFF_HEADS_PER_STEP, n_steps + 1),
            in_specs=[pl.BlockSpec((4, HEAD_DIM), lambda h, s, qi_, kj_, sl_: (0, 0)),
                      pl.BlockSpec((w, LANES), lambda h, s, qi_, kj_, sl_: (0, 0)),
                      pl.BlockSpec((tq, wb), lambda h, s, qi_, kj_, sl_: (qi_[ahead(s)], qb + h)),
                      pl.BlockSpec((tq, wb), lambda h, s, qi_, kj_, sl_: (kj_[ahead(s)], kb + h)),
                      pl.BlockSpec((tq, LANES), lambda h, s, qi_, kj_, sl_: (kj_[ahead(s)], 0)),
                      pl.BlockSpec((DIFF_HEADS_PER_STEP * (w + ONES_ROWS), tq),
                                   lambda h, s, qi_, kj_, sl_: (h, kj_[behind(s)]))],
            out_specs=pl.BlockSpec((tq, wb), lambda h, s, qi_, kj_, sl_: (qi_[behind(s)], h)),
            scratch_shapes=[pltpu.VMEM((nmaps, tq, w), BF16), pltpu.VMEM((nmaps, tq, w), BF16),
                            pltpu.VMEM((nmaps, tq, tq), F32), pltpu.VMEM((nmaps, tq, tq), F32),
                            pltpu.VMEM((nmaps, 1, tq), F32),
                            pltpu.VMEM((nmaps, w + ONES_ROWS, tq), F32)]),
        out_shape=jax.ShapeDtypeStruct((s_len, N_DIFF_HEADS * w), BF16),
        compiler_params=_cparams(("parallel", "arbitrary"), 48),
        name="diffattn",
    )(qi, kj, slopes, lam4, g_lanes, z, z, kaug, vt)


IQ_SPLIT = 4


HALF_BITS = 16
HALF_MASK = (1 << HALF_BITS) - 1
I16_MIN = -(1 << (HALF_BITS - 1))


def _indexer_kernel(iq0_ref, iq1_ref, iq2_ref, iq3_ref, ik_ref, iwt_ref, o_ref,
                    iqh_scr, key_scr, hi_scr, lo_scr, acc_scr, pcut_scr, *, tq, tkc, topk, iw_scale, pos_bits):
    i = pl.program_id(0)
    nk = key_scr.shape[0]
    nchunks = (i * tq + tq - 1) // tkc + 1
    per = N_IDX_HEADS // IQ_SPLIT

    for h in range(N_IDX_HEADS):
        src = (iq0_ref, iq1_ref, iq2_ref, iq3_ref)[h // per]
        iqh_scr[h] = src[:, (h % per) * IDX_DIM:(h % per + 1) * IDX_DIM]

    qpos = i * tq + lax.broadcasted_iota(jnp.int32, (1, tq), 1)

    def key_pos(c):
        return c * tkc + lax.broadcasted_iota(jnp.int32, (tkc, 1), 0)

    def count_keys(pred):
        def body(c, acc):
            x = jnp.where(pred(c, key_scr[c]), 1, 0)
            return acc + jnp.sum(x.reshape(tkc // SUBLANES, SUBLANES, tq), axis=0)

        cnt = lax.fori_loop(0, nchunks, body, jnp.zeros((SUBLANES, tq), jnp.int32))
        return jnp.sum(cnt, axis=0, keepdims=True)

    def chunk(c, carry):
        kc = ik_ref[pl.ds(pl.multiple_of(c * tkc, tkc), tkc), :]
        acc_scr[...] = jnp.zeros(acc_scr.shape, F32)

        def head(h, carry2):
            sc = _dot_nt(kc, iqh_scr[h])
            acc_scr[...] += (iwt_ref[pl.ds(h, 1), :] * iw_scale) * jnp.maximum(sc, 0.0)
            return carry2

        lax.fori_loop(0, N_IDX_HEADS, head, 0, unroll=16)
        score = acc_scr[...]
        score = jnp.where(score == 0.0, 0.0, score)
        bits = pltpu.bitcast(score, jnp.int32)
        skey = bits ^ ((bits >> 31) & 0x7FFFFFFF)
        key = jnp.where(key_pos(c) <= qpos, skey, INT_MIN)
        key_scr[c] = key
        hi_scr[c] = (key >> HALF_BITS).astype(jnp.int16)
        lo_scr[c] = ((key & HALF_MASK) + I16_MIN).astype(jnp.int16)
        return carry

    lax.fori_loop(0, nchunks, chunk, 0)

    def count_halves(half_scr, pred):
        rows = 2 * SUBLANES

        def body(c, acc):
            x = jnp.where(pred(half_scr[c]), jnp.int16(1), jnp.int16(0))
            for g in range(tkc // rows):
                acc = acc + x[g * rows:(g + 1) * rows, :]
            return acc

        cnt = lax.fori_loop(0, nchunks, body, jnp.zeros((rows, tq), jnp.int16))
        return jnp.sum(cnt.astype(jnp.int32), axis=0, keepdims=True)

    def radix_select(half_scr, want):
        def bit_step(bi, u):
            cand = u | lax.shift_left(jnp.int32(1), HALF_BITS - 1 - bi)
            cand16 = (cand + I16_MIN).astype(jnp.int16)
            return jnp.where(count_halves(half_scr, lambda k: k >= cand16) >= want, cand, u)

        return lax.fori_loop(0, HALF_BITS, bit_step, jnp.zeros((1, tq), jnp.int32)) + I16_MIN

    t_hi = radix_select(hi_scr, topk)
    t_hi16 = t_hi.astype(jnp.int16)
    want_lo = topk - count_halves(hi_scr, lambda k: k > t_hi16)

    def keep_lo(c, carry):
        lo_scr[c] = jnp.where(hi_scr[c] == t_hi16, lo_scr[c], jnp.int16(I16_MIN))
        return carry

    lax.fori_loop(0, nchunks, keep_lo, 0)
    t_lo = radix_select(lo_scr, want_lo)
    thr = lax.shift_left(t_hi, HALF_BITS) | (t_lo - I16_MIN)
    thr = jnp.maximum(thr, INT_MIN + 1)

    pcut_scr[...] = jnp.full(pcut_scr.shape, (1 << pos_bits) - 1, jnp.int32)

    @pl.when(jnp.max(count_keys(lambda c, k: k >= thr)) > topk)
    def _():
        need = topk - count_keys(lambda c, k: k > thr)

        def pos_step(bi, p):
            cand = p | lax.shift_left(jnp.int32(1), pos_bits - 1 - bi)
            tied_before = count_keys(lambda c, k: jnp.where(k == thr, key_pos(c), cand) < cand)
            return jnp.where(tied_before < need, cand, p)

        pcut_scr[...] = lax.fori_loop(0, pos_bits, pos_step, jnp.zeros((1, tq), jnp.int32))

    pcut = pcut_scr[...]

    def emit(c, carry):
        k = key_scr[c]
        tie_bias = jnp.where(key_pos(c) <= pcut, 0.0, NEG)
        o_ref[0, c] = jnp.where(k > thr, 0.0, jnp.where(k == thr, tie_bias, NEG)).astype(o_ref.dtype)
        return carry

    lax.fori_loop(0, nchunks, emit, 0)

    def fill(c, carry):
        o_ref[0, c] = jnp.full((tkc, tq), NEG, o_ref.dtype)
        return carry

    lax.fori_loop(nchunks, nk, fill, 0)


def _indexer(z, ik, iwt, col_iq, s_len, tq, tkc, topk):
    nq, nk = s_len // tq, s_len // tkc
    iw_scale = N_IDX_HEADS ** -0.5 * IDX_DIM ** -0.5
    wq = N_IDX_HEADS * IDX_DIM // IQ_SPLIT
    qb = col_iq // wq
    iq_specs = [pl.BlockSpec((tq, wq), functools.partial(lambda i, n: (i, qb + n), n=n)) for n in range(IQ_SPLIT)]
    return pl.pallas_call(
        functools.partial(_indexer_kernel, tq=tq, tkc=tkc, topk=topk, iw_scale=iw_scale,
                          pos_bits=max(1, (s_len - 1).bit_length())),
        grid=(nq,),
        in_specs=iq_specs + [pl.BlockSpec((s_len, IDX_DIM), lambda i: (0, 0)),
                             pl.BlockSpec((N_IDX_HEADS, tq), lambda i: (0, i))],
        out_specs=pl.BlockSpec((1, nk, tkc, tq), lambda i: (i, 0, 0, 0)),
        out_shape=jax.ShapeDtypeStruct((nq, nk, tkc, tq), BF16),
        scratch_shapes=[pltpu.VMEM((N_IDX_HEADS, tq, IDX_DIM), BF16),
                        pltpu.VMEM((nk, tkc, tq), jnp.int32),
                        pltpu.VMEM((nk, tkc, tq), jnp.int16),
                        pltpu.VMEM((nk, tkc, tq), jnp.int16),
                        pltpu.VMEM((tkc, tq), F32),
                        pltpu.VMEM((1, tq), jnp.int32)],
        compiler_params=_cparams(("parallel",), 48),
        name="indexer",
    )(z, z, z, z, ik, iwt)


DSA_GROUPS_PER_STEP = 2


def _dsa_kernel(qi_ref, kj_ref, slopes_ref, q_ref, k_ref, ka_ref, vt_ref, mb_ref, o_ref,
                qa_scr, kk_scr, m_scr, acc_scr, *, tq, tk):
    gp, step_id = pl.program_id(0), pl.program_id(1)
    i, j = qi_ref[step_id], kj_ref[step_id]
    last = (i * tq + tq - 1) // tk
    heads = DSA_GROUPS_PER_STEP * DSA_REP

    @pl.when(j == 0)
    def _():
        m_scr[...] = jnp.full(m_scr.shape, NEG, F32)
        acc_scr[...] = jnp.zeros(acc_scr.shape, F32)
        for n in range(heads):
            rows = slice(n * tq, (n + 1) * tq)
            qa_scr[rows, :HEAD_DIM] = q_ref[:, n * HEAD_DIM:(n + 1) * HEAD_DIM]
            qa_scr[rows, HEAD_DIM:] = _query_aug(slopes_ref[gp * heads + n], (i * tq).astype(F32), tq)

    ka = ka_ref[...]
    for gg in range(DSA_GROUPS_PER_STEP):
        kk_scr[gg, :, :HEAD_DIM] = k_ref[:, gg * HEAD_DIM:(gg + 1) * HEAD_DIM]
        kk_scr[gg, :, HEAD_DIM:] = ka
    mb = mb_ref[0, 0].astype(F32)
    scores = [_dot_nt(kk_scr[gg], qa_scr[gg * DSA_REP * tq:(gg + 1) * DSA_REP * tq, :])
              for gg in range(DSA_GROUPS_PER_STEP)]
    for gg in range(DSA_GROUPS_PER_STEP):
        vt1 = vt_ref[gg * (HEAD_DIM + ONES_ROWS):(gg + 1) * (HEAD_DIM + ONES_ROWS), :]
        for r in range(DSA_REP):
            n = gg * DSA_REP + r
            _online_softmax_pv(scores[gg][:, r * tq:(r + 1) * tq] + mb, vt1, m_scr, acc_scr, (),
                               slice(n * tq, (n + 1) * tq))

    @pl.when(j == last)
    def _():
        o = acc_scr[:HEAD_DIM, :] / acc_scr[HEAD_DIM:HEAD_DIM + 1, :]
        for n in range(heads):
            o_ref[:, n * HEAD_DIM:(n + 1) * HEAD_DIM] = o[:, n * tq:(n + 1) * tq].T.astype(o_ref.dtype)


def _dsa_attention(z, vt, kaug, maskb, slopes, col_q, col_k, s_len, tq, tk):
    nq = s_len // tq
    gps = DSA_GROUPS_PER_STEP
    heads = gps * DSA_REP
    qw, kw = heads * HEAD_DIM, gps * HEAD_DIM
    qb, kb = col_q // qw, col_k // kw
    qi, kj = _causal_steps(nq, lambda i: (i * tq + tq - 1) // tk)
    return pl.pallas_call(
        functools.partial(_dsa_kernel, tq=tq, tk=tk),
        grid_spec=pltpu.PrefetchScalarGridSpec(
            num_scalar_prefetch=3,
            grid=(N_DSA_KV // gps, qi.shape[0]),
            in_specs=[pl.BlockSpec((tq, qw), lambda g, s, qi_, kj_, sl_: (qi_[s], qb + g)),
                      pl.BlockSpec((tk, kw), lambda g, s, qi_, kj_, sl_: (kj_[s], kb + g)),
                      pl.BlockSpec((tk, LANES), lambda g, s, qi_, kj_, sl_: (kj_[s], 0)),
                      pl.BlockSpec((gps * (HEAD_DIM + ONES_ROWS), tk), lambda g, s, qi_, kj_, sl_: (g, kj_[s])),
                      pl.BlockSpec((1, 1, tk, tq), lambda g, s, qi_, kj_, sl_: (qi_[s], kj_[s], 0, 0))],
            out_specs=pl.BlockSpec((tq, qw), lambda g, s, qi_, kj_, sl_: (qi_[s], g)),
            scratch_shapes=[pltpu.VMEM((heads * tq, 2 * HEAD_DIM), BF16),
                            pltpu.VMEM((gps, tk, 2 * HEAD_DIM), BF16),
                            pltpu.VMEM((1, heads * tq), F32),
                            pltpu.VMEM((HEAD_DIM + ONES_ROWS, heads * tq), F32)]),
        out_shape=jax.ShapeDtypeStruct((s_len, N_DSA_HEADS * HEAD_DIM), BF16),
        compiler_params=_cparams(("parallel", "arbitrary"), 40),
        name="dsa",
    )(qi, kj, slopes, z, z, kaug, vt, maskb)


def _merge_kernel(x_ref, wga_ref, wgb_ref, a_ref, wpa_ref, b_ref, wpb_ref, o_ref):
    x = x_ref[...]
    ga = _dot_nt(x, wga_ref[...])
    gb = _dot_nt(x, wgb_ref[...])
    pa = jnp.dot(a_ref[...], wpa_ref[...], preferred_element_type=F32)
    pb = jnp.dot(b_ref[...], wpb_ref[...], preferred_element_type=F32)
    o_ref[...] = (jax.nn.sigmoid(ga) * pa + jax.nn.sigmoid(gb) * pb).astype(o_ref.dtype)


def _merge(xb, wt_gates, a, wpa, b, wpb, tm, tn):
    m, d = xb.shape
    ka, kb = a.shape[1], b.shape[1]
    tm, tn = min(tm, m), min(tn, d)
    row = lambda i, j: (i, 0)
    col = lambda i, j: (0, j)
    return pl.pallas_call(
        _merge_kernel,
        grid=(m // tm, d // tn),
        in_specs=[pl.BlockSpec((tm, d), row),
                  pl.BlockSpec((tn, d), lambda i, j: (j, 0)), pl.BlockSpec((tn, d), lambda i, j: (d // tn + j, 0)),
                  pl.BlockSpec((tm, ka), row), pl.BlockSpec((ka, tn), col),
                  pl.BlockSpec((tm, kb), row), pl.BlockSpec((kb, tn), col)],
        out_specs=pl.BlockSpec((tm, tn), lambda i, j: (i, j)),
        out_shape=jax.ShapeDtypeStruct((m, d), BF16),
        compiler_params=_cparams(("parallel", "arbitrary"), 52),
        name="merge",
    )(xb, wt_gates, wt_gates, a, wpa, b, wpb)


def _split_bf16(x):
    hi = x.astype(BF16)
    lo = (x - hi.astype(F32)).astype(BF16)
    return hi, lo


def _out_kernel(mg_ref, wo_ref, x_ref, g_ref, b_ref, wr_ref, br_ref,
                h_ref, route_ref, oh_ref, pre_scr, *, tn, nj):
    j = pl.program_id(1)
    y = jnp.dot(mg_ref[...], wo_ref[...], preferred_element_type=F32)
    pre_scr[j] = DEEPNORM_ALPHA * x_ref[...] + y

    @pl.when(j == nj - 1)
    def _():
        d = nj * tn
        tot = pre_scr[0].sum(axis=1, keepdims=True)
        for jj in range(1, nj):
            tot = tot + pre_scr[jj].sum(axis=1, keepdims=True)
        mu = tot / d
        sq = jnp.square(pre_scr[0] - mu).sum(axis=1, keepdims=True)
        for jj in range(1, nj):
            sq = sq + jnp.square(pre_scr[jj] - mu).sum(axis=1, keepdims=True)
        rstd = lax.rsqrt(sq / d + LN_EPS)
        logits = jnp.zeros(route_ref.shape, F32)
        for jj in range(nj):
            cs = slice(jj * tn, (jj + 1) * tn)
            hn = (pre_scr[jj] - mu) * rstd * g_ref[:, cs] + b_ref[:, cs]
            h_ref[:, cs] = hn
            h_hi, h_lo = _split_bf16(hn)
            w_hi, w_lo = _split_bf16(wr_ref[cs, :])
            logits = logits + (jnp.dot(h_hi, w_hi, preferred_element_type=F32)
                               + jnp.dot(h_hi, w_lo, preferred_element_type=F32)
                               + jnp.dot(h_lo, w_hi, preferred_element_type=F32))
        logits = logits + br_ref[...]
        lane = lax.broadcasted_iota(jnp.int32, logits.shape, 1)
        big = jnp.int32(4 * LANES)
        gl = jnp.where(lane < N_GROUPS, logits, -jnp.inf)
        gmax = jnp.max(gl, axis=1, keepdims=True)
        gsel = jnp.min(jnp.where(gl == gmax, lane, big), axis=1, keepdims=True)
        ggate = 1.0 / jnp.sum(jnp.exp(gl - gmax), axis=1, keepdims=True)
        eid = lane - N_GROUPS
        ingrp = (eid >= gsel * EXPERTS_PER_GROUP) & (eid < (gsel + 1) * EXPERTS_PER_GROUP)
        el = jnp.where(ingrp, logits, -jnp.inf)
        v1 = jnp.max(el, axis=1, keepdims=True)
        i1 = jnp.min(jnp.where(el == v1, lane, big), axis=1, keepdims=True)
        el2 = jnp.where(lane == i1, -jnp.inf, el)
        v2 = jnp.max(el2, axis=1, keepdims=True)
        i2 = jnp.min(jnp.where(el2 == v2, lane, big), axis=1, keepdims=True)
        t = jnp.exp(v2 - v1)
        g1 = ggate / (1.0 + t)
        g2 = g1 * t
        e1 = (i1 - N_GROUPS).astype(F32)
        e2 = (i2 - N_GROUPS).astype(F32)
        route_ref[...] = jnp.where(lane == 0, g1, jnp.where(lane == 1, g2,
                                   jnp.where(lane == 2, e1, jnp.where(lane == 3, e2, 0.0))))
        oh_ref[...] = jnp.where(lane == i1 - N_GROUPS, 1.0,
                                jnp.where(lane == i2 - N_GROUPS, 1.0, 0.0)).astype(oh_ref.dtype)


def _out_ln_router(mg, wo, x, g, b, wr, br, tm, tn):
    m, d = x.shape
    tm, tn = min(tm, m), min(tn, d)
    nj = d // tn
    return pl.pallas_call(
        functools.partial(_out_kernel, tn=tn, nj=nj),
        grid=(m // tm, nj),
        in_specs=[pl.BlockSpec((tm, d), lambda i, j: (i, 0)),
                  pl.BlockSpec((d, tn), lambda i, j: (0, j)),
                  pl.BlockSpec((tm, tn), lambda i, j: (i, j)),
                  pl.BlockSpec((1, d), lambda i, j: (0, 0)),
                  pl.BlockSpec((1, d), lambda i, j: (0, 0)),
                  pl.BlockSpec((d, LANES), lambda i, j: (0, 0)),
                  pl.BlockSpec((1, LANES), lambda i, j: (0, 0))],
        out_specs=[pl.BlockSpec((tm, d), lambda i, j: (i, 0)),
                   pl.BlockSpec((tm, LANES), lambda i, j: (i, 0)),
                   pl.BlockSpec((tm, LANES), lambda i, j: (i, 0))],
        out_shape=[jax.ShapeDtypeStruct((m, d), F32),
                   jax.ShapeDtypeStruct((m, LANES), F32),
                   jax.ShapeDtypeStruct((m, LANES), BF16)],
        scratch_shapes=[pltpu.VMEM((nj, tm, tn), F32)],
        compiler_params=_cparams(("parallel", "arbitrary"), 56),
        name="outln",
    )(mg, wo, x, g, b, wr, br)


def _rank_kernel(oh_ref, pos_ref, cnt_ref, base_scr, *, tb):
    @pl.when(pl.program_id(0) == 0)
    def _():
        base_scr[...] = jnp.zeros(base_scr.shape, F32)

    oh = oh_ref[...]
    r = lax.broadcasted_iota(jnp.int32, (tb, tb), 0)
    c = lax.broadcasted_iota(jnp.int32, (tb, tb), 1)
    tri = jnp.where(c <= r, 1.0, 0.0).astype(BF16)
    cs = jnp.dot(tri, oh, preferred_element_type=F32)
    pos_ref[...] = cs - oh.astype(F32) + base_scr[0:1, :]
    base_scr[...] = base_scr[...] + cs[tb - 1:tb, :]
    cnt_ref[...] = base_scr[...]


def _rank(onehot, tb):
    m = onehot.shape[0]
    tb = min(tb, m)
    return pl.pallas_call(
        functools.partial(_rank_kernel, tb=tb),
        grid=(m // tb,),
        in_specs=[pl.BlockSpec((tb, LANES), lambda i: (i, 0))],
        out_specs=[pl.BlockSpec((tb, LANES), lambda i: (i, 0)),
                   pl.BlockSpec((8, LANES), lambda i: (0, 0))],
        out_shape=[jax.ShapeDtypeStruct((m, LANES), F32), jax.ShapeDtypeStruct((8, LANES), F32)],
        scratch_shapes=[pltpu.VMEM((8, LANES), F32)],
        compiler_params=_cparams(("arbitrary",), 32),
        name="rank",
    )(onehot)


def _dest_kernel(pos_ref, route_ref, start_ref, dest_ref):
    lane = lax.broadcasted_iota(jnp.int32, pos_ref.shape, 1).astype(F32)
    v = pos_ref[...] + start_ref[...]
    d1 = jnp.sum(jnp.where(lane == route_ref[:, 2:3], v, 0.0), axis=1, keepdims=True)
    d2 = jnp.sum(jnp.where(lane == route_ref[:, 3:4], v, 0.0), axis=1, keepdims=True)
    dest_ref[...] = jnp.where(lane == 0.0, d1, jnp.where(lane == 1.0, d2, 0.0)).astype(jnp.int32)


def _dest(pos, route, start, tb):
    m = pos.shape[0]
    tb = min(tb, m)
    return pl.pallas_call(
        _dest_kernel,
        grid=(m // tb,),
        in_specs=[pl.BlockSpec((tb, LANES), lambda i: (i, 0)),
                  pl.BlockSpec((tb, LANES), lambda i: (i, 0)),
                  pl.BlockSpec((1, LANES), lambda i: (0, 0))],
        out_specs=pl.BlockSpec((tb, LANES), lambda i: (i, 0)),
        out_shape=jax.ShapeDtypeStruct((m, LANES), jnp.int32),
        compiler_params=_cparams(("parallel",), 32),
        name="dest",
    )(pos, route, start)


DMA_ISSUE_UNROLL = 4


def _pack_bf16_pair(left, right):
    lo = pltpu.bitcast(left.astype(BF16).astype(F32), jnp.int32)
    hi = pltpu.bitcast(right.astype(BF16).astype(F32), jnp.int32)
    return hi | lax.shift_right_logical(lo, 16)


def _unpack_bf16_pair(words):
    left = pltpu.bitcast(lax.shift_left(words, 16), F32)
    right = pltpu.bitcast(words & jnp.int32(-65536), F32)
    return left, right


def _scatter_kernel(dest_ref, h_ref, xs_in_ref, xs_ref, hp_scr, sem, *, tb):
    del xs_in_ref
    base = pl.program_id(0) * tb
    half = h_ref.shape[1] // 2
    hp_scr[...] = _pack_bf16_pair(h_ref[:, :half], h_ref[:, half:])

    def row_copy(t, slot):
        d = dest_ref[(base + t) * 2 + slot]
        return pltpu.make_async_copy(hp_scr.at[pl.ds(t, 1), :], xs_ref.at[pl.ds(d, 1), :], sem)

    def start(t, carry):
        row_copy(t, 0).start()
        row_copy(t, 1).start()
        return carry

    lax.fori_loop(0, tb, start, 0, unroll=DMA_ISSUE_UNROLL)
    for _ in range(2):
        pltpu.make_async_copy(hp_scr, xs_ref.at[pl.ds(0, tb), :], sem).wait()


def _scatter_rows(dest_flat, h, n_rows, tb):
    m, d = h.shape
    tb = min(tb, m)
    xs0 = jnp.zeros((n_rows, d // 2), jnp.int32)
    return pl.pallas_call(
        functools.partial(_scatter_kernel, tb=tb),
        grid_spec=pltpu.PrefetchScalarGridSpec(
            num_scalar_prefetch=1,
            grid=(m // tb,),
            in_specs=[pl.BlockSpec((tb, d), lambda i, dest: (i, 0)),
                      pl.BlockSpec(memory_space=pl.ANY)],
            out_specs=pl.BlockSpec(memory_space=pl.ANY),
            scratch_shapes=[pltpu.VMEM((tb, d // 2), jnp.int32), pltpu.SemaphoreType.DMA(())]),
        out_shape=jax.ShapeDtypeStruct((n_rows, d // 2), jnp.int32),
        input_output_aliases={2: 0},
        compiler_params=_cparams(("arbitrary",), 32),
        name="scatter",
    )(dest_flat, h, xs0)


CAST_ROWS = 256


def _segment_tables(te, nv, end_tile, n_tiles):
    ids = jnp.arange(n_tiles, dtype=jnp.int32)
    fresh = jnp.concatenate([jnp.ones((1,), bool), te[1:] != te[:-1]]) & (ids < nv[0])
    seg = jnp.cumsum(fresh.astype(jnp.int32)) - 1
    after = end_tile[te]
    nxt = jnp.where(after < nv[0], te[jnp.minimum(after, n_tiles - 1)], -1)
    nseg = seg[jnp.maximum(nv[0] - 1, 0)][None] + 1
    return seg.astype(jnp.int32), nxt.astype(jnp.int32), nseg.astype(jnp.int32)


def _expert_weight_stream(te_ref, nv_ref, seg_ref, nxt_ref, nseg_ref, w_hbms, slabs, bf16_scrs, sem, width):
    c, r = pl.program_id(0), pl.program_id(1)
    valid = r < nv_ref[0]
    fresh = jnp.logical_and(valid, jnp.logical_or(r == 0, te_ref[r] != te_ref[jnp.maximum(r - 1, 0)]))
    ordinal = c * nseg_ref[0] + seg_ref[r]
    slot = ordinal % 2

    def slab_copies(expert, chunk, s):
        cols = pl.ds(pl.multiple_of(chunk * width, width), width)
        return [pltpu.make_async_copy(w.at[expert, :, cols], slab.at[s], sem.at[n, s])
                for n, (w, slab) in enumerate(zip(w_hbms, slabs))]

    def start(expert, chunk, s):
        for cp in slab_copies(expert, chunk, s):
            cp.start()

    @pl.when(jnp.logical_and(fresh, ordinal == 0))
    def _():
        start(te_ref[0], 0, 0)

    @pl.when(fresh)
    def _():
        for cp in slab_copies(te_ref[r], c, slot):
            cp.wait()
        for slab, scr in zip(slabs, bf16_scrs):
            def cast_rows(n, carry, slab=slab, scr=scr):
                rows = pl.ds(pl.multiple_of(n * CAST_ROWS, CAST_ROWS), CAST_ROWS)
                scr[rows, :] = slab[slot, rows, :].astype(BF16)
                return carry

            lax.fori_loop(0, scr.shape[0] // CAST_ROWS, cast_rows, 0)
        nxt = nxt_ref[r]

        @pl.when(nxt >= 0)
        def _():
            start(nxt, c, 1 - slot)

        @pl.when(jnp.logical_and(nxt < 0, c + 1 < pl.num_programs(0)))
        def _():
            start(te_ref[0], c + 1, 1 - slot)

    return valid


def _gm1_kernel(te_ref, nv_ref, seg_ref, nxt_ref, nseg_ref, xs_ref, w1_ref, w3_ref, o_ref,
                w1f_scr, w3f_scr, w1b_scr, w3b_scr, sem):
    valid = _expert_weight_stream(te_ref, nv_ref, seg_ref, nxt_ref, nseg_ref, (w1_ref, w3_ref),
                                  (w1f_scr, w3f_scr), (w1b_scr, w3b_scr), sem, w1b_scr.shape[1])

    @pl.when(valid)
    def _():
        left, right = _unpack_bf16_pair(xs_ref[...])
        x = jnp.concatenate([left.astype(BF16), right.astype(BF16)], axis=1)
        a = jnp.dot(x, w1b_scr[...], preferred_element_type=F32)
        b = jnp.dot(x, w3b_scr[...], preferred_element_type=F32)
        o_ref[...] = (a * jax.nn.sigmoid(a) * b).astype(o_ref.dtype)

    @pl.when(jnp.logical_not(valid))
    def _():
        o_ref[...] = jnp.zeros(o_ref.shape, o_ref.dtype)


def _gm1(tables, xs, w1, w3, tf):
    n_rows = xs.shape[0]
    d, f = w1.shape[1], w1.shape[2]
    nt = n_rows // ROW_TILE

    def rc(r, nv_):
        return jnp.minimum(r, nv_[0] - 1)

    return pl.pallas_call(
        _gm1_kernel,
        grid_spec=pltpu.PrefetchScalarGridSpec(
            num_scalar_prefetch=len(tables),
            grid=(f // tf, nt),
            in_specs=[pl.BlockSpec((ROW_TILE, d // 2), lambda c, r, te_, nv_, *_: (rc(r, nv_), 0)),
                      pl.BlockSpec(memory_space=pl.ANY),
                      pl.BlockSpec(memory_space=pl.ANY)],
            out_specs=pl.BlockSpec((ROW_TILE, tf), lambda c, r, *_: (r, c)),
            scratch_shapes=[pltpu.VMEM((2, d, tf), F32), pltpu.VMEM((2, d, tf), F32),
                            pltpu.VMEM((d, tf), BF16), pltpu.VMEM((d, tf), BF16),
                            pltpu.SemaphoreType.DMA((2, 2))]),
        out_shape=jax.ShapeDtypeStruct((n_rows, f), BF16),
        compiler_params=_cparams(("arbitrary", "arbitrary"), 56),
        name="gm1",
    )(*tables, xs, w1, w3)


def _gm2_kernel(te_ref, nv_ref, seg_ref, nxt_ref, nseg_ref, h_ref, w2_ref, o_ref, w2f_scr, w2b_scr, sem):
    valid = _expert_weight_stream(te_ref, nv_ref, seg_ref, nxt_ref, nseg_ref, (w2_ref,),
                                  (w2f_scr,), (w2b_scr,), sem, w2b_scr.shape[1])

    @pl.when(valid)
    def _():
        y = jnp.dot(h_ref[...], w2b_scr[...], preferred_element_type=F32)
        half = y.shape[1] // 2
        o_ref[...] = _pack_bf16_pair(y[:, :half], y[:, half:])

    @pl.when(jnp.logical_not(valid))
    def _():
        o_ref[...] = jnp.zeros(o_ref.shape, o_ref.dtype)


def _gm2(tables, hid, w2, tn):
    n_rows, f = hid.shape
    d = w2.shape[2]
    tn = min(tn, d)
    nt = n_rows // ROW_TILE

    def rc(r, nv_):
        return jnp.minimum(r, nv_[0] - 1)

    return pl.pallas_call(
        _gm2_kernel,
        grid_spec=pltpu.PrefetchScalarGridSpec(
            num_scalar_prefetch=len(tables),
            grid=(d // tn, nt),
            in_specs=[pl.BlockSpec((ROW_TILE, f), lambda c, r, te_, nv_, *_: (rc(r, nv_), 0)),
                      pl.BlockSpec(memory_space=pl.ANY)],
            out_specs=pl.BlockSpec((ROW_TILE, tn // 2), lambda c, r, *_: (r, c)),
            scratch_shapes=[pltpu.VMEM((2, f, tn), F32), pltpu.VMEM((f, tn), BF16),
                            pltpu.SemaphoreType.DMA((1, 2))]),
        out_shape=jax.ShapeDtypeStruct((n_rows, d // 2), jnp.int32),
        compiler_params=_cparams(("arbitrary", "arbitrary"), 48),
        name="gm2",
    )(*tables, hid, w2)


def _combine_kernel(dest_ref, y_ref, h_ref, route_ref, g_ref, b_ref, o_ref, ybuf, sem, *, tb, tn):
    i, nblk = pl.program_id(0), pl.num_programs(0)

    def row_copy(blk, buf, t, e):
        d = dest_ref[(blk * tb + t) * 2 + e]
        return pltpu.make_async_copy(y_ref.at[pl.ds(d, 1), :], ybuf.at[buf, e, pl.ds(t, 1), :], sem.at[buf])

    def fetch(blk, buf):
        def body(t, carry):
            row_copy(blk, buf, t, 0).start()
            row_copy(blk, buf, t, 1).start()
            return carry

        lax.fori_loop(0, tb, body, 0, unroll=DMA_ISSUE_UNROLL)

    @pl.when(i == 0)
    def _():
        fetch(0, 0)

    @pl.when(i + 1 < nblk)
    def _():
        fetch(i + 1, (i + 1) % 2)

    cur = i % 2

    for e in range(2):
        pltpu.make_async_copy(y_ref.at[pl.ds(0, tb), :], ybuf.at[cur, e], sem.at[cur]).wait()
    half = tn // 2
    pieces = []
    for c in range(h_ref.shape[1] // tn):
        l0, r0 = _unpack_bf16_pair(ybuf[cur, 0, :, c * half:(c + 1) * half])
        l1, r1 = _unpack_bf16_pair(ybuf[cur, 1, :, c * half:(c + 1) * half])
        pieces += [route_ref[:, 0:1] * l0 + route_ref[:, 1:2] * l1, route_ref[:, 0:1] * r0 + route_ref[:, 1:2] * r1]
    moe = jnp.concatenate(pieces, axis=1)
    pre = DEEPNORM_ALPHA * h_ref[...] + moe
    mu = jnp.mean(pre, axis=1, keepdims=True)
    var = jnp.mean(jnp.square(pre - mu), axis=1, keepdims=True)
    o_ref[...] = (pre - mu) * lax.rsqrt(var + LN_EPS) * g_ref[...] + b_ref[...]


def _combine_ln(dest_flat, y, h, route, g, b, tb, tn):
    m, d = h.shape
    tb = min(tb, m)
    return pl.pallas_call(
        functools.partial(_combine_kernel, tb=tb, tn=tn),
        grid_spec=pltpu.PrefetchScalarGridSpec(
            num_scalar_prefetch=1,
            grid=(m // tb,),
            in_specs=[pl.BlockSpec(memory_space=pl.ANY),
                      pl.BlockSpec((tb, d), lambda i, dest: (i, 0)),
                      pl.BlockSpec((tb, LANES), lambda i, dest: (i, 0)),
                      pl.BlockSpec((1, d), lambda i, dest: (0, 0)),
                      pl.BlockSpec((1, d), lambda i, dest: (0, 0))],
            out_specs=pl.BlockSpec((tb, d), lambda i, dest: (i, 0)),
            scratch_shapes=[pltpu.VMEM((2, 2, tb, d // 2), jnp.int32), pltpu.SemaphoreType.DMA((2,))]),
        out_shape=jax.ShapeDtypeStruct((m, d), F32),
        compiler_params=_cparams(("arbitrary",), 40),
        name="combine",
    )(dest_flat, y, h, route, g, b)


def _alibi_slopes(n):
    return jnp.asarray(2.0 ** (-8.0 * np.arange(1, n + 1) / n), dtype=F32)


def kernel(x, w_in, lam_q1, lam_k1, lam_q2, lam_k2, diff_subln_g, w_pa, w_pb, w_o, ln1_g, ln1_b,
           router_wg, router_bg, router_we, router_be, w1, w3, w2, ln2_g, ln2_b):
    bsz, s_len, d = x.shape
    assert bsz == 1 and w_in.shape[0] == DEPTH
    topk = min(TOPK_MAX, s_len // 4)
    x2 = x[0]
    xb = x2.astype(BF16)

    qk_w = N_DIFF_HEADS * 2 * HEAD_DIM
    c_dq, c_dk, c_dv = 0, qk_w, 2 * qk_w
    c_sq = 3 * qk_w
    c_sk = c_sq + N_DSA_HEADS * HEAD_DIM
    c_sv = c_sk + N_DSA_KV * HEAD_DIM
    c_iq = c_sv + N_DSA_KV * HEAD_DIM
    c_ik = c_iq + N_IDX_HEADS * IDX_DIM
    c_ga = c_ik + IDX_DIM + N_IDX_HEADS
    qscale = HEAD_DIM ** -0.5 * LOG2E
    col = np.arange(c_ik)
    is_q = ((col >= c_dq) & (col < c_dk)) | ((col >= c_sq) & (col < c_sk))
    col_scale = jnp.asarray(np.where(is_q, qscale, 1.0)[None, :], F32)
    wt3d = jnp.swapaxes(w_in, 1, 2)
    wt = wt3d[0]
    wt_small = jnp.concatenate([wt[c_ik:c_ga], jnp.zeros((2 * LANES - IDX_DIM - N_IDX_HEADS, d), F32)],
                               axis=0).astype(BF16)
    wt_gates = _cast_rows(wt, c_ga, 2 * d, min(PROJ_TILE[1], d))

    z = _proj_from_f32(xb, wt3d, col_scale, c_ik, *PROJ_TILE)
    zs = _matmul_nt(xb, wt_small, F32, PROJ_TILE[0], "proj_small")
    ik = zs[:, :IDX_DIM].astype(BF16)
    iwt = zs[:, IDX_DIM:IDX_DIM + N_IDX_HEADS].T
    dvt = _with_ones_rows(z[:, c_dv:c_dv + qk_w].T, N_DIFF_HEADS)
    svt = _with_ones_rows(z[:, c_sv:c_sv + N_DSA_KV * HEAD_DIM].T, N_DSA_KV)
    kaug = _key_aug_table(s_len)

    lam4 = jnp.stack([lam_q1[0], lam_k1[0], lam_q2[0], lam_k2[0]]).astype(F32)
    g_lanes = jnp.broadcast_to(diff_subln_g[0][:, None], (2 * HEAD_DIM, LANES))
    a = _diff_attention(z, dvt, kaug, lam4, g_lanes, _alibi_slopes(N_DIFF_HEADS), c_dq, c_dk, s_len,
                        min(ATTN_TILE, s_len))

    tq_i, tk_i = min(DSA_TILE[0], s_len), min(DSA_TILE[1], s_len)
    maskb = _indexer(z, ik, iwt, c_iq, s_len, tq_i, tk_i, topk)
    b = _dsa_attention(z, svt, kaug, maskb, _alibi_slopes(N_DSA_HEADS), c_sq, c_sk, s_len, tq_i, tk_i)

    merged = _merge(xb, wt_gates, a, w_pa[0].astype(BF16), b, w_pb[0].astype(BF16), *MERGE_TILE)

    wr = jnp.concatenate([router_wg[0], router_we[0],
                          jnp.zeros((d, LANES - N_GROUPS - N_EXPERTS), F32)], axis=1)
    br = jnp.concatenate([router_bg[0], router_be[0],
                          jnp.zeros((LANES - N_GROUPS - N_EXPERTS,), F32)])[None, :]
    h1, route, onehot = _out_ln_router(merged, w_o[0].astype(BF16), x2, ln1_g[0][None, :], ln1_b[0][None, :],
                                       wr, br, *OUT_TILE)

    pos, cnt = _rank(onehot, RANK_ROWS)
    counts = cnt[0, :N_EXPERTS].astype(jnp.int32)
    padded = ((counts + ROW_TILE - 1) // ROW_TILE) * ROW_TILE
    ends = jnp.cumsum(padded)
    start = jnp.zeros((1, LANES), F32).at[0, :N_EXPERTS].set((ends - padded).astype(F32))
    n_tiles = (2 * s_len) // ROW_TILE + N_EXPERTS
    tile_ids = jnp.arange(n_tiles, dtype=jnp.int32)
    te = jnp.minimum(jnp.sum(tile_ids[:, None] >= (ends // ROW_TILE)[None, :], axis=1), N_EXPERTS - 1)
    te = te.astype(jnp.int32)
    nv = (ends[-1] // ROW_TILE).astype(jnp.int32)[None]
    dest = _dest(pos, route, start, RANK_ROWS)
    dest_flat = dest[:, :2].reshape(-1)

    xs = _scatter_rows(dest_flat, h1, n_tiles * ROW_TILE, SCATTER_ROWS)
    tables = (te, nv) + _segment_tables(te, nv, ends // ROW_TILE, n_tiles)
    hid = _gm1(tables, xs, w1[0], w3[0], min(GM1_COLS, w1.shape[3]))
    tn_y = min(GM2_COLS, d)
    y = _gm2(tables, hid, w2[0], tn_y)
    out = _combine_ln(dest_flat, y, h1, route, ln2_g[0][None, :], ln2_b[0][None, :], COMBINE_ROWS, tn_y)
    return out[None]
```

```python
import functools
import math

import numpy as np
import jax
import jax.numpy as jnp
from jax import lax
from jax.experimental import pallas as pl
from jax.experimental.pallas import tpu as pltpu

HEAD_DIM = 128
N_DIFF_HEADS = 8
N_DSA_HEADS = 16
N_DSA_KV = 4
DSA_REP = N_DSA_HEADS // N_DSA_KV
N_IDX_HEADS = 32
IDX_DIM = 128
TOPK_MAX = 256
N_GROUPS = 4
EXPERTS_PER_GROUP = 8
N_EXPERTS = N_GROUPS * EXPERTS_PER_GROUP
LN_EPS = 1e-5
RMS_EPS = 1e-5
DEPTH = 1
DEEPNORM_ALPHA = (2.0 * DEPTH) ** 0.25
LAM_INIT = 0.8 - 0.6 * math.exp(-0.3 * 0)

LANES = 128
SUBLANES = 8
NEG = -1e30
INT_MIN = -(2 ** 31)
ROW_TILE = 256
LOG2E = 1.4426950408889634
POS_RADIX = 256

PROJ_TILE = (1024, 512)
ATTN_TILE = 512
DSA_TILE = (256, 512)
MERGE_TILE = (1024, 256)
OUT_TILE = (512, 512)
RANK_ROWS = 512
SCATTER_ROWS = 256
COMBINE_ROWS = 128
GM1_COLS = 512
GM2_COLS = 2048
VMEM_SMALL_MB = 32
VMEM_MB = 48
VMEM_LARGE_MB = 56

F32 = jnp.float32
BF16 = jnp.bfloat16


def _cparams(sem, vmem_mb):
    return pltpu.CompilerParams(dimension_semantics=sem, vmem_limit_bytes=vmem_mb << 20)


def _dot_nt(a, b):
    return lax.dot_general(a, b, (((1,), (1,)), ((), ())), preferred_element_type=F32)


def _proj_kernel(x_ref, wt_ref, sc_ref, o_ref, wb_scr):
    @pl.when(pl.program_id(1) == 0)
    def _():
        wb_scr[...] = wt_ref[0].astype(BF16)

    o_ref[...] = (_dot_nt(x_ref[...], wb_scr[...]) * sc_ref[...]).astype(o_ref.dtype)


def _proj_from_f32(xb, wt3d, col_scale, n_cols, tm, tn):
    m, d = xb.shape
    tm = min(tm, m)
    return pl.pallas_call(
        _proj_kernel,
        grid=(n_cols // tn, m // tm),
        in_specs=[pl.BlockSpec((tm, d), lambda j, i: (i, 0)),
                  pl.BlockSpec((1, tn, d), lambda j, i: (0, j, 0)),
                  pl.BlockSpec((1, tn), lambda j, i: (0, j))],
        out_specs=pl.BlockSpec((tm, tn), lambda j, i: (i, j)),
        out_shape=jax.ShapeDtypeStruct((m, n_cols), BF16),
        scratch_shapes=[pltpu.VMEM((tn, d), BF16)],
        compiler_params=_cparams(("parallel", "arbitrary"), VMEM_MB),
        name="proj_main",
    )(xb, wt3d, col_scale)


def _cast_rows_kernel(w_ref, o_ref):
    o_ref[...] = w_ref[...].astype(o_ref.dtype)


def _cast_rows(wt, row0, n_rows, tr):
    d = wt.shape[1]
    assert row0 % SUBLANES == 0 and n_rows % tr == 0
    return pl.pallas_call(
        _cast_rows_kernel,
        grid=(n_rows // tr,),
        in_specs=[pl.BlockSpec((pl.Element(tr), pl.Element(d)),
                               lambda i: (pl.multiple_of(row0 + i * tr, SUBLANES), 0))],
        out_specs=pl.BlockSpec((tr, d), lambda i: (i, 0)),
        out_shape=jax.ShapeDtypeStruct((n_rows, d), BF16),
        compiler_params=_cparams(("parallel",), VMEM_MB),
        name="cast_gates",
    )(wt)


def _mm_nt_kernel(a_ref, bt_ref, o_ref):
    o_ref[...] = _dot_nt(a_ref[...], bt_ref[...].astype(BF16)).astype(o_ref.dtype)


def _matmul_nt(a, wt, row_block, n, out_dtype, tm, name):
    m, k = a.shape
    tm = min(tm, m)
    return pl.pallas_call(
        _mm_nt_kernel,
        grid=(m // tm,),
        in_specs=[pl.BlockSpec((tm, k), lambda i: (i, 0)),
                  pl.BlockSpec((n, k), lambda i: (row_block, 0))],
        out_specs=pl.BlockSpec((tm, n), lambda i: (i, 0)),
        out_shape=jax.ShapeDtypeStruct((m, n), out_dtype),
        compiler_params=_cparams(("parallel",), VMEM_MB),
        name=name,
    )(a, wt)


def _key_aug_table(s_len):
    pos = jnp.arange(s_len, dtype=jnp.int32)[:, None]
    lane = jnp.arange(LANES, dtype=jnp.int32)[None, :]
    hi = (pos // POS_RADIX).astype(F32)
    lo = (pos % POS_RADIX).astype(F32)
    t = jnp.where(lane < 2, hi, jnp.where(lane < 4, lo, jnp.where(lane < 7, 1.0, 0.0)))
    return t.astype(BF16)


def _bf16_piece(x):
    return x.astype(BF16).astype(F32)


def _query_aug(slope, qbase, rows):
    s2 = jnp.full((SUBLANES, LANES), slope, F32) * LOG2E
    big = s2 * POS_RADIX
    off = -s2 * jnp.full((SUBLANES, LANES), qbase, F32)
    big_hi = _bf16_piece(big)
    s2_hi = _bf16_piece(s2)
    off_hi = _bf16_piece(off)
    off_mid = _bf16_piece(off - off_hi)
    lane = lax.broadcasted_iota(jnp.int32, (SUBLANES, LANES), 1)
    pieces = [big_hi, big - big_hi, s2_hi, s2 - s2_hi, off_hi, off_mid, off - off_hi - off_mid]
    row = jnp.zeros((SUBLANES, LANES), F32)
    for n, piece in enumerate(pieces):
        row = jnp.where(lane == n, piece, row)
    return jnp.broadcast_to(row[0:1, :], (rows, LANES)).astype(BF16)


QUERY_SUB = 256


def _causal_steps(nq, last_of):
    pairs = [(i, j) for i in range(nq) for j in range(last_of(i) + 1)]
    return (jnp.asarray([p[0] for p in pairs], jnp.int32), jnp.asarray([p[1] for p in pairs], jnp.int32))


ONES_ROWS = 16


def _with_ones_rows(vt, n_blocks):
    r = vt.shape[0] // n_blocks
    v3 = vt.reshape(n_blocks, r, vt.shape[1])
    ones = jnp.ones((n_blocks, ONES_ROWS, vt.shape[1]), vt.dtype)
    return jnp.concatenate([v3, ones], axis=1).reshape(n_blocks * (r + ONES_ROWS), vt.shape[1])


def _online_softmax_pv(s, vt1, m_scr, acc_scr, idx, cols):
    at = idx + (slice(None), cols)
    m_prev = m_scr[at]
    m_next = jnp.maximum(m_prev, jnp.max(s, axis=0, keepdims=True))
    p = jnp.exp2((s - m_next).astype(BF16))
    alpha = jnp.exp2(m_prev - m_next)
    m_scr[at] = m_next
    acc_scr[at] = acc_scr[at] * alpha + jnp.dot(vt1, p, preferred_element_type=F32)


DIFF_HEADS_PER_STEP = 2


def _diff_kernel(qi_ref, kj_ref, slopes_ref, lam_ref, g_ref, q_ref, k_ref, ka_ref, vt_ref, o_ref,
                 qa_scr, kk_scr, s0_scr, s1_scr, m_scr, acc_scr, *, tq, n_steps):
    hp, n = pl.program_id(0), pl.program_id(1)
    reps = tq // LANES
    nsub = tq // QUERY_SUB
    w = 2 * HEAD_DIM
    maps = [(hh, c) for hh in range(DIFF_HEADS_PER_STEP) for c in range(2)]
    na = jnp.minimum(n, n_steps - 1)
    nb = jnp.maximum(n - 1, 0)
    ia = qi_ref[na]
    ib, jb = qi_ref[nb], kj_ref[nb]
    has_a, has_b = n < n_steps, n >= 1
    diag_b = jnp.logical_and(has_b, jb == ib)

    score_slots = (s0_scr, s1_scr)

    def map_cols(hh, c):
        return slice(hh * w + c * HEAD_DIM, hh * w + (c + 1) * HEAD_DIM)

    def load_queries():
        for hh in range(DIFF_HEADS_PER_STEP):
            qaug = _query_aug(slopes_ref[hp * DIFF_HEADS_PER_STEP + hh], (ia * tq).astype(F32), tq)
            for c in range(2):
                qa_scr[2 * hh + c, :, :HEAD_DIM] = q_ref[:, map_cols(hh, c)]
                qa_scr[2 * hh + c, :, HEAD_DIM:] = qaug

    def form_scores(slot):
        ka = ka_ref[...]
        for hh, c in maps:
            kk_scr[2 * hh + c, :, :HEAD_DIM] = k_ref[:, map_cols(hh, c)]
            kk_scr[2 * hh + c, :, HEAD_DIM:] = ka
        for m in range(len(maps)):
            score_slots[slot][m] = _dot_nt(kk_scr[m], qa_scr[m])

    def softmax_pv(slot, masked):
        for m, (hh, c) in enumerate(maps):
            vt1 = vt_ref[hh * (w + ONES_ROWS):(hh + 1) * (w + ONES_ROWS), :]
            for u in range(nsub):
                cols = slice(u * QUERY_SUB, (u + 1) * QUERY_SUB)
                s = score_slots[slot][m, :, cols]
                if masked:
                    keep = (lax.broadcasted_iota(jnp.int32, (tq, 1), 0)
                            <= u * QUERY_SUB + lax.broadcasted_iota(jnp.int32, (1, QUERY_SUB), 1))
                    s = jnp.where(keep, s, NEG)
                _online_softmax_pv(s, vt1, m_scr, acc_scr, (m,), cols)

    @pl.when(jnp.logical_and(has_b, jb == 0))
    def _():
        m_scr[...] = jnp.full(m_scr.shape, NEG, F32)
        acc_scr[...] = jnp.zeros(acc_scr.shape, F32)

    def emit_block():
        lam = (jnp.exp(jnp.sum(lam_ref[0:1, :] * lam_ref[1:2, :], axis=1, keepdims=True))
               - jnp.exp(jnp.sum(lam_ref[2:3, :] * lam_ref[3:4, :], axis=1, keepdims=True)) + LAM_INIT)
        g = jnp.concatenate([g_ref[...]] * reps, axis=1)
        for hh in range(DIFF_HEADS_PER_STEP):
            m = 2 * hh
            o = (acc_scr[m, :w, :] / acc_scr[m, w:w + 1, :]
                 - lam * (acc_scr[m + 1, :w, :] / acc_scr[m + 1, w:w + 1, :]))
            o = o * lax.rsqrt(jnp.mean(o * o, axis=0, keepdims=True) + RMS_EPS) * g
            o_ref[:, hh * w:(hh + 1) * w] = (o * (1.0 - LAM_INIT)).T.astype(o_ref.dtype)

    for par in range(2):
        mine = (n % 2) == par

        @pl.when(jnp.logical_and(mine, jnp.logical_and(has_b, jnp.logical_not(diag_b))))
        def _(par=par):
            form_scores(par)
            softmax_pv(1 - par, False)

        @pl.when(jnp.logical_and(mine, jnp.logical_and(diag_b, has_a)))
        def _(par=par):
            load_queries()
            form_scores(par)
            softmax_pv(1 - par, True)
            emit_block()

        @pl.when(jnp.logical_and(mine, jnp.logical_and(diag_b, jnp.logical_not(has_a))))
        def _(par=par):
            softmax_pv(1 - par, True)
            emit_block()

        @pl.when(jnp.logical_and(mine, n == 0))
        def _(par=par):
            load_queries()
            form_scores(par)


def _diff_attention(z, vt, kaug, lam4, g_lanes, slopes, col_q, col_k, s_len, tq):
    nq = s_len // tq
    w = 2 * HEAD_DIM
    wb = DIFF_HEADS_PER_STEP * w
    nmaps = 2 * DIFF_HEADS_PER_STEP
    qb, kb = col_q // wb, col_k // wb
    qi, kj = _causal_steps(nq, lambda i: i)
    n_steps = int(qi.shape[0])

    def ahead(s):
        return jnp.minimum(s, n_steps - 1)

    def behind(s):
        return jnp.maximum(s - 1, 0)

    return pl.pallas_call(
        functools.partial(_diff_kernel, tq=tq, n_steps=n_steps),
        grid_spec=pltpu.PrefetchScalarGridSpec(
            num_scalar_prefetch=3,
            grid=(N_DIFF_HEADS // DIFF_HEADS_PER_STEP, n_steps + 1),
            in_specs=[pl.BlockSpec((4, HEAD_DIM), lambda h, s, qi_, kj_, sl_: (0, 0)),
                      pl.BlockSpec((w, LANES), lambda h, s, qi_, kj_, sl_: (0, 0)),
                      pl.BlockSpec((tq, wb), lambda h, s, qi_, kj_, sl_: (qi_[ahead(s)], qb + h)),
                      pl.BlockSpec((tq, wb), lambda h, s, qi_, kj_, sl_: (kj_[ahead(s)], kb + h)),
                      pl.BlockSpec((tq, LANES), lambda h, s, qi_, kj_, sl_: (kj_[ahead(s)], 0)),
                      pl.BlockSpec((DIFF_HEADS_PER_STEP * (w + ONES_ROWS), tq),
                                   lambda h, s, qi_, kj_, sl_: (h, kj_[behind(s)]))],
            out_specs=pl.BlockSpec((tq, wb), lambda h, s, qi_, kj_, sl_: (qi_[behind(s)], h)),
            scratch_shapes=[pltpu.VMEM((nmaps, tq, w), BF16), pltpu.VMEM((nmaps, tq, w), BF16),
                            pltpu.VMEM((nmaps, tq, tq), F32), pltpu.VMEM((nmaps, tq, tq), F32),
                            pltpu.VMEM((nmaps, 1, tq), F32),
                            pltpu.VMEM((nmaps, w + ONES_ROWS, tq), F32)]),
        out_shape=jax.ShapeDtypeStruct((s_len, N_DIFF_HEADS * w), BF16),
        compiler_params=_cparams(("parallel", "arbitrary"), VMEM_MB),
        name="diffattn",
    )(qi, kj, slopes, lam4, g_lanes, z, z, kaug, vt)


IQ_SPLIT = 4


HALF_BITS = 16
HALF_MASK = (1 << HALF_BITS) - 1
I16_MIN = -(1 << (HALF_BITS - 1))


def _indexer_kernel(iq0_ref, iq1_ref, iq2_ref, iq3_ref, ik_ref, iwt_ref, o_ref,
                    iqh_scr, key_scr, hi_scr, lo_scr, acc_scr, pcut_scr, *, tq, tkc, topk, iw_scale, pos_bits):
    i = pl.program_id(0)
    nk = key_scr.shape[0]
    nchunks = (i * tq + tq - 1) // tkc + 1
    per = N_IDX_HEADS // IQ_SPLIT

    for h in range(N_IDX_HEADS):
        src = (iq0_ref, iq1_ref, iq2_ref, iq3_ref)[h // per]
        iqh_scr[h] = src[:, (h % per) * IDX_DIM:(h % per + 1) * IDX_DIM]

    qpos = i * tq + lax.broadcasted_iota(jnp.int32, (1, tq), 1)

    def key_pos(c):
        return c * tkc + lax.broadcasted_iota(jnp.int32, (tkc, 1), 0)

    def count_keys(pred):
        def body(c, acc):
            x = jnp.where(pred(c, key_scr[c]), 1, 0)
            return acc + jnp.sum(x.reshape(tkc // SUBLANES, SUBLANES, tq), axis=0)

        cnt = lax.fori_loop(0, nchunks, body, jnp.zeros((SUBLANES, tq), jnp.int32))
        return jnp.sum(cnt, axis=0, keepdims=True)

    def chunk(c, carry):
        kc = ik_ref[pl.ds(pl.multiple_of(c * tkc, tkc), tkc), :]
        acc_scr[...] = jnp.zeros(acc_scr.shape, F32)

        def head(h, carry2):
            sc = _dot_nt(kc, iqh_scr[h])
            acc_scr[...] += (iwt_ref[pl.ds(h, 1), :] * iw_scale) * jnp.maximum(sc, 0.0)
            return carry2

        lax.fori_loop(0, N_IDX_HEADS, head, 0, unroll=16)
        score = acc_scr[...]
        score = jnp.where(score == 0.0, 0.0, score)
        bits = pltpu.bitcast(score, jnp.int32)
        skey = bits ^ ((bits >> 31) & 0x7FFFFFFF)
        key = jnp.where(key_pos(c) <= qpos, skey, INT_MIN)
        key_scr[c] = key
        hi_scr[c] = (key >> HALF_BITS).astype(jnp.int16)
        lo_scr[c] = ((key & HALF_MASK) + I16_MIN).astype(jnp.int16)
        return carry

    lax.fori_loop(0, nchunks, chunk, 0)

    def count_halves(half_scr, pred):
        rows = 2 * SUBLANES

        def body(c, acc):
            x = jnp.where(pred(half_scr[c]), jnp.int16(1), jnp.int16(0))
            for g in range(tkc // rows):
                acc = acc + x[g * rows:(g + 1) * rows, :]
            return acc

        cnt = lax.fori_loop(0, nchunks, body, jnp.zeros((rows, tq), jnp.int16))
        return jnp.sum(cnt.astype(jnp.int32), axis=0, keepdims=True)

    def radix_select(half_scr, want):
        def bit_step(bi, u):
            cand = u | lax.shift_left(jnp.int32(1), HALF_BITS - 1 - bi)
            cand16 = (cand + I16_MIN).astype(jnp.int16)
            return jnp.where(count_halves(half_scr, lambda k: k >= cand16) >= want, cand, u)

        return lax.fori_loop(0, HALF_BITS, bit_step, jnp.zeros((1, tq), jnp.int32)) + I16_MIN

    t_hi = radix_select(hi_scr, topk)
    t_hi16 = t_hi.astype(jnp.int16)
    want_lo = topk - count_halves(hi_scr, lambda k: k > t_hi16)

    def keep_lo(c, carry):
        lo_scr[c] = jnp.where(hi_scr[c] == t_hi16, lo_scr[c], jnp.int16(I16_MIN))
        return carry

    lax.fori_loop(0, nchunks, keep_lo, 0)
    t_lo = radix_select(lo_scr, want_lo)
    thr = lax.shift_left(t_hi, HALF_BITS) | (t_lo - I16_MIN)
    thr = jnp.maximum(thr, INT_MIN + 1)

    pcut_scr[...] = jnp.full(pcut_scr.shape, (1 << pos_bits) - 1, jnp.int32)

    @pl.when(jnp.max(count_keys(lambda c, k: k >= thr)) > topk)
    def _():
        need = topk - count_keys(lambda c, k: k > thr)

        def pos_step(bi, p):
            cand = p | lax.shift_left(jnp.int32(1), pos_bits - 1 - bi)
            tied_before = count_keys(lambda c, k: jnp.where(k == thr, key_pos(c), cand) < cand)
            return jnp.where(tied_before < need, cand, p)

        pcut_scr[...] = lax.fori_loop(0, pos_bits, pos_step, jnp.zeros((1, tq), jnp.int32))

    pcut = pcut_scr[...]

    def emit(c, carry):
        k = key_scr[c]
        tie_bias = jnp.where(key_pos(c) <= pcut, 0.0, NEG)
        o_ref[0, c] = jnp.where(k > thr, 0.0, jnp.where(k == thr, tie_bias, NEG)).astype(o_ref.dtype)
        return carry

    lax.fori_loop(0, nchunks, emit, 0)

    def fill(c, carry):
        o_ref[0, c] = jnp.full((tkc, tq), NEG, o_ref.dtype)
        return carry

    lax.fori_loop(nchunks, nk, fill, 0)


def _indexer(z, ik, iwt, col_iq, s_len, tq, tkc, topk):
    nq, nk = s_len // tq, s_len // tkc
    iw_scale = N_IDX_HEADS ** -0.5 * IDX_DIM ** -0.5
    wq = N_IDX_HEADS * IDX_DIM // IQ_SPLIT
    qb = col_iq // wq
    iq_specs = [pl.BlockSpec((tq, wq), functools.partial(lambda i, n: (i, qb + n), n=n)) for n in range(IQ_SPLIT)]
    return pl.pallas_call(
        functools.partial(_indexer_kernel, tq=tq, tkc=tkc, topk=topk, iw_scale=iw_scale,
                          pos_bits=max(1, (s_len - 1).bit_length())),
        grid=(nq,),
        in_specs=iq_specs + [pl.BlockSpec((s_len, IDX_DIM), lambda i: (0, 0)),
                             pl.BlockSpec((N_IDX_HEADS, tq), lambda i: (0, i))],
        out_specs=pl.BlockSpec((1, nk, tkc, tq), lambda i: (i, 0, 0, 0)),
        out_shape=jax.ShapeDtypeStruct((nq, nk, tkc, tq), BF16),
        scratch_shapes=[pltpu.VMEM((N_IDX_HEADS, tq, IDX_DIM), BF16),
                        pltpu.VMEM((nk, tkc, tq), jnp.int32),
                        pltpu.VMEM((nk, tkc, tq), jnp.int16),
                        pltpu.VMEM((nk, tkc, tq), jnp.int16),
                        pltpu.VMEM((tkc, tq), F32),
                        pltpu.VMEM((1, tq), jnp.int32)],
        compiler_params=_cparams(("parallel",), VMEM_MB),
        name="indexer",
    )(z, z, z, z, ik, iwt)


DSA_GROUPS_PER_STEP = 2


def _dsa_kernel(qi_ref, kj_ref, slopes_ref, q_ref, k_ref, ka_ref, vt_ref, mb_ref, o_ref,
                qa_scr, kk_scr, m_scr, acc_scr, *, tq, tk):
    gp, step_id = pl.program_id(0), pl.program_id(1)
    i, j = qi_ref[step_id], kj_ref[step_id]
    last = (i * tq + tq - 1) // tk
    heads = DSA_GROUPS_PER_STEP * DSA_REP

    @pl.when(j == 0)
    def _():
        m_scr[...] = jnp.full(m_scr.shape, NEG, F32)
        acc_scr[...] = jnp.zeros(acc_scr.shape, F32)
        for n in range(heads):
            rows = slice(n * tq, (n + 1) * tq)
            qa_scr[rows, :HEAD_DIM] = q_ref[:, n * HEAD_DIM:(n + 1) * HEAD_DIM]
            qa_scr[rows, HEAD_DIM:] = _query_aug(slopes_ref[gp * heads + n], (i * tq).astype(F32), tq)

    ka = ka_ref[...]
    for gg in range(DSA_GROUPS_PER_STEP):
        kk_scr[gg, :, :HEAD_DIM] = k_ref[:, gg * HEAD_DIM:(gg + 1) * HEAD_DIM]
        kk_scr[gg, :, HEAD_DIM:] = ka
    mb = mb_ref[0, 0].astype(F32)
    scores = [_dot_nt(kk_scr[gg], qa_scr[gg * DSA_REP * tq:(gg + 1) * DSA_REP * tq, :])
              for gg in range(DSA_GROUPS_PER_STEP)]
    for gg in range(DSA_GROUPS_PER_STEP):
        vt1 = vt_ref[gg * (HEAD_DIM + ONES_ROWS):(gg + 1) * (HEAD_DIM + ONES_ROWS), :]
        for r in range(DSA_REP):
            n = gg * DSA_REP + r
            _online_softmax_pv(scores[gg][:, r * tq:(r + 1) * tq] + mb, vt1, m_scr, acc_scr, (),
                               slice(n * tq, (n + 1) * tq))

    @pl.when(j == last)
    def _():
        o = acc_scr[:HEAD_DIM, :] / acc_scr[HEAD_DIM:HEAD_DIM + 1, :]
        for n in range(heads):
            o_ref[:, n * HEAD_DIM:(n + 1) * HEAD_DIM] = o[:, n * tq:(n + 1) * tq].T.astype(o_ref.dtype)


def _dsa_attention(z, vt, kaug, maskb, slopes, col_q, col_k, s_len, tq, tk):
    nq = s_len // tq
    gps = DSA_GROUPS_PER_STEP
    heads = gps * DSA_REP
    qw, kw = heads * HEAD_DIM, gps * HEAD_DIM
    qb, kb = col_q // qw, col_k // kw
    qi, kj = _causal_steps(nq, lambda i: (i * tq + tq - 1) // tk)
    return pl.pallas_call(
        functools.partial(_dsa_kernel, tq=tq, tk=tk),
        grid_spec=pltpu.PrefetchScalarGridSpec(
            num_scalar_prefetch=3,
            grid=(N_DSA_KV // gps, qi.shape[0]),
            in_specs=[pl.BlockSpec((tq, qw), lambda g, s, qi_, kj_, sl_: (qi_[s], qb + g)),
                      pl.BlockSpec((tk, kw), lambda g, s, qi_, kj_, sl_: (kj_[s], kb + g)),
                      pl.BlockSpec((tk, LANES), lambda g, s, qi_, kj_, sl_: (kj_[s], 0)),
                      pl.BlockSpec((gps * (HEAD_DIM + ONES_ROWS), tk), lambda g, s, qi_, kj_, sl_: (g, kj_[s])),
                      pl.BlockSpec((1, 1, tk, tq), lambda g, s, qi_, kj_, sl_: (qi_[s], kj_[s], 0, 0))],
            out_specs=pl.BlockSpec((tq, qw), lambda g, s, qi_, kj_, sl_: (qi_[s], g)),
            scratch_shapes=[pltpu.VMEM((heads * tq, 2 * HEAD_DIM), BF16),
                            pltpu.VMEM((gps, tk, 2 * HEAD_DIM), BF16),
                            pltpu.VMEM((1, heads * tq), F32),
                            pltpu.VMEM((HEAD_DIM + ONES_ROWS, heads * tq), F32)]),
        out_shape=jax.ShapeDtypeStruct((s_len, N_DSA_HEADS * HEAD_DIM), BF16),
        compiler_params=_cparams(("parallel", "arbitrary"), VMEM_MB),
        name="dsa",
    )(qi, kj, slopes, z, z, kaug, vt, maskb)


def _merge_kernel(x_ref, wga_ref, wgb_ref, a_ref, wpa_ref, b_ref, wpb_ref, o_ref):
    x = x_ref[...]
    ga = _dot_nt(x, wga_ref[...])
    gb = _dot_nt(x, wgb_ref[...])
    pa = jnp.dot(a_ref[...], wpa_ref[...], preferred_element_type=F32)
    pb = jnp.dot(b_ref[...], wpb_ref[...], preferred_element_type=F32)
    o_ref[...] = (jax.nn.sigmoid(ga) * pa + jax.nn.sigmoid(gb) * pb).astype(o_ref.dtype)


def _merge(xb, wt_gates, a, wpa, b, wpb, tm, tn):
    m, d = xb.shape
    ka, kb = a.shape[1], b.shape[1]
    tm, tn = min(tm, m), min(tn, d)
    row = lambda i, j: (i, 0)
    col = lambda i, j: (0, j)
    return pl.pallas_call(
        _merge_kernel,
        grid=(m // tm, d // tn),
        in_specs=[pl.BlockSpec((tm, d), row),
                  pl.BlockSpec((tn, d), lambda i, j: (j, 0)), pl.BlockSpec((tn, d), lambda i, j: (d // tn + j, 0)),
                  pl.BlockSpec((tm, ka), row), pl.BlockSpec((ka, tn), col),
                  pl.BlockSpec((tm, kb), row), pl.BlockSpec((kb, tn), col)],
        out_specs=pl.BlockSpec((tm, tn), lambda i, j: (i, j)),
        out_shape=jax.ShapeDtypeStruct((m, d), BF16),
        compiler_params=_cparams(("parallel", "arbitrary"), VMEM_LARGE_MB),
        name="merge",
    )(xb, wt_gates, wt_gates, a, wpa, b, wpb)


def _split_bf16(x):
    hi = x.astype(BF16)
    lo = (x - hi.astype(F32)).astype(BF16)
    return hi, lo


def _out_kernel(mg_ref, wo_ref, x_ref, g_ref, b_ref, wr_ref, br_ref,
                h_ref, route_ref, oh_ref, pre_scr, *, tn, nj):
    j = pl.program_id(1)
    y = jnp.dot(mg_ref[...], wo_ref[...], preferred_element_type=F32)
    pre_scr[j] = DEEPNORM_ALPHA * x_ref[...] + y

    @pl.when(j == nj - 1)
    def _():
        d = nj * tn
        tot = pre_scr[0].sum(axis=1, keepdims=True)
        for jj in range(1, nj):
            tot = tot + pre_scr[jj].sum(axis=1, keepdims=True)
        mu = tot / d
        sq = jnp.square(pre_scr[0] - mu).sum(axis=1, keepdims=True)
        for jj in range(1, nj):
            sq = sq + jnp.square(pre_scr[jj] - mu).sum(axis=1, keepdims=True)
        rstd = lax.rsqrt(sq / d + LN_EPS)
        logits = jnp.zeros(route_ref.shape, F32)
        for jj in range(nj):
            cs = slice(jj * tn, (jj + 1) * tn)
            hn = (pre_scr[jj] - mu) * rstd * g_ref[:, cs] + b_ref[:, cs]
            h_ref[:, cs] = hn
            h_hi, h_lo = _split_bf16(hn)
            w_hi, w_lo = _split_bf16(wr_ref[cs, :])
            logits = logits + (jnp.dot(h_hi, w_hi, preferred_element_type=F32)
                               + jnp.dot(h_hi, w_lo, preferred_element_type=F32)
                               + jnp.dot(h_lo, w_hi, preferred_element_type=F32))
        logits = logits + br_ref[...]
        lane = lax.broadcasted_iota(jnp.int32, logits.shape, 1)
        big = jnp.int32(4 * LANES)
        gl = jnp.where(lane < N_GROUPS, logits, -jnp.inf)
        gmax = jnp.max(gl, axis=1, keepdims=True)
        gsel = jnp.min(jnp.where(gl == gmax, lane, big), axis=1, keepdims=True)
        ggate = 1.0 / jnp.sum(jnp.exp(gl - gmax), axis=1, keepdims=True)
        eid = lane - N_GROUPS
        ingrp = (eid >= gsel * EXPERTS_PER_GROUP) & (eid < (gsel + 1) * EXPERTS_PER_GROUP)
        el = jnp.where(ingrp, logits, -jnp.inf)
        v1 = jnp.max(el, axis=1, keepdims=True)
        i1 = jnp.min(jnp.where(el == v1, lane, big), axis=1, keepdims=True)
        el2 = jnp.where(lane == i1, -jnp.inf, el)
        v2 = jnp.max(el2, axis=1, keepdims=True)
        i2 = jnp.min(jnp.where(el2 == v2, lane, big), axis=1, keepdims=True)
        t = jnp.exp(v2 - v1)
        g1 = ggate / (1.0 + t)
        g2 = g1 * t
        e1 = (i1 - N_GROUPS).astype(F32)
        e2 = (i2 - N_GROUPS).astype(F32)
        route_ref[...] = jnp.where(lane == 0, g1, jnp.where(lane == 1, g2,
                                   jnp.where(lane == 2, e1, jnp.where(lane == 3, e2, 0.0))))
        oh_ref[...] = jnp.where(lane == i1 - N_GROUPS, 1.0,
                                jnp.where(lane == i2 - N_GROUPS, 1.0, 0.0)).astype(oh_ref.dtype)


def _out_ln_router(mg, wo, x, g, b, wr, br, tm, tn):
    m, d = x.shape
    tm, tn = min(tm, m), min(tn, d)
    nj = d // tn
    return pl.pallas_call(
        functools.partial(_out_kernel, tn=tn, nj=nj),
        grid=(m // tm, nj),
        in_specs=[pl.BlockSpec((tm, d), lambda i, j: (i, 0)),
                  pl.BlockSpec((d, tn), lambda i, j: (0, j)),
                  pl.BlockSpec((tm, tn), lambda i, j: (i, j)),
                  pl.BlockSpec((1, d), lambda i, j: (0, 0)),
                  pl.BlockSpec((1, d), lambda i, j: (0, 0)),
                  pl.BlockSpec((d, LANES), lambda i, j: (0, 0)),
                  pl.BlockSpec((1, LANES), lambda i, j: (0, 0))],
        out_specs=[pl.BlockSpec((tm, d), lambda i, j: (i, 0)),
                   pl.BlockSpec((tm, LANES), lambda i, j: (i, 0)),
                   pl.BlockSpec((tm, LANES), lambda i, j: (i, 0))],
        out_shape=[jax.ShapeDtypeStruct((m, d), F32),
                   jax.ShapeDtypeStruct((m, LANES), F32),
                   jax.ShapeDtypeStruct((m, LANES), BF16)],
        scratch_shapes=[pltpu.VMEM((nj, tm, tn), F32)],
        compiler_params=_cparams(("parallel", "arbitrary"), VMEM_LARGE_MB),
        name="outln",
    )(mg, wo, x, g, b, wr, br)


def _rank_kernel(oh_ref, pos_ref, cnt_ref, base_scr, *, tb):
    @pl.when(pl.program_id(0) == 0)
    def _():
        base_scr[...] = jnp.zeros(base_scr.shape, F32)

    oh = oh_ref[...]
    r = lax.broadcasted_iota(jnp.int32, (tb, tb), 0)
    c = lax.broadcasted_iota(jnp.int32, (tb, tb), 1)
    tri = jnp.where(c <= r, 1.0, 0.0).astype(BF16)
    cs = jnp.dot(tri, oh, preferred_element_type=F32)
    pos_ref[...] = cs - oh.astype(F32) + base_scr[0:1, :]
    base_scr[...] = base_scr[...] + cs[tb - 1:tb, :]
    cnt_ref[...] = base_scr[...]


def _rank(onehot, tb):
    m = onehot.shape[0]
    tb = min(tb, m)
    return pl.pallas_call(
        functools.partial(_rank_kernel, tb=tb),
        grid=(m // tb,),
        in_specs=[pl.BlockSpec((tb, LANES), lambda i: (i, 0))],
        out_specs=[pl.BlockSpec((tb, LANES), lambda i: (i, 0)),
                   pl.BlockSpec((8, LANES), lambda i: (0, 0))],
        out_shape=[jax.ShapeDtypeStruct((m, LANES), F32), jax.ShapeDtypeStruct((8, LANES), F32)],
        scratch_shapes=[pltpu.VMEM((8, LANES), F32)],
        compiler_params=_cparams(("arbitrary",), VMEM_SMALL_MB),
        name="rank",
    )(onehot)


def _dest_kernel(pos_ref, route_ref, start_ref, dest_ref):
    lane = lax.broadcasted_iota(jnp.int32, pos_ref.shape, 1).astype(F32)
    v = pos_ref[...] + start_ref[...]
    d1 = jnp.sum(jnp.where(lane == route_ref[:, 2:3], v, 0.0), axis=1, keepdims=True)
    d2 = jnp.sum(jnp.where(lane == route_ref[:, 3:4], v, 0.0), axis=1, keepdims=True)
    dest_ref[...] = jnp.where(lane == 0.0, d1, jnp.where(lane == 1.0, d2, 0.0)).astype(jnp.int32)


def _dest(pos, route, start, tb):
    m = pos.shape[0]
    tb = min(tb, m)
    return pl.pallas_call(
        _dest_kernel,
        grid=(m // tb,),
        in_specs=[pl.BlockSpec((tb, LANES), lambda i: (i, 0)),
                  pl.BlockSpec((tb, LANES), lambda i: (i, 0)),
                  pl.BlockSpec((1, LANES), lambda i: (0, 0))],
        out_specs=pl.BlockSpec((tb, LANES), lambda i: (i, 0)),
        out_shape=jax.ShapeDtypeStruct((m, LANES), jnp.int32),
        compiler_params=_cparams(("parallel",), VMEM_SMALL_MB),
        name="dest",
    )(pos, route, start)


DMA_ISSUE_UNROLL = 4


def _pack_bf16_pair(left, right):
    lo = pltpu.bitcast(left.astype(BF16).astype(F32), jnp.int32)
    hi = pltpu.bitcast(right.astype(BF16).astype(F32), jnp.int32)
    return hi | lax.shift_right_logical(lo, 16)


def _unpack_bf16_pair(words):
    left = pltpu.bitcast(lax.shift_left(words, 16), F32)
    right = pltpu.bitcast(words & jnp.int32(-65536), F32)
    return left, right


def _scatter_kernel(dest_ref, h_ref, xs_in_ref, xs_ref, hp_scr, sem, *, tb):
    del xs_in_ref
    base = pl.program_id(0) * tb
    half = h_ref.shape[1] // 2
    hp_scr[...] = _pack_bf16_pair(h_ref[:, :half], h_ref[:, half:])

    def row_copy(t, slot):
        d = dest_ref[(base + t) * 2 + slot]
        return pltpu.make_async_copy(hp_scr.at[pl.ds(t, 1), :], xs_ref.at[pl.ds(d, 1), :], sem)

    def start(t, carry):
        row_copy(t, 0).start()
        row_copy(t, 1).start()
        return carry

    lax.fori_loop(0, tb, start, 0, unroll=DMA_ISSUE_UNROLL)
    for _ in range(2):
        pltpu.make_async_copy(hp_scr, xs_ref.at[pl.ds(0, tb), :], sem).wait()


def _scatter_rows(dest_flat, h, n_rows, tb):
    m, d = h.shape
    tb = min(tb, m)
    xs0 = jnp.zeros((n_rows, d // 2), jnp.int32)
    return pl.pallas_call(
        functools.partial(_scatter_kernel, tb=tb),
        grid_spec=pltpu.PrefetchScalarGridSpec(
            num_scalar_prefetch=1,
            grid=(m // tb,),
            in_specs=[pl.BlockSpec((tb, d), lambda i, dest: (i, 0)),
                      pl.BlockSpec(memory_space=pl.ANY)],
            out_specs=pl.BlockSpec(memory_space=pl.ANY),
            scratch_shapes=[pltpu.VMEM((tb, d // 2), jnp.int32), pltpu.SemaphoreType.DMA(())]),
        out_shape=jax.ShapeDtypeStruct((n_rows, d // 2), jnp.int32),
        input_output_aliases={2: 0},
        compiler_params=_cparams(("arbitrary",), VMEM_SMALL_MB),
        name="scatter",
    )(dest_flat, h, xs0)


CAST_ROWS = 256


def _segment_tables(te, nv, end_tile, n_tiles):
    ids = jnp.arange(n_tiles, dtype=jnp.int32)
    fresh = jnp.concatenate([jnp.ones((1,), bool), te[1:] != te[:-1]]) & (ids < nv[0])
    seg = jnp.cumsum(fresh.astype(jnp.int32)) - 1
    after = end_tile[te]
    nxt = jnp.where(after < nv[0], te[jnp.minimum(after, n_tiles - 1)], -1)
    nseg = seg[jnp.maximum(nv[0] - 1, 0)][None] + 1
    return seg.astype(jnp.int32), nxt.astype(jnp.int32), nseg.astype(jnp.int32)


def _expert_weight_stream(te_ref, nv_ref, seg_ref, nxt_ref, nseg_ref, w_hbms, slabs, bf16_scrs, sem, width):
    c, r = pl.program_id(0), pl.program_id(1)
    valid = r < nv_ref[0]
    fresh = jnp.logical_and(valid, jnp.logical_or(r == 0, te_ref[r] != te_ref[jnp.maximum(r - 1, 0)]))
    ordinal = c * nseg_ref[0] + seg_ref[r]
    slot = ordinal % 2

    def slab_copies(expert, chunk, s):
        cols = pl.ds(pl.multiple_of(chunk * width, width), width)
        return [pltpu.make_async_copy(w.at[expert, :, cols], slab.at[s], sem.at[n, s])
                for n, (w, slab) in enumerate(zip(w_hbms, slabs))]

    def start(expert, chunk, s):
        for cp in slab_copies(expert, chunk, s):
            cp.start()

    @pl.when(jnp.logical_and(fresh, ordinal == 0))
    def _():
        start(te_ref[0], 0, 0)

    @pl.when(fresh)
    def _():
        for cp in slab_copies(te_ref[r], c, slot):
            cp.wait()
        for slab, scr in zip(slabs, bf16_scrs):
            def cast_rows(n, carry, slab=slab, scr=scr):
                rows = pl.ds(pl.multiple_of(n * CAST_ROWS, CAST_ROWS), CAST_ROWS)
                scr[rows, :] = slab[slot, rows, :].astype(BF16)
                return carry

            lax.fori_loop(0, scr.shape[0] // CAST_ROWS, cast_rows, 0)
        nxt = nxt_ref[r]

        @pl.when(nxt >= 0)
        def _():
            start(nxt, c, 1 - slot)

        @pl.when(jnp.logical_and(nxt < 0, c + 1 < pl.num_programs(0)))
        def _():
            start(te_ref[0], c + 1, 1 - slot)

    return valid


def _gm1_kernel(te_ref, nv_ref, seg_ref, nxt_ref, nseg_ref, xs_ref, w1_ref, w3_ref, o_ref,
                w1f_scr, w3f_scr, w1b_scr, w3b_scr, sem):
    valid = _expert_weight_stream(te_ref, nv_ref, seg_ref, nxt_ref, nseg_ref, (w1_ref, w3_ref),
                                  (w1f_scr, w3f_scr), (w1b_scr, w3b_scr), sem, w1b_scr.shape[1])

    @pl.when(valid)
    def _():
        left, right = _unpack_bf16_pair(xs_ref[...])
        x = jnp.concatenate([left.astype(BF16), right.astype(BF16)], axis=1)
        a = jnp.dot(x, w1b_scr[...], preferred_element_type=F32)
        b = jnp.dot(x, w3b_scr[...], preferred_element_type=F32)
        o_ref[...] = (a * jax.nn.sigmoid(a) * b).astype(o_ref.dtype)

    @pl.when(jnp.logical_not(valid))
    def _():
        o_ref[...] = jnp.zeros(o_ref.shape, o_ref.dtype)


def _gm1(tables, xs, w1, w3, tf):
    n_rows = xs.shape[0]
    d, f = w1.shape[1], w1.shape[2]
    nt = n_rows // ROW_TILE

    def rc(r, nv_):
        return jnp.minimum(r, nv_[0] - 1)

    return pl.pallas_call(
        _gm1_kernel,
        grid_spec=pltpu.PrefetchScalarGridSpec(
            num_scalar_prefetch=len(tables),
            grid=(f // tf, nt),
            in_specs=[pl.BlockSpec((ROW_TILE, d // 2), lambda c, r, te_, nv_, *_: (rc(r, nv_), 0)),
                      pl.BlockSpec(memory_space=pl.ANY),
                      pl.BlockSpec(memory_space=pl.ANY)],
            out_specs=pl.BlockSpec((ROW_TILE, tf), lambda c, r, *_: (r, c)),
            scratch_shapes=[pltpu.VMEM((2, d, tf), F32), pltpu.VMEM((2, d, tf), F32),
                            pltpu.VMEM((d, tf), BF16), pltpu.VMEM((d, tf), BF16),
                            pltpu.SemaphoreType.DMA((2, 2))]),
        out_shape=jax.ShapeDtypeStruct((n_rows, f), BF16),
        compiler_params=_cparams(("arbitrary", "arbitrary"), VMEM_LARGE_MB),
        name="gm1",
    )(*tables, xs, w1, w3)


def _gm2_kernel(te_ref, nv_ref, seg_ref, nxt_ref, nseg_ref, h_ref, w2_ref, o_ref, w2f_scr, w2b_scr, sem):
    valid = _expert_weight_stream(te_ref, nv_ref, seg_ref, nxt_ref, nseg_ref, (w2_ref,),
                                  (w2f_scr,), (w2b_scr,), sem, w2b_scr.shape[1])

    @pl.when(valid)
    def _():
        y = jnp.dot(h_ref[...], w2b_scr[...], preferred_element_type=F32)
        half = y.shape[1] // 2
        o_ref[...] = _pack_bf16_pair(y[:, :half], y[:, half:])

    @pl.when(jnp.logical_not(valid))
    def _():
        o_ref[...] = jnp.zeros(o_ref.shape, o_ref.dtype)


def _gm2(tables, hid, w2, tn):
    n_rows, f = hid.shape
    d = w2.shape[2]
    tn = min(tn, d)
    nt = n_rows // ROW_TILE

    def rc(r, nv_):
        return jnp.minimum(r, nv_[0] - 1)

    return pl.pallas_call(
        _gm2_kernel,
        grid_spec=pltpu.PrefetchScalarGridSpec(
            num_scalar_prefetch=len(tables),
            grid=(d // tn, nt),
            in_specs=[pl.BlockSpec((ROW_TILE, f), lambda c, r, te_, nv_, *_: (rc(r, nv_), 0)),
                      pl.BlockSpec(memory_space=pl.ANY)],
            out_specs=pl.BlockSpec((ROW_TILE, tn // 2), lambda c, r, *_: (r, c)),
            scratch_shapes=[pltpu.VMEM((2, f, tn), F32), pltpu.VMEM((f, tn), BF16),
                            pltpu.SemaphoreType.DMA((1, 2))]),
        out_shape=jax.ShapeDtypeStruct((n_rows, d // 2), jnp.int32),
        compiler_params=_cparams(("arbitrary", "arbitrary"), VMEM_MB),
        name="gm2",
    )(*tables, hid, w2)


def _combine_kernel(dest_ref, y_ref, h_ref, route_ref, g_ref, b_ref, o_ref, ybuf, sem, *, tb, tn):
    i, nblk = pl.program_id(0), pl.num_programs(0)

    def row_copy(blk, buf, t, e):
        d = dest_ref[(blk * tb + t) * 2 + e]
        return pltpu.make_async_copy(y_ref.at[pl.ds(d, 1), :], ybuf.at[buf, e, pl.ds(t, 1), :], sem.at[buf])

    def fetch(blk, buf):
        def body(t, carry):
            row_copy(blk, buf, t, 0).start()
            row_copy(blk, buf, t, 1).start()
            return carry

        lax.fori_loop(0, tb, body, 0, unroll=DMA_ISSUE_UNROLL)

    @pl.when(i == 0)
    def _():
        fetch(0, 0)

    @pl.when(i + 1 < nblk)
    def _():
        fetch(i + 1, (i + 1) % 2)

    cur = i % 2

    for e in range(2):
        pltpu.make_async_copy(y_ref.at[pl.ds(0, tb), :], ybuf.at[cur, e], sem.at[cur]).wait()
    half = tn // 2
    pieces = []
    for c in range(h_ref.shape[1] // tn):
        l0, r0 = _unpack_bf16_pair(ybuf[cur, 0, :, c * half:(c + 1) * half])
        l1, r1 = _unpack_bf16_pair(ybuf[cur, 1, :, c * half:(c + 1) * half])
        pieces += [route_ref[:, 0:1] * l0 + route_ref[:, 1:2] * l1, route_ref[:, 0:1] * r0 + route_ref[:, 1:2] * r1]
    moe = jnp.concatenate(pieces, axis=1)
    pre = DEEPNORM_ALPHA * h_ref[...] + moe
    mu = jnp.mean(pre, axis=1, keepdims=True)
    var = jnp.mean(jnp.square(pre - mu), axis=1, keepdims=True)
    o_ref[...] = (pre - mu) * lax.rsqrt(var + LN_EPS) * g_ref[...] + b_ref[...]


def _combine_ln(dest_flat, y, h, route, g, b, tb, tn):
    m, d = h.shape
    tb = min(tb, m)
    return pl.pallas_call(
        functools.partial(_combine_kernel, tb=tb, tn=tn),
        grid_spec=pltpu.PrefetchScalarGridSpec(
            num_scalar_prefetch=1,
            grid=(m // tb,),
            in_specs=[pl.BlockSpec(memory_space=pl.ANY),
                      pl.BlockSpec((tb, d), lambda i, dest: (i, 0)),
                      pl.BlockSpec((tb, LANES), lambda i, dest: (i, 0)),
                      pl.BlockSpec((1, d), lambda i, dest: (0, 0)),
                      pl.BlockSpec((1, d), lambda i, dest: (0, 0))],
            out_specs=pl.BlockSpec((tb, d), lambda i, dest: (i, 0)),
            scratch_shapes=[pltpu.VMEM((2, 2, tb, d // 2), jnp.int32), pltpu.SemaphoreType.DMA((2,))]),
        out_shape=jax.ShapeDtypeStruct((m, d), F32),
        compiler_params=_cparams(("arbitrary",), VMEM_MB),
        name="combine",
    )(dest_flat, y, h, route, g, b)


def _alibi_slopes(n):
    return jnp.asarray(2.0 ** (-8.0 * np.arange(1, n + 1) / n), dtype=F32)


def kernel(x, w_in, lam_q1, lam_k1, lam_q2, lam_k2, diff_subln_g, w_pa, w_pb, w_o, ln1_g, ln1_b,
           router_wg, router_bg, router_we, router_be, w1, w3, w2, ln2_g, ln2_b):
    bsz, s_len, d = x.shape
    assert bsz == 1 and w_in.shape[0] == DEPTH
    topk = min(TOPK_MAX, s_len // 4)
    x2 = x[0]
    xb = x2.astype(BF16)

    qk_w = N_DIFF_HEADS * 2 * HEAD_DIM
    c_dq, c_dk, c_dv = 0, qk_w, 2 * qk_w
    c_sq = 3 * qk_w
    c_sk = c_sq + N_DSA_HEADS * HEAD_DIM
    c_sv = c_sk + N_DSA_KV * HEAD_DIM
    c_iq = c_sv + N_DSA_KV * HEAD_DIM
    c_ik = c_iq + N_IDX_HEADS * IDX_DIM
    c_ga = c_ik + IDX_DIM + N_IDX_HEADS
    qscale = HEAD_DIM ** -0.5 * LOG2E
    col = np.arange(c_ik)
    is_q = ((col >= c_dq) & (col < c_dk)) | ((col >= c_sq) & (col < c_sk))
    col_scale = jnp.asarray(np.where(is_q, qscale, 1.0)[None, :], F32)
    wt3d = jnp.swapaxes(w_in, 1, 2)
    wt = wt3d[0]
    wt_gates = _cast_rows(wt, c_ga, 2 * d, min(PROJ_TILE[1], d))

    z = _proj_from_f32(xb, wt3d, col_scale, c_ik, *PROJ_TILE)
    small = 2 * LANES
    assert c_ik % small == 0 and c_ik + small <= wt.shape[0] and s_len <= POS_RADIX * POS_RADIX
    zs = _matmul_nt(xb, wt, c_ik // small, small, F32, PROJ_TILE[0], "proj_small")
    ik = zs[:, :IDX_DIM].astype(BF16)
    iwt = zs[:, IDX_DIM:IDX_DIM + N_IDX_HEADS].T
    dvt = _with_ones_rows(z[:, c_dv:c_dv + qk_w].T, N_DIFF_HEADS)
    svt = _with_ones_rows(z[:, c_sv:c_sv + N_DSA_KV * HEAD_DIM].T, N_DSA_KV)
    kaug = _key_aug_table(s_len)

    lam4 = jnp.stack([lam_q1[0], lam_k1[0], lam_q2[0], lam_k2[0]]).astype(F32)
    g_lanes = jnp.broadcast_to(diff_subln_g[0][:, None], (2 * HEAD_DIM, LANES))
    a = _diff_attention(z, dvt, kaug, lam4, g_lanes, _alibi_slopes(N_DIFF_HEADS), c_dq, c_dk, s_len,
                        min(ATTN_TILE, s_len))

    tq_i, tk_i = min(DSA_TILE[0], s_len), min(DSA_TILE[1], s_len)
    maskb = _indexer(z, ik, iwt, c_iq, s_len, tq_i, tk_i, topk)
    b = _dsa_attention(z, svt, kaug, maskb, _alibi_slopes(N_DSA_HEADS), c_sq, c_sk, s_len, tq_i, tk_i)

    merged = _merge(xb, wt_gates, a, w_pa[0].astype(BF16), b, w_pb[0].astype(BF16), *MERGE_TILE)

    wr = jnp.concatenate([router_wg[0], router_we[0],
                          jnp.zeros((d, LANES - N_GROUPS - N_EXPERTS), F32)], axis=1)
    br = jnp.concatenate([router_bg[0], router_be[0],
                          jnp.zeros((LANES - N_GROUPS - N_EXPERTS,), F32)])[None, :]
    h1, route, onehot = _out_ln_router(merged, w_o[0].astype(BF16), x2, ln1_g[0][None, :], ln1_b[0][None, :],
                                       wr, br, *OUT_TILE)

    pos, cnt = _rank(onehot, RANK_ROWS)
    counts = cnt[0, :N_EXPERTS].astype(jnp.int32)
    padded = ((counts + ROW_TILE - 1) // ROW_TILE) * ROW_TILE
    ends = jnp.cumsum(padded)
    start = jnp.zeros((1, LANES), F32).at[0, :N_EXPERTS].set((ends - padded).astype(F32))
    n_tiles = (2 * s_len) // ROW_TILE + N_EXPERTS
    tile_ids = jnp.arange(n_tiles, dtype=jnp.int32)
    te = jnp.minimum(jnp.sum(tile_ids[:, None] >= (ends // ROW_TILE)[None, :], axis=1), N_EXPERTS - 1)
    te = te.astype(jnp.int32)
    nv = (ends[-1] // ROW_TILE).astype(jnp.int32)[None]
    dest = _dest(pos, route, start, RANK_ROWS)
    dest_flat = dest[:, :2].reshape(-1)

    xs = _scatter_rows(dest_flat, h1, n_tiles * ROW_TILE, SCATTER_ROWS)
    tables = (te, nv) + _segment_tables(te, nv, ends // ROW_TILE, n_tiles)
    hid = _gm1(tables, xs, w1[0], w3[0], min(GM1_COLS, w1.shape[3]))
    tn_y = min(GM2_COLS, d)
    y = _gm2(tables, hid, w2[0], tn_y)
    out = _combine_ln(dest_flat, y, h1, route, ln2_g[0][None, :], ln2_b[0][None, :], COMBINE_ROWS, tn_y)
    return out[None]
```

```python
import functools
import math

import numpy as np
import jax
import jax.numpy as jnp
from jax import lax
from jax.experimental import pallas as pl
from jax.experimental.pallas import tpu as pltpu

HEAD_DIM = 128
N_DIFF_HEADS = 8
N_DSA_HEADS = 16
N_DSA_KV = 4
DSA_REP = N_DSA_HEADS // N_DSA_KV
N_IDX_HEADS = 32
IDX_DIM = 128
TOPK_MAX = 256
N_GROUPS = 4
EXPERTS_PER_GROUP = 8
N_EXPERTS = N_GROUPS * EXPERTS_PER_GROUP
LN_EPS = 1e-5
RMS_EPS = 1e-5
DEPTH = 1
DEEPNORM_ALPHA = (2.0 * DEPTH) ** 0.25
LAM_INIT = 0.8 - 0.6 * math.exp(-0.3 * 0)

LANES = 128
SUBLANES = 8
NEG = -1e30
INT_MIN = -(2 ** 31)
ROW_TILE = 256
LOG2E = 1.4426950408889634
POS_RADIX = 256

PROJ_TILE = (1024, 512)
ATTN_TILE = 512
DSA_TILE = (256, 512)
MERGE_TILE = (1024, 256)
OUT_TILE = (512, 512)
RANK_ROWS = 512
SCATTER_ROWS = 256
COMBINE_ROWS = 128
GM1_COLS = 512
GM2_COLS = 4096
VMEM_SMALL_MB = 32
VMEM_MB = 48
VMEM_LARGE_MB = 56

F32 = jnp.float32
BF16 = jnp.bfloat16


def _cparams(sem, vmem_mb):
    return pltpu.CompilerParams(dimension_semantics=sem, vmem_limit_bytes=vmem_mb << 20)


def _dot_nt(a, b):
    return lax.dot_general(a, b, (((1,), (1,)), ((), ())), preferred_element_type=F32)


def _proj_kernel(x_ref, wt_ref, sc_ref, o_ref, wb_scr):
    @pl.when(pl.program_id(1) == 0)
    def _():
        wb_scr[...] = wt_ref[0].astype(BF16)

    o_ref[...] = (_dot_nt(x_ref[...], wb_scr[...]) * sc_ref[...]).astype(o_ref.dtype)


def _proj_from_f32(xb, wt3d, col_scale, n_cols, tm, tn):
    m, d = xb.shape
    tm = min(tm, m)
    return pl.pallas_call(
        _proj_kernel,
        grid=(n_cols // tn, m // tm),
        in_specs=[pl.BlockSpec((tm, d), lambda j, i: (i, 0)),
                  pl.BlockSpec((1, tn, d), lambda j, i: (0, j, 0)),
                  pl.BlockSpec((1, tn), lambda j, i: (0, j))],
        out_specs=pl.BlockSpec((tm, tn), lambda j, i: (i, j)),
        out_shape=jax.ShapeDtypeStruct((m, n_cols), BF16),
        scratch_shapes=[pltpu.VMEM((tn, d), BF16)],
        compiler_params=_cparams(("parallel", "arbitrary"), VMEM_MB),
        name="proj_main",
    )(xb, wt3d, col_scale)


def _cast_rows_kernel(w_ref, o_ref):
    o_ref[...] = w_ref[...].astype(o_ref.dtype)


def _cast_rows(wt, row0, n_rows, tr):
    d = wt.shape[1]
    assert row0 % SUBLANES == 0 and n_rows % tr == 0
    return pl.pallas_call(
        _cast_rows_kernel,
        grid=(n_rows // tr,),
        in_specs=[pl.BlockSpec((pl.Element(tr), pl.Element(d)),
                               lambda i: (pl.multiple_of(row0 + i * tr, SUBLANES), 0))],
        out_specs=pl.BlockSpec((tr, d), lambda i: (i, 0)),
        out_shape=jax.ShapeDtypeStruct((n_rows, d), BF16),
        compiler_params=_cparams(("parallel",), VMEM_MB),
        name="cast_gates",
    )(wt)


def _mm_nt_kernel(a_ref, bt_ref, o_ref):
    o_ref[...] = _dot_nt(a_ref[...], bt_ref[...].astype(BF16)).astype(o_ref.dtype)


def _matmul_nt(a, wt, row_block, n, out_dtype, tm, name):
    m, k = a.shape
    tm = min(tm, m)
    return pl.pallas_call(
        _mm_nt_kernel,
        grid=(m // tm,),
        in_specs=[pl.BlockSpec((tm, k), lambda i: (i, 0)),
                  pl.BlockSpec((n, k), lambda i: (row_block, 0))],
        out_specs=pl.BlockSpec((tm, n), lambda i: (i, 0)),
        out_shape=jax.ShapeDtypeStruct((m, n), out_dtype),
        compiler_params=_cparams(("parallel",), VMEM_MB),
        name=name,
    )(a, wt)


def _key_aug_table(s_len):
    pos = jnp.arange(s_len, dtype=jnp.int32)[:, None]
    lane = jnp.arange(LANES, dtype=jnp.int32)[None, :]
    hi = (pos // POS_RADIX).astype(F32)
    lo = (pos % POS_RADIX).astype(F32)
    t = jnp.where(lane < 2, hi, jnp.where(lane < 4, lo, jnp.where(lane < 7, 1.0, 0.0)))
    return t.astype(BF16)


def _bf16_piece(x):
    return x.astype(BF16).astype(F32)


def _query_aug(slope, qbase, rows):
    s2 = jnp.full((SUBLANES, LANES), slope, F32) * LOG2E
    big = s2 * POS_RADIX
    off = -s2 * jnp.full((SUBLANES, LANES), qbase, F32)
    big_hi = _bf16_piece(big)
    s2_hi = _bf16_piece(s2)
    off_hi = _bf16_piece(off)
    off_mid = _bf16_piece(off - off_hi)
    lane = lax.broadcasted_iota(jnp.int32, (SUBLANES, LANES), 1)
    pieces = [big_hi, big - big_hi, s2_hi, s2 - s2_hi, off_hi, off_mid, off - off_hi - off_mid]
    row = jnp.zeros((SUBLANES, LANES), F32)
    for n, piece in enumerate(pieces):
        row = jnp.where(lane == n, piece, row)
    return jnp.broadcast_to(row[0:1, :], (rows, LANES)).astype(BF16)


QUERY_SUB = 256


def _causal_steps(nq, last_of):
    pairs = [(i, j) for i in range(nq) for j in range(last_of(i) + 1)]
    return (jnp.asarray([p[0] for p in pairs], jnp.int32), jnp.asarray([p[1] for p in pairs], jnp.int32))


ONES_ROWS = 16


def _with_ones_rows(vt, n_blocks):
    r = vt.shape[0] // n_blocks
    v3 = vt.reshape(n_blocks, r, vt.shape[1])
    ones = jnp.ones((n_blocks, ONES_ROWS, vt.shape[1]), vt.dtype)
    return jnp.concatenate([v3, ones], axis=1).reshape(n_blocks * (r + ONES_ROWS), vt.shape[1])


def _online_softmax_pv(s, vt1, m_scr, acc_scr, idx, cols):
    at = idx + (slice(None), cols)
    m_prev = m_scr[at]
    m_next = jnp.maximum(m_prev, jnp.max(s, axis=0, keepdims=True))
    p = jnp.exp2((s - m_next).astype(BF16))
    alpha = jnp.exp2(m_prev - m_next)
    m_scr[at] = m_next
    acc_scr[at] = acc_scr[at] * alpha + jnp.dot(vt1, p, preferred_element_type=F32)


DIFF_HEADS_PER_STEP = 2


def _diff_kernel(qi_ref, kj_ref, slopes_ref, lam_ref, g_ref, q_ref, k_ref, ka_ref, vt_ref, o_ref,
                 qa_scr, kk_scr, s0_scr, s1_scr, m_scr, acc_scr, *, tq, n_steps):
    hp, n = pl.program_id(0), pl.program_id(1)
    reps = tq // LANES
    nsub = tq // QUERY_SUB
    w = 2 * HEAD_DIM
    maps = [(hh, c) for hh in range(DIFF_HEADS_PER_STEP) for c in range(2)]
    na = jnp.minimum(n, n_steps - 1)
    nb = jnp.maximum(n - 1, 0)
    ia = qi_ref[na]
    ib, jb = qi_ref[nb], kj_ref[nb]
    has_a, has_b = n < n_steps, n >= 1
    diag_b = jnp.logical_and(has_b, jb == ib)

    score_slots = (s0_scr, s1_scr)

    def map_cols(hh, c):
        return slice(hh * w + c * HEAD_DIM, hh * w + (c + 1) * HEAD_DIM)

    def load_queries():
        for hh in range(DIFF_HEADS_PER_STEP):
            qaug = _query_aug(slopes_ref[hp * DIFF_HEADS_PER_STEP + hh], (ia * tq).astype(F32), tq)
            for c in range(2):
                qa_scr[2 * hh + c, :, :HEAD_DIM] = q_ref[:, map_cols(hh, c)]
                qa_scr[2 * hh + c, :, HEAD_DIM:] = qaug

    def form_scores(slot):
        ka = ka_ref[...]
        for hh, c in maps:
            kk_scr[2 * hh + c, :, :HEAD_DIM] = k_ref[:, map_cols(hh, c)]
            kk_scr[2 * hh + c, :, HEAD_DIM:] = ka
        for m in range(len(maps)):
            score_slots[slot][m] = _dot_nt(kk_scr[m], qa_scr[m])

    def softmax_pv(slot, masked):
        for m, (hh, c) in enumerate(maps):
            vt1 = vt_ref[hh * (w + ONES_ROWS):(hh + 1) * (w + ONES_ROWS), :]
            for u in range(nsub):
                cols = slice(u * QUERY_SUB, (u + 1) * QUERY_SUB)
                s = score_slots[slot][m, :, cols]
                if masked:
                    keep = (lax.broadcasted_iota(jnp.int32, (tq, 1), 0)
                            <= u * QUERY_SUB + lax.broadcasted_iota(jnp.int32, (1, QUERY_SUB), 1))
                    s = jnp.where(keep, s, NEG)
                _online_softmax_pv(s, vt1, m_scr, acc_scr, (m,), cols)

    @pl.when(jnp.logical_and(has_b, jb == 0))
    def _():
        m_scr[...] = jnp.full(m_scr.shape, NEG, F32)
        acc_scr[...] = jnp.zeros(acc_scr.shape, F32)

    def emit_block():
        lam = (jnp.exp(jnp.sum(lam_ref[0:1, :] * lam_ref[1:2, :], axis=1, keepdims=True))
               - jnp.exp(jnp.sum(lam_ref[2:3, :] * lam_ref[3:4, :], axis=1, keepdims=True)) + LAM_INIT)
        g = jnp.concatenate([g_ref[...]] * reps, axis=1)
        for hh in range(DIFF_HEADS_PER_STEP):
            m = 2 * hh
            o = (acc_scr[m, :w, :] / acc_scr[m, w:w + 1, :]
                 - lam * (acc_scr[m + 1, :w, :] / acc_scr[m + 1, w:w + 1, :]))
            o = o * lax.rsqrt(jnp.mean(o * o, axis=0, keepdims=True) + RMS_EPS) * g
            o_ref[:, hh * w:(hh + 1) * w] = (o * (1.0 - LAM_INIT)).T.astype(o_ref.dtype)

    for par in range(2):
        mine = (n % 2) == par

        @pl.when(jnp.logical_and(mine, jnp.logical_and(has_b, jnp.logical_not(diag_b))))
        def _(par=par):
            form_scores(par)
            softmax_pv(1 - par, False)

        @pl.when(jnp.logical_and(mine, jnp.logical_and(diag_b, has_a)))
        def _(par=par):
            load_queries()
            form_scores(par)
            softmax_pv(1 - par, True)
            emit_block()

        @pl.when(jnp.logical_and(mine, jnp.logical_and(diag_b, jnp.logical_not(has_a))))
        def _(par=par):
            softmax_pv(1 - par, True)
            emit_block()

        @pl.when(jnp.logical_and(mine, n == 0))
        def _(par=par):
            load_queries()
            form_scores(par)


def _diff_attention(z, vt, kaug, lam4, g_lanes, slopes, col_q, col_k, s_len, tq):
    nq = s_len // tq
    w = 2 * HEAD_DIM
    wb = DIFF_HEADS_PER_STEP * w
    nmaps = 2 * DIFF_HEADS_PER_STEP
    qb, kb = col_q // wb, col_k // wb
    qi, kj = _causal_steps(nq, lambda i: i)
    n_steps = int(qi.shape[0])

    def ahead(s):
        return jnp.minimum(s, n_steps - 1)

    def behind(s):
        return jnp.maximum(s - 1, 0)

    return pl.pallas_call(
        functools.partial(_diff_kernel, tq=tq, n_steps=n_steps),
        grid_spec=pltpu.PrefetchScalarGridSpec(
            num_scalar_prefetch=3,
            grid=(N_DIFF_HEADS // DIFF_HEADS_PER_STEP, n_steps + 1),
            in_specs=[pl.BlockSpec((4, HEAD_DIM), lambda h, s, qi_, kj_, sl_: (0, 0)),
                      pl.BlockSpec((w, LANES), lambda h, s, qi_, kj_, sl_: (0, 0)),
                      pl.BlockSpec((tq, wb), lambda h, s, qi_, kj_, sl_: (qi_[ahead(s)], qb + h)),
                      pl.BlockSpec((tq, wb), lambda h, s, qi_, kj_, sl_: (kj_[ahead(s)], kb + h)),
                      pl.BlockSpec((tq, LANES), lambda h, s, qi_, kj_, sl_: (kj_[ahead(s)], 0)),
                      pl.BlockSpec((DIFF_HEADS_PER_STEP * (w + ONES_ROWS), tq),
                                   lambda h, s, qi_, kj_, sl_: (h, kj_[behind(s)]))],
            out_specs=pl.BlockSpec((tq, wb), lambda h, s, qi_, kj_, sl_: (qi_[behind(s)], h)),
            scratch_shapes=[pltpu.VMEM((nmaps, tq, w), BF16), pltpu.VMEM((nmaps, tq, w), BF16),
                            pltpu.VMEM((nmaps, tq, tq), F32), pltpu.VMEM((nmaps, tq, tq), F32),
                            pltpu.VMEM((nmaps, 1, tq), F32),
                            pltpu.VMEM((nmaps, w + ONES_ROWS, tq), F32)]),
        out_shape=jax.ShapeDtypeStruct((s_len, N_DIFF_HEADS * w), BF16),
        compiler_params=_cparams(("parallel", "arbitrary"), VMEM_MB),
        name="diffattn",
    )(qi, kj, slopes, lam4, g_lanes, z, z, kaug, vt)


IQ_SPLIT = 4


HALF_BITS = 16
HALF_MASK = (1 << HALF_BITS) - 1
I16_MIN = -(1 << (HALF_BITS - 1))


def _indexer_kernel(iq0_ref, iq1_ref, iq2_ref, iq3_ref, ik_ref, iwt_ref, o_ref,
                    iqh_scr, key_scr, hi_scr, lo_scr, acc_scr, pcut_scr, *, tq, tkc, topk, iw_scale, pos_bits):
    i = pl.program_id(0)
    nk = key_scr.shape[0]
    nchunks = (i * tq + tq - 1) // tkc + 1
    per = N_IDX_HEADS // IQ_SPLIT

    for h in range(N_IDX_HEADS):
        src = (iq0_ref, iq1_ref, iq2_ref, iq3_ref)[h // per]
        iqh_scr[h] = src[:, (h % per) * IDX_DIM:(h % per + 1) * IDX_DIM]

    qpos = i * tq + lax.broadcasted_iota(jnp.int32, (1, tq), 1)

    def key_pos(c):
        return c * tkc + lax.broadcasted_iota(jnp.int32, (tkc, 1), 0)

    def count_keys(pred):
        def body(c, acc):
            x = jnp.where(pred(c, key_scr[c]), 1, 0)
            return acc + jnp.sum(x.reshape(tkc // SUBLANES, SUBLANES, tq), axis=0)

        cnt = lax.fori_loop(0, nchunks, body, jnp.zeros((SUBLANES, tq), jnp.int32))
        return jnp.sum(cnt, axis=0, keepdims=True)

    def chunk(c, carry):
        kc = ik_ref[pl.ds(pl.multiple_of(c * tkc, tkc), tkc), :]
        acc_scr[...] = jnp.zeros(acc_scr.shape, F32)

        def head(h, carry2):
            sc = _dot_nt(kc, iqh_scr[h])
            acc_scr[...] += (iwt_ref[pl.ds(h, 1), :] * iw_scale) * jnp.maximum(sc, 0.0)
            return carry2

        lax.fori_loop(0, N_IDX_HEADS, head, 0, unroll=16)
        score = acc_scr[...]
        score = jnp.where(score == 0.0, 0.0, score)
        bits = pltpu.bitcast(score, jnp.int32)
        skey = bits ^ ((bits >> 31) & 0x7FFFFFFF)
        key = jnp.where(key_pos(c) <= qpos, skey, INT_MIN)
        key_scr[c] = key
        hi_scr[c] = (key >> HALF_BITS).astype(jnp.int16)
        lo_scr[c] = ((key & HALF_MASK) + I16_MIN).astype(jnp.int16)
        return carry

    lax.fori_loop(0, nchunks, chunk, 0)

    def count_halves(half_scr, pred):
        rows = 2 * SUBLANES

        def body(c, acc):
            x = jnp.where(pred(half_scr[c]), jnp.int16(1), jnp.int16(0))
            for g in range(tkc // rows):
                acc = acc + x[g * rows:(g + 1) * rows, :]
            return acc

        cnt = lax.fori_loop(0, nchunks, body, jnp.zeros((rows, tq), jnp.int16))
        return jnp.sum(cnt.astype(jnp.int32), axis=0, keepdims=True)

    def radix_select(half_scr, want):
        def bit_step(bi, u):
            cand = u | lax.shift_left(jnp.int32(1), HALF_BITS - 1 - bi)
            cand16 = (cand + I16_MIN).astype(jnp.int16)
            return jnp.where(count_halves(half_scr, lambda k: k >= cand16) >= want, cand, u)

        return lax.fori_loop(0, HALF_BITS, bit_step, jnp.zeros((1, tq), jnp.int32)) + I16_MIN

    t_hi = radix_select(hi_scr, topk)
    t_hi16 = t_hi.astype(jnp.int16)
    want_lo = topk - count_halves(hi_scr, lambda k: k > t_hi16)

    def keep_lo(c, carry):
        lo_scr[c] = jnp.where(hi_scr[c] == t_hi16, lo_scr[c], jnp.int16(I16_MIN))
        return carry

    lax.fori_loop(0, nchunks, keep_lo, 0)
    t_lo = radix_select(lo_scr, want_lo)
    thr = lax.shift_left(t_hi, HALF_BITS) | (t_lo - I16_MIN)
    thr = jnp.maximum(thr, INT_MIN + 1)

    pcut_scr[...] = jnp.full(pcut_scr.shape, (1 << pos_bits) - 1, jnp.int32)

    @pl.when(jnp.max(count_keys(lambda c, k: k >= thr)) > topk)
    def _():
        need = topk - count_keys(lambda c, k: k > thr)

        def pos_step(bi, p):
            cand = p | lax.shift_left(jnp.int32(1), pos_bits - 1 - bi)
            tied_before = count_keys(lambda c, k: jnp.where(k == thr, key_pos(c), cand) < cand)
            return jnp.where(tied_before < need, cand, p)

        pcut_scr[...] = lax.fori_loop(0, pos_bits, pos_step, jnp.zeros((1, tq), jnp.int32))

    pcut = pcut_scr[...]

    def emit(c, carry):
        k = key_scr[c]
        tie_bias = jnp.where(key_pos(c) <= pcut, 0.0, NEG)
        o_ref[0, c] = jnp.where(k > thr, 0.0, jnp.where(k == thr, tie_bias, NEG)).astype(o_ref.dtype)
        return carry

    lax.fori_loop(0, nchunks, emit, 0)

    def fill(c, carry):
        o_ref[0, c] = jnp.full((tkc, tq), NEG, o_ref.dtype)
        return carry

    lax.fori_loop(nchunks, nk, fill, 0)


def _indexer(z, ik, iwt, col_iq, s_len, tq, tkc, topk):
    nq, nk = s_len // tq, s_len // tkc
    iw_scale = N_IDX_HEADS ** -0.5 * IDX_DIM ** -0.5
    wq = N_IDX_HEADS * IDX_DIM // IQ_SPLIT
    qb = col_iq // wq
    iq_specs = [pl.BlockSpec((tq, wq), functools.partial(lambda i, n: (i, qb + n), n=n)) for n in range(IQ_SPLIT)]
    return pl.pallas_call(
        functools.partial(_indexer_kernel, tq=tq, tkc=tkc, topk=topk, iw_scale=iw_scale,
                          pos_bits=max(1, (s_len - 1).bit_length())),
        grid=(nq,),
        in_specs=iq_specs + [pl.BlockSpec((s_len, IDX_DIM), lambda i: (0, 0)),
                             pl.BlockSpec((N_IDX_HEADS, tq), lambda i: (0, i))],
        out_specs=pl.BlockSpec((1, nk, tkc, tq), lambda i: (i, 0, 0, 0)),
        out_shape=jax.ShapeDtypeStruct((nq, nk, tkc, tq), BF16),
        scratch_shapes=[pltpu.VMEM((N_IDX_HEADS, tq, IDX_DIM), BF16),
                        pltpu.VMEM((nk, tkc, tq), jnp.int32),
                        pltpu.VMEM((nk, tkc, tq), jnp.int16),
                        pltpu.VMEM((nk, tkc, tq), jnp.int16),
                        pltpu.VMEM((tkc, tq), F32),
                        pltpu.VMEM((1, tq), jnp.int32)],
        compiler_params=_cparams(("parallel",), VMEM_MB),
        name="indexer",
    )(z, z, z, z, ik, iwt)


DSA_GROUPS_PER_STEP = 2


def _dsa_kernel(qi_ref, kj_ref, slopes_ref, q_ref, k_ref, ka_ref, vt_ref, mb_ref, o_ref,
                qa_scr, kk_scr, m_scr, acc_scr, *, tq, tk):
    gp, step_id = pl.program_id(0), pl.program_id(1)
    i, j = qi_ref[step_id], kj_ref[step_id]
    last = (i * tq + tq - 1) // tk
    heads = DSA_GROUPS_PER_STEP * DSA_REP

    @pl.when(j == 0)
    def _():
        m_scr[...] = jnp.full(m_scr.shape, NEG, F32)
        acc_scr[...] = jnp.zeros(acc_scr.shape, F32)
        for n in range(heads):
            rows = slice(n * tq, (n + 1) * tq)
            qa_scr[rows, :HEAD_DIM] = q_ref[:, n * HEAD_DIM:(n + 1) * HEAD_DIM]
            qa_scr[rows, HEAD_DIM:] = _query_aug(slopes_ref[gp * heads + n], (i * tq).astype(F32), tq)

    ka = ka_ref[...]
    for gg in range(DSA_GROUPS_PER_STEP):
        kk_scr[gg, :, :HEAD_DIM] = k_ref[:, gg * HEAD_DIM:(gg + 1) * HEAD_DIM]
        kk_scr[gg, :, HEAD_DIM:] = ka
    mb = mb_ref[0, 0].astype(F32)
    scores = [_dot_nt(kk_scr[gg], qa_scr[gg * DSA_REP * tq:(gg + 1) * DSA_REP * tq, :])
              for gg in range(DSA_GROUPS_PER_STEP)]
    for gg in range(DSA_GROUPS_PER_STEP):
        vt1 = vt_ref[gg * (HEAD_DIM + ONES_ROWS):(gg + 1) * (HEAD_DIM + ONES_ROWS), :]
        for r in range(DSA_REP):
            n = gg * DSA_REP + r
            _online_softmax_pv(scores[gg][:, r * tq:(r + 1) * tq] + mb, vt1, m_scr, acc_scr, (),
                               slice(n * tq, (n + 1) * tq))

    @pl.when(j == last)
    def _():
        o = acc_scr[:HEAD_DIM, :] / acc_scr[HEAD_DIM:HEAD_DIM + 1, :]
        for n in range(heads):
            o_ref[:, n * HEAD_DIM:(n + 1) * HEAD_DIM] = o[:, n * tq:(n + 1) * tq].T.astype(o_ref.dtype)


def _dsa_attention(z, vt, kaug, maskb, slopes, col_q, col_k, s_len, tq, tk):
    nq = s_len // tq
    gps = DSA_GROUPS_PER_STEP
    heads = gps * DSA_REP
    qw, kw = heads * HEAD_DIM, gps * HEAD_DIM
    qb, kb = col_q // qw, col_k // kw
    qi, kj = _causal_steps(nq, lambda i: (i * tq + tq - 1) // tk)
    return pl.pallas_call(
        functools.partial(_dsa_kernel, tq=tq, tk=tk),
        grid_spec=pltpu.PrefetchScalarGridSpec(
            num_scalar_prefetch=3,
            grid=(N_DSA_KV // gps, qi.shape[0]),
            in_specs=[pl.BlockSpec((tq, qw), lambda g, s, qi_, kj_, sl_: (qi_[s], qb + g)),
                      pl.BlockSpec((tk, kw), lambda g, s, qi_, kj_, sl_: (kj_[s], kb + g)),
                      pl.BlockSpec((tk, LANES), lambda g, s, qi_, kj_, sl_: (kj_[s], 0)),
                      pl.BlockSpec((gps * (HEAD_DIM + ONES_ROWS), tk), lambda g, s, qi_, kj_, sl_: (g, kj_[s])),
                      pl.BlockSpec((1, 1, tk, tq), lambda g, s, qi_, kj_, sl_: (qi_[s], kj_[s], 0, 0))],
            out_specs=pl.BlockSpec((tq, qw), lambda g, s, qi_, kj_, sl_: (qi_[s], g)),
            scratch_shapes=[pltpu.VMEM((heads * tq, 2 * HEAD_DIM), BF16),
                            pltpu.VMEM((gps, tk, 2 * HEAD_DIM), BF16),
                            pltpu.VMEM((1, heads * tq), F32),
                            pltpu.VMEM((HEAD_DIM + ONES_ROWS, heads * tq), F32)]),
        out_shape=jax.ShapeDtypeStruct((s_len, N_DSA_HEADS * HEAD_DIM), BF16),
        compiler_params=_cparams(("parallel", "arbitrary"), VMEM_MB),
        name="dsa",
    )(qi, kj, slopes, z, z, kaug, vt, maskb)


def _merge_kernel(x_ref, wga_ref, wgb_ref, a_ref, wpa_ref, b_ref, wpb_ref, o_ref):
    x = x_ref[...]
    ga = _dot_nt(x, wga_ref[...])
    gb = _dot_nt(x, wgb_ref[...])
    pa = jnp.dot(a_ref[...], wpa_ref[...], preferred_element_type=F32)
    pb = jnp.dot(b_ref[...], wpb_ref[...], preferred_element_type=F32)
    o_ref[...] = (jax.nn.sigmoid(ga) * pa + jax.nn.sigmoid(gb) * pb).astype(o_ref.dtype)


def _merge(xb, wt_gates, a, wpa, b, wpb, tm, tn):
    m, d = xb.shape
    ka, kb = a.shape[1], b.shape[1]
    tm, tn = min(tm, m), min(tn, d)
    row = lambda i, j: (i, 0)
    col = lambda i, j: (0, j)
    return pl.pallas_call(
        _merge_kernel,
        grid=(m // tm, d // tn),
        in_specs=[pl.BlockSpec((tm, d), row),
                  pl.BlockSpec((tn, d), lambda i, j: (j, 0)), pl.BlockSpec((tn, d), lambda i, j: (d // tn + j, 0)),
                  pl.BlockSpec((tm, ka), row), pl.BlockSpec((ka, tn), col),
                  pl.BlockSpec((tm, kb), row), pl.BlockSpec((kb, tn), col)],
        out_specs=pl.BlockSpec((tm, tn), lambda i, j: (i, j)),
        out_shape=jax.ShapeDtypeStruct((m, d), BF16),
        compiler_params=_cparams(("parallel", "arbitrary"), VMEM_LARGE_MB),
        name="merge",
    )(xb, wt_gates, wt_gates, a, wpa, b, wpb)


def _split_bf16(x):
    hi = x.astype(BF16)
    lo = (x - hi.astype(F32)).astype(BF16)
    return hi, lo


def _out_kernel(mg_ref, wo_ref, x_ref, g_ref, b_ref, wr_ref, br_ref,
                h_ref, route_ref, oh_ref, pre_scr, *, tn, nj):
    j = pl.program_id(1)
    y = jnp.dot(mg_ref[...], wo_ref[...], preferred_element_type=F32)
    pre_scr[j] = DEEPNORM_ALPHA * x_ref[...] + y

    @pl.when(j == nj - 1)
    def _():
        d = nj * tn
        tot = pre_scr[0].sum(axis=1, keepdims=True)
        for jj in range(1, nj):
            tot = tot + pre_scr[jj].sum(axis=1, keepdims=True)
        mu = tot / d
        sq = jnp.square(pre_scr[0] - mu).sum(axis=1, keepdims=True)
        for jj in range(1, nj):
            sq = sq + jnp.square(pre_scr[jj] - mu).sum(axis=1, keepdims=True)
        rstd = lax.rsqrt(sq / d + LN_EPS)
        logits = jnp.zeros(route_ref.shape, F32)
        for jj in range(nj):
            cs = slice(jj * tn, (jj + 1) * tn)
            hn = (pre_scr[jj] - mu) * rstd * g_ref[:, cs] + b_ref[:, cs]
            h_ref[:, cs] = hn
            h_hi, h_lo = _split_bf16(hn)
            w_hi, w_lo = _split_bf16(wr_ref[cs, :])
            logits = logits + (jnp.dot(h_hi, w_hi, preferred_element_type=F32)
                               + jnp.dot(h_hi, w_lo, preferred_element_type=F32)
                               + jnp.dot(h_lo, w_hi, preferred_element_type=F32))
        logits = logits + br_ref[...]
        lane = lax.broadcasted_iota(jnp.int32, logits.shape, 1)
        big = jnp.int32(4 * LANES)
        gl = jnp.where(lane < N_GROUPS, logits, -jnp.inf)
        gmax = jnp.max(gl, axis=1, keepdims=True)
        gsel = jnp.min(jnp.where(gl == gmax, lane, big), axis=1, keepdims=True)
        ggate = 1.0 / jnp.sum(jnp.exp(gl - gmax), axis=1, keepdims=True)
        eid = lane - N_GROUPS
        ingrp = (eid >= gsel * EXPERTS_PER_GROUP) & (eid < (gsel + 1) * EXPERTS_PER_GROUP)
        el = jnp.where(ingrp, logits, -jnp.inf)
        v1 = jnp.max(el, axis=1, keepdims=True)
        i1 = jnp.min(jnp.where(el == v1, lane, big), axis=1, keepdims=True)
        el2 = jnp.where(lane == i1, -jnp.inf, el)
        v2 = jnp.max(el2, axis=1, keepdims=True)
        i2 = jnp.min(jnp.where(el2 == v2, lane, big), axis=1, keepdims=True)
        t = jnp.exp(v2 - v1)
        g1 = ggate / (1.0 + t)
        g2 = g1 * t
        e1 = (i1 - N_GROUPS).astype(F32)
        e2 = (i2 - N_GROUPS).astype(F32)
        route_ref[...] = jnp.where(lane == 0, g1, jnp.where(lane == 1, g2,
                                   jnp.where(lane == 2, e1, jnp.where(lane == 3, e2, 0.0))))
        oh_ref[...] = jnp.where(lane == i1 - N_GROUPS, 1.0,
                                jnp.where(lane == i2 - N_GROUPS, 1.0, 0.0)).astype(oh_ref.dtype)


def _out_ln_router(mg, wo, x, g, b, wr, br, tm, tn):
    m, d = x.shape
    tm, tn = min(tm, m), min(tn, d)
    nj = d // tn
    return pl.pallas_call(
        functools.partial(_out_kernel, tn=tn, nj=nj),
        grid=(m // tm, nj),
        in_specs=[pl.BlockSpec((tm, d), lambda i, j: (i, 0)),
                  pl.BlockSpec((d, tn), lambda i, j: (0, j)),
                  pl.BlockSpec((tm, tn), lambda i, j: (i, j)),
                  pl.BlockSpec((1, d), lambda i, j: (0, 0)),
                  pl.BlockSpec((1, d), lambda i, j: (0, 0)),
                  pl.BlockSpec((d, LANES), lambda i, j: (0, 0)),
                  pl.BlockSpec((1, LANES), lambda i, j: (0, 0))],
        out_specs=[pl.BlockSpec((tm, d), lambda i, j: (i, 0)),
                   pl.BlockSpec((tm, LANES), lambda i, j: (i, 0)),
                   pl.BlockSpec((tm, LANES), lambda i, j: (i, 0))],
        out_shape=[jax.ShapeDtypeStruct((m, d), F32),
                   jax.ShapeDtypeStruct((m, LANES), F32),
                   jax.ShapeDtypeStruct((m, LANES), BF16)],
        scratch_shapes=[pltpu.VMEM((nj, tm, tn), F32)],
        compiler_params=_cparams(("parallel", "arbitrary"), VMEM_LARGE_MB),
        name="outln",
    )(mg, wo, x, g, b, wr, br)


def _rank_kernel(oh_ref, pos_ref, cnt_ref, base_scr, *, tb):
    @pl.when(pl.program_id(0) == 0)
    def _():
        base_scr[...] = jnp.zeros(base_scr.shape, F32)

    oh = oh_ref[...]
    r = lax.broadcasted_iota(jnp.int32, (tb, tb), 0)
    c = lax.broadcasted_iota(jnp.int32, (tb, tb), 1)
    tri = jnp.where(c <= r, 1.0, 0.0).astype(BF16)
    cs = jnp.dot(tri, oh, preferred_element_type=F32)
    pos_ref[...] = cs - oh.astype(F32) + base_scr[0:1, :]
    base_scr[...] = base_scr[...] + cs[tb - 1:tb, :]
    cnt_ref[...] = base_scr[...]


def _rank(onehot, tb):
    m = onehot.shape[0]
    tb = min(tb, m)
    return pl.pallas_call(
        functools.partial(_rank_kernel, tb=tb),
        grid=(m // tb,),
        in_specs=[pl.BlockSpec((tb, LANES), lambda i: (i, 0))],
        out_specs=[pl.BlockSpec((tb, LANES), lambda i: (i, 0)),
                   pl.BlockSpec((8, LANES), lambda i: (0, 0))],
        out_shape=[jax.ShapeDtypeStruct((m, LANES), F32), jax.ShapeDtypeStruct((8, LANES), F32)],
        scratch_shapes=[pltpu.VMEM((8, LANES), F32)],
        compiler_params=_cparams(("arbitrary",), VMEM_SMALL_MB),
        name="rank",
    )(onehot)


def _dest_kernel(pos_ref, route_ref, start_ref, dest_ref):
    lane = lax.broadcasted_iota(jnp.int32, pos_ref.shape, 1).astype(F32)
    v = pos_ref[...] + start_ref[...]
    d1 = jnp.sum(jnp.where(lane == route_ref[:, 2:3], v, 0.0), axis=1, keepdims=True)
    d2 = jnp.sum(jnp.where(lane == route_ref[:, 3:4], v, 0.0), axis=1, keepdims=True)
    dest_ref[...] = jnp.where(lane == 0.0, d1, jnp.where(lane == 1.0, d2, 0.0)).astype(jnp.int32)


def _dest(pos, route, start, tb):
    m = pos.shape[0]
    tb = min(tb, m)
    return pl.pallas_call(
        _dest_kernel,
        grid=(m // tb,),
        in_specs=[pl.BlockSpec((tb, LANES), lambda i: (i, 0)),
                  pl.BlockSpec((tb, LANES), lambda i: (i, 0)),
                  pl.BlockSpec((1, LANES), lambda i: (0, 0))],
        out_specs=pl.BlockSpec((tb, LANES), lambda i: (i, 0)),
        out_shape=jax.ShapeDtypeStruct((m, LANES), jnp.int32),
        compiler_params=_cparams(("parallel",), VMEM_SMALL_MB),
        name="dest",
    )(pos, route, start)


DMA_ISSUE_UNROLL = 4


def _pack_bf16_pair(left, right):
    lo = pltpu.bitcast(left.astype(BF16).astype(F32), jnp.int32)
    hi = pltpu.bitcast(right.astype(BF16).astype(F32), jnp.int32)
    return hi | lax.shift_right_logical(lo, 16)


def _unpack_bf16_pair(words):
    left = pltpu.bitcast(lax.shift_left(words, 16), F32)
    right = pltpu.bitcast(words & jnp.int32(-65536), F32)
    return left, right


def _scatter_kernel(dest_ref, h_ref, xs_in_ref, xs_ref, hp_scr, sem, *, tb):
    del xs_in_ref
    base = pl.program_id(0) * tb
    half = h_ref.shape[1] // 2
    hp_scr[...] = _pack_bf16_pair(h_ref[:, :half], h_ref[:, half:])

    def row_copy(t, slot):
        d = dest_ref[(base + t) * 2 + slot]
        return pltpu.make_async_copy(hp_scr.at[pl.ds(t, 1), :], xs_ref.at[pl.ds(d, 1), :], sem)

    def start(t, carry):
        row_copy(t, 0).start()
        row_copy(t, 1).start()
        return carry

    lax.fori_loop(0, tb, start, 0, unroll=DMA_ISSUE_UNROLL)
    for _ in range(2):
        pltpu.make_async_copy(hp_scr, xs_ref.at[pl.ds(0, tb), :], sem).wait()


def _scatter_rows(dest_flat, h, n_rows, tb):
    m, d = h.shape
    tb = min(tb, m)
    xs0 = jnp.zeros((n_rows, d // 2), jnp.int32)
    return pl.pallas_call(
        functools.partial(_scatter_kernel, tb=tb),
        grid_spec=pltpu.PrefetchScalarGridSpec(
            num_scalar_prefetch=1,
            grid=(m // tb,),
            in_specs=[pl.BlockSpec((tb, d), lambda i, dest: (i, 0)),
                      pl.BlockSpec(memory_space=pl.ANY)],
            out_specs=pl.BlockSpec(memory_space=pl.ANY),
            scratch_shapes=[pltpu.VMEM((tb, d // 2), jnp.int32), pltpu.SemaphoreType.DMA(())]),
        out_shape=jax.ShapeDtypeStruct((n_rows, d // 2), jnp.int32),
        input_output_aliases={2: 0},
        compiler_params=_cparams(("arbitrary",), VMEM_SMALL_MB),
        name="scatter",
    )(dest_flat, h, xs0)


CAST_ROWS = 256


def _segment_tables(te, nv, end_tile, n_tiles):
    ids = jnp.arange(n_tiles, dtype=jnp.int32)
    fresh = jnp.concatenate([jnp.ones((1,), bool), te[1:] != te[:-1]]) & (ids < nv[0])
    seg = jnp.cumsum(fresh.astype(jnp.int32)) - 1
    after = end_tile[te]
    nxt = jnp.where(after < nv[0], te[jnp.minimum(after, n_tiles - 1)], -1)
    nseg = seg[jnp.maximum(nv[0] - 1, 0)][None] + 1
    return seg.astype(jnp.int32), nxt.astype(jnp.int32), nseg.astype(jnp.int32)


def _expert_weight_stream(te_ref, nv_ref, seg_ref, nxt_ref, nseg_ref, w_hbms, slabs, bf16_scrs, sem, width):
    c, r = pl.program_id(0), pl.program_id(1)
    valid = r < nv_ref[0]
    fresh = jnp.logical_and(valid, jnp.logical_or(r == 0, te_ref[r] != te_ref[jnp.maximum(r - 1, 0)]))
    ordinal = c * nseg_ref[0] + seg_ref[r]
    slot = ordinal % 2

    def slab_copies(expert, chunk, s):
        cols = pl.ds(pl.multiple_of(chunk * width, width), width)
        return [pltpu.make_async_copy(w.at[expert, :, cols], slab.at[s], sem.at[n, s])
                for n, (w, slab) in enumerate(zip(w_hbms, slabs))]

    def start(expert, chunk, s):
        for cp in slab_copies(expert, chunk, s):
            cp.start()

    @pl.when(jnp.logical_and(fresh, ordinal == 0))
    def _():
        start(te_ref[0], 0, 0)

    @pl.when(fresh)
    def _():
        for cp in slab_copies(te_ref[r], c, slot):
            cp.wait()
        for slab, scr in zip(slabs, bf16_scrs):
            def cast_rows(n, carry, slab=slab, scr=scr):
                rows = pl.ds(pl.multiple_of(n * CAST_ROWS, CAST_ROWS), CAST_ROWS)
                scr[rows, :] = slab[slot, rows, :].astype(BF16)
                return carry

            lax.fori_loop(0, scr.shape[0] // CAST_ROWS, cast_rows, 0)
        nxt = nxt_ref[r]

        @pl.when(nxt >= 0)
        def _():
            start(nxt, c, 1 - slot)

        @pl.when(jnp.logical_and(nxt < 0, c + 1 < pl.num_programs(0)))
        def _():
            start(te_ref[0], c + 1, 1 - slot)

    return valid


def _gm1_kernel(te_ref, nv_ref, seg_ref, nxt_ref, nseg_ref, xs_ref, w1_ref, w3_ref, o_ref,
                w1f_scr, w3f_scr, w1b_scr, w3b_scr, sem):
    valid = _expert_weight_stream(te_ref, nv_ref, seg_ref, nxt_ref, nseg_ref, (w1_ref, w3_ref),
                                  (w1f_scr, w3f_scr), (w1b_scr, w3b_scr), sem, w1b_scr.shape[1])

    @pl.when(valid)
    def _():
        left, right = _unpack_bf16_pair(xs_ref[...])
        x = jnp.concatenate([left.astype(BF16), right.astype(BF16)], axis=1)
        a = jnp.dot(x, w1b_scr[...], preferred_element_type=F32)
        b = jnp.dot(x, w3b_scr[...], preferred_element_type=F32)
        o_ref[...] = (a * jax.nn.sigmoid(a) * b).astype(o_ref.dtype)

    @pl.when(jnp.logical_not(valid))
    def _():
        o_ref[...] = jnp.zeros(o_ref.shape, o_ref.dtype)


def _gm1(tables, xs, w1, w3, tf):
    n_rows = xs.shape[0]
    d, f = w1.shape[1], w1.shape[2]
    nt = n_rows // ROW_TILE

    def rc(r, nv_):
        return jnp.minimum(r, nv_[0] - 1)

    return pl.pallas_call(
        _gm1_kernel,
        grid_spec=pltpu.PrefetchScalarGridSpec(
            num_scalar_prefetch=len(tables),
            grid=(f // tf, nt),
            in_specs=[pl.BlockSpec((ROW_TILE, d // 2), lambda c, r, te_, nv_, *_: (rc(r, nv_), 0)),
                      pl.BlockSpec(memory_space=pl.ANY),
                      pl.BlockSpec(memory_space=pl.ANY)],
            out_specs=pl.BlockSpec((ROW_TILE, tf), lambda c, r, *_: (r, c)),
            scratch_shapes=[pltpu.VMEM((2, d, tf), F32), pltpu.VMEM((2, d, tf), F32),
                            pltpu.VMEM((d, tf), BF16), pltpu.VMEM((d, tf), BF16),
                            pltpu.SemaphoreType.DMA((2, 2))]),
        out_shape=jax.ShapeDtypeStruct((n_rows, f), BF16),
        compiler_params=_cparams(("arbitrary", "arbitrary"), VMEM_LARGE_MB),
        name="gm1",
    )(*tables, xs, w1, w3)


def _gm2_kernel(te_ref, nv_ref, seg_ref, nxt_ref, nseg_ref, h_ref, w2_ref, o_ref, w2f_scr, w2b_scr, sem):
    valid = _expert_weight_stream(te_ref, nv_ref, seg_ref, nxt_ref, nseg_ref, (w2_ref,),
                                  (w2f_scr,), (w2b_scr,), sem, w2b_scr.shape[1])

    @pl.when(valid)
    def _():
        y = jnp.dot(h_ref[...], w2b_scr[...], preferred_element_type=F32)
        half = y.shape[1] // 2
        o_ref[...] = _pack_bf16_pair(y[:, :half], y[:, half:])

    @pl.when(jnp.logical_not(valid))
    def _():
        o_ref[...] = jnp.zeros(o_ref.shape, o_ref.dtype)


def _gm2(tables, hid, w2, tn):
    n_rows, f = hid.shape
    d = w2.shape[2]
    tn = min(tn, d)
    nt = n_rows // ROW_TILE

    def rc(r, nv_):
        return jnp.minimum(r, nv_[0] - 1)

    return pl.pallas_call(
        _gm2_kernel,
        grid_spec=pltpu.PrefetchScalarGridSpec(
            num_scalar_prefetch=len(tables),
            grid=(d // tn, nt),
            in_specs=[pl.BlockSpec((ROW_TILE, f), lambda c, r, te_, nv_, *_: (rc(r, nv_), 0)),
                      pl.BlockSpec(memory_space=pl.ANY)],
            out_specs=pl.BlockSpec((ROW_TILE, tn // 2), lambda c, r, *_: (r, c)),
            scratch_shapes=[pltpu.VMEM((2, f, tn), F32), pltpu.VMEM((f, tn), BF16),
                            pltpu.SemaphoreType.DMA((1, 2))]),
        out_shape=jax.ShapeDtypeStruct((n_rows, d // 2), jnp.int32),
        compiler_params=_cparams(("arbitrary", "arbitrary"), VMEM_LARGE_MB),
        name="gm2",
    )(*tables, hid, w2)


def _combine_kernel(dest_ref, y_ref, h_ref, route_ref, g_ref, b_ref, o_ref, ybuf, sem, *, tb, tn):
    i, nblk = pl.program_id(0), pl.num_programs(0)

    def row_copy(blk, buf, t, e):
        d = dest_ref[(blk * tb + t) * 2 + e]
        return pltpu.make_async_copy(y_ref.at[pl.ds(d, 1), :], ybuf.at[buf, e, pl.ds(t, 1), :], sem.at[buf])

    def fetch(blk, buf):
        def body(t, carry):
            row_copy(blk, buf, t, 0).start()
            row_copy(blk, buf, t, 1).start()
            return carry

        lax.fori_loop(0, tb, body, 0, unroll=DMA_ISSUE_UNROLL)

    @pl.when(i == 0)
    def _():
        fetch(0, 0)

    @pl.when(i + 1 < nblk)
    def _():
        fetch(i + 1, (i + 1) % 2)

    cur = i % 2

    for e in range(2):
        pltpu.make_async_copy(y_ref.at[pl.ds(0, tb), :], ybuf.at[cur, e], sem.at[cur]).wait()
    half = tn // 2
    pieces = []
    for c in range(h_ref.shape[1] // tn):
        l0, r0 = _unpack_bf16_pair(ybuf[cur, 0, :, c * half:(c + 1) * half])
        l1, r1 = _unpack_bf16_pair(ybuf[cur, 1, :, c * half:(c + 1) * half])
        pieces += [route_ref[:, 0:1] * l0 + route_ref[:, 1:2] * l1, route_ref[:, 0:1] * r0 + route_ref[:, 1:2] * r1]
    moe = jnp.concatenate(pieces, axis=1)
    pre = DEEPNORM_ALPHA * h_ref[...] + moe
    mu = jnp.mean(pre, axis=1, keepdims=True)
    var = jnp.mean(jnp.square(pre - mu), axis=1, keepdims=True)
    o_ref[...] = (pre - mu) * lax.rsqrt(var + LN_EPS) * g_ref[...] + b_ref[...]


def _combine_ln(dest_flat, y, h, route, g, b, tb, tn):
    m, d = h.shape
    tb = min(tb, m)
    return pl.pallas_call(
        functools.partial(_combine_kernel, tb=tb, tn=tn),
        grid_spec=pltpu.PrefetchScalarGridSpec(
            num_scalar_prefetch=1,
            grid=(m // tb,),
            in_specs=[pl.BlockSpec(memory_space=pl.ANY),
                      pl.BlockSpec((tb, d), lambda i, dest: (i, 0)),
                      pl.BlockSpec((tb, LANES), lambda i, dest: (i, 0)),
                      pl.BlockSpec((1, d), lambda i, dest: (0, 0)),
                      pl.BlockSpec((1, d), lambda i, dest: (0, 0))],
            out_specs=pl.BlockSpec((tb, d), lambda i, dest: (i, 0)),
            scratch_shapes=[pltpu.VMEM((2, 2, tb, d // 2), jnp.int32), pltpu.SemaphoreType.DMA((2,))]),
        out_shape=jax.ShapeDtypeStruct((m, d), F32),
        compiler_params=_cparams(("arbitrary",), VMEM_MB),
        name="combine",
    )(dest_flat, y, h, route, g, b)


def _alibi_slopes(n):
    return jnp.asarray(2.0 ** (-8.0 * np.arange(1, n + 1) / n), dtype=F32)


def kernel(x, w_in, lam_q1, lam_k1, lam_q2, lam_k2, diff_subln_g, w_pa, w_pb, w_o, ln1_g, ln1_b,
           router_wg, router_bg, router_we, router_be, w1, w3, w2, ln2_g, ln2_b):
    bsz, s_len, d = x.shape
    assert bsz == 1 and w_in.shape[0] == DEPTH
    topk = min(TOPK_MAX, s_len // 4)
    x2 = x[0]
    xb = x2.astype(BF16)

    qk_w = N_DIFF_HEADS * 2 * HEAD_DIM
    c_dq, c_dk, c_dv = 0, qk_w, 2 * qk_w
    c_sq = 3 * qk_w
    c_sk = c_sq + N_DSA_HEADS * HEAD_DIM
    c_sv = c_sk + N_DSA_KV * HEAD_DIM
    c_iq = c_sv + N_DSA_KV * HEAD_DIM
    c_ik = c_iq + N_IDX_HEADS * IDX_DIM
    c_ga = c_ik + IDX_DIM + N_IDX_HEADS
    qscale = HEAD_DIM ** -0.5 * LOG2E
    col = np.arange(c_ik)
    is_q = ((col >= c_dq) & (col < c_dk)) | ((col >= c_sq) & (col < c_sk))
    col_scale = jnp.asarray(np.where(is_q, qscale, 1.0)[None, :], F32)
    wt3d = jnp.swapaxes(w_in, 1, 2)
    wt = wt3d[0]
    wt_gates = _cast_rows(wt, c_ga, 2 * d, min(PROJ_TILE[1], d))

    z = _proj_from_f32(xb, wt3d, col_scale, c_ik, *PROJ_TILE)
    small = 2 * LANES
    assert c_ik % small == 0 and c_ik + small <= wt.shape[0] and s_len <= POS_RADIX * POS_RADIX
    zs = _matmul_nt(xb, wt, c_ik // small, small, F32, PROJ_TILE[0], "proj_small")
    ik = zs[:, :IDX_DIM].astype(BF16)
    iwt = zs[:, IDX_DIM:IDX_DIM + N_IDX_HEADS].T
    dvt = _with_ones_rows(z[:, c_dv:c_dv + qk_w].T, N_DIFF_HEADS)
    svt = _with_ones_rows(z[:, c_sv:c_sv + N_DSA_KV * HEAD_DIM].T, N_DSA_KV)
    kaug = _key_aug_table(s_len)

    lam4 = jnp.stack([lam_q1[0], lam_k1[0], lam_q2[0], lam_k2[0]]).astype(F32)
    g_lanes = jnp.broadcast_to(diff_subln_g[0][:, None], (2 * HEAD_DIM, LANES))
    a = _diff_attention(z, dvt, kaug, lam4, g_lanes, _alibi_slopes(N_DIFF_HEADS), c_dq, c_dk, s_len,
                        min(ATTN_TILE, s_len))

    tq_i, tk_i = min(DSA_TILE[0], s_len), min(DSA_TILE[1], s_len)
    maskb = _indexer(z, ik, iwt, c_iq, s_len, tq_i, tk_i, topk)
    b = _dsa_attention(z, svt, kaug, maskb, _alibi_slopes(N_DSA_HEADS), c_sq, c_sk, s_len, tq_i, tk_i)

    merged = _merge(xb, wt_gates, a, w_pa[0].astype(BF16), b, w_pb[0].astype(BF16), *MERGE_TILE)

    wr = jnp.concatenate([router_wg[0], router_we[0],
                          jnp.zeros((d, LANES - N_GROUPS - N_EXPERTS), F32)], axis=1)
    br = jnp.concatenate([router_bg[0], router_be[0],
                          jnp.zeros((LANES - N_GROUPS - N_EXPERTS,), F32)])[None, :]
    h1, route, onehot = _out_ln_router(merged, w_o[0].astype(BF16), x2, ln1_g[0][None, :], ln1_b[0][None, :],
                                       wr, br, *OUT_TILE)

    pos, cnt = _rank(onehot, RANK_ROWS)
    counts = cnt[0, :N_EXPERTS].astype(jnp.int32)
    padded = ((counts + ROW_TILE - 1) // ROW_TILE) * ROW_TILE
    ends = jnp.cumsum(padded)
    start = jnp.zeros((1, LANES), F32).at[0, :N_EXPERTS].set((ends - padded).astype(F32))
    n_tiles = (2 * s_len) // ROW_TILE + N_EXPERTS
    tile_ids = jnp.arange(n_tiles, dtype=jnp.int32)
    te = jnp.minimum(jnp.sum(tile_ids[:, None] >= (ends // ROW_TILE)[None, :], axis=1), N_EXPERTS - 1)
    te = te.astype(jnp.int32)
    nv = (ends[-1] // ROW_TILE).astype(jnp.int32)[None]
    dest = _dest(pos, route, start, RANK_ROWS)
    dest_flat = dest[:, :2].reshape(-1)

    xs = _scatter_rows(dest_flat, h1, n_tiles * ROW_TILE, SCATTER_ROWS)
    tables = (te, nv) + _segment_tables(te, nv, ends // ROW_TILE, n_tiles)
    hid = _gm1(tables, xs, w1[0], w3[0], min(GM1_COLS, w1.shape[3]))
    tn_y = min(GM2_COLS, d)
    y = _gm2(tables, hid, w2[0], tn_y)
    out = _combine_ln(dest_flat, y, h1, route, ln2_g[0][None, :], ln2_b[0][None, :], COMBINE_ROWS, tn_y)
    return out[None]
```

```python
import functools
import math

import numpy as np
import jax
import jax.numpy as jnp
from jax import lax
from jax.experimental import pallas as pl
from jax.experimental.pallas import tpu as pltpu

HEAD_DIM = 128
N_DIFF_HEADS = 8
N_DSA_HEADS = 16
N_DSA_KV = 4
DSA_REP = N_DSA_HEADS // N_DSA_KV
N_IDX_HEADS = 32
IDX_DIM = 128
TOPK_MAX = 256
N_GROUPS = 4
EXPERTS_PER_GROUP = 8
N_EXPERTS = N_GROUPS * EXPERTS_PER_GROUP
LN_EPS = 1e-5
RMS_EPS = 1e-5
DEPTH = 1
DEEPNORM_ALPHA = (2.0 * DEPTH) ** 0.25
LAM_INIT = 0.8 - 0.6 * math.exp(-0.3 * 0)

LANES = 128
SUBLANES = 8
NEG = -1e30
INT_MIN = -(2 ** 31)
ROW_TILE = 256
LOG2E = 1.4426950408889634
POS_RADIX = 256

PROJ_TILE = (1024, 512)
ATTN_TILE = 512
DSA_TILE = (256, 512)
MERGE_TILE = (1024, 256)
OUT_TILE = (512, 512)
RANK_ROWS = 512
SCATTER_ROWS = 256
COMBINE_ROWS = 128
GM1_COLS = 512
GM2_COLS = 4096
VMEM_SMALL_MB = 32
VMEM_MB = 48
VMEM_LARGE_MB = 56

F32 = jnp.float32
BF16 = jnp.bfloat16


def _cparams(sem, vmem_mb):
    return pltpu.CompilerParams(dimension_semantics=sem, vmem_limit_bytes=vmem_mb << 20)


def _dot_nt(a, b):
    return lax.dot_general(a, b, (((1,), (1,)), ((), ())), preferred_element_type=F32)


def _proj_kernel(x_ref, wt_ref, sc_ref, o_ref, wb_scr):
    @pl.when(pl.program_id(1) == 0)
    def _():
        wb_scr[...] = wt_ref[0].astype(BF16)

    o_ref[...] = (_dot_nt(x_ref[...], wb_scr[...]) * sc_ref[...]).astype(o_ref.dtype)


def _proj_from_f32(xb, wt3d, col_scale, n_cols, tm, tn):
    m, d = xb.shape
    tm = min(tm, m)
    return pl.pallas_call(
        _proj_kernel,
        grid=(n_cols // tn, m // tm),
        in_specs=[pl.BlockSpec((tm, d), lambda j, i: (i, 0)),
                  pl.BlockSpec((1, tn, d), lambda j, i: (0, j, 0)),
                  pl.BlockSpec((1, tn), lambda j, i: (0, j))],
        out_specs=pl.BlockSpec((tm, tn), lambda j, i: (i, j)),
        out_shape=jax.ShapeDtypeStruct((m, n_cols), BF16),
        scratch_shapes=[pltpu.VMEM((tn, d), BF16)],
        compiler_params=_cparams(("parallel", "arbitrary"), VMEM_MB),
        name="proj_main",
    )(xb, wt3d, col_scale)


def _cast_rows_kernel(w_ref, o_ref):
    o_ref[...] = w_ref[...].astype(o_ref.dtype)


def _cast_rows(wt, row0, n_rows, tr):
    d = wt.shape[1]
    assert row0 % SUBLANES == 0 and n_rows % tr == 0
    return pl.pallas_call(
        _cast_rows_kernel,
        grid=(n_rows // tr,),
        in_specs=[pl.BlockSpec((pl.Element(tr), pl.Element(d)),
                               lambda i: (pl.multiple_of(row0 + i * tr, SUBLANES), 0))],
        out_specs=pl.BlockSpec((tr, d), lambda i: (i, 0)),
        out_shape=jax.ShapeDtypeStruct((n_rows, d), BF16),
        compiler_params=_cparams(("parallel",), VMEM_MB),
        name="cast_gates",
    )(wt)


def _mm_nt_kernel(a_ref, bt_ref, o_ref):
    o_ref[...] = _dot_nt(a_ref[...], bt_ref[...].astype(BF16)).astype(o_ref.dtype)


def _matmul_nt(a, wt, row_block, n, out_dtype, tm, name):
    m, k = a.shape
    tm = min(tm, m)
    return pl.pallas_call(
        _mm_nt_kernel,
        grid=(m // tm,),
        in_specs=[pl.BlockSpec((tm, k), lambda i: (i, 0)),
                  pl.BlockSpec((n, k), lambda i: (row_block, 0))],
        out_specs=pl.BlockSpec((tm, n), lambda i: (i, 0)),
        out_shape=jax.ShapeDtypeStruct((m, n), out_dtype),
        compiler_params=_cparams(("parallel",), VMEM_MB),
        name=name,
    )(a, wt)


def _key_aug_table(s_len):
    pos = jnp.arange(s_len, dtype=jnp.int32)[:, None]
    lane = jnp.arange(LANES, dtype=jnp.int32)[None, :]
    hi = (pos // POS_RADIX).astype(F32)
    lo = (pos % POS_RADIX).astype(F32)
    t = jnp.where(lane < 2, hi, jnp.where(lane < 4, lo, jnp.where(lane < 7, 1.0, 0.0)))
    return t.astype(BF16)


def _bf16_piece(x):
    return x.astype(BF16).astype(F32)


def _query_aug(slope, qbase, rows):
    s2 = jnp.full((SUBLANES, LANES), slope, F32) * LOG2E
    big = s2 * POS_RADIX
    off = -s2 * jnp.full((SUBLANES, LANES), qbase, F32)
    big_hi = _bf16_piece(big)
    s2_hi = _bf16_piece(s2)
    off_hi = _bf16_piece(off)
    off_mid = _bf16_piece(off - off_hi)
    lane = lax.broadcasted_iota(jnp.int32, (SUBLANES, LANES), 1)
    pieces = [big_hi, big - big_hi, s2_hi, s2 - s2_hi, off_hi, off_mid, off - off_hi - off_mid]
    row = jnp.zeros((SUBLANES, LANES), F32)
    for n, piece in enumerate(pieces):
        row = jnp.where(lane == n, piece, row)
    return jnp.broadcast_to(row[0:1, :], (rows, LANES)).astype(BF16)


QUERY_SUB = 512


def _causal_steps(nq, last_of):
    pairs = [(i, j) for i in range(nq) for j in range(last_of(i) + 1)]
    return (jnp.asarray([p[0] for p in pairs], jnp.int32), jnp.asarray([p[1] for p in pairs], jnp.int32))


ONES_ROWS = 16


def _with_ones_rows(vt, n_blocks):
    r = vt.shape[0] // n_blocks
    v3 = vt.reshape(n_blocks, r, vt.shape[1])
    ones = jnp.ones((n_blocks, ONES_ROWS, vt.shape[1]), vt.dtype)
    return jnp.concatenate([v3, ones], axis=1).reshape(n_blocks * (r + ONES_ROWS), vt.shape[1])


def _online_softmax_pv(s, vt1, m_scr, acc_scr, idx, cols):
    at = idx + (slice(None), cols)
    m_prev = m_scr[at]
    m_next = jnp.maximum(m_prev, jnp.max(s, axis=0, keepdims=True))
    p = jnp.exp2((s - m_next).astype(BF16))
    alpha = jnp.exp2(m_prev - m_next)
    m_scr[at] = m_next
    acc_scr[at] = acc_scr[at] * alpha + jnp.dot(vt1, p, preferred_element_type=F32)


DIFF_HEADS_PER_STEP = 2


def _diff_kernel(qi_ref, kj_ref, slopes_ref, lam_ref, g_ref, q_ref, k_ref, ka_ref, vt_ref, o_ref,
                 qa_scr, kk_scr, s0_scr, s1_scr, m_scr, acc_scr, *, tq, n_steps):
    hp, n = pl.program_id(0), pl.program_id(1)
    reps = tq // LANES
    nsub = tq // QUERY_SUB
    w = 2 * HEAD_DIM
    maps = [(hh, c) for hh in range(DIFF_HEADS_PER_STEP) for c in range(2)]
    na = jnp.minimum(n, n_steps - 1)
    nb = jnp.maximum(n - 1, 0)
    ia = qi_ref[na]
    ib, jb = qi_ref[nb], kj_ref[nb]
    has_a, has_b = n < n_steps, n >= 1
    diag_b = jnp.logical_and(has_b, jb == ib)

    score_slots = (s0_scr, s1_scr)

    def map_cols(hh, c):
        return slice(hh * w + c * HEAD_DIM, hh * w + (c + 1) * HEAD_DIM)

    def load_queries():
        for hh in range(DIFF_HEADS_PER_STEP):
            qaug = _query_aug(slopes_ref[hp * DIFF_HEADS_PER_STEP + hh], (ia * tq).astype(F32), tq)
            for c in range(2):
                qa_scr[2 * hh + c, :, :HEAD_DIM] = q_ref[:, map_cols(hh, c)]
                qa_scr[2 * hh + c, :, HEAD_DIM:] = qaug

    def form_scores(slot):
        ka = ka_ref[...]
        for hh, c in maps:
            kk_scr[2 * hh + c, :, :HEAD_DIM] = k_ref[:, map_cols(hh, c)]
            kk_scr[2 * hh + c, :, HEAD_DIM:] = ka
        for m in range(len(maps)):
            score_slots[slot][m] = _dot_nt(kk_scr[m], qa_scr[m])

    def softmax_pv(slot, masked):
        for m, (hh, c) in enumerate(maps):
            vt1 = vt_ref[hh * (w + ONES_ROWS):(hh + 1) * (w + ONES_ROWS), :]
            for u in range(nsub):
                cols = slice(u * QUERY_SUB, (u + 1) * QUERY_SUB)
                s = score_slots[slot][m, :, cols]
                if masked:
                    keep = (lax.broadcasted_iota(jnp.int32, (tq, 1), 0)
                            <= u * QUERY_SUB + lax.broadcasted_iota(jnp.int32, (1, QUERY_SUB), 1))
                    s = jnp.where(keep, s, NEG)
                _online_softmax_pv(s, vt1, m_scr, acc_scr, (m,), cols)

    @pl.when(jnp.logical_and(has_b, jb == 0))
    def _():
        m_scr[...] = jnp.full(m_scr.shape, NEG, F32)
        acc_scr[...] = jnp.zeros(acc_scr.shape, F32)

    def emit_block():
        lam = (jnp.exp(jnp.sum(lam_ref[0:1, :] * lam_ref[1:2, :], axis=1, keepdims=True))
               - jnp.exp(jnp.sum(lam_ref[2:3, :] * lam_ref[3:4, :], axis=1, keepdims=True)) + LAM_INIT)
        g = jnp.concatenate([g_ref[...]] * reps, axis=1)
        for hh in range(DIFF_HEADS_PER_STEP):
            m = 2 * hh
            o = (acc_scr[m, :w, :] / acc_scr[m, w:w + 1, :]
                 - lam * (acc_scr[m + 1, :w, :] / acc_scr[m + 1, w:w + 1, :]))
            o = o * lax.rsqrt(jnp.mean(o * o, axis=0, keepdims=True) + RMS_EPS) * g
            o_ref[:, hh * w:(hh + 1) * w] = (o * (1.0 - LAM_INIT)).T.astype(o_ref.dtype)

    for par in range(2):
        mine = (n % 2) == par

        @pl.when(jnp.logical_and(mine, jnp.logical_and(has_b, jnp.logical_not(diag_b))))
        def _(par=par):
            form_scores(par)
            softmax_pv(1 - par, False)

        @pl.when(jnp.logical_and(mine, jnp.logical_and(diag_b, has_a)))
        def _(par=par):
            load_queries()
            form_scores(par)
            softmax_pv(1 - par, True)
            emit_block()

        @pl.when(jnp.logical_and(mine, jnp.logical_and(diag_b, jnp.logical_not(has_a))))
        def _(par=par):
            softmax_pv(1 - par, True)
            emit_block()

        @pl.when(jnp.logical_and(mine, n == 0))
        def _(par=par):
            load_queries()
            form_scores(par)


def _diff_attention(z, vt, kaug, lam4, g_lanes, slopes, col_q, col_k, s_len, tq):
    nq = s_len // tq
    w = 2 * HEAD_DIM
    wb = DIFF_HEADS_PER_STEP * w
    nmaps = 2 * DIFF_HEADS_PER_STEP
    qb, kb = col_q // wb, col_k // wb
    qi, kj = _causal_steps(nq, lambda i: i)
    n_steps = int(qi.shape[0])

    def ahead(s):
        return jnp.minimum(s, n_steps - 1)

    def behind(s):
        return jnp.maximum(s - 1, 0)

    return pl.pallas_call(
        functools.partial(_diff_kernel, tq=tq, n_steps=n_steps),
        grid_spec=pltpu.PrefetchScalarGridSpec(
            num_scalar_prefetch=3,
            grid=(N_DIFF_HEADS // DIFF_HEADS_PER_STEP, n_steps + 1),
            in_specs=[pl.BlockSpec((4, HEAD_DIM), lambda h, s, qi_, kj_, sl_: (0, 0)),
                      pl.BlockSpec((w, LANES), lambda h, s, qi_, kj_, sl_: (0, 0)),
                      pl.BlockSpec((tq, wb), lambda h, s, qi_, kj_, sl_: (qi_[ahead(s)], qb + h)),
                      pl.BlockSpec((tq, wb), lambda h, s, qi_, kj_, sl_: (kj_[ahead(s)], kb + h)),
                      pl.BlockSpec((tq, LANES), lambda h, s, qi_, kj_, sl_: (kj_[ahead(s)], 0)),
                      pl.BlockSpec((DIFF_HEADS_PER_STEP * (w + ONES_ROWS), tq),
                                   lambda h, s, qi_, kj_, sl_: (h, kj_[behind(s)]))],
            out_specs=pl.BlockSpec((tq, wb), lambda h, s, qi_, kj_, sl_: (qi_[behind(s)], h)),
            scratch_shapes=[pltpu.VMEM((nmaps, tq, w), BF16), pltpu.VMEM((nmaps, tq, w), BF16),
                            pltpu.VMEM((nmaps, tq, tq), F32), pltpu.VMEM((nmaps, tq, tq), F32),
                            pltpu.VMEM((nmaps, 1, tq), F32),
                            pltpu.VMEM((nmaps, w + ONES_ROWS, tq), F32)]),
        out_shape=jax.ShapeDtypeStruct((s_len, N_DIFF_HEADS * w), BF16),
        compiler_params=_cparams(("parallel", "arbitrary"), VMEM_MB),
        name="diffattn",
    )(qi, kj, slopes, lam4, g_lanes, z, z, kaug, vt)


IQ_SPLIT = 4


HALF_BITS = 16
HALF_MASK = (1 << HALF_BITS) - 1
I16_MIN = -(1 << (HALF_BITS - 1))


def _indexer_kernel(iq0_ref, iq1_ref, iq2_ref, iq3_ref, ik_ref, iwt_ref, o_ref,
                    iqh_scr, key_scr, hi_scr, lo_scr, acc_scr, pcut_scr, *, tq, tkc, topk, iw_scale, pos_bits):
    i = pl.program_id(0)
    nk = key_scr.shape[0]
    nchunks = (i * tq + tq - 1) // tkc + 1
    per = N_IDX_HEADS // IQ_SPLIT

    for h in range(N_IDX_HEADS):
        src = (iq0_ref, iq1_ref, iq2_ref, iq3_ref)[h // per]
        iqh_scr[h] = src[:, (h % per) * IDX_DIM:(h % per + 1) * IDX_DIM]

    qpos = i * tq + lax.broadcasted_iota(jnp.int32, (1, tq), 1)

    def key_pos(c):
        return c * tkc + lax.broadcasted_iota(jnp.int32, (tkc, 1), 0)

    def count_keys(pred):
        def body(c, acc):
            x = jnp.where(pred(c, key_scr[c]), 1, 0)
            return acc + jnp.sum(x.reshape(tkc // SUBLANES, SUBLANES, tq), axis=0)

        cnt = lax.fori_loop(0, nchunks, body, jnp.zeros((SUBLANES, tq), jnp.int32))
        return jnp.sum(cnt, axis=0, keepdims=True)

    def chunk(c, carry):
        kc = ik_ref[pl.ds(pl.multiple_of(c * tkc, tkc), tkc), :]
        acc_scr[...] = jnp.zeros(acc_scr.shape, F32)

        def head(h, carry2):
            sc = _dot_nt(kc, iqh_scr[h])
            acc_scr[...] += (iwt_ref[pl.ds(h, 1), :] * iw_scale) * jnp.maximum(sc, 0.0)
            return carry2

        lax.fori_loop(0, N_IDX_HEADS, head, 0, unroll=True)
        score = acc_scr[...]
        score = jnp.where(score == 0.0, 0.0, score)
        bits = pltpu.bitcast(score, jnp.int32)
        skey = bits ^ ((bits >> 31) & 0x7FFFFFFF)
        key = jnp.where(key_pos(c) <= qpos, skey, INT_MIN)
        key_scr[c] = key
        hi_scr[c] = (key >> HALF_BITS).astype(jnp.int16)
        lo_scr[c] = ((key & HALF_MASK) + I16_MIN).astype(jnp.int16)
        return carry

    lax.fori_loop(0, nchunks, chunk, 0)

    def count_halves(half_scr, pred):
        rows = 2 * SUBLANES

        def body(c, acc):
            x = jnp.where(pred(half_scr[c]), jnp.int16(1), jnp.int16(0))
            for g in range(tkc // rows):
                acc = acc + x[g * rows:(g + 1) * rows, :]
            return acc

        cnt = lax.fori_loop(0, nchunks, body, jnp.zeros((rows, tq), jnp.int16))
        return jnp.sum(cnt.astype(jnp.int32), axis=0, keepdims=True)

    def radix_select(half_scr, want):
        def bit_step(bi, u):
            cand = u | lax.shift_left(jnp.int32(1), HALF_BITS - 1 - bi)
            cand16 = (cand + I16_MIN).astype(jnp.int16)
            return jnp.where(count_halves(half_scr, lambda k: k >= cand16) >= want, cand, u)

        return lax.fori_loop(0, HALF_BITS, bit_step, jnp.zeros((1, tq), jnp.int32)) + I16_MIN

    t_hi = radix_select(hi_scr, topk)
    t_hi16 = t_hi.astype(jnp.int16)
    want_lo = topk - count_halves(hi_scr, lambda k: k > t_hi16)

    def keep_lo(c, carry):
        lo_scr[c] = jnp.where(hi_scr[c] == t_hi16, lo_scr[c], jnp.int16(I16_MIN))
        return carry

    lax.fori_loop(0, nchunks, keep_lo, 0)
    t_lo = radix_select(lo_scr, want_lo)
    thr = lax.shift_left(t_hi, HALF_BITS) | (t_lo - I16_MIN)
    thr = jnp.maximum(thr, INT_MIN + 1)

    pcut_scr[...] = jnp.full(pcut_scr.shape, (1 << pos_bits) - 1, jnp.int32)

    @pl.when(jnp.max(count_keys(lambda c, k: k >= thr)) > topk)
    def _():
        need = topk - count_keys(lambda c, k: k > thr)

        def pos_step(bi, p):
            cand = p | lax.shift_left(jnp.int32(1), pos_bits - 1 - bi)
            tied_before = count_keys(lambda c, k: jnp.where(k == thr, key_pos(c), cand) < cand)
            return jnp.where(tied_before < need, cand, p)

        pcut_scr[...] = lax.fori_loop(0, pos_bits, pos_step, jnp.zeros((1, tq), jnp.int32))

    pcut = pcut_scr[...]

    def emit(c, carry):
        k = key_scr[c]
        tie_bias = jnp.where(key_pos(c) <= pcut, 0.0, NEG)
        o_ref[0, c] = jnp.where(k > thr, 0.0, jnp.where(k == thr, tie_bias, NEG)).astype(o_ref.dtype)
        return carry

    lax.fori_loop(0, nchunks, emit, 0)

    def fill(c, carry):
        o_ref[0, c] = jnp.full((tkc, tq), NEG, o_ref.dtype)
        return carry

    lax.fori_loop(nchunks, nk, fill, 0)


def _indexer(z, ik, iwt, col_iq, s_len, tq, tkc, topk):
    nq, nk = s_len // tq, s_len // tkc
    iw_scale = N_IDX_HEADS ** -0.5 * IDX_DIM ** -0.5
    wq = N_IDX_HEADS * IDX_DIM // IQ_SPLIT
    qb = col_iq // wq
    iq_specs = [pl.BlockSpec((tq, wq), functools.partial(lambda i, n: (i, qb + n), n=n)) for n in range(IQ_SPLIT)]
    return pl.pallas_call(
        functools.partial(_indexer_kernel, tq=tq, tkc=tkc, topk=topk, iw_scale=iw_scale,
                          pos_bits=max(1, (s_len - 1).bit_length())),
        grid=(nq,),
        in_specs=iq_specs + [pl.BlockSpec((s_len, IDX_DIM), lambda i: (0, 0)),
                             pl.BlockSpec((N_IDX_HEADS, tq), lambda i: (0, i))],
        out_specs=pl.BlockSpec((1, nk, tkc, tq), lambda i: (i, 0, 0, 0)),
        out_shape=jax.ShapeDtypeStruct((nq, nk, tkc, tq), BF16),
        scratch_shapes=[pltpu.VMEM((N_IDX_HEADS, tq, IDX_DIM), BF16),
                        pltpu.VMEM((nk, tkc, tq), jnp.int32),
                        pltpu.VMEM((nk, tkc, tq), jnp.int16),
                        pltpu.VMEM((nk, tkc, tq), jnp.int16),
                        pltpu.VMEM((tkc, tq), F32),
                        pltpu.VMEM((1, tq), jnp.int32)],
        compiler_params=_cparams(("parallel",), VMEM_MB),
        name="indexer",
    )(z, z, z, z, ik, iwt)


DSA_GROUPS_PER_STEP = 2


def _dsa_kernel(qi_ref, kj_ref, slopes_ref, q_ref, k_ref, ka_ref, vt_ref, mb_ref, o_ref,
                qa_scr, kk_scr, m_scr, acc_scr, *, tq, tk):
    gp, step_id = pl.program_id(0), pl.program_id(1)
    i, j = qi_ref[step_id], kj_ref[step_id]
    last = (i * tq + tq - 1) // tk
    heads = DSA_GROUPS_PER_STEP * DSA_REP

    @pl.when(j == 0)
    def _():
        m_scr[...] = jnp.full(m_scr.shape, NEG, F32)
        acc_scr[...] = jnp.zeros(acc_scr.shape, F32)
        for n in range(heads):
            rows = slice(n * tq, (n + 1) * tq)
            qa_scr[rows, :HEAD_DIM] = q_ref[:, n * HEAD_DIM:(n + 1) * HEAD_DIM]
            qa_scr[rows, HEAD_DIM:] = _query_aug(slopes_ref[gp * heads + n], (i * tq).astype(F32), tq)

    ka = ka_ref[...]
    for gg in range(DSA_GROUPS_PER_STEP):
        kk_scr[gg, :, :HEAD_DIM] = k_ref[:, gg * HEAD_DIM:(gg + 1) * HEAD_DIM]
        kk_scr[gg, :, HEAD_DIM:] = ka
    mb = mb_ref[0, 0].astype(F32)
    scores = [_dot_nt(kk_scr[gg], qa_scr[gg * DSA_REP * tq:(gg + 1) * DSA_REP * tq, :])
              for gg in range(DSA_GROUPS_PER_STEP)]
    for gg in range(DSA_GROUPS_PER_STEP):
        vt1 = vt_ref[gg * (HEAD_DIM + ONES_ROWS):(gg + 1) * (HEAD_DIM + ONES_ROWS), :]
        for r in range(DSA_REP):
            n = gg * DSA_REP + r
            _online_softmax_pv(scores[gg][:, r * tq:(r + 1) * tq] + mb, vt1, m_scr, acc_scr, (),
                               slice(n * tq, (n + 1) * tq))

    @pl.when(j == last)
    def _():
        o = acc_scr[:HEAD_DIM, :] / acc_scr[HEAD_DIM:HEAD_DIM + 1, :]
        for n in range(heads):
            o_ref[:, n * HEAD_DIM:(n + 1) * HEAD_DIM] = o[:, n * tq:(n + 1) * tq].T.astype(o_ref.dtype)


def _dsa_attention(z, vt, kaug, maskb, slopes, col_q, col_k, s_len, tq, tk):
    nq = s_len // tq
    gps = DSA_GROUPS_PER_STEP
    heads = gps * DSA_REP
    qw, kw = heads * HEAD_DIM, gps * HEAD_DIM
    qb, kb = col_q // qw, col_k // kw
    qi, kj = _causal_steps(nq, lambda i: (i * tq + tq - 1) // tk)
    return pl.pallas_call(
        functools.partial(_dsa_kernel, tq=tq, tk=tk),
        grid_spec=pltpu.PrefetchScalarGridSpec(
            num_scalar_prefetch=3,
            grid=(N_DSA_KV // gps, qi.shape[0]),
            in_specs=[pl.BlockSpec((tq, qw), lambda g, s, qi_, kj_, sl_: (qi_[s], qb + g)),
                      pl.BlockSpec((tk, kw), lambda g, s, qi_, kj_, sl_: (kj_[s], kb + g)),
                      pl.BlockSpec((tk, LANES), lambda g, s, qi_, kj_, sl_: (kj_[s], 0)),
                      pl.BlockSpec((gps * (HEAD_DIM + ONES_ROWS), tk), lambda g, s, qi_, kj_, sl_: (g, kj_[s])),
                      pl.BlockSpec((1, 1, tk, tq), lambda g, s, qi_, kj_, sl_: (qi_[s], kj_[s], 0, 0))],
            out_specs=pl.BlockSpec((tq, qw), lambda g, s, qi_, kj_, sl_: (qi_[s], g)),
            scratch_shapes=[pltpu.VMEM((heads * tq, 2 * HEAD_DIM), BF16),
                            pltpu.VMEM((gps, tk, 2 * HEAD_DIM), BF16),
                            pltpu.VMEM((1, heads * tq), F32),
                            pltpu.VMEM((HEAD_DIM + ONES_ROWS, heads * tq), F32)]),
        out_shape=jax.ShapeDtypeStruct((s_len, N_DSA_HEADS * HEAD_DIM), BF16),
        compiler_params=_cparams(("parallel", "arbitrary"), VMEM_MB),
        name="dsa",
    )(qi, kj, slopes, z, z, kaug, vt, maskb)


def _merge_kernel(x_ref, wga_ref, wgb_ref, a_ref, wpa_ref, b_ref, wpb_ref, o_ref):
    x = x_ref[...]
    ga = _dot_nt(x, wga_ref[...])
    gb = _dot_nt(x, wgb_ref[...])
    pa = jnp.dot(a_ref[...], wpa_ref[...], preferred_element_type=F32)
    pb = jnp.dot(b_ref[...], wpb_ref[...], preferred_element_type=F32)
    o_ref[...] = (jax.nn.sigmoid(ga) * pa + jax.nn.sigmoid(gb) * pb).astype(o_ref.dtype)


def _merge(xb, wt_gates, a, wpa, b, wpb, tm, tn):
    m, d = xb.shape
    ka, kb = a.shape[1], b.shape[1]
    tm, tn = min(tm, m), min(tn, d)
    row = lambda i, j: (i, 0)
    col = lambda i, j: (0, j)
    return pl.pallas_call(
        _merge_kernel,
        grid=(m // tm, d // tn),
        in_specs=[pl.BlockSpec((tm, d), row),
                  pl.BlockSpec((tn, d), lambda i, j: (j, 0)), pl.BlockSpec((tn, d), lambda i, j: (d // tn + j, 0)),
                  pl.BlockSpec((tm, ka), row), pl.BlockSpec((ka, tn), col),
                  pl.BlockSpec((tm, kb), row), pl.BlockSpec((kb, tn), col)],
        out_specs=pl.BlockSpec((tm, tn), lambda i, j: (i, j)),
        out_shape=jax.ShapeDtypeStruct((m, d), BF16),
        compiler_params=_cparams(("parallel", "arbitrary"), VMEM_LARGE_MB),
        name="merge",
    )(xb, wt_gates, wt_gates, a, wpa, b, wpb)


def _split_bf16(x):
    hi = x.astype(BF16)
    lo = (x - hi.astype(F32)).astype(BF16)
    return hi, lo


def _out_kernel(mg_ref, wo_ref, x_ref, g_ref, b_ref, wr_ref, br_ref,
                h_ref, route_ref, oh_ref, pre_scr, *, tn, nj):
    j = pl.program_id(1)
    y = jnp.dot(mg_ref[...], wo_ref[...], preferred_element_type=F32)
    pre_scr[j] = DEEPNORM_ALPHA * x_ref[...] + y

    @pl.when(j == nj - 1)
    def _():
        d = nj * tn
        tot = pre_scr[0].sum(axis=1, keepdims=True)
        for jj in range(1, nj):
            tot = tot + pre_scr[jj].sum(axis=1, keepdims=True)
        mu = tot / d
        sq = jnp.square(pre_scr[0] - mu).sum(axis=1, keepdims=True)
        for jj in range(1, nj):
            sq = sq + jnp.square(pre_scr[jj] - mu).sum(axis=1, keepdims=True)
        rstd = lax.rsqrt(sq / d + LN_EPS)
        logits = jnp.zeros(route_ref.shape, F32)
        for jj in range(nj):
            cs = slice(jj * tn, (jj + 1) * tn)
            hn = (pre_scr[jj] - mu) * rstd * g_ref[:, cs] + b_ref[:, cs]
            h_ref[:, cs] = hn
            h_hi, h_lo = _split_bf16(hn)
            w_hi, w_lo = _split_bf16(wr_ref[cs, :])
            logits = logits + (jnp.dot(h_hi, w_hi, preferred_element_type=F32)
                               + jnp.dot(h_hi, w_lo, preferred_element_type=F32)
                               + jnp.dot(h_lo, w_hi, preferred_element_type=F32))
        logits = logits + br_ref[...]
        lane = lax.broadcasted_iota(jnp.int32, logits.shape, 1)
        big = jnp.int32(4 * LANES)
        gl = jnp.where(lane < N_GROUPS, logits, -jnp.inf)
        gmax = jnp.max(gl, axis=1, keepdims=True)
        gsel = jnp.min(jnp.where(gl == gmax, lane, big), axis=1, keepdims=True)
        ggate = 1.0 / jnp.sum(jnp.exp(gl - gmax), axis=1, keepdims=True)
        eid = lane - N_GROUPS
        ingrp = (eid >= gsel * EXPERTS_PER_GROUP) & (eid < (gsel + 1) * EXPERTS_PER_GROUP)
        el = jnp.where(ingrp, logits, -jnp.inf)
        v1 = jnp.max(el, axis=1, keepdims=True)
        i1 = jnp.min(jnp.where(el == v1, lane, big), axis=1, keepdims=True)
        el2 = jnp.where(lane == i1, -jnp.inf, el)
        v2 = jnp.max(el2, axis=1, keepdims=True)
        i2 = jnp.min(jnp.where(el2 == v2, lane, big), axis=1, keepdims=True)
        t = jnp.exp(v2 - v1)
        g1 = ggate / (1.0 + t)
        g2 = g1 * t
        e1 = (i1 - N_GROUPS).astype(F32)
        e2 = (i2 - N_GROUPS).astype(F32)
        route_ref[...] = jnp.where(lane == 0, g1, jnp.where(lane == 1, g2,
                                   jnp.where(lane == 2, e1, jnp.where(lane == 3, e2, 0.0))))
        oh_ref[...] = jnp.where(lane == i1 - N_GROUPS, 1.0,
                                jnp.where(lane == i2 - N_GROUPS, 1.0, 0.0)).astype(oh_ref.dtype)


def _out_ln_router(mg, wo, x, g, b, wr, br, tm, tn):
    m, d = x.shape
    tm, tn = min(tm, m), min(tn, d)
    nj = d // tn
    return pl.pallas_call(
        functools.partial(_out_kernel, tn=tn, nj=nj),
        grid=(m // tm, nj),
        in_specs=[pl.BlockSpec((tm, d), lambda i, j: (i, 0)),
                  pl.BlockSpec((d, tn), lambda i, j: (0, j)),
                  pl.BlockSpec((tm, tn), lambda i, j: (i, j)),
                  pl.BlockSpec((1, d), lambda i, j: (0, 0)),
                  pl.BlockSpec((1, d), lambda i, j: (0, 0)),
                  pl.BlockSpec((d, LANES), lambda i, j: (0, 0)),
                  pl.BlockSpec((1, LANES), lambda i, j: (0, 0))],
        out_specs=[pl.BlockSpec((tm, d), lambda i, j: (i, 0)),
                   pl.BlockSpec((tm, LANES), lambda i, j: (i, 0)),
                   pl.BlockSpec((tm, LANES), lambda i, j: (i, 0))],
        out_shape=[jax.ShapeDtypeStruct((m, d), F32),
                   jax.ShapeDtypeStruct((m, LANES), F32),
                   jax.ShapeDtypeStruct((m, LANES), BF16)],
        scratch_shapes=[pltpu.VMEM((nj, tm, tn), F32)],
        compiler_params=_cparams(("parallel", "arbitrary"), VMEM_LARGE_MB),
        name="outln",
    )(mg, wo, x, g, b, wr, br)


def _rank_kernel(oh_ref, pos_ref, cnt_ref, base_scr, *, tb):
    @pl.when(pl.program_id(0) == 0)
    def _():
        base_scr[...] = jnp.zeros(base_scr.shape, F32)

    oh = oh_ref[...]
    r = lax.broadcasted_iota(jnp.int32, (tb, tb), 0)
    c = lax.broadcasted_iota(jnp.int32, (tb, tb), 1)
    tri = jnp.where(c <= r, 1.0, 0.0).astype(BF16)
    cs = jnp.dot(tri, oh, preferred_element_type=F32)
    pos_ref[...] = cs - oh.astype(F32) + base_scr[0:1, :]
    base_scr[...] = base_scr[...] + cs[tb - 1:tb, :]
    cnt_ref[...] = base_scr[...]


def _rank(onehot, tb):
    m = onehot.shape[0]
    tb = min(tb, m)
    return pl.pallas_call(
        functools.partial(_rank_kernel, tb=tb),
        grid=(m // tb,),
        in_specs=[pl.BlockSpec((tb, LANES), lambda i: (i, 0))],
        out_specs=[pl.BlockSpec((tb, LANES), lambda i: (i, 0)),
                   pl.BlockSpec((8, LANES), lambda i: (0, 0))],
        out_shape=[jax.ShapeDtypeStruct((m, LANES), F32), jax.ShapeDtypeStruct((8, LANES), F32)],
        scratch_shapes=[pltpu.VMEM((8, LANES), F32)],
        compiler_params=_cparams(("arbitrary",), VMEM_SMALL_MB),
        name="rank",
    )(onehot)


def _dest_kernel(pos_ref, route_ref, start_ref, dest_ref):
    lane = lax.broadcasted_iota(jnp.int32, pos_ref.shape, 1).astype(F32)
    v = pos_ref[...] + start_ref[...]
    d1 = jnp.sum(jnp.where(lane == route_ref[:, 2:3], v, 0.0), axis=1, keepdims=True)
    d2 = jnp.sum(jnp.where(lane == route_ref[:, 3:4], v, 0.0), axis=1, keepdims=True)
    dest_ref[...] = jnp.where(lane == 0.0, d1, jnp.where(lane == 1.0, d2, 0.0)).astype(jnp.int32)


def _dest(pos, route, start, tb):
    m = pos.shape[0]
    tb = min(tb, m)
    return pl.pallas_call(
        _dest_kernel,
        grid=(m // tb,),
        in_specs=[pl.BlockSpec((tb, LANES), lambda i: (i, 0)),
                  pl.BlockSpec((tb, LANES), lambda i: (i, 0)),
                  pl.BlockSpec((1, LANES), lambda i: (0, 0))],
        out_specs=pl.BlockSpec((tb, LANES), lambda i: (i, 0)),
        out_shape=jax.ShapeDtypeStruct((m, LANES), jnp.int32),
        compiler_params=_cparams(("parallel",), VMEM_SMALL_MB),
        name="dest",
    )(pos, route, start)


DMA_ISSUE_UNROLL = 4


def _pack_bf16_pair(left, right):
    lo = pltpu.bitcast(left.astype(BF16).astype(F32), jnp.int32)
    hi = pltpu.bitcast(right.astype(BF16).astype(F32), jnp.int32)
    return hi | lax.shift_right_logical(lo, 16)


def _unpack_bf16_pair(words):
    left = pltpu.bitcast(lax.shift_left(words, 16), F32)
    right = pltpu.bitcast(words & jnp.int32(-65536), F32)
    return left, right


def _scatter_kernel(dest_ref, h_ref, xs_in_ref, xs_ref, hp_scr, sem, *, tb):
    del xs_in_ref
    base = pl.program_id(0) * tb
    half = h_ref.shape[1] // 2
    hp_scr[...] = _pack_bf16_pair(h_ref[:, :half], h_ref[:, half:])

    def row_copy(t, slot):
        d = dest_ref[(base + t) * 2 + slot]
        return pltpu.make_async_copy(hp_scr.at[pl.ds(t, 1), :], xs_ref.at[pl.ds(d, 1), :], sem)

    def start(t, carry):
        row_copy(t, 0).start()
        row_copy(t, 1).start()
        return carry

    lax.fori_loop(0, tb, start, 0, unroll=DMA_ISSUE_UNROLL)
    for _ in range(2):
        pltpu.make_async_copy(hp_scr, xs_ref.at[pl.ds(0, tb), :], sem).wait()


def _scatter_rows(dest_flat, h, n_rows, tb):
    m, d = h.shape
    tb = min(tb, m)
    xs0 = jnp.zeros((n_rows, d // 2), jnp.int32)
    return pl.pallas_call(
        functools.partial(_scatter_kernel, tb=tb),
        grid_spec=pltpu.PrefetchScalarGridSpec(
            num_scalar_prefetch=1,
            grid=(m // tb,),
            in_specs=[pl.BlockSpec((tb, d), lambda i, dest: (i, 0)),
                      pl.BlockSpec(memory_space=pl.ANY)],
            out_specs=pl.BlockSpec(memory_space=pl.ANY),
            scratch_shapes=[pltpu.VMEM((tb, d // 2), jnp.int32), pltpu.SemaphoreType.DMA(())]),
        out_shape=jax.ShapeDtypeStruct((n_rows, d // 2), jnp.int32),
        input_output_aliases={2: 0},
        compiler_params=_cparams(("arbitrary",), VMEM_SMALL_MB),
        name="scatter",
    )(dest_flat, h, xs0)


CAST_ROWS = 256


def _segment_tables(te, nv, end_tile, n_tiles):
    ids = jnp.arange(n_tiles, dtype=jnp.int32)
    fresh = jnp.concatenate([jnp.ones((1,), bool), te[1:] != te[:-1]]) & (ids < nv[0])
    seg = jnp.cumsum(fresh.astype(jnp.int32)) - 1
    after = end_tile[te]
    nxt = jnp.where(after < nv[0], te[jnp.minimum(after, n_tiles - 1)], -1)
    nseg = seg[jnp.maximum(nv[0] - 1, 0)][None] + 1
    return seg.astype(jnp.int32), nxt.astype(jnp.int32), nseg.astype(jnp.int32)


def _expert_weight_stream(te_ref, nv_ref, seg_ref, nxt_ref, nseg_ref, w_hbms, slabs, bf16_scrs, sem, width):
    c, r = pl.program_id(0), pl.program_id(1)
    valid = r < nv_ref[0]
    fresh = jnp.logical_and(valid, jnp.logical_or(r == 0, te_ref[r] != te_ref[jnp.maximum(r - 1, 0)]))
    ordinal = c * nseg_ref[0] + seg_ref[r]
    slot = ordinal % 2

    def slab_copies(expert, chunk, s):
        cols = pl.ds(pl.multiple_of(chunk * width, width), width)
        return [pltpu.make_async_copy(w.at[expert, :, cols], slab.at[s], sem.at[n, s])
                for n, (w, slab) in enumerate(zip(w_hbms, slabs))]

    def start(expert, chunk, s):
        for cp in slab_copies(expert, chunk, s):
            cp.start()

    @pl.when(jnp.logical_and(fresh, ordinal == 0))
    def _():
        start(te_ref[0], 0, 0)

    @pl.when(fresh)
    def _():
        for cp in slab_copies(te_ref[r], c, slot):
            cp.wait()
        for slab, scr in zip(slabs, bf16_scrs):
            def cast_rows(n, carry, slab=slab, scr=scr):
                rows = pl.ds(pl.multiple_of(n * CAST_ROWS, CAST_ROWS), CAST_ROWS)
                scr[rows, :] = slab[slot, rows, :].astype(BF16)
                return carry

            lax.fori_loop(0, scr.shape[0] // CAST_ROWS, cast_rows, 0)
        nxt = nxt_ref[r]

        @pl.when(nxt >= 0)
        def _():
            start(nxt, c, 1 - slot)

        @pl.when(jnp.logical_and(nxt < 0, c + 1 < pl.num_programs(0)))
        def _():
            start(te_ref[0], c + 1, 1 - slot)

    return valid


def _gm1_kernel(te_ref, nv_ref, seg_ref, nxt_ref, nseg_ref, xs_ref, w1_ref, w3_ref, o_ref,
                w1f_scr, w3f_scr, w1b_scr, w3b_scr, sem):
    valid = _expert_weight_stream(te_ref, nv_ref, seg_ref, nxt_ref, nseg_ref, (w1_ref, w3_ref),
                                  (w1f_scr, w3f_scr), (w1b_scr, w3b_scr), sem, w1b_scr.shape[1])

    @pl.when(valid)
    def _():
        left, right = _unpack_bf16_pair(xs_ref[...])
        x = jnp.concatenate([left.astype(BF16), right.astype(BF16)], axis=1)
        a = jnp.dot(x, w1b_scr[...], preferred_element_type=F32)
        b = jnp.dot(x, w3b_scr[...], preferred_element_type=F32)
        o_ref[...] = (a * jax.nn.sigmoid(a) * b).astype(o_ref.dtype)

    @pl.when(jnp.logical_not(valid))
    def _():
        o_ref[...] = jnp.zeros(o_ref.shape, o_ref.dtype)


def _gm1(tables, xs, w1, w3, tf):
    n_rows = xs.shape[0]
    d, f = w1.shape[1], w1.shape[2]
    nt = n_rows // ROW_TILE

    def rc(r, nv_):
        return jnp.minimum(r, nv_[0] - 1)

    return pl.pallas_call(
        _gm1_kernel,
        grid_spec=pltpu.PrefetchScalarGridSpec(
            num_scalar_prefetch=len(tables),
            grid=(f // tf, nt),
            in_specs=[pl.BlockSpec((ROW_TILE, d // 2), lambda c, r, te_, nv_, *_: (rc(r, nv_), 0)),
                      pl.BlockSpec(memory_space=pl.ANY),
                      pl.BlockSpec(memory_space=pl.ANY)],
            out_specs=pl.BlockSpec((ROW_TILE, tf), lambda c, r, *_: (r, c)),
            scratch_shapes=[pltpu.VMEM((2, d, tf), F32), pltpu.VMEM((2, d, tf), F32),
                            pltpu.VMEM((d, tf), BF16), pltpu.VMEM((d, tf), BF16),
                            pltpu.SemaphoreType.DMA((2, 2))]),
        out_shape=jax.ShapeDtypeStruct((n_rows, f), BF16),
        compiler_params=_cparams(("arbitrary", "arbitrary"), VMEM_LARGE_MB),
        name="gm1",
    )(*tables, xs, w1, w3)


def _gm2_kernel(te_ref, nv_ref, seg_ref, nxt_ref, nseg_ref, h_ref, w2_ref, o_ref, w2f_scr, w2b_scr, sem):
    valid = _expert_weight_stream(te_ref, nv_ref, seg_ref, nxt_ref, nseg_ref, (w2_ref,),
                                  (w2f_scr,), (w2b_scr,), sem, w2b_scr.shape[1])

    @pl.when(valid)
    def _():
        y = jnp.dot(h_ref[...], w2b_scr[...], preferred_element_type=F32)
        half = y.shape[1] // 2
        o_ref[...] = _pack_bf16_pair(y[:, :half], y[:, half:])

    @pl.when(jnp.logical_not(valid))
    def _():
        o_ref[...] = jnp.zeros(o_ref.shape, o_ref.dtype)


def _gm2(tables, hid, w2, tn):
    n_rows, f = hid.shape
    d = w2.shape[2]
    tn = min(tn, d)
    nt = n_rows // ROW_TILE

    def rc(r, nv_):
        return jnp.minimum(r, nv_[0] - 1)

    return pl.pallas_call(
        _gm2_kernel,
        grid_spec=pltpu.PrefetchScalarGridSpec(
            num_scalar_prefetch=len(tables),
            grid=(d // tn, nt),
            in_specs=[pl.BlockSpec((ROW_TILE, f), lambda c, r, te_, nv_, *_: (rc(r, nv_), 0)),
                      pl.BlockSpec(memory_space=pl.ANY)],
            out_specs=pl.BlockSpec((ROW_TILE, tn // 2), lambda c, r, *_: (r, c)),
            scratch_shapes=[pltpu.VMEM((2, f, tn), F32), pltpu.VMEM((f, tn), BF16),
                            pltpu.SemaphoreType.DMA((1, 2))]),
        out_shape=jax.ShapeDtypeStruct((n_rows, d // 2), jnp.int32),
        compiler_params=_cparams(("arbitrary", "arbitrary"), VMEM_LARGE_MB),
        name="gm2",
    )(*tables, hid, w2)


def _combine_kernel(dest_ref, y_ref, h_ref, route_ref, g_ref, b_ref, o_ref, ybuf, sem, *, tb, tn):
    i, nblk = pl.program_id(0), pl.num_programs(0)

    def row_copy(blk, buf, t, e):
        d = dest_ref[(blk * tb + t) * 2 + e]
        return pltpu.make_async_copy(y_ref.at[pl.ds(d, 1), :], ybuf.at[buf, e, pl.ds(t, 1), :], sem.at[buf])

    def fetch(blk, buf):
        def body(t, carry):
            row_copy(blk, buf, t, 0).start()
            row_copy(blk, buf, t, 1).start()
            return carry

        lax.fori_loop(0, tb, body, 0, unroll=DMA_ISSUE_UNROLL)

    @pl.when(i == 0)
    def _():
        fetch(0, 0)

    @pl.when(i + 1 < nblk)
    def _():
        fetch(i + 1, (i + 1) % 2)

    cur = i % 2

    for e in range(2):
        pltpu.make_async_copy(y_ref.at[pl.ds(0, tb), :], ybuf.at[cur, e], sem.at[cur]).wait()
    half = tn // 2
    pieces = []
    for c in range(h_ref.shape[1] // tn):
        l0, r0 = _unpack_bf16_pair(ybuf[cur, 0, :, c * half:(c + 1) * half])
        l1, r1 = _unpack_bf16_pair(ybuf[cur, 1, :, c * half:(c + 1) * half])
        pieces += [route_ref[:, 0:1] * l0 + route_ref[:, 1:2] * l1, route_ref[:, 0:1] * r0 + route_ref[:, 1:2] * r1]
    moe = jnp.concatenate(pieces, axis=1)
    pre = DEEPNORM_ALPHA * h_ref[...] + moe
    mu = jnp.mean(pre, axis=1, keepdims=True)
    var = jnp.mean(jnp.square(pre - mu), axis=1, keepdims=True)
    o_ref[...] = (pre - mu) * lax.rsqrt(var + LN_EPS) * g_ref[...] + b_ref[...]


def _combine_ln(dest_flat, y, h, route, g, b, tb, tn):
    m, d = h.shape
    tb = min(tb, m)
    return pl.pallas_call(
        functools.partial(_combine_kernel, tb=tb, tn=tn),
        grid_spec=pltpu.PrefetchScalarGridSpec(
            num_scalar_prefetch=1,
            grid=(m // tb,),
            in_specs=[pl.BlockSpec(memory_space=pl.ANY),
                      pl.BlockSpec((tb, d), lambda i, dest: (i, 0)),
                      pl.BlockSpec((tb, LANES), lambda i, dest: (i, 0)),
                      pl.BlockSpec((1, d), lambda i, dest: (0, 0)),
                      pl.BlockSpec((1, d), lambda i, dest: (0, 0))],
            out_specs=pl.BlockSpec((tb, d), lambda i, dest: (i, 0)),
            scratch_shapes=[pltpu.VMEM((2, 2, tb, d // 2), jnp.int32), pltpu.SemaphoreType.DMA((2,))]),
        out_shape=jax.ShapeDtypeStruct((m, d), F32),
        compiler_params=_cparams(("arbitrary",), VMEM_MB),
        name="combine",
    )(dest_flat, y, h, route, g, b)


def _alibi_slopes(n):
    return jnp.asarray(2.0 ** (-8.0 * np.arange(1, n + 1) / n), dtype=F32)


def kernel(x, w_in, lam_q1, lam_k1, lam_q2, lam_k2, diff_subln_g, w_pa, w_pb, w_o, ln1_g, ln1_b,
           router_wg, router_bg, router_we, router_be, w1, w3, w2, ln2_g, ln2_b):
    bsz, s_len, d = x.shape
    assert bsz == 1 and w_in.shape[0] == DEPTH
    topk = min(TOPK_MAX, s_len // 4)
    x2 = x[0]
    xb = x2.astype(BF16)

    qk_w = N_DIFF_HEADS * 2 * HEAD_DIM
    c_dq, c_dk, c_dv = 0, qk_w, 2 * qk_w
    c_sq = 3 * qk_w
    c_sk = c_sq + N_DSA_HEADS * HEAD_DIM
    c_sv = c_sk + N_DSA_KV * HEAD_DIM
    c_iq = c_sv + N_DSA_KV * HEAD_DIM
    c_ik = c_iq + N_IDX_HEADS * IDX_DIM
    c_ga = c_ik + IDX_DIM + N_IDX_HEADS
    qscale = HEAD_DIM ** -0.5 * LOG2E
    col = np.arange(c_ik)
    is_q = ((col >= c_dq) & (col < c_dk)) | ((col >= c_sq) & (col < c_sk))
    col_scale = jnp.asarray(np.where(is_q, qscale, 1.0)[None, :], F32)
    wt3d = jnp.swapaxes(w_in, 1, 2)
    wt = wt3d[0]
    wt_gates = _cast_rows(wt, c_ga, 2 * d, min(PROJ_TILE[1], d))

    z = _proj_from_f32(xb, wt3d, col_scale, c_ik, *PROJ_TILE)
    small = 2 * LANES
    assert c_ik % small == 0 and c_ik + small <= wt.shape[0] and s_len <= POS_RADIX * POS_RADIX
    zs = _matmul_nt(xb, wt, c_ik // small, small, F32, PROJ_TILE[0], "proj_small")
    ik = zs[:, :IDX_DIM].astype(BF16)
    iwt = zs[:, IDX_DIM:IDX_DIM + N_IDX_HEADS].T
    dvt = _with_ones_rows(z[:, c_dv:c_dv + qk_w].T, N_DIFF_HEADS)
    svt = _with_ones_rows(z[:, c_sv:c_sv + N_DSA_KV * HEAD_DIM].T, N_DSA_KV)
    kaug = _key_aug_table(s_len)

    lam4 = jnp.stack([lam_q1[0], lam_k1[0], lam_q2[0], lam_k2[0]]).astype(F32)
    g_lanes = jnp.broadcast_to(diff_subln_g[0][:, None], (2 * HEAD_DIM, LANES))
    a = _diff_attention(z, dvt, kaug, lam4, g_lanes, _alibi_slopes(N_DIFF_HEADS), c_dq, c_dk, s_len,
                        min(ATTN_TILE, s_len))

    tq_i, tk_i = min(DSA_TILE[0], s_len), min(DSA_TILE[1], s_len)
    maskb = _indexer(z, ik, iwt, c_iq, s_len, tq_i, tk_i, topk)
    b = _dsa_attention(z, svt, kaug, maskb, _alibi_slopes(N_DSA_HEADS), c_sq, c_sk, s_len, tq_i, tk_i)

    merged = _merge(xb, wt_gates, a, w_pa[0].astype(BF16), b, w_pb[0].astype(BF16), *MERGE_TILE)

    wr = jnp.concatenate([router_wg[0], router_we[0],
                          jnp.zeros((d, LANES - N_GROUPS - N_EXPERTS), F32)], axis=1)
    br = jnp.concatenate([router_bg[0], router_be[0],
                          jnp.zeros((LANES - N_GROUPS - N_EXPERTS,), F32)])[None, :]
    h1, route, onehot = _out_ln_router(merged, w_o[0].astype(BF16), x2, ln1_g[0][None, :], ln1_b[0][None, :],
                                       wr, br, *OUT_TILE)

    pos, cnt = _rank(onehot, RANK_ROWS)
    counts = cnt[0, :N_EXPERTS].astype(jnp.int32)
    padded = ((counts + ROW_TILE - 1) // ROW_TILE) * ROW_TILE
    ends = jnp.cumsum(padded)
    start = jnp.zeros((1, LANES), F32).at[0, :N_EXPERTS].set((ends - padded).astype(F32))
    n_tiles = (2 * s_len) // ROW_TILE + N_EXPERTS
    tile_ids = jnp.arange(n_tiles, dtype=jnp.int32)
    te = jnp.minimum(jnp.sum(tile_ids[:, None] >= (ends // ROW_TILE)[None, :], axis=1), N_EXPERTS - 1)
    te = te.astype(jnp.int32)
    nv = (ends[-1] // ROW_TILE).astype(jnp.int32)[None]
    dest = _dest(pos, route, start, RANK_ROWS)
    dest_flat = dest[:, :2].reshape(-1)

    xs = _scatter_rows(dest_flat, h1, n_tiles * ROW_TILE, SCATTER_ROWS)
    tables = (te, nv) + _segment_tables(te, nv, ends // ROW_TILE, n_tiles)
    hid = _gm1(tables, xs, w1[0], w3[0], min(GM1_COLS, w1.shape[3]))
    tn_y = min(GM2_COLS, d)
    y = _gm2(tables, hid, w2[0], tn_y)
    out = _combine_ln(dest_flat, y, h1, route, ln2_g[0][None, :], ln2_b[0][None, :], COMBINE_ROWS, tn_y)
    return out[None]
```

```python
import functools
import math

import numpy as np
import jax
import jax.numpy as jnp
from jax import lax
from jax.experimental import pallas as pl
from jax.experimental.pallas import tpu as pltpu

HEAD_DIM = 128
N_DIFF_HEADS = 8
N_DSA_HEADS = 16
N_DSA_KV = 4
DSA_REP = N_DSA_HEADS // N_DSA_KV
N_IDX_HEADS = 32
IDX_DIM = 128
TOPK_MAX = 256
N_GROUPS = 4
EXPERTS_PER_GROUP = 8
N_EXPERTS = N_GROUPS * EXPERTS_PER_GROUP
LN_EPS = 1e-5
RMS_EPS = 1e-5
DEPTH = 1
DEEPNORM_ALPHA = (2.0 * DEPTH) ** 0.25
LAM_INIT = 0.8 - 0.6 * math.exp(-0.3 * 0)

LANES = 128
SUBLANES = 8
NEG = -1e30
INT_MIN = -(2 ** 31)
ROW_TILE = 256
LOG2E = 1.4426950408889634
POS_RADIX = 256

PROJ_TILE = (1024, 512)
ATTN_TILE = 512
DSA_TILE = (256, 512)
MERGE_TILE = (1024, 256)
OUT_TILE = (512, 512)
RANK_ROWS = 512
SCATTER_ROWS = 256
COMBINE_ROWS = 128
GM1_COLS = 512
GM2_COLS = 4096
VMEM_SMALL_MB = 32
VMEM_MB = 48
VMEM_LARGE_MB = 56

F32 = jnp.float32
BF16 = jnp.bfloat16


def _cparams(sem, vmem_mb):
    return pltpu.CompilerParams(dimension_semantics=sem, vmem_limit_bytes=vmem_mb << 20)


def _dot_nt(a, b):
    return lax.dot_general(a, b, (((1,), (1,)), ((), ())), preferred_element_type=F32)


def _proj_kernel(x_ref, wt_ref, sc_ref, o_ref, wb_scr):
    @pl.when(pl.program_id(1) == 0)
    def _():
        wb_scr[...] = wt_ref[0].astype(BF16)

    o_ref[...] = (_dot_nt(x_ref[...], wb_scr[...]) * sc_ref[...]).astype(o_ref.dtype)


def _proj_from_f32(xb, wt3d, col_scale, n_cols, tm, tn):
    m, d = xb.shape
    tm = min(tm, m)
    return pl.pallas_call(
        _proj_kernel,
        grid=(n_cols // tn, m // tm),
        in_specs=[pl.BlockSpec((tm, d), lambda j, i: (i, 0)),
                  pl.BlockSpec((1, tn, d), lambda j, i: (0, j, 0)),
                  pl.BlockSpec((1, tn), lambda j, i: (0, j))],
        out_specs=pl.BlockSpec((tm, tn), lambda j, i: (i, j)),
        out_shape=jax.ShapeDtypeStruct((m, n_cols), BF16),
        scratch_shapes=[pltpu.VMEM((tn, d), BF16)],
        compiler_params=_cparams(("parallel", "arbitrary"), VMEM_MB),
        name="proj_main",
    )(xb, wt3d, col_scale)


def _cast_rows_kernel(w_ref, o_ref):
    o_ref[...] = w_ref[...].astype(o_ref.dtype)


def _cast_rows(wt, row0, n_rows, tr):
    d = wt.shape[1]
    assert row0 % SUBLANES == 0 and n_rows % tr == 0
    return pl.pallas_call(
        _cast_rows_kernel,
        grid=(n_rows // tr,),
        in_specs=[pl.BlockSpec((pl.Element(tr), pl.Element(d)),
                               lambda i: (pl.multiple_of(row0 + i * tr, SUBLANES), 0))],
        out_specs=pl.BlockSpec((tr, d), lambda i: (i, 0)),
        out_shape=jax.ShapeDtypeStruct((n_rows, d), BF16),
        compiler_params=_cparams(("parallel",), VMEM_MB),
        name="cast_gates",
    )(wt)


def _mm_nt_kernel(a_ref, bt_ref, o_ref):
    o_ref[...] = _dot_nt(a_ref[...], bt_ref[...].astype(BF16)).astype(o_ref.dtype)


def _matmul_nt(a, wt, row_block, n, out_dtype, tm, name):
    m, k = a.shape
    tm = min(tm, m)
    return pl.pallas_call(
        _mm_nt_kernel,
        grid=(m // tm,),
        in_specs=[pl.BlockSpec((tm, k), lambda i: (i, 0)),
                  pl.BlockSpec((n, k), lambda i: (row_block, 0))],
        out_specs=pl.BlockSpec((tm, n), lambda i: (i, 0)),
        out_shape=jax.ShapeDtypeStruct((m, n), out_dtype),
        compiler_params=_cparams(("parallel",), VMEM_MB),
        name=name,
    )(a, wt)


def _key_aug_table(s_len):
    pos = jnp.arange(s_len, dtype=jnp.int32)[:, None]
    lane = jnp.arange(LANES, dtype=jnp.int32)[None, :]
    hi = (pos // POS_RADIX).astype(F32)
    lo = (pos % POS_RADIX).astype(F32)
    t = jnp.where(lane < 2, hi, jnp.where(lane < 4, lo, jnp.where(lane < 7, 1.0, 0.0)))
    return t.astype(BF16)


def _bf16_piece(x):
    return x.astype(BF16).astype(F32)


def _query_aug(slope, qbase, rows):
    s2 = jnp.full((SUBLANES, LANES), slope, F32) * LOG2E
    big = s2 * POS_RADIX
    off = -s2 * jnp.full((SUBLANES, LANES), qbase, F32)
    big_hi = _bf16_piece(big)
    s2_hi = _bf16_piece(s2)
    off_hi = _bf16_piece(off)
    off_mid = _bf16_piece(off - off_hi)
    lane = lax.broadcasted_iota(jnp.int32, (SUBLANES, LANES), 1)
    pieces = [big_hi, big - big_hi, s2_hi, s2 - s2_hi, off_hi, off_mid, off - off_hi - off_mid]
    row = jnp.zeros((SUBLANES, LANES), F32)
    for n, piece in enumerate(pieces):
        row = jnp.where(lane == n, piece, row)
    return jnp.broadcast_to(row[0:1, :], (rows, LANES)).astype(BF16)


QUERY_SUB = 512


def _causal_steps(nq, last_of):
    pairs = [(i, j) for i in range(nq) for j in range(last_of(i) + 1)]
    return (jnp.asarray([p[0] for p in pairs], jnp.int32), jnp.asarray([p[1] for p in pairs], jnp.int32))


ONES_ROWS = 16


def _with_ones_rows(vt, n_blocks):
    r = vt.shape[0] // n_blocks
    v3 = vt.reshape(n_blocks, r, vt.shape[1])
    ones = jnp.ones((n_blocks, ONES_ROWS, vt.shape[1]), vt.dtype)
    return jnp.concatenate([v3, ones], axis=1).reshape(n_blocks * (r + ONES_ROWS), vt.shape[1])


def _online_softmax_pv(s, vt1, m_scr, acc_scr, idx, cols):
    at = idx + (slice(None), cols)
    m_prev = m_scr[at]
    m_next = jnp.maximum(m_prev, jnp.max(s, axis=0, keepdims=True))
    p = jnp.exp2((s - m_next).astype(BF16))
    alpha = jnp.exp2(m_prev - m_next)
    m_scr[at] = m_next
    acc_scr[at] = acc_scr[at] * alpha + jnp.dot(vt1, p, preferred_element_type=F32)


DIFF_HEADS_PER_STEP = 2


def _diff_kernel(qi_ref, kj_ref, slopes_ref, lam_ref, g_ref, q_ref, k_ref, ka_ref, vt_ref, o_ref,
                 qa_scr, kk_scr, s0_scr, s1_scr, m_scr, acc_scr, *, tq, n_steps):
    hp, n = pl.program_id(0), pl.program_id(1)
    reps = tq // LANES
    nsub = tq // QUERY_SUB
    w = 2 * HEAD_DIM
    maps = [(hh, c) for hh in range(DIFF_HEADS_PER_STEP) for c in range(2)]
    na = jnp.minimum(n, n_steps - 1)
    nb = jnp.maximum(n - 1, 0)
    ia = qi_ref[na]
    ib, jb = qi_ref[nb], kj_ref[nb]
    has_a, has_b = n < n_steps, n >= 1
    diag_b = jnp.logical_and(has_b, jb == ib)

    score_slots = (s0_scr, s1_scr)

    def map_cols(hh, c):
        return slice(hh * w + c * HEAD_DIM, hh * w + (c + 1) * HEAD_DIM)

    def load_queries():
        for hh in range(DIFF_HEADS_PER_STEP):
            qaug = _query_aug(slopes_ref[hp * DIFF_HEADS_PER_STEP + hh], (ia * tq).astype(F32), tq)
            for c in range(2):
                qa_scr[2 * hh + c, :, :HEAD_DIM] = q_ref[:, map_cols(hh, c)]
                qa_scr[2 * hh + c, :, HEAD_DIM:] = qaug

    def form_scores(slot):
        ka = ka_ref[...]
        for hh, c in maps:
            kk_scr[2 * hh + c, :, :HEAD_DIM] = k_ref[:, map_cols(hh, c)]
            kk_scr[2 * hh + c, :, HEAD_DIM:] = ka
        for m in range(len(maps)):
            score_slots[slot][m] = _dot_nt(kk_scr[m], qa_scr[m])

    def softmax_pv(slot, masked):
        for m, (hh, c) in enumerate(maps):
            vt1 = vt_ref[hh * (w + ONES_ROWS):(hh + 1) * (w + ONES_ROWS), :]
            for u in range(nsub):
                cols = slice(u * QUERY_SUB, (u + 1) * QUERY_SUB)
                s = score_slots[slot][m, :, cols]
                if masked:
                    keep = (lax.broadcasted_iota(jnp.int32, (tq, 1), 0)
                            <= u * QUERY_SUB + lax.broadcasted_iota(jnp.int32, (1, QUERY_SUB), 1))
                    s = jnp.where(keep, s, NEG)
                _online_softmax_pv(s, vt1, m_scr, acc_scr, (m,), cols)

    @pl.when(jnp.logical_and(has_b, jb == 0))
    def _():
        m_scr[...] = jnp.full(m_scr.shape, NEG, F32)
        acc_scr[...] = jnp.zeros(acc_scr.shape, F32)

    def emit_block():
        lam = (jnp.exp(jnp.sum(lam_ref[0:1, :] * lam_ref[1:2, :], axis=1, keepdims=True))
               - jnp.exp(jnp.sum(lam_ref[2:3, :] * lam_ref[3:4, :], axis=1, keepdims=True)) + LAM_INIT)
        g = jnp.concatenate([g_ref[...]] * reps, axis=1)
        for hh in range(DIFF_HEADS_PER_STEP):
            m = 2 * hh
            o = (acc_scr[m, :w, :] / acc_scr[m, w:w + 1, :]
                 - lam * (acc_scr[m + 1, :w, :] / acc_scr[m + 1, w:w + 1, :]))
            o = o * lax.rsqrt(jnp.mean(o * o, axis=0, keepdims=True) + RMS_EPS) * g
            o_ref[:, hh * w:(hh + 1) * w] = (o * (1.0 - LAM_INIT)).T.astype(o_ref.dtype)

    for par in range(2):
        mine = (n % 2) == par

        @pl.when(jnp.logical_and(mine, jnp.logical_and(has_b, jnp.logical_not(diag_b))))
        def _(par=par):
            form_scores(par)
            softmax_pv(1 - par, False)

        @pl.when(jnp.logical_and(mine, jnp.logical_and(diag_b, has_a)))
        def _(par=par):
            load_queries()
            form_scores(par)
            softmax_pv(1 - par, True)
            emit_block()

        @pl.when(jnp.logical_and(mine, jnp.logical_and(diag_b, jnp.logical_not(has_a))))
        def _(par=par):
            softmax_pv(1 - par, True)
            emit_block()

        @pl.when(jnp.logical_and(mine, n == 0))
        def _(par=par):
            load_queries()
            form_scores(par)


def _diff_attention(z, vt, kaug, lam4, g_lanes, slopes, col_q, col_k, s_len, tq):
    nq = s_len // tq
    w = 2 * HEAD_DIM
    wb = DIFF_HEADS_PER_STEP * w
    nmaps = 2 * DIFF_HEADS_PER_STEP
    qb, kb = col_q // wb, col_k // wb
    qi, kj = _causal_steps(nq, lambda i: i)
    n_steps = int(qi.shape[0])

    def ahead(s):
        return jnp.minimum(s, n_steps - 1)

    def behind(s):
        return jnp.maximum(s - 1, 0)

    return pl.pallas_call(
        functools.partial(_diff_kernel, tq=tq, n_steps=n_steps),
        grid_spec=pltpu.PrefetchScalarGridSpec(
            num_scalar_prefetch=3,
            grid=(N_DIFF_HEADS // DIFF_HEADS_PER_STEP, n_steps + 1),
            in_specs=[pl.BlockSpec((4, HEAD_DIM), lambda h, s, qi_, kj_, sl_: (0, 0)),
                      pl.BlockSpec((w, LANES), lambda h, s, qi_, kj_, sl_: (0, 0)),
                      pl.BlockSpec((tq, wb), lambda h, s, qi_, kj_, sl_: (qi_[ahead(s)], qb + h)),
                      pl.BlockSpec((tq, wb), lambda h, s, qi_, kj_, sl_: (kj_[ahead(s)], kb + h)),
                      pl.BlockSpec((tq, LANES), lambda h, s, qi_, kj_, sl_: (kj_[ahead(s)], 0)),
                      pl.BlockSpec((DIFF_HEADS_PER_STEP * (w + ONES_ROWS), tq),
                                   lambda h, s, qi_, kj_, sl_: (h, kj_[behind(s)]))],
            out_specs=pl.BlockSpec((tq, wb), lambda h, s, qi_, kj_, sl_: (qi_[behind(s)], h)),
            scratch_shapes=[pltpu.VMEM((nmaps, tq, w), BF16), pltpu.VMEM((nmaps, tq, w), BF16),
                            pltpu.VMEM((nmaps, tq, tq), F32), pltpu.VMEM((nmaps, tq, tq), F32),
                            pltpu.VMEM((nmaps, 1, tq), F32),
                            pltpu.VMEM((nmaps, w + ONES_ROWS, tq), F32)]),
        out_shape=jax.ShapeDtypeStruct((s_len, N_DIFF_HEADS * w), BF16),
        compiler_params=_cparams(("parallel", "arbitrary"), VMEM_MB),
        name="diffattn",
    )(qi, kj, slopes, lam4, g_lanes, z, z, kaug, vt)


IQ_SPLIT = 4


HALF_BITS = 16
HALF_MASK = (1 << HALF_BITS) - 1
I16_MIN = -(1 << (HALF_BITS - 1))


def _indexer_kernel(iq0_ref, iq1_ref, iq2_ref, iq3_ref, ik_ref, iwt_ref, o_ref,
                    iqh_scr, key_scr, hi_scr, lo_scr, acc_scr, pcut_scr, *, tq, tkc, topk, iw_scale, pos_bits):
    i = pl.program_id(0)
    nk = key_scr.shape[0]
    nchunks = (i * tq + tq - 1) // tkc + 1
    per = N_IDX_HEADS // IQ_SPLIT

    for h in range(N_IDX_HEADS):
        src = (iq0_ref, iq1_ref, iq2_ref, iq3_ref)[h // per]
        iqh_scr[h] = src[:, (h % per) * IDX_DIM:(h % per + 1) * IDX_DIM]

    qpos = i * tq + lax.broadcasted_iota(jnp.int32, (1, tq), 1)

    def key_pos(c):
        return c * tkc + lax.broadcasted_iota(jnp.int32, (tkc, 1), 0)

    def count_keys(pred):
        def body(c, acc):
            x = jnp.where(pred(c, key_scr[c]), 1, 0)
            return acc + jnp.sum(x.reshape(tkc // SUBLANES, SUBLANES, tq), axis=0)

        cnt = lax.fori_loop(0, nchunks, body, jnp.zeros((SUBLANES, tq), jnp.int32))
        return jnp.sum(cnt, axis=0, keepdims=True)

    def chunk(c, carry):
        kc = ik_ref[pl.ds(pl.multiple_of(c * tkc, tkc), tkc), :]
        acc_scr[...] = jnp.zeros(acc_scr.shape, F32)

        def head(h, carry2):
            sc = _dot_nt(kc, iqh_scr[h])
            acc_scr[...] += (iwt_ref[pl.ds(h, 1), :] * iw_scale) * jnp.maximum(sc, 0.0)
            return carry2

        lax.fori_loop(0, N_IDX_HEADS, head, 0, unroll=True)
        score = acc_scr[...]
        score = jnp.where(score == 0.0, 0.0, score)
        bits = pltpu.bitcast(score, jnp.int32)
        skey = bits ^ ((bits >> 31) & 0x7FFFFFFF)
        key = jnp.where(key_pos(c) <= qpos, skey, INT_MIN)
        key_scr[c] = key
        hi_scr[c] = (key >> HALF_BITS).astype(jnp.int16)
        lo_scr[c] = ((key & HALF_MASK) + I16_MIN).astype(jnp.int16)
        return carry

    lax.fori_loop(0, nchunks, chunk, 0)

    def count_halves(half_scr, pred):
        rows = 2 * SUBLANES

        def body(c, acc):
            x = jnp.where(pred(half_scr[c]), jnp.int16(1), jnp.int16(0))
            for g in range(tkc // rows):
                acc = acc + x[g * rows:(g + 1) * rows, :]
            return acc

        cnt = lax.fori_loop(0, nchunks, body, jnp.zeros((rows, tq), jnp.int16))
        return jnp.sum(cnt.astype(jnp.int32), axis=0, keepdims=True)

    def radix_select(half_scr, want):
        def bit_step(bi, u):
            cand = u | lax.shift_left(jnp.int32(1), HALF_BITS - 1 - bi)
            cand16 = (cand + I16_MIN).astype(jnp.int16)
            return jnp.where(count_halves(half_scr, lambda k: k >= cand16) >= want, cand, u)

        return lax.fori_loop(0, HALF_BITS, bit_step, jnp.zeros((1, tq), jnp.int32)) + I16_MIN

    t_hi = radix_select(hi_scr, topk)
    t_hi16 = t_hi.astype(jnp.int16)
    want_lo = topk - count_halves(hi_scr, lambda k: k > t_hi16)

    def keep_lo(c, carry):
        lo_scr[c] = jnp.where(hi_scr[c] == t_hi16, lo_scr[c], jnp.int16(I16_MIN))
        return carry

    lax.fori_loop(0, nchunks, keep_lo, 0)
    t_lo = radix_select(lo_scr, want_lo)
    thr = lax.shift_left(t_hi, HALF_BITS) | (t_lo - I16_MIN)
    thr = jnp.maximum(thr, INT_MIN + 1)

    pcut_scr[...] = jnp.full(pcut_scr.shape, (1 << pos_bits) - 1, jnp.int32)

    @pl.when(jnp.max(count_keys(lambda c, k: k >= thr)) > topk)
    def _():
        need = topk - count_keys(lambda c, k: k > thr)

        def pos_step(bi, p):
            cand = p | lax.shift_left(jnp.int32(1), pos_bits - 1 - bi)
            tied_before = count_keys(lambda c, k: jnp.where(k == thr, key_pos(c), cand) < cand)
            return jnp.where(tied_before < need, cand, p)

        pcut_scr[...] = lax.fori_loop(0, pos_bits, pos_step, jnp.zeros((1, tq), jnp.int32))

    pcut = pcut_scr[...]

    def emit(c, carry):
        k = key_scr[c]
        tie_bias = jnp.where(key_pos(c) <= pcut, 0.0, NEG)
        o_ref[0, c] = jnp.where(k > thr, 0.0, jnp.where(k == thr, tie_bias, NEG)).astype(o_ref.dtype)
        return carry

    lax.fori_loop(0, nchunks, emit, 0)

    def fill(c, carry):
        o_ref[0, c] = jnp.full((tkc, tq), NEG, o_ref.dtype)
        return carry

    lax.fori_loop(nchunks, nk, fill, 0)


def _indexer(z, ik, iwt, col_iq, s_len, tq, tkc, topk):
    nq, nk = s_len // tq, s_len // tkc
    iw_scale = N_IDX_HEADS ** -0.5 * IDX_DIM ** -0.5
    wq = N_IDX_HEADS * IDX_DIM // IQ_SPLIT
    qb = col_iq // wq
    iq_specs = [pl.BlockSpec((tq, wq), functools.partial(lambda i, n: (i, qb + n), n=n)) for n in range(IQ_SPLIT)]
    return pl.pallas_call(
        functools.partial(_indexer_kernel, tq=tq, tkc=tkc, topk=topk, iw_scale=iw_scale,
                          pos_bits=max(1, (s_len - 1).bit_length())),
        grid=(nq,),
        in_specs=iq_specs + [pl.BlockSpec((s_len, IDX_DIM), lambda i: (0, 0)),
                             pl.BlockSpec((N_IDX_HEADS, tq), lambda i: (0, i))],
        out_specs=pl.BlockSpec((1, nk, tkc, tq), lambda i: (i, 0, 0, 0)),
        out_shape=jax.ShapeDtypeStruct((nq, nk, tkc, tq), BF16),
        scratch_shapes=[pltpu.VMEM((N_IDX_HEADS, tq, IDX_DIM), BF16),
                        pltpu.VMEM((nk, tkc, tq), jnp.int32),
                        pltpu.VMEM((nk, tkc, tq), jnp.int16),
                        pltpu.VMEM((nk, tkc, tq), jnp.int16),
                        pltpu.VMEM((tkc, tq), F32),
                        pltpu.VMEM((1, tq), jnp.int32)],
        compiler_params=_cparams(("parallel",), VMEM_MB),
        name="indexer",
    )(z, z, z, z, ik, iwt)


DSA_GROUPS_PER_STEP = 2


def _dsa_kernel(qi_ref, kj_ref, slopes_ref, q_ref, k_ref, ka_ref, vt_ref, mb_ref, o_ref,
                qa_scr, kk_scr, m_scr, acc_scr, *, tq, tk):
    gp, step_id = pl.program_id(0), pl.program_id(1)
    i, j = qi_ref[step_id], kj_ref[step_id]
    last = (i * tq + tq - 1) // tk
    heads = DSA_GROUPS_PER_STEP * DSA_REP

    @pl.when(j == 0)
    def _():
        m_scr[...] = jnp.full(m_scr.shape, NEG, F32)
        acc_scr[...] = jnp.zeros(acc_scr.shape, F32)
        for n in range(heads):
            rows = slice(n * tq, (n + 1) * tq)
            qa_scr[rows, :HEAD_DIM] = q_ref[:, n * HEAD_DIM:(n + 1) * HEAD_DIM]
            qa_scr[rows, HEAD_DIM:] = _query_aug(slopes_ref[gp * heads + n], (i * tq).astype(F32), tq)

    ka = ka_ref[...]
    for gg in range(DSA_GROUPS_PER_STEP):
        kk_scr[gg, :, :HEAD_DIM] = k_ref[:, gg * HEAD_DIM:(gg + 1) * HEAD_DIM]
        kk_scr[gg, :, HEAD_DIM:] = ka
    mb = mb_ref[0, 0].astype(F32)
    scores = [_dot_nt(kk_scr[gg], qa_scr[gg * DSA_REP * tq:(gg + 1) * DSA_REP * tq, :])
              for gg in range(DSA_GROUPS_PER_STEP)]
    for gg in range(DSA_GROUPS_PER_STEP):
        vt1 = vt_ref[gg * (HEAD_DIM + ONES_ROWS):(gg + 1) * (HEAD_DIM + ONES_ROWS), :]
        for r in range(0, DSA_REP, 2):
            n = gg * DSA_REP + r
            _online_softmax_pv(scores[gg][:, r * tq:(r + 2) * tq] + jnp.concatenate([mb, mb], axis=1),
                               vt1, m_scr, acc_scr, (), slice(n * tq, (n + 2) * tq))

    @pl.when(j == last)
    def _():
        o = acc_scr[:HEAD_DIM, :] / acc_scr[HEAD_DIM:HEAD_DIM + 1, :]
        for n in range(heads):
            o_ref[:, n * HEAD_DIM:(n + 1) * HEAD_DIM] = o[:, n * tq:(n + 1) * tq].T.astype(o_ref.dtype)


def _dsa_attention(z, vt, kaug, maskb, slopes, col_q, col_k, s_len, tq, tk):
    nq = s_len // tq
    gps = DSA_GROUPS_PER_STEP
    heads = gps * DSA_REP
    qw, kw = heads * HEAD_DIM, gps * HEAD_DIM
    qb, kb = col_q // qw, col_k // kw
    qi, kj = _causal_steps(nq, lambda i: (i * tq + tq - 1) // tk)
    return pl.pallas_call(
        functools.partial(_dsa_kernel, tq=tq, tk=tk),
        grid_spec=pltpu.PrefetchScalarGridSpec(
            num_scalar_prefetch=3,
            grid=(N_DSA_KV // gps, qi.shape[0]),
            in_specs=[pl.BlockSpec((tq, qw), lambda g, s, qi_, kj_, sl_: (qi_[s], qb + g)),
                      pl.BlockSpec((tk, kw), lambda g, s, qi_, kj_, sl_: (kj_[s], kb + g)),
                      pl.BlockSpec((tk, LANES), lambda g, s, qi_, kj_, sl_: (kj_[s], 0)),
                      pl.BlockSpec((gps * (HEAD_DIM + ONES_ROWS), tk), lambda g, s, qi_, kj_, sl_: (g, kj_[s])),
                      pl.BlockSpec((1, 1, tk, tq), lambda g, s, qi_, kj_, sl_: (qi_[s], kj_[s], 0, 0))],
            out_specs=pl.BlockSpec((tq, qw), lambda g, s, qi_, kj_, sl_: (qi_[s], g)),
            scratch_shapes=[pltpu.VMEM((heads * tq, 2 * HEAD_DIM), BF16),
                            pltpu.VMEM((gps, tk, 2 * HEAD_DIM), BF16),
                            pltpu.VMEM((1, heads * tq), F32),
                            pltpu.VMEM((HEAD_DIM + ONES_ROWS, heads * tq), F32)]),
        out_shape=jax.ShapeDtypeStruct((s_len, N_DSA_HEADS * HEAD_DIM), BF16),
        compiler_params=_cparams(("parallel", "arbitrary"), VMEM_MB),
        name="dsa",
    )(qi, kj, slopes, z, z, kaug, vt, maskb)


def _merge_kernel(x_ref, wga_ref, wgb_ref, a_ref, wpa_ref, b_ref, wpb_ref, o_ref):
    x = x_ref[...]
    ga = _dot_nt(x, wga_ref[...])
    gb = _dot_nt(x, wgb_ref[...])
    pa = jnp.dot(a_ref[...], wpa_ref[...], preferred_element_type=F32)
    pb = jnp.dot(b_ref[...], wpb_ref[...], preferred_element_type=F32)
    o_ref[...] = (jax.nn.sigmoid(ga) * pa + jax.nn.sigmoid(gb) * pb).astype(o_ref.dtype)


def _merge(xb, wt_gates, a, wpa, b, wpb, tm, tn):
    m, d = xb.shape
    ka, kb = a.shape[1], b.shape[1]
    tm, tn = min(tm, m), min(tn, d)
    row = lambda i, j: (i, 0)
    col = lambda i, j: (0, j)
    return pl.pallas_call(
        _merge_kernel,
        grid=(m // tm, d // tn),
        in_specs=[pl.BlockSpec((tm, d), row),
                  pl.BlockSpec((tn, d), lambda i, j: (j, 0)), pl.BlockSpec((tn, d), lambda i, j: (d // tn + j, 0)),
                  pl.BlockSpec((tm, ka), row), pl.BlockSpec((ka, tn), col),
                  pl.BlockSpec((tm, kb), row), pl.BlockSpec((kb, tn), col)],
        out_specs=pl.BlockSpec((tm, tn), lambda i, j: (i, j)),
        out_shape=jax.ShapeDtypeStruct((m, d), BF16),
        compiler_params=_cparams(("parallel", "arbitrary"), VMEM_LARGE_MB),
        name="merge",
    )(xb, wt_gates, wt_gates, a, wpa, b, wpb)


def _split_bf16(x):
    hi = x.astype(BF16)
    lo = (x - hi.astype(F32)).astype(BF16)
    return hi, lo


def _out_kernel(mg_ref, wo_ref, x_ref, g_ref, b_ref, wr_ref, br_ref,
                h_ref, route_ref, oh_ref, pre_scr, *, tn, nj):
    j = pl.program_id(1)
    y = jnp.dot(mg_ref[...], wo_ref[...], preferred_element_type=F32)
    pre_scr[j] = DEEPNORM_ALPHA * x_ref[...] + y

    @pl.when(j == nj - 1)
    def _():
        d = nj * tn
        tot = pre_scr[0].sum(axis=1, keepdims=True)
        for jj in range(1, nj):
            tot = tot + pre_scr[jj].sum(axis=1, keepdims=True)
        mu = tot / d
        sq = jnp.square(pre_scr[0] - mu).sum(axis=1, keepdims=True)
        for jj in range(1, nj):
            sq = sq + jnp.square(pre_scr[jj] - mu).sum(axis=1, keepdims=True)
        rstd = lax.rsqrt(sq / d + LN_EPS)
        logits = jnp.zeros(route_ref.shape, F32)
        for jj in range(nj):
            cs = slice(jj * tn, (jj + 1) * tn)
            hn = (pre_scr[jj] - mu) * rstd * g_ref[:, cs] + b_ref[:, cs]
            h_ref[:, cs] = hn
            h_hi, h_lo = _split_bf16(hn)
            w_hi, w_lo = _split_bf16(wr_ref[cs, :])
            logits = logits + (jnp.dot(h_hi, w_hi, preferred_element_type=F32)
                               + jnp.dot(h_hi, w_lo, preferred_element_type=F32)
                               + jnp.dot(h_lo, w_hi, preferred_element_type=F32))
        logits = logits + br_ref[...]
        lane = lax.broadcasted_iota(jnp.int32, logits.shape, 1)
        big = jnp.int32(4 * LANES)
        gl = jnp.where(lane < N_GROUPS, logits, -jnp.inf)
        gmax = jnp.max(gl, axis=1, keepdims=True)
        gsel = jnp.min(jnp.where(gl == gmax, lane, big), axis=1, keepdims=True)
        ggate = 1.0 / jnp.sum(jnp.exp(gl - gmax), axis=1, keepdims=True)
        eid = lane - N_GROUPS
        ingrp = (eid >= gsel * EXPERTS_PER_GROUP) & (eid < (gsel + 1) * EXPERTS_PER_GROUP)
        el = jnp.where(ingrp, logits, -jnp.inf)
        v1 = jnp.max(el, axis=1, keepdims=True)
        i1 = jnp.min(jnp.where(el == v1, lane, big), axis=1, keepdims=True)
        el2 = jnp.where(lane == i1, -jnp.inf, el)
        v2 = jnp.max(el2, axis=1, keepdims=True)
        i2 = jnp.min(jnp.where(el2 == v2, lane, big), axis=1, keepdims=True)
        t = jnp.exp(v2 - v1)
        g1 = ggate / (1.0 + t)
        g2 = g1 * t
        e1 = (i1 - N_GROUPS).astype(F32)
        e2 = (i2 - N_GROUPS).astype(F32)
        route_ref[...] = jnp.where(lane == 0, g1, jnp.where(lane == 1, g2,
                                   jnp.where(lane == 2, e1, jnp.where(lane == 3, e2, 0.0))))
        oh_ref[...] = jnp.where(lane == i1 - N_GROUPS, 1.0,
                                jnp.where(lane == i2 - N_GROUPS, 1.0, 0.0)).astype(oh_ref.dtype)


def _out_ln_router(mg, wo, x, g, b, wr, br, tm, tn):
    m, d = x.shape
    tm, tn = min(tm, m), min(tn, d)
    nj = d // tn
    return pl.pallas_call(
        functools.partial(_out_kernel, tn=tn, nj=nj),
        grid=(m // tm, nj),
        in_specs=[pl.BlockSpec((tm, d), lambda i, j: (i, 0)),
                  pl.BlockSpec((d, tn), lambda i, j: (0, j)),
                  pl.BlockSpec((tm, tn), lambda i, j: (i, j)),
                  pl.BlockSpec((1, d), lambda i, j: (0, 0)),
                  pl.BlockSpec((1, d), lambda i, j: (0, 0)),
                  pl.BlockSpec((d, LANES), lambda i, j: (0, 0)),
                  pl.BlockSpec((1, LANES), lambda i, j: (0, 0))],
        out_specs=[pl.BlockSpec((tm, d), lambda i, j: (i, 0)),
                   pl.BlockSpec((tm, LANES), lambda i, j: (i, 0)),
                   pl.BlockSpec((tm, LANES), lambda i, j: (i, 0))],
        out_shape=[jax.ShapeDtypeStruct((m, d), F32),
                   jax.ShapeDtypeStruct((m, LANES), F32),
                   jax.ShapeDtypeStruct((m, LANES), BF16)],
        scratch_shapes=[pltpu.VMEM((nj, tm, tn), F32)],
        compiler_params=_cparams(("parallel", "arbitrary"), VMEM_LARGE_MB),
        name="outln",
    )(mg, wo, x, g, b, wr, br)


def _rank_kernel(oh_ref, pos_ref, cnt_ref, base_scr, *, tb):
    @pl.when(pl.program_id(0) == 0)
    def _():
        base_scr[...] = jnp.zeros(base_scr.shape, F32)

    oh = oh_ref[...]
    r = lax.broadcasted_iota(jnp.int32, (tb, tb), 0)
    c = lax.broadcasted_iota(jnp.int32, (tb, tb), 1)
    tri = jnp.where(c <= r, 1.0, 0.0).astype(BF16)
    cs = jnp.dot(tri, oh, preferred_element_type=F32)
    pos_ref[...] = cs - oh.astype(F32) + base_scr[0:1, :]
    base_scr[...] = base_scr[...] + cs[tb - 1:tb, :]
    cnt_ref[...] = base_scr[...]


def _rank(onehot, tb):
    m = onehot.shape[0]
    tb = min(tb, m)
    return pl.pallas_call(
        functools.partial(_rank_kernel, tb=tb),
        grid=(m // tb,),
        in_specs=[pl.BlockSpec((tb, LANES), lambda i: (i, 0))],
        out_specs=[pl.BlockSpec((tb, LANES), lambda i: (i, 0)),
                   pl.BlockSpec((8, LANES), lambda i: (0, 0))],
        out_shape=[jax.ShapeDtypeStruct((m, LANES), F32), jax.ShapeDtypeStruct((8, LANES), F32)],
        scratch_shapes=[pltpu.VMEM((8, LANES), F32)],
        compiler_params=_cparams(("arbitrary",), VMEM_SMALL_MB),
        name="rank",
    )(onehot)


def _dest_kernel(pos_ref, route_ref, start_ref, dest_ref):
    lane = lax.broadcasted_iota(jnp.int32, pos_ref.shape, 1).astype(F32)
    v = pos_ref[...] + start_ref[...]
    d1 = jnp.sum(jnp.where(lane == route_ref[:, 2:3], v, 0.0), axis=1, keepdims=True)
    d2 = jnp.sum(jnp.where(lane == route_ref[:, 3:4], v, 0.0), axis=1, keepdims=True)
    dest_ref[...] = jnp.where(lane == 0.0, d1, jnp.where(lane == 1.0, d2, 0.0)).astype(jnp.int32)


def _dest(pos, route, start, tb):
    m = pos.shape[0]
    tb = min(tb, m)
    return pl.pallas_call(
        _dest_kernel,
        grid=(m // tb,),
        in_specs=[pl.BlockSpec((tb, LANES), lambda i: (i, 0)),
                  pl.BlockSpec((tb, LANES), lambda i: (i, 0)),
                  pl.BlockSpec((1, LANES), lambda i: (0, 0))],
        out_specs=pl.BlockSpec((tb, LANES), lambda i: (i, 0)),
        out_shape=jax.ShapeDtypeStruct((m, LANES), jnp.int32),
        compiler_params=_cparams(("parallel",), VMEM_SMALL_MB),
        name="dest",
    )(pos, route, start)


DMA_ISSUE_UNROLL = 4


def _pack_bf16_pair(left, right):
    lo = pltpu.bitcast(left.astype(BF16).astype(F32), jnp.int32)
    hi = pltpu.bitcast(right.astype(BF16).astype(F32), jnp.int32)
    return hi | lax.shift_right_logical(lo, 16)


def _unpack_bf16_pair(words):
    left = pltpu.bitcast(lax.shift_left(words, 16), F32)
    right = pltpu.bitcast(words & jnp.int32(-65536), F32)
    return left, right


def _scatter_kernel(dest_ref, h_ref, xs_in_ref, xs_ref, hp_scr, sem, *, tb):
    del xs_in_ref
    base = pl.program_id(0) * tb
    half = h_ref.shape[1] // 2
    hp_scr[...] = _pack_bf16_pair(h_ref[:, :half], h_ref[:, half:])

    def row_copy(t, slot):
        d = dest_ref[(base + t) * 2 + slot]
        return pltpu.make_async_copy(hp_scr.at[pl.ds(t, 1), :], xs_ref.at[pl.ds(d, 1), :], sem)

    def start(t, carry):
        row_copy(t, 0).start()
        row_copy(t, 1).start()
        return carry

    lax.fori_loop(0, tb, start, 0, unroll=DMA_ISSUE_UNROLL)
    for _ in range(2):
        pltpu.make_async_copy(hp_scr, xs_ref.at[pl.ds(0, tb), :], sem).wait()


def _scatter_rows(dest_flat, h, n_rows, tb):
    m, d = h.shape
    tb = min(tb, m)
    xs0 = jnp.zeros((n_rows, d // 2), jnp.int32)
    return pl.pallas_call(
        functools.partial(_scatter_kernel, tb=tb),
        grid_spec=pltpu.PrefetchScalarGridSpec(
            num_scalar_prefetch=1,
            grid=(m // tb,),
            in_specs=[pl.BlockSpec((tb, d), lambda i, dest: (i, 0)),
                      pl.BlockSpec(memory_space=pl.ANY)],
            out_specs=pl.BlockSpec(memory_space=pl.ANY),
            scratch_shapes=[pltpu.VMEM((tb, d // 2), jnp.int32), pltpu.SemaphoreType.DMA(())]),
        out_shape=jax.ShapeDtypeStruct((n_rows, d // 2), jnp.int32),
        input_output_aliases={2: 0},
        compiler_params=_cparams(("arbitrary",), VMEM_SMALL_MB),
        name="scatter",
    )(dest_flat, h, xs0)


CAST_ROWS = 256


def _segment_tables(te, nv, end_tile, n_tiles):
    ids = jnp.arange(n_tiles, dtype=jnp.int32)
    fresh = jnp.concatenate([jnp.ones((1,), bool), te[1:] != te[:-1]]) & (ids < nv[0])
    seg = jnp.cumsum(fresh.astype(jnp.int32)) - 1
    after = end_tile[te]
    nxt = jnp.where(after < nv[0], te[jnp.minimum(after, n_tiles - 1)], -1)
    nseg = seg[jnp.maximum(nv[0] - 1, 0)][None] + 1
    return seg.astype(jnp.int32), nxt.astype(jnp.int32), nseg.astype(jnp.int32)


def _expert_weight_stream(te_ref, nv_ref, seg_ref, nxt_ref, nseg_ref, w_hbms, slabs, bf16_scrs, sem, width):
    c, r = pl.program_id(0), pl.program_id(1)
    valid = r < nv_ref[0]
    fresh = jnp.logical_and(valid, jnp.logical_or(r == 0, te_ref[r] != te_ref[jnp.maximum(r - 1, 0)]))
    ordinal = c * nseg_ref[0] + seg_ref[r]
    slot = ordinal % 2

    def slab_copies(expert, chunk, s):
        cols = pl.ds(pl.multiple_of(chunk * width, width), width)
        return [pltpu.make_async_copy(w.at[expert, :, cols], slab.at[s], sem.at[n, s])
                for n, (w, slab) in enumerate(zip(w_hbms, slabs))]

    def start(expert, chunk, s):
        for cp in slab_copies(expert, chunk, s):
            cp.start()

    @pl.when(jnp.logical_and(fresh, ordinal == 0))
    def _():
        start(te_ref[0], 0, 0)

    @pl.when(fresh)
    def _():
        for cp in slab_copies(te_ref[r], c, slot):
            cp.wait()
        for slab, scr in zip(slabs, bf16_scrs):
            def cast_rows(n, carry, slab=slab, scr=scr):
                rows = pl.ds(pl.multiple_of(n * CAST_ROWS, CAST_ROWS), CAST_ROWS)
                scr[rows, :] = slab[slot, rows, :].astype(BF16)
                return carry

            lax.fori_loop(0, scr.shape[0] // CAST_ROWS, cast_rows, 0)
        nxt = nxt_ref[r]

        @pl.when(nxt >= 0)
        def _():
            start(nxt, c, 1 - slot)

        @pl.when(jnp.logical_and(nxt < 0, c + 1 < pl.num_programs(0)))
        def _():
            start(te_ref[0], c + 1, 1 - slot)

    return valid


def _gm1_kernel(te_ref, nv_ref, seg_ref, nxt_ref, nseg_ref, xs_ref, w1_ref, w3_ref, o_ref,
                w1f_scr, w3f_scr, w1b_scr, w3b_scr, sem):
    valid = _expert_weight_stream(te_ref, nv_ref, seg_ref, nxt_ref, nseg_ref, (w1_ref, w3_ref),
                                  (w1f_scr, w3f_scr), (w1b_scr, w3b_scr), sem, w1b_scr.shape[1])

    @pl.when(valid)
    def _():
        left, right = _unpack_bf16_pair(xs_ref[...])
        x = jnp.concatenate([left.astype(BF16), right.astype(BF16)], axis=1)
        a = jnp.dot(x, w1b_scr[...], preferred_element_type=F32)
        b = jnp.dot(x, w3b_scr[...], preferred_element_type=F32)
        o_ref[...] = (a * jax.nn.sigmoid(a) * b).astype(o_ref.dtype)

    @pl.when(jnp.logical_not(valid))
    def _():
        o_ref[...] = jnp.zeros(o_ref.shape, o_ref.dtype)


def _gm1(tables, xs, w1, w3, tf):
    n_rows = xs.shape[0]
    d, f = w1.shape[1], w1.shape[2]
    nt = n_rows // ROW_TILE

    def rc(r, nv_):
        return jnp.minimum(r, nv_[0] - 1)

    return pl.pallas_call(
        _gm1_kernel,
        grid_spec=pltpu.PrefetchScalarGridSpec(
            num_scalar_prefetch=len(tables),
            grid=(f // tf, nt),
            in_specs=[pl.BlockSpec((ROW_TILE, d // 2), lambda c, r, te_, nv_, *_: (rc(r, nv_), 0)),
                      pl.BlockSpec(memory_space=pl.ANY),
                      pl.BlockSpec(memory_space=pl.ANY)],
            out_specs=pl.BlockSpec((ROW_TILE, tf), lambda c, r, *_: (r, c)),
            scratch_shapes=[pltpu.VMEM((2, d, tf), F32), pltpu.VMEM((2, d, tf), F32),
                            pltpu.VMEM((d, tf), BF16), pltpu.VMEM((d, tf), BF16),
                            pltpu.SemaphoreType.DMA((2, 2))]),
        out_shape=jax.ShapeDtypeStruct((n_rows, f), BF16),
        compiler_params=_cparams(("arbitrary", "arbitrary"), VMEM_LARGE_MB),
        name="gm1",
    )(*tables, xs, w1, w3)


def _gm2_kernel(te_ref, nv_ref, seg_ref, nxt_ref, nseg_ref, h_ref, w2_ref, o_ref, w2f_scr, w2b_scr, sem):
    valid = _expert_weight_stream(te_ref, nv_ref, seg_ref, nxt_ref, nseg_ref, (w2_ref,),
                                  (w2f_scr,), (w2b_scr,), sem, w2b_scr.shape[1])

    @pl.when(valid)
    def _():
        y = jnp.dot(h_ref[...], w2b_scr[...], preferred_element_type=F32)
        half = y.shape[1] // 2
        o_ref[...] = _pack_bf16_pair(y[:, :half], y[:, half:])

    @pl.when(jnp.logical_not(valid))
    def _():
        o_ref[...] = jnp.zeros(o_ref.shape, o_ref.dtype)


def _gm2(tables, hid, w2, tn):
    n_rows, f = hid.shape
    d = w2.shape[2]
    tn = min(tn, d)
    nt = n_rows // ROW_TILE

    def rc(r, nv_):
        return jnp.minimum(r, nv_[0] - 1)

    return pl.pallas_call(
        _gm2_kernel,
        grid_spec=pltpu.PrefetchScalarGridSpec(
            num_scalar_prefetch=len(tables),
            grid=(d // tn, nt),
            in_specs=[pl.BlockSpec((ROW_TILE, f), lambda c, r, te_, nv_, *_: (rc(r, nv_), 0)),
                      pl.BlockSpec(memory_space=pl.ANY)],
            out_specs=pl.BlockSpec((ROW_TILE, tn // 2), lambda c, r, *_: (r, c)),
            scratch_shapes=[pltpu.VMEM((2, f, tn), F32), pltpu.VMEM((f, tn), BF16),
                            pltpu.SemaphoreType.DMA((1, 2))]),
        out_shape=jax.ShapeDtypeStruct((n_rows, d // 2), jnp.int32),
        compiler_params=_cparams(("arbitrary", "arbitrary"), VMEM_LARGE_MB),
        name="gm2",
    )(*tables, hid, w2)


def _combine_kernel(dest_ref, y_ref, h_ref, route_ref, g_ref, b_ref, o_ref, ybuf, sem, *, tb, tn):
    i, nblk = pl.program_id(0), pl.num_programs(0)

    def row_copy(blk, buf, t, e):
        d = dest_ref[(blk * tb + t) * 2 + e]
        return pltpu.make_async_copy(y_ref.at[pl.ds(d, 1), :], ybuf.at[buf, e, pl.ds(t, 1), :], sem.at[buf])

    def fetch(blk, buf):
        def body(t, carry):
            row_copy(blk, buf, t, 0).start()
            row_copy(blk, buf, t, 1).start()
            return carry

        lax.fori_loop(0, tb, body, 0, unroll=DMA_ISSUE_UNROLL)

    @pl.when(i == 0)
    def _():
        fetch(0, 0)

    @pl.when(i + 1 < nblk)
    def _():
        fetch(i + 1, (i + 1) % 2)

    cur = i % 2

    for e in range(2):
        pltpu.make_async_copy(y_ref.at[pl.ds(0, tb), :], ybuf.at[cur, e], sem.at[cur]).wait()
    half = tn // 2
    pieces = []
    for c in range(h_ref.shape[1] // tn):
        l0, r0 = _unpack_bf16_pair(ybuf[cur, 0, :, c * half:(c + 1) * half])
        l1, r1 = _unpack_bf16_pair(ybuf[cur, 1, :, c * half:(c + 1) * half])
        pieces += [route_ref[:, 0:1] * l0 + route_ref[:, 1:2] * l1, route_ref[:, 0:1] * r0 + route_ref[:, 1:2] * r1]
    moe = jnp.concatenate(pieces, axis=1)
    pre = DEEPNORM_ALPHA * h_ref[...] + moe
    mu = jnp.mean(pre, axis=1, keepdims=True)
    var = jnp.mean(jnp.square(pre - mu), axis=1, keepdims=True)
    o_ref[...] = (pre - mu) * lax.rsqrt(var + LN_EPS) * g_ref[...] + b_ref[...]


def _combine_ln(dest_flat, y, h, route, g, b, tb, tn):
    m, d = h.shape
    tb = min(tb, m)
    return pl.pallas_call(
        functools.partial(_combine_kernel, tb=tb, tn=tn),
        grid_spec=pltpu.PrefetchScalarGridSpec(
            num_scalar_prefetch=1,
            grid=(m // tb,),
            in_specs=[pl.BlockSpec(memory_space=pl.ANY),
                      pl.BlockSpec((tb, d), lambda i, dest: (i, 0)),
                      pl.BlockSpec((tb, LANES), lambda i, dest: (i, 0)),
                      pl.BlockSpec((1, d), lambda i, dest: (0, 0)),
                      pl.BlockSpec((1, d), lambda i, dest: (0, 0))],
            out_specs=pl.BlockSpec((tb, d), lambda i, dest: (i, 0)),
            scratch_shapes=[pltpu.VMEM((2, 2, tb, d // 2), jnp.int32), pltpu.SemaphoreType.DMA((2,))]),
        out_shape=jax.ShapeDtypeStruct((m, d), F32),
        compiler_params=_cparams(("arbitrary",), VMEM_MB),
        name="combine",
    )(dest_flat, y, h, route, g, b)


def _alibi_slopes(n):
    return jnp.asarray(2.0 ** (-8.0 * np.arange(1, n + 1) / n), dtype=F32)


def kernel(x, w_in, lam_q1, lam_k1, lam_q2, lam_k2, diff_subln_g, w_pa, w_pb, w_o, ln1_g, ln1_b,
           router_wg, router_bg, router_we, router_be, w1, w3, w2, ln2_g, ln2_b):
    bsz, s_len, d = x.shape
    assert bsz == 1 and w_in.shape[0] == DEPTH
    topk = min(TOPK_MAX, s_len // 4)
    x2 = x[0]
    xb = x2.astype(BF16)

    qk_w = N_DIFF_HEADS * 2 * HEAD_DIM
    c_dq, c_dk, c_dv = 0, qk_w, 2 * qk_w
    c_sq = 3 * qk_w
    c_sk = c_sq + N_DSA_HEADS * HEAD_DIM
    c_sv = c_sk + N_DSA_KV * HEAD_DIM
    c_iq = c_sv + N_DSA_KV * HEAD_DIM
    c_ik = c_iq + N_IDX_HEADS * IDX_DIM
    c_ga = c_ik + IDX_DIM + N_IDX_HEADS
    qscale = HEAD_DIM ** -0.5 * LOG2E
    col = np.arange(c_ik)
    is_q = ((col >= c_dq) & (col < c_dk)) | ((col >= c_sq) & (col < c_sk))
    col_scale = jnp.asarray(np.where(is_q, qscale, 1.0)[None, :], F32)
    wt3d = jnp.swapaxes(w_in, 1, 2)
    wt = wt3d[0]
    wt_gates = _cast_rows(wt, c_ga, 2 * d, min(PROJ_TILE[1], d))

    z = _proj_from_f32(xb, wt3d, col_scale, c_ik, *PROJ_TILE)
    small = 2 * LANES
    assert c_ik % small == 0 and c_ik + small <= wt.shape[0] and s_len <= POS_RADIX * POS_RADIX
    zs = _matmul_nt(xb, wt, c_ik // small, small, F32, PROJ_TILE[0], "proj_small")
    ik = zs[:, :IDX_DIM].astype(BF16)
    iwt = zs[:, IDX_DIM:IDX_DIM + N_IDX_HEADS].T
    dvt = _with_ones_rows(z[:, c_dv:c_dv + qk_w].T, N_DIFF_HEADS)
    svt = _with_ones_rows(z[:, c_sv:c_sv + N_DSA_KV * HEAD_DIM].T, N_DSA_KV)
    kaug = _key_aug_table(s_len)

    lam4 = jnp.stack([lam_q1[0], lam_k1[0], lam_q2[0], lam_k2[0]]).astype(F32)
    g_lanes = jnp.broadcast_to(diff_subln_g[0][:, None], (2 * HEAD_DIM, LANES))
    a = _diff_attention(z, dvt, kaug, lam4, g_lanes, _alibi_slopes(N_DIFF_HEADS), c_dq, c_dk, s_len,
                        min(ATTN_TILE, s_len))

    tq_i, tk_i = min(DSA_TILE[0], s_len), min(DSA_TILE[1], s_len)
    maskb = _indexer(z, ik, iwt, c_iq, s_len, tq_i, tk_i, topk)
    b = _dsa_attention(z, svt, kaug, maskb, _alibi_slopes(N_DSA_HEADS), c_sq, c_sk, s_len, tq_i, tk_i)

    merged = _merge(xb, wt_gates, a, w_pa[0].astype(BF16), b, w_pb[0].astype(BF16), *MERGE_TILE)

    wr = jnp.concatenate([router_wg[0], router_we[0],
                          jnp.zeros((d, LANES - N_GROUPS - N_EXPERTS), F32)], axis=1)
    br = jnp.concatenate([router_bg[0], router_be[0],
                          jnp.zeros((LANES - N_GROUPS - N_EXPERTS,), F32)])[None, :]
    h1, route, onehot = _out_ln_router(merged, w_o[0].astype(BF16), x2, ln1_g[0][None, :], ln1_b[0][None, :],
                                       wr, br, *OUT_TILE)

    pos, cnt = _rank(onehot, RANK_ROWS)
    counts = cnt[0, :N_EXPERTS].astype(jnp.int32)
    padded = ((counts + ROW_TILE - 1) // ROW_TILE) * ROW_TILE
    ends = jnp.cumsum(padded)
    start = jnp.zeros((1, LANES), F32).at[0, :N_EXPERTS].set((ends - padded).astype(F32))
    n_tiles = (2 * s_len) // ROW_TILE + N_EXPERTS
    tile_ids = jnp.arange(n_tiles, dtype=jnp.int32)
    te = jnp.minimum(jnp.sum(tile_ids[:, None] >= (ends // ROW_TILE)[None, :], axis=1), N_EXPERTS - 1)
    te = te.astype(jnp.int32)
    nv = (ends[-1] // ROW_TILE).astype(jnp.int32)[None]
    dest = _dest(pos, route, start, RANK_ROWS)
    dest_flat = dest[:, :2].reshape(-1)

    xs = _scatter_rows(dest_flat, h1, n_tiles * ROW_TILE, SCATTER_ROWS)
    tables = (te, nv) + _segment_tables(te, nv, ends // ROW_TILE, n_tiles)
    hid = _gm1(tables, xs, w1[0], w3[0], min(GM1_COLS, w1.shape[3]))
    tn_y = min(GM2_COLS, d)
    y = _gm2(tables, hid, w2[0], tn_y)
    out = _combine_ln(dest_flat, y, h1, route, ln2_g[0][None, :], ln2_b[0][None, :], COMBINE_ROWS, tn_y)
    return out[None]
```

```python
import functools
import math

import numpy as np
import jax
import jax.numpy as jnp
from jax import lax
from jax.experimental import pallas as pl
from jax.experimental.pallas import tpu as pltpu

HEAD_DIM = 128
N_DIFF_HEADS = 8
N_DSA_HEADS = 16
N_DSA_KV = 4
DSA_REP = N_DSA_HEADS // N_DSA_KV
N_IDX_HEADS = 32
IDX_DIM = 128
TOPK_MAX = 256
N_GROUPS = 4
EXPERTS_PER_GROUP = 8
N_EXPERTS = N_GROUPS * EXPERTS_PER_GROUP
LN_EPS = 1e-5
RMS_EPS = 1e-5
DEPTH = 1
DEEPNORM_ALPHA = (2.0 * DEPTH) ** 0.25
LAM_INIT = 0.8 - 0.6 * math.exp(-0.3 * 0)

LANES = 128
SUBLANES = 8
NEG = -1e30
INT_MIN = -(2 ** 31)
ROW_TILE = 256
LOG2E = 1.4426950408889634
POS_RADIX = 256

PROJ_TILE = (1024, 512)
ATTN_TILE = 512
DSA_TILE = (256, 512)
MERGE_TILE = (1024, 256)
OUT_TILE = (512, 512)
RANK_ROWS = 512
SCATTER_ROWS = 256
COMBINE_ROWS = 128
GM1_COLS = 512
GM2_COLS = 4096
VMEM_SMALL_MB = 32
VMEM_MB = 48
VMEM_LARGE_MB = 56

F32 = jnp.float32
BF16 = jnp.bfloat16


def _cparams(sem, vmem_mb):
    return pltpu.CompilerParams(dimension_semantics=sem, vmem_limit_bytes=vmem_mb << 20)


def _dot_nt(a, b):
    return lax.dot_general(a, b, (((1,), (1,)), ((), ())), preferred_element_type=F32)


def _proj_kernel(x_ref, wt_ref, sc_ref, o_ref, wb_scr):
    @pl.when(pl.program_id(1) == 0)
    def _():
        wb_scr[...] = wt_ref[0].astype(BF16)

    o_ref[...] = (_dot_nt(x_ref[...], wb_scr[...]) * sc_ref[...]).astype(o_ref.dtype)


def _proj_from_f32(xb, wt3d, col_scale, n_cols, tm, tn):
    m, d = xb.shape
    tm = min(tm, m)
    return pl.pallas_call(
        _proj_kernel,
        grid=(n_cols // tn, m // tm),
        in_specs=[pl.BlockSpec((tm, d), lambda j, i: (i, 0)),
                  pl.BlockSpec((1, tn, d), lambda j, i: (0, j, 0)),
                  pl.BlockSpec((1, tn), lambda j, i: (0, j))],
        out_specs=pl.BlockSpec((tm, tn), lambda j, i: (i, j)),
        out_shape=jax.ShapeDtypeStruct((m, n_cols), BF16),
        scratch_shapes=[pltpu.VMEM((tn, d), BF16)],
        compiler_params=_cparams(("parallel", "arbitrary"), VMEM_MB),
        name="proj_main",
    )(xb, wt3d, col_scale)


def _cast_rows_kernel(w_ref, o_ref):
    o_ref[...] = w_ref[...].astype(o_ref.dtype)


def _cast_rows(wt, row0, n_rows, tr):
    d = wt.shape[1]
    assert row0 % SUBLANES == 0 and n_rows % tr == 0
    return pl.pallas_call(
        _cast_rows_kernel,
        grid=(n_rows // tr,),
        in_specs=[pl.BlockSpec((pl.Element(tr), pl.Element(d)),
                               lambda i: (pl.multiple_of(row0 + i * tr, SUBLANES), 0))],
        out_specs=pl.BlockSpec((tr, d), lambda i: (i, 0)),
        out_shape=jax.ShapeDtypeStruct((n_rows, d), BF16),
        compiler_params=_cparams(("parallel",), VMEM_MB),
        name="cast_gates",
    )(wt)


def _mm_nt_kernel(a_ref, bt_ref, o_ref):
    o_ref[...] = _dot_nt(a_ref[...], bt_ref[...].astype(BF16)).astype(o_ref.dtype)


def _matmul_nt(a, wt, row_block, n, out_dtype, tm, name):
    m, k = a.shape
    tm = min(tm, m)
    return pl.pallas_call(
        _mm_nt_kernel,
        grid=(m // tm,),
        in_specs=[pl.BlockSpec((tm, k), lambda i: (i, 0)),
                  pl.BlockSpec((n, k), lambda i: (row_block, 0))],
        out_specs=pl.BlockSpec((tm, n), lambda i: (i, 0)),
        out_shape=jax.ShapeDtypeStruct((m, n), out_dtype),
        compiler_params=_cparams(("parallel",), VMEM_MB),
        name=name,
    )(a, wt)


def _key_aug_table(s_len):
    pos = jnp.arange(s_len, dtype=jnp.int32)[:, None]
    lane = jnp.arange(LANES, dtype=jnp.int32)[None, :]
    hi = (pos // POS_RADIX).astype(F32)
    lo = (pos % POS_RADIX).astype(F32)
    t = jnp.where(lane < 2, hi, jnp.where(lane < 4, lo, jnp.where(lane < 7, 1.0, 0.0)))
    return t.astype(BF16)


def _bf16_piece(x):
    return x.astype(BF16).astype(F32)


def _query_aug(slope, qbase, rows):
    s2 = jnp.full((SUBLANES, LANES), slope, F32) * LOG2E
    big = s2 * POS_RADIX
    off = -s2 * jnp.full((SUBLANES, LANES), qbase, F32)
    big_hi = _bf16_piece(big)
    s2_hi = _bf16_piece(s2)
    off_hi = _bf16_piece(off)
    off_mid = _bf16_piece(off - off_hi)
    lane = lax.broadcasted_iota(jnp.int32, (SUBLANES, LANES), 1)
    pieces = [big_hi, big - big_hi, s2_hi, s2 - s2_hi, off_hi, off_mid, off - off_hi - off_mid]
    row = jnp.zeros((SUBLANES, LANES), F32)
    for n, piece in enumerate(pieces):
        row = jnp.where(lane == n, piece, row)
    return jnp.broadcast_to(row[0:1, :], (rows, LANES)).astype(BF16)


QUERY_SUB = 512


def _causal_steps(nq, last_of):
    pairs = [(i, j) for i in range(nq) for j in range(last_of(i) + 1)]
    return (jnp.asarray([p[0] for p in pairs], jnp.int32), jnp.asarray([p[1] for p in pairs], jnp.int32))


ONES_ROWS = 16


def _with_ones_rows(vt, n_blocks):
    r = vt.shape[0] // n_blocks
    v3 = vt.reshape(n_blocks, r, vt.shape[1])
    ones = jnp.ones((n_blocks, ONES_ROWS, vt.shape[1]), vt.dtype)
    return jnp.concatenate([v3, ones], axis=1).reshape(n_blocks * (r + ONES_ROWS), vt.shape[1])


def _online_softmax_pv(s, vt1, m_scr, acc_scr, idx, cols):
    at = idx + (slice(None), cols)
    m_prev = m_scr[at]
    m_next = jnp.maximum(m_prev, jnp.max(s, axis=0, keepdims=True))
    p = jnp.exp2((s - m_next).astype(BF16))
    alpha = jnp.exp2(m_prev - m_next)
    m_scr[at] = m_next
    acc_scr[at] = acc_scr[at] * alpha + jnp.dot(vt1, p, preferred_element_type=F32)


DIFF_HEADS_PER_STEP = 2


def _diff_kernel(qi_ref, kj_ref, slopes_ref, lam_ref, g_ref, q_ref, k_ref, ka_ref, vt_ref, o_ref,
                 qa_scr, kk_scr, s0_scr, s1_scr, m_scr, acc_scr, *, tq, n_steps):
    hp, n = pl.program_id(0), pl.program_id(1)
    reps = tq // LANES
    nsub = tq // QUERY_SUB
    w = 2 * HEAD_DIM
    maps = [(hh, c) for hh in range(DIFF_HEADS_PER_STEP) for c in range(2)]
    na = jnp.minimum(n, n_steps - 1)
    nb = jnp.maximum(n - 1, 0)
    ia = qi_ref[na]
    ib, jb = qi_ref[nb], kj_ref[nb]
    has_a, has_b = n < n_steps, n >= 1
    diag_b = jnp.logical_and(has_b, jb == ib)

    score_slots = (s0_scr, s1_scr)

    def map_cols(hh, c):
        return slice(hh * w + c * HEAD_DIM, hh * w + (c + 1) * HEAD_DIM)

    def load_queries():
        for hh in range(DIFF_HEADS_PER_STEP):
            qaug = _query_aug(slopes_ref[hp * DIFF_HEADS_PER_STEP + hh], (ia * tq).astype(F32), tq)
            for c in range(2):
                qa_scr[2 * hh + c, :, :HEAD_DIM] = q_ref[:, map_cols(hh, c)]
                qa_scr[2 * hh + c, :, HEAD_DIM:] = qaug

    def form_scores(slot):
        ka = ka_ref[...]
        for hh, c in maps:
            kk_scr[2 * hh + c, :, :HEAD_DIM] = k_ref[:, map_cols(hh, c)]
            kk_scr[2 * hh + c, :, HEAD_DIM:] = ka
        for m in range(len(maps)):
            score_slots[slot][m] = _dot_nt(kk_scr[m], qa_scr[m])

    def softmax_pv(slot, masked):
        for m, (hh, c) in enumerate(maps):
            vt1 = vt_ref[hh * (w + ONES_ROWS):(hh + 1) * (w + ONES_ROWS), :]
            for u in range(nsub):
                cols = slice(u * QUERY_SUB, (u + 1) * QUERY_SUB)
                s = score_slots[slot][m, :, cols]
                if masked:
                    keep = (lax.broadcasted_iota(jnp.int32, (tq, 1), 0)
                            <= u * QUERY_SUB + lax.broadcasted_iota(jnp.int32, (1, QUERY_SUB), 1))
                    s = jnp.where(keep, s, NEG)
                _online_softmax_pv(s, vt1, m_scr, acc_scr, (m,), cols)

    @pl.when(jnp.logical_and(has_b, jb == 0))
    def _():
        m_scr[...] = jnp.full(m_scr.shape, NEG, F32)
        acc_scr[...] = jnp.zeros(acc_scr.shape, F32)

    def emit_block():
        lam = (jnp.exp(jnp.sum(lam_ref[0:1, :] * lam_ref[1:2, :], axis=1, keepdims=True))
               - jnp.exp(jnp.sum(lam_ref[2:3, :] * lam_ref[3:4, :], axis=1, keepdims=True)) + LAM_INIT)
        g = jnp.concatenate([g_ref[...]] * reps, axis=1)
        for hh in range(DIFF_HEADS_PER_STEP):
            m = 2 * hh
            o = (acc_scr[m, :w, :] / acc_scr[m, w:w + 1, :]
                 - lam * (acc_scr[m + 1, :w, :] / acc_scr[m + 1, w:w + 1, :]))
            o = o * lax.rsqrt(jnp.mean(o * o, axis=0, keepdims=True) + RMS_EPS) * g
            o_ref[:, hh * w:(hh + 1) * w] = (o * (1.0 - LAM_INIT)).T.astype(o_ref.dtype)

    for par in range(2):
        mine = (n % 2) == par

        @pl.when(jnp.logical_and(mine, jnp.logical_and(has_b, jnp.logical_not(diag_b))))
        def _(par=par):
            form_scores(par)
            softmax_pv(1 - par, False)

        @pl.when(jnp.logical_and(mine, jnp.logical_and(diag_b, has_a)))
        def _(par=par):
            load_queries()
            form_scores(par)
            softmax_pv(1 - par, True)
            emit_block()

        @pl.when(jnp.logical_and(mine, jnp.logical_and(diag_b, jnp.logical_not(has_a))))
        def _(par=par):
            softmax_pv(1 - par, True)
            emit_block()

        @pl.when(jnp.logical_and(mine, n == 0))
        def _(par=par):
            load_queries()
            form_scores(par)


def _diff_attention(z, vt, kaug, lam4, g_lanes, slopes, col_q, col_k, s_len, tq):
    nq = s_len // tq
    w = 2 * HEAD_DIM
    wb = DIFF_HEADS_PER_STEP * w
    nmaps = 2 * DIFF_HEADS_PER_STEP
    qb, kb = col_q // wb, col_k // wb
    qi, kj = _causal_steps(nq, lambda i: i)
    n_steps = int(qi.shape[0])

    def ahead(s):
        return jnp.minimum(s, n_steps - 1)

    def behind(s):
        return jnp.maximum(s - 1, 0)

    return pl.pallas_call(
        functools.partial(_diff_kernel, tq=tq, n_steps=n_steps),
        grid_spec=pltpu.PrefetchScalarGridSpec(
            num_scalar_prefetch=3,
            grid=(N_DIFF_HEADS // DIFF_HEADS_PER_STEP, n_steps + 1),
            in_specs=[pl.BlockSpec((4, HEAD_DIM), lambda h, s, qi_, kj_, sl_: (0, 0)),
                      pl.BlockSpec((w, LANES), lambda h, s, qi_, kj_, sl_: (0, 0)),
                      pl.BlockSpec((tq, wb), lambda h, s, qi_, kj_, sl_: (qi_[ahead(s)], qb + h)),
                      pl.BlockSpec((tq, wb), lambda h, s, qi_, kj_, sl_: (kj_[ahead(s)], kb + h)),
                      pl.BlockSpec((tq, LANES), lambda h, s, qi_, kj_, sl_: (kj_[ahead(s)], 0)),
                      pl.BlockSpec((DIFF_HEADS_PER_STEP * (w + ONES_ROWS), tq),
                                   lambda h, s, qi_, kj_, sl_: (h, kj_[behind(s)]))],
            out_specs=pl.BlockSpec((tq, wb), lambda h, s, qi_, kj_, sl_: (qi_[behind(s)], h)),
            scratch_shapes=[pltpu.VMEM((nmaps, tq, w), BF16), pltpu.VMEM((nmaps, tq, w), BF16),
                            pltpu.VMEM((nmaps, tq, tq), F32), pltpu.VMEM((nmaps, tq, tq), F32),
                            pltpu.VMEM((nmaps, 1, tq), F32),
                            pltpu.VMEM((nmaps, w + ONES_ROWS, tq), F32)]),
        out_shape=jax.ShapeDtypeStruct((s_len, N_DIFF_HEADS * w), BF16),
        compiler_params=_cparams(("parallel", "arbitrary"), VMEM_MB),
        name="diffattn",
    )(qi, kj, slopes, lam4, g_lanes, z, z, kaug, vt)


IQ_SPLIT = 4


HALF_BITS = 16
HALF_MASK = (1 << HALF_BITS) - 1
I16_MIN = -(1 << (HALF_BITS - 1))


def _indexer_kernel(iq0_ref, iq1_ref, iq2_ref, iq3_ref, ik_ref, iwt_ref, o_ref,
                    iqh_scr, key_scr, hi_scr, lo_scr, acc_scr, pcut_scr, *, tq, tkc, topk, iw_scale, pos_bits):
    i = pl.program_id(0)
    nk = key_scr.shape[0]
    nchunks = (i * tq + tq - 1) // tkc + 1
    per = N_IDX_HEADS // IQ_SPLIT

    for h in range(N_IDX_HEADS):
        src = (iq0_ref, iq1_ref, iq2_ref, iq3_ref)[h // per]
        iqh_scr[h] = src[:, (h % per) * IDX_DIM:(h % per + 1) * IDX_DIM]

    qpos = i * tq + lax.broadcasted_iota(jnp.int32, (1, tq), 1)

    def key_pos(c):
        return c * tkc + lax.broadcasted_iota(jnp.int32, (tkc, 1), 0)

    def count_keys(pred):
        def body(c, acc):
            x = jnp.where(pred(c, key_scr[c]), 1, 0)
            return acc + jnp.sum(x.reshape(tkc // SUBLANES, SUBLANES, tq), axis=0)

        cnt = lax.fori_loop(0, nchunks, body, jnp.zeros((SUBLANES, tq), jnp.int32))
        return jnp.sum(cnt, axis=0, keepdims=True)

    def chunk(c, carry):
        kc = ik_ref[pl.ds(pl.multiple_of(c * tkc, tkc), tkc), :]
        acc_scr[...] = jnp.zeros(acc_scr.shape, F32)

        def head(h, carry2):
            sc = _dot_nt(kc, iqh_scr[h])
            acc_scr[...] += (iwt_ref[pl.ds(h, 1), :] * iw_scale) * jnp.maximum(sc, 0.0)
            return carry2

        lax.fori_loop(0, N_IDX_HEADS, head, 0, unroll=True)
        score = acc_scr[...]
        score = jnp.where(score == 0.0, 0.0, score)
        bits = pltpu.bitcast(score, jnp.int32)
        skey = bits ^ ((bits >> 31) & 0x7FFFFFFF)
        key = jnp.where(key_pos(c) <= qpos, skey, INT_MIN)
        key_scr[c] = key
        hi_scr[c] = (key >> HALF_BITS).astype(jnp.int16)
        lo_scr[c] = ((key & HALF_MASK) + I16_MIN).astype(jnp.int16)
        return carry

    lax.fori_loop(0, nchunks, chunk, 0)

    def count_halves(half_scr, pred):
        rows = 2 * SUBLANES

        def body(c, acc):
            x = jnp.where(pred(half_scr[c]), jnp.int16(1), jnp.int16(0))
            for g in range(tkc // rows):
                acc = acc + x[g * rows:(g + 1) * rows, :]
            return acc

        cnt = lax.fori_loop(0, nchunks, body, jnp.zeros((rows, tq), jnp.int16))
        return jnp.sum(cnt.astype(jnp.int32), axis=0, keepdims=True)

    def radix_select(half_scr, want):
        def bit_step(bi, u):
            cand = u | lax.shift_left(jnp.int32(1), HALF_BITS - 1 - bi)
            cand16 = (cand + I16_MIN).astype(jnp.int16)
            return jnp.where(count_halves(half_scr, lambda k: k >= cand16) >= want, cand, u)

        return lax.fori_loop(0, HALF_BITS, bit_step, jnp.zeros((1, tq), jnp.int32)) + I16_MIN

    t_hi = radix_select(hi_scr, topk)
    t_hi16 = t_hi.astype(jnp.int16)
    want_lo = topk - count_halves(hi_scr, lambda k: k > t_hi16)

    def keep_lo(c, carry):
        lo_scr[c] = jnp.where(hi_scr[c] == t_hi16, lo_scr[c], jnp.int16(I16_MIN))
        return carry

    lax.fori_loop(0, nchunks, keep_lo, 0)
    t_lo = radix_select(lo_scr, want_lo)
    thr = lax.shift_left(t_hi, HALF_BITS) | (t_lo - I16_MIN)
    thr = jnp.maximum(thr, INT_MIN + 1)

    pcut_scr[...] = jnp.full(pcut_scr.shape, (1 << pos_bits) - 1, jnp.int32)

    @pl.when(jnp.max(count_keys(lambda c, k: k >= thr)) > topk)
    def _():
        need = topk - count_keys(lambda c, k: k > thr)

        def pos_step(bi, p):
            cand = p | lax.shift_left(jnp.int32(1), pos_bits - 1 - bi)
            tied_before = count_keys(lambda c, k: jnp.where(k == thr, key_pos(c), cand) < cand)
            return jnp.where(tied_before < need, cand, p)

        pcut_scr[...] = lax.fori_loop(0, pos_bits, pos_step, jnp.zeros((1, tq), jnp.int32))

    pcut = pcut_scr[...]

    def emit(c, carry):
        k = key_scr[c]
        tie_bias = jnp.where(key_pos(c) <= pcut, 0.0, NEG)
        o_ref[0, c] = jnp.where(k > thr, 0.0, jnp.where(k == thr, tie_bias, NEG)).astype(o_ref.dtype)
        return carry

    lax.fori_loop(0, nchunks, emit, 0)

    def fill(c, carry):
        o_ref[0, c] = jnp.full((tkc, tq), NEG, o_ref.dtype)
        return carry

    lax.fori_loop(nchunks, nk, fill, 0)


def _indexer(z, ik, iwt, col_iq, s_len, tq, tkc, topk):
    nq, nk = s_len // tq, s_len // tkc
    iw_scale = N_IDX_HEADS ** -0.5 * IDX_DIM ** -0.5
    wq = N_IDX_HEADS * IDX_DIM // IQ_SPLIT
    qb = col_iq // wq
    iq_specs = [pl.BlockSpec((tq, wq), functools.partial(lambda i, n: (i, qb + n), n=n)) for n in range(IQ_SPLIT)]
    return pl.pallas_call(
        functools.partial(_indexer_kernel, tq=tq, tkc=tkc, topk=topk, iw_scale=iw_scale,
                          pos_bits=max(1, (s_len - 1).bit_length())),
        grid=(nq,),
        in_specs=iq_specs + [pl.BlockSpec((s_len, IDX_DIM), lambda i: (0, 0)),
                             pl.BlockSpec((N_IDX_HEADS, tq), lambda i: (0, i))],
        out_specs=pl.BlockSpec((1, nk, tkc, tq), lambda i: (i, 0, 0, 0)),
        out_shape=jax.ShapeDtypeStruct((nq, nk, tkc, tq), BF16),
        scratch_shapes=[pltpu.VMEM((N_IDX_HEADS, tq, IDX_DIM), BF16),
                        pltpu.VMEM((nk, tkc, tq), jnp.int32),
                        pltpu.VMEM((nk, tkc, tq), jnp.int16),
                        pltpu.VMEM((nk, tkc, tq), jnp.int16),
                        pltpu.VMEM((tkc, tq), F32),
                        pltpu.VMEM((1, tq), jnp.int32)],
        compiler_params=_cparams(("parallel",), VMEM_MB),
        name="indexer",
    )(z, z, z, z, ik, iwt)


DSA_GROUPS_PER_STEP = 2


def _dsa_kernel(qi_ref, kj_ref, slopes_ref, q_ref, k_ref, ka_ref, vt_ref, mb_ref, o_ref,
                qa_scr, kk_scr, m_scr, acc_scr, *, tq, tk):
    gp, step_id = pl.program_id(0), pl.program_id(1)
    i, j = qi_ref[step_id], kj_ref[step_id]
    last = (i * tq + tq - 1) // tk
    heads = DSA_GROUPS_PER_STEP * DSA_REP

    @pl.when(j == 0)
    def _():
        m_scr[...] = jnp.full(m_scr.shape, NEG, F32)
        acc_scr[...] = jnp.zeros(acc_scr.shape, F32)
        for n in range(heads):
            rows = slice(n * tq, (n + 1) * tq)
            qa_scr[rows, :HEAD_DIM] = q_ref[:, n * HEAD_DIM:(n + 1) * HEAD_DIM]
            qa_scr[rows, HEAD_DIM:] = _query_aug(slopes_ref[gp * heads + n], (i * tq).astype(F32), tq)

    ka = ka_ref[...]
    for gg in range(DSA_GROUPS_PER_STEP):
        kk_scr[gg, :, :HEAD_DIM] = k_ref[:, gg * HEAD_DIM:(gg + 1) * HEAD_DIM]
        kk_scr[gg, :, HEAD_DIM:] = ka
    mb = mb_ref[0, 0].astype(F32)
    scores = [_dot_nt(kk_scr[gg], qa_scr[gg * DSA_REP * tq:(gg + 1) * DSA_REP * tq, :])
              for gg in range(DSA_GROUPS_PER_STEP)]
    for gg in range(DSA_GROUPS_PER_STEP):
        vt1 = vt_ref[gg * (HEAD_DIM + ONES_ROWS):(gg + 1) * (HEAD_DIM + ONES_ROWS), :]
        for r in range(0, DSA_REP, 2):
            n = gg * DSA_REP + r
            _online_softmax_pv(scores[gg][:, r * tq:(r + 2) * tq] + jnp.concatenate([mb, mb], axis=1),
                               vt1, m_scr, acc_scr, (), slice(n * tq, (n + 2) * tq))

    @pl.when(j == last)
    def _():
        o = acc_scr[:HEAD_DIM, :] / acc_scr[HEAD_DIM:HEAD_DIM + 1, :]
        for n in range(heads):
            o_ref[:, n * HEAD_DIM:(n + 1) * HEAD_DIM] = o[:, n * tq:(n + 1) * tq].T.astype(o_ref.dtype)


def _dsa_attention(z, vt, kaug, maskb, slopes, col_q, col_k, s_len, tq, tk):
    nq = s_len // tq
    gps = DSA_GROUPS_PER_STEP
    heads = gps * DSA_REP
    qw, kw = heads * HEAD_DIM, gps * HEAD_DIM
    qb, kb = col_q // qw, col_k // kw
    qi, kj = _causal_steps(nq, lambda i: (i * tq + tq - 1) // tk)
    return pl.pallas_call(
        functools.partial(_dsa_kernel, tq=tq, tk=tk),
        grid_spec=pltpu.PrefetchScalarGridSpec(
            num_scalar_prefetch=3,
            grid=(N_DSA_KV // gps, qi.shape[0]),
            in_specs=[pl.BlockSpec((tq, qw), lambda g, s, qi_, kj_, sl_: (qi_[s], qb + g)),
                      pl.BlockSpec((tk, kw), lambda g, s, qi_, kj_, sl_: (kj_[s], kb + g)),
                      pl.BlockSpec((tk, LANES), lambda g, s, qi_, kj_, sl_: (kj_[s], 0)),
                      pl.BlockSpec((gps * (HEAD_DIM + ONES_ROWS), tk), lambda g, s, qi_, kj_, sl_: (g, kj_[s])),
                      pl.BlockSpec((1, 1, tk, tq), lambda g, s, qi_, kj_, sl_: (qi_[s], kj_[s], 0, 0))],
            out_specs=pl.BlockSpec((tq, qw), lambda g, s, qi_, kj_, sl_: (qi_[s], g)),
            scratch_shapes=[pltpu.VMEM((heads * tq, 2 * HEAD_DIM), BF16),
                            pltpu.VMEM((gps, tk, 2 * HEAD_DIM), BF16),
                            pltpu.VMEM((1, heads * tq), F32),
                            pltpu.VMEM((HEAD_DIM + ONES_ROWS, heads * tq), F32)]),
        out_shape=jax.ShapeDtypeStruct((s_len, N_DSA_HEADS * HEAD_DIM), BF16),
        compiler_params=_cparams(("parallel", "arbitrary"), VMEM_MB),
        name="dsa",
    )(qi, kj, slopes, z, z, kaug, vt, maskb)


def _merge_kernel(x_ref, wga_ref, wgb_ref, a_ref, wpa_ref, b_ref, wpb_ref, o_ref):
    x = x_ref[...]
    ga = _dot_nt(x, wga_ref[...])
    gb = _dot_nt(x, wgb_ref[...])
    pa = jnp.dot(a_ref[...], wpa_ref[...], preferred_element_type=F32)
    pb = jnp.dot(b_ref[...], wpb_ref[...], preferred_element_type=F32)
    o_ref[...] = (jax.nn.sigmoid(ga) * pa + jax.nn.sigmoid(gb) * pb).astype(o_ref.dtype)


def _merge(xb, wt_gates, a, wpa, b, wpb, tm, tn):
    m, d = xb.shape
    ka, kb = a.shape[1], b.shape[1]
    tm, tn = min(tm, m), min(tn, d)
    row = lambda i, j: (i, 0)
    col = lambda i, j: (0, j)
    return pl.pallas_call(
        _merge_kernel,
        grid=(m // tm, d // tn),
        in_specs=[pl.BlockSpec((tm, d), row),
                  pl.BlockSpec((tn, d), lambda i, j: (j, 0)), pl.BlockSpec((tn, d), lambda i, j: (d // tn + j, 0)),
                  pl.BlockSpec((tm, ka), row), pl.BlockSpec((ka, tn), col),
                  pl.BlockSpec((tm, kb), row), pl.BlockSpec((kb, tn), col)],
        out_specs=pl.BlockSpec((tm, tn), lambda i, j: (i, j)),
        out_shape=jax.ShapeDtypeStruct((m, d), BF16),
        compiler_params=_cparams(("parallel", "arbitrary"), VMEM_LARGE_MB),
        name="merge",
    )(xb, wt_gates, wt_gates, a, wpa, b, wpb)


def _split_bf16(x):
    hi = x.astype(BF16)
    lo = (x - hi.astype(F32)).astype(BF16)
    return hi, lo


def _out_kernel(mg_ref, wo_ref, x_ref, g_ref, b_ref, wr_ref, br_ref,
                h_ref, route_ref, oh_ref, pre_scr, *, tn, nj):
    j = pl.program_id(1)
    y = jnp.dot(mg_ref[...], wo_ref[...], preferred_element_type=F32)
    pre_scr[j] = DEEPNORM_ALPHA * x_ref[...] + y

    @pl.when(j == nj - 1)
    def _():
        d = nj * tn
        tot = pre_scr[0].sum(axis=1, keepdims=True)
        for jj in range(1, nj):
            tot = tot + pre_scr[jj].sum(axis=1, keepdims=True)
        mu = tot / d
        sq = jnp.square(pre_scr[0] - mu).sum(axis=1, keepdims=True)
        for jj in range(1, nj):
            sq = sq + jnp.square(pre_scr[jj] - mu).sum(axis=1, keepdims=True)
        rstd = lax.rsqrt(sq / d + LN_EPS)
        logits = jnp.zeros(route_ref.shape, F32)
        for jj in range(nj):
            cs = slice(jj * tn, (jj + 1) * tn)
            hn = (pre_scr[jj] - mu) * rstd * g_ref[:, cs] + b_ref[:, cs]
            h_ref[:, cs] = hn
            h_hi, h_lo = _split_bf16(hn)
            w_hi, w_lo = _split_bf16(wr_ref[cs, :])
            logits = logits + (jnp.dot(h_hi, w_hi, preferred_element_type=F32)
                               + jnp.dot(h_hi, w_lo, preferred_element_type=F32)
                               + jnp.dot(h_lo, w_hi, preferred_element_type=F32))
        logits = logits + br_ref[...]
        lane = lax.broadcasted_iota(jnp.int32, logits.shape, 1)
        big = jnp.int32(4 * LANES)
        gl = jnp.where(lane < N_GROUPS, logits, -jnp.inf)
        gmax = jnp.max(gl, axis=1, keepdims=True)
        gsel = jnp.min(jnp.where(gl == gmax, lane, big), axis=1, keepdims=True)
        ggate = 1.0 / jnp.sum(jnp.exp(gl - gmax), axis=1, keepdims=True)
        eid = lane - N_GROUPS
        ingrp = (eid >= gsel * EXPERTS_PER_GROUP) & (eid < (gsel + 1) * EXPERTS_PER_GROUP)
        el = jnp.where(ingrp, logits, -jnp.inf)
        v1 = jnp.max(el, axis=1, keepdims=True)
        i1 = jnp.min(jnp.where(el == v1, lane, big), axis=1, keepdims=True)
        el2 = jnp.where(lane == i1, -jnp.inf, el)
        v2 = jnp.max(el2, axis=1, keepdims=True)
        i2 = jnp.min(jnp.where(el2 == v2, lane, big), axis=1, keepdims=True)
        t = jnp.exp(v2 - v1)
        g1 = ggate / (1.0 + t)
        g2 = g1 * t
        e1 = (i1 - N_GROUPS).astype(F32)
        e2 = (i2 - N_GROUPS).astype(F32)
        route_ref[...] = jnp.where(lane == 0, g1, jnp.where(lane == 1, g2,
                                   jnp.where(lane == 2, e1, jnp.where(lane == 3, e2, 0.0))))
        oh_ref[...] = jnp.where(lane == i1 - N_GROUPS, 1.0,
                                jnp.where(lane == i2 - N_GROUPS, 1.0, 0.0)).astype(oh_ref.dtype)


def _out_ln_router(mg, wo, x, g, b, wr, br, tm, tn):
    m, d = x.shape
    tm, tn = min(tm, m), min(tn, d)
    nj = d // tn
    return pl.pallas_call(
        functools.partial(_out_kernel, tn=tn, nj=nj),
        grid=(m // tm, nj),
        in_specs=[pl.BlockSpec((tm, d), lambda i, j: (i, 0)),
                  pl.BlockSpec((d, tn), lambda i, j: (0, j)),
                  pl.BlockSpec((tm, tn), lambda i, j: (i, j)),
                  pl.BlockSpec((1, d), lambda i, j: (0, 0)),
                  pl.BlockSpec((1, d), lambda i, j: (0, 0)),
                  pl.BlockSpec((d, LANES), lambda i, j: (0, 0)),
                  pl.BlockSpec((1, LANES), lambda i, j: (0, 0))],
        out_specs=[pl.BlockSpec((tm, d), lambda i, j: (i, 0)),
                   pl.BlockSpec((tm, LANES), lambda i, j: (i, 0)),
                   pl.BlockSpec((tm, LANES), lambda i, j: (i, 0))],
        out_shape=[jax.ShapeDtypeStruct((m, d), F32),
                   jax.ShapeDtypeStruct((m, LANES), F32),
                   jax.ShapeDtypeStruct((m, LANES), BF16)],
        scratch_shapes=[pltpu.VMEM((nj, tm, tn), F32)],
        compiler_params=_cparams(("parallel", "arbitrary"), VMEM_LARGE_MB),
        name="outln",
    )(mg, wo, x, g, b, wr, br)


def _rank_kernel(oh_ref, pos_ref, cnt_ref, base_scr, *, tb):
    @pl.when(pl.program_id(0) == 0)
    def _():
        base_scr[...] = jnp.zeros(base_scr.shape, F32)

    oh = oh_ref[...]
    r = lax.broadcasted_iota(jnp.int32, (tb, tb), 0)
    c = lax.broadcasted_iota(jnp.int32, (tb, tb), 1)
    tri = jnp.where(c <= r, 1.0, 0.0).astype(BF16)
    cs = jnp.dot(tri, oh, preferred_element_type=F32)
    pos_ref[...] = cs - oh.astype(F32) + base_scr[0:1, :]
    base_scr[...] = base_scr[...] + cs[tb - 1:tb, :]
    cnt_ref[...] = base_scr[...]


def _rank(onehot, tb):
    m = onehot.shape[0]
    tb = min(tb, m)
    return pl.pallas_call(
        functools.partial(_rank_kernel, tb=tb),
        grid=(m // tb,),
        in_specs=[pl.BlockSpec((tb, LANES), lambda i: (i, 0))],
        out_specs=[pl.BlockSpec((tb, LANES), lambda i: (i, 0)),
                   pl.BlockSpec((8, LANES), lambda i: (0, 0))],
        out_shape=[jax.ShapeDtypeStruct((m, LANES), F32), jax.ShapeDtypeStruct((8, LANES), F32)],
        scratch_shapes=[pltpu.VMEM((8, LANES), F32)],
        compiler_params=_cparams(("arbitrary",), VMEM_SMALL_MB),
        name="rank",
    )(onehot)


def _dest_kernel(pos_ref, route_ref, start_ref, dest_ref):
    lane = lax.broadcasted_iota(jnp.int32, pos_ref.shape, 1).astype(F32)
    v = pos_ref[...] + start_ref[...]
    d1 = jnp.sum(jnp.where(lane == route_ref[:, 2:3], v, 0.0), axis=1, keepdims=True)
    d2 = jnp.sum(jnp.where(lane == route_ref[:, 3:4], v, 0.0), axis=1, keepdims=True)
    dest_ref[...] = jnp.where(lane == 0.0, d1, jnp.where(lane == 1.0, d2, 0.0)).astype(jnp.int32)


def _dest(pos, route, start, tb):
    m = pos.shape[0]
    tb = min(tb, m)
    return pl.pallas_call(
        _dest_kernel,
        grid=(m // tb,),
        in_specs=[pl.BlockSpec((tb, LANES), lambda i: (i, 0)),
                  pl.BlockSpec((tb, LANES), lambda i: (i, 0)),
                  pl.BlockSpec((1, LANES), lambda i: (0, 0))],
        out_specs=pl.BlockSpec((tb, LANES), lambda i: (i, 0)),
        out_shape=jax.ShapeDtypeStruct((m, LANES), jnp.int32),
        compiler_params=_cparams(("parallel",), VMEM_SMALL_MB),
        name="dest",
    )(pos, route, start)


DMA_ISSUE_UNROLL = 4


def _pack_bf16_pair(left, right):
    lo = pltpu.bitcast(left.astype(BF16).astype(F32), jnp.int32)
    hi = pltpu.bitcast(right.astype(BF16).astype(F32), jnp.int32)
    return hi | lax.shift_right_logical(lo, 16)


def _unpack_bf16_pair(words):
    left = pltpu.bitcast(lax.shift_left(words, 16), F32)
    right = pltpu.bitcast(words & jnp.int32(-65536), F32)
    return left, right


def _scatter_kernel(dest_ref, h_ref, xs_in_ref, xs_ref, hp_scr, sem, *, tb):
    del xs_in_ref
    base = pl.program_id(0) * tb
    half = h_ref.shape[1] // 2
    hp_scr[...] = _pack_bf16_pair(h_ref[:, :half], h_ref[:, half:])

    def row_copy(t, slot):
        d = dest_ref[(base + t) * 2 + slot]
        return pltpu.make_async_copy(hp_scr.at[pl.ds(t, 1), :], xs_ref.at[pl.ds(d, 1), :], sem)

    def start(t, carry):
        row_copy(t, 0).start(priority=0)
        row_copy(t, 1).start(priority=1)
        return carry

    lax.fori_loop(0, tb, start, 0, unroll=DMA_ISSUE_UNROLL)
    for _ in range(2):
        pltpu.make_async_copy(hp_scr, xs_ref.at[pl.ds(0, tb), :], sem).wait()


def _scatter_rows(dest_flat, h, n_rows, tb):
    m, d = h.shape
    tb = min(tb, m)
    xs0 = jnp.zeros((n_rows, d // 2), jnp.int32)
    return pl.pallas_call(
        functools.partial(_scatter_kernel, tb=tb),
        grid_spec=pltpu.PrefetchScalarGridSpec(
            num_scalar_prefetch=1,
            grid=(m // tb,),
            in_specs=[pl.BlockSpec((tb, d), lambda i, dest: (i, 0)),
                      pl.BlockSpec(memory_space=pl.ANY)],
            out_specs=pl.BlockSpec(memory_space=pl.ANY),
            scratch_shapes=[pltpu.VMEM((tb, d // 2), jnp.int32), pltpu.SemaphoreType.DMA(())]),
        out_shape=jax.ShapeDtypeStruct((n_rows, d // 2), jnp.int32),
        input_output_aliases={2: 0},
        compiler_params=_cparams(("arbitrary",), VMEM_SMALL_MB),
        name="scatter",
    )(dest_flat, h, xs0)


CAST_ROWS = 256


def _segment_tables(te, nv, end_tile, n_tiles):
    ids = jnp.arange(n_tiles, dtype=jnp.int32)
    fresh = jnp.concatenate([jnp.ones((1,), bool), te[1:] != te[:-1]]) & (ids < nv[0])
    seg = jnp.cumsum(fresh.astype(jnp.int32)) - 1
    after = end_tile[te]
    nxt = jnp.where(after < nv[0], te[jnp.minimum(after, n_tiles - 1)], -1)
    nseg = seg[jnp.maximum(nv[0] - 1, 0)][None] + 1
    return seg.astype(jnp.int32), nxt.astype(jnp.int32), nseg.astype(jnp.int32)


def _expert_weight_stream(te_ref, nv_ref, seg_ref, nxt_ref, nseg_ref, w_hbms, slabs, bf16_scrs, sem, width):
    c, r = pl.program_id(0), pl.program_id(1)
    valid = r < nv_ref[0]
    fresh = jnp.logical_and(valid, jnp.logical_or(r == 0, te_ref[r] != te_ref[jnp.maximum(r - 1, 0)]))
    ordinal = c * nseg_ref[0] + seg_ref[r]
    slot = ordinal % 2

    def slab_copies(expert, chunk, s):
        cols = pl.ds(pl.multiple_of(chunk * width, width), width)
        return [pltpu.make_async_copy(w.at[expert, :, cols], slab.at[s], sem.at[n, s])
                for n, (w, slab) in enumerate(zip(w_hbms, slabs))]

    def start(expert, chunk, s):
        for cp in slab_copies(expert, chunk, s):
            cp.start()

    @pl.when(jnp.logical_and(fresh, ordinal == 0))
    def _():
        start(te_ref[0], 0, 0)

    @pl.when(fresh)
    def _():
        for cp in slab_copies(te_ref[r], c, slot):
            cp.wait()
        for slab, scr in zip(slabs, bf16_scrs):
            def cast_rows(n, carry, slab=slab, scr=scr):
                rows = pl.ds(pl.multiple_of(n * CAST_ROWS, CAST_ROWS), CAST_ROWS)
                scr[rows, :] = slab[slot, rows, :].astype(BF16)
                return carry

            lax.fori_loop(0, scr.shape[0] // CAST_ROWS, cast_rows, 0)
        nxt = nxt_ref[r]

        @pl.when(nxt >= 0)
        def _():
            start(nxt, c, 1 - slot)

        @pl.when(jnp.logical_and(nxt < 0, c + 1 < pl.num_programs(0)))
        def _():
            start(te_ref[0], c + 1, 1 - slot)

    return valid


def _gm1_kernel(te_ref, nv_ref, seg_ref, nxt_ref, nseg_ref, xs_ref, w1_ref, w3_ref, o_ref,
                w1f_scr, w3f_scr, w1b_scr, w3b_scr, sem):
    valid = _expert_weight_stream(te_ref, nv_ref, seg_ref, nxt_ref, nseg_ref, (w1_ref, w3_ref),
                                  (w1f_scr, w3f_scr), (w1b_scr, w3b_scr), sem, w1b_scr.shape[1])

    @pl.when(valid)
    def _():
        left, right = _unpack_bf16_pair(xs_ref[...])
        x = jnp.concatenate([left.astype(BF16), right.astype(BF16)], axis=1)
        a = jnp.dot(x, w1b_scr[...], preferred_element_type=F32)
        b = jnp.dot(x, w3b_scr[...], preferred_element_type=F32)
        o_ref[...] = (a * jax.nn.sigmoid(a) * b).astype(o_ref.dtype)

    @pl.when(jnp.logical_not(valid))
    def _():
        o_ref[...] = jnp.zeros(o_ref.shape, o_ref.dtype)


def _gm1(tables, xs, w1, w3, tf):
    n_rows = xs.shape[0]
    d, f = w1.shape[1], w1.shape[2]
    nt = n_rows // ROW_TILE

    def rc(r, nv_):
        return jnp.minimum(r, nv_[0] - 1)

    return pl.pallas_call(
        _gm1_kernel,
        grid_spec=pltpu.PrefetchScalarGridSpec(
            num_scalar_prefetch=len(tables),
            grid=(f // tf, nt),
            in_specs=[pl.BlockSpec((ROW_TILE, d // 2), lambda c, r, te_, nv_, *_: (rc(r, nv_), 0)),
                      pl.BlockSpec(memory_space=pl.ANY),
                      pl.BlockSpec(memory_space=pl.ANY)],
            out_specs=pl.BlockSpec((ROW_TILE, tf), lambda c, r, *_: (r, c)),
            scratch_shapes=[pltpu.VMEM((2, d, tf), F32), pltpu.VMEM((2, d, tf), F32),
                            pltpu.VMEM((d, tf), BF16), pltpu.VMEM((d, tf), BF16),
                            pltpu.SemaphoreType.DMA((2, 2))]),
        out_shape=jax.ShapeDtypeStruct((n_rows, f), BF16),
        compiler_params=_cparams(("arbitrary", "arbitrary"), VMEM_LARGE_MB),
        name="gm1",
    )(*tables, xs, w1, w3)


def _gm2_kernel(te_ref, nv_ref, seg_ref, nxt_ref, nseg_ref, h_ref, w2_ref, o_ref, w2f_scr, w2b_scr, sem):
    valid = _expert_weight_stream(te_ref, nv_ref, seg_ref, nxt_ref, nseg_ref, (w2_ref,),
                                  (w2f_scr,), (w2b_scr,), sem, w2b_scr.shape[1])

    @pl.when(valid)
    def _():
        y = jnp.dot(h_ref[...], w2b_scr[...], preferred_element_type=F32)
        half = y.shape[1] // 2
        o_ref[...] = _pack_bf16_pair(y[:, :half], y[:, half:])

    @pl.when(jnp.logical_not(valid))
    def _():
        o_ref[...] = jnp.zeros(o_ref.shape, o_ref.dtype)


def _gm2(tables, hid, w2, tn):
    n_rows, f = hid.shape
    d = w2.shape[2]
    tn = min(tn, d)
    nt = n_rows // ROW_TILE

    def rc(r, nv_):
        return jnp.minimum(r, nv_[0] - 1)

    return pl.pallas_call(
        _gm2_kernel,
        grid_spec=pltpu.PrefetchScalarGridSpec(
            num_scalar_prefetch=len(tables),
            grid=(d // tn, nt),
            in_specs=[pl.BlockSpec((ROW_TILE, f), lambda c, r, te_, nv_, *_: (rc(r, nv_), 0)),
                      pl.BlockSpec(memory_space=pl.ANY)],
            out_specs=pl.BlockSpec((ROW_TILE, tn // 2), lambda c, r, *_: (r, c)),
            scratch_shapes=[pltpu.VMEM((2, f, tn), F32), pltpu.VMEM((f, tn), BF16),
                            pltpu.SemaphoreType.DMA((1, 2))]),
        out_shape=jax.ShapeDtypeStruct((n_rows, d // 2), jnp.int32),
        compiler_params=_cparams(("arbitrary", "arbitrary"), VMEM_LARGE_MB),
        name="gm2",
    )(*tables, hid, w2)


def _combine_kernel(dest_ref, y_ref, h_ref, route_ref, g_ref, b_ref, o_ref, ybuf, sem, *, tb, tn):
    i, nblk = pl.program_id(0), pl.num_programs(0)

    def row_copy(blk, buf, t, e):
        d = dest_ref[(blk * tb + t) * 2 + e]
        return pltpu.make_async_copy(y_ref.at[pl.ds(d, 1), :], ybuf.at[buf, e, pl.ds(t, 1), :], sem.at[buf])

    def fetch(blk, buf):
        def body(t, carry):
            row_copy(blk, buf, t, 0).start(priority=0)
            row_copy(blk, buf, t, 1).start(priority=1)
            return carry

        lax.fori_loop(0, tb, body, 0, unroll=DMA_ISSUE_UNROLL)

    @pl.when(i == 0)
    def _():
        fetch(0, 0)

    @pl.when(i + 1 < nblk)
    def _():
        fetch(i + 1, (i + 1) % 2)

    cur = i % 2

    for e in range(2):
        pltpu.make_async_copy(y_ref.at[pl.ds(0, tb), :], ybuf.at[cur, e], sem.at[cur]).wait()
    half = tn // 2
    pieces = []
    for c in range(h_ref.shape[1] // tn):
        l0, r0 = _unpack_bf16_pair(ybuf[cur, 0, :, c * half:(c + 1) * half])
        l1, r1 = _unpack_bf16_pair(ybuf[cur, 1, :, c * half:(c + 1) * half])
        pieces += [route_ref[:, 0:1] * l0 + route_ref[:, 1:2] * l1, route_ref[:, 0:1] * r0 + route_ref[:, 1:2] * r1]
    moe = jnp.concatenate(pieces, axis=1)
    pre = DEEPNORM_ALPHA * h_ref[...] + moe
    mu = jnp.mean(pre, axis=1, keepdims=True)
    var = jnp.mean(jnp.square(pre - mu), axis=1, keepdims=True)
    o_ref[...] = (pre - mu) * lax.rsqrt(var + LN_EPS) * g_ref[...] + b_ref[...]


def _combine_ln(dest_flat, y, h, route, g, b, tb, tn):
    m, d = h.shape
    tb = min(tb, m)
    return pl.pallas_call(
        functools.partial(_combine_kernel, tb=tb, tn=tn),
        grid_spec=pltpu.PrefetchScalarGridSpec(
            num_scalar_prefetch=1,
            grid=(m // tb,),
            in_specs=[pl.BlockSpec(memory_space=pl.ANY),
                      pl.BlockSpec((tb, d), lambda i, dest: (i, 0)),
                      pl.BlockSpec((tb, LANES), lambda i, dest: (i, 0)),
                      pl.BlockSpec((1, d), lambda i, dest: (0, 0)),
                      pl.BlockSpec((1, d), lambda i, dest: (0, 0))],
            out_specs=pl.BlockSpec((tb, d), lambda i, dest: (i, 0)),
            scratch_shapes=[pltpu.VMEM((2, 2, tb, d // 2), jnp.int32), pltpu.SemaphoreType.DMA((2,))]),
        out_shape=jax.ShapeDtypeStruct((m, d), F32),
        compiler_params=_cparams(("arbitrary",), VMEM_MB),
        name="combine",
    )(dest_flat, y, h, route, g, b)


def _alibi_slopes(n):
    return jnp.asarray(2.0 ** (-8.0 * np.arange(1, n + 1) / n), dtype=F32)


def kernel(x, w_in, lam_q1, lam_k1, lam_q2, lam_k2, diff_subln_g, w_pa, w_pb, w_o, ln1_g, ln1_b,
           router_wg, router_bg, router_we, router_be, w1, w3, w2, ln2_g, ln2_b):
    bsz, s_len, d = x.shape
    assert bsz == 1 and w_in.shape[0] == DEPTH
    topk = min(TOPK_MAX, s_len // 4)
    x2 = x[0]
    xb = x2.astype(BF16)

    qk_w = N_DIFF_HEADS * 2 * HEAD_DIM
    c_dq, c_dk, c_dv = 0, qk_w, 2 * qk_w
    c_sq = 3 * qk_w
    c_sk = c_sq + N_DSA_HEADS * HEAD_DIM
    c_sv = c_sk + N_DSA_KV * HEAD_DIM
    c_iq = c_sv + N_DSA_KV * HEAD_DIM
    c_ik = c_iq + N_IDX_HEADS * IDX_DIM
    c_ga = c_ik + IDX_DIM + N_IDX_HEADS
    qscale = HEAD_DIM ** -0.5 * LOG2E
    col = np.arange(c_ik)
    is_q = ((col >= c_dq) & (col < c_dk)) | ((col >= c_sq) & (col < c_sk))
    col_scale = jnp.asarray(np.where(is_q, qscale, 1.0)[None, :], F32)
    wt3d = jnp.swapaxes(w_in, 1, 2)
    wt = wt3d[0]
    wt_gates = _cast_rows(wt, c_ga, 2 * d, min(PROJ_TILE[1], d))

    z = _proj_from_f32(xb, wt3d, col_scale, c_ik, *PROJ_TILE)
    small = 2 * LANES
    assert c_ik % small == 0 and c_ik + small <= wt.shape[0] and s_len <= POS_RADIX * POS_RADIX
    zs = _matmul_nt(xb, wt, c_ik // small, small, F32, PROJ_TILE[0], "proj_small")
    ik = zs[:, :IDX_DIM].astype(BF16)
    iwt = zs[:, IDX_DIM:IDX_DIM + N_IDX_HEADS].T
    dvt = _with_ones_rows(z[:, c_dv:c_dv + qk_w].T, N_DIFF_HEADS)
    svt = _with_ones_rows(z[:, c_sv:c_sv + N_DSA_KV * HEAD_DIM].T, N_DSA_KV)
    kaug = _key_aug_table(s_len)

    lam4 = jnp.stack([lam_q1[0], lam_k1[0], lam_q2[0], lam_k2[0]]).astype(F32)
    g_lanes = jnp.broadcast_to(diff_subln_g[0][:, None], (2 * HEAD_DIM, LANES))
    a = _diff_attention(z, dvt, kaug, lam4, g_lanes, _alibi_slopes(N_DIFF_HEADS), c_dq, c_dk, s_len,
                        min(ATTN_TILE, s_len))

    tq_i, tk_i = min(DSA_TILE[0], s_len), min(DSA_TILE[1], s_len)
    maskb = _indexer(z, ik, iwt, c_iq, s_len, tq_i, tk_i, topk)
    b = _dsa_attention(z, svt, kaug, maskb, _alibi_slopes(N_DSA_HEADS), c_sq, c_sk, s_len, tq_i, tk_i)

    merged = _merge(xb, wt_gates, a, w_pa[0].astype(BF16), b, w_pb[0].astype(BF16), *MERGE_TILE)

    wr = jnp.concatenate([router_wg[0], router_we[0],
                          jnp.zeros((d, LANES - N_GROUPS - N_EXPERTS), F32)], axis=1)
    br = jnp.concatenate([router_bg[0], router_be[0],
                          jnp.zeros((LANES - N_GROUPS - N_EXPERTS,), F32)])[None, :]
    h1, route, onehot = _out_ln_router(merged, w_o[0].astype(BF16), x2, ln1_g[0][None, :], ln1_b[0][None, :],
                                       wr, br, *OUT_TILE)

    pos, cnt = _rank(onehot, RANK_ROWS)
    counts = cnt[0, :N_EXPERTS].astype(jnp.int32)
    padded = ((counts + ROW_TILE - 1) // ROW_TILE) * ROW_TILE
    ends = jnp.cumsum(padded)
    start = jnp.zeros((1, LANES), F32).at[0, :N_EXPERTS].set((ends - padded).astype(F32))
    n_tiles = (2 * s_len) // ROW_TILE + N_EXPERTS
    tile_ids = jnp.arange(n_tiles, dtype=jnp.int32)
    te = jnp.minimum(jnp.sum(tile_ids[:, None] >= (ends // ROW_TILE)[None, :], axis=1), N_EXPERTS - 1)
    te = te.astype(jnp.int32)
    nv = (ends[-1] // ROW_TILE).astype(jnp.int32)[None]
    dest = _dest(pos, route, start, RANK_ROWS)
    dest_flat = dest[:, :2].reshape(-1)

    xs = _scatter_rows(dest_flat, h1, n_tiles * ROW_TILE, SCATTER_ROWS)
    tables = (te, nv) + _segment_tables(te, nv, ends // ROW_TILE, n_tiles)
    hid = _gm1(tables, xs, w1[0], w3[0], min(GM1_COLS, w1.shape[3]))
    tn_y = min(GM2_COLS, d)
    y = _gm2(tables, hid, w2[0], tn_y)
    out = _combine_ln(dest_flat, y, h1, route, ln2_g[0][None, :], ln2_b[0][None, :], COMBINE_ROWS, tn_y)
    return out[None]
```
